```python
import jax, jax.numpy as jnp
from jax import lax
import numpy as np

D_MODEL = 1024
BATCH = 8
SEQ = 2048
DEPTH = 1
DEC_BATCH = 128
DEC_SEQ = 8
PAST_LEN = 8192
PAGE_SIZE = 128

D_CONV = D_MODEL
CONV_WIDTH = 3
HEAD_DIM = 64
N_HEADS = D_MODEL // HEAD_DIM
N_KV_HEADS = 4
GROUP = N_HEADS // N_KV_HEADS
WINDOW = 128
BLOCK = WINDOW
D_FF = 2816
Q_WIDTH = N_HEADS * HEAD_DIM
KV_WIDTH = N_KV_HEADS * HEAD_DIM
IN_WIDTH = 3 * D_CONV + Q_WIDTH + 2 * KV_WIDTH + 2 * D_MODEL
RMS_EPS = 1e-6
NEG_INF = -1e30

kernel_name = 'hybrid_conv_swa_sink_convffn_adaln_step'


def _rmsnorm(x, g):
    x32 = x.astype(jnp.float32)
    y = x32 * lax.rsqrt(jnp.mean(x32 * x32, axis=-1, keepdims=True) + RMS_EPS)
    return (y * g.astype(jnp.float32)).astype(x.dtype)


def _causal_conv(u, buf, w):
    t = u.shape[1]
    full = jnp.concatenate([buf.astype(u.dtype), u], axis=1)
    y = full[:, 0:t] * w[0]
    for tap in range(1, CONV_WIDTH):
        y = y + full[:, tap:tap + t] * w[tap]
    return y, full[:, t:]


def _alibi_slopes():
    h = jnp.arange(1, N_HEADS + 1, dtype=jnp.float32)
    return jnp.exp2(-8.0 * h / N_HEADS).reshape(N_KV_HEADS, GROUP)


def _sink_attend(q, k, v, dist, valid, sinks):
    s = jnp.einsum('...qhgd,...khd->...hgqk', q, k).astype(jnp.float32) * (HEAD_DIM ** -0.5)
    s = s - _alibi_slopes()[:, :, None, None] * dist[..., None, None, :, :]
    s = jnp.where(valid[..., None, None, :, :], s, NEG_INF)
    sink = sinks.astype(jnp.float32)[:, :, None, None]
    m = jnp.maximum(jnp.max(s, axis=-1, keepdims=True), sink)
    p = jnp.exp(s - m)
    p = p / (jnp.sum(p, axis=-1, keepdims=True) + jnp.exp(sink - m))
    return jnp.einsum('...hgqk,...khd->...qhgd', p.astype(v.dtype), v)


def _attend_prompt(q, k, v, sinks):
    b, s_len = q.shape[0], q.shape[1]
    nb = s_len // BLOCK
    qb = q.reshape(b, nb, BLOCK, N_KV_HEADS, GROUP, HEAD_DIM)
    kb = k.reshape(b, nb, BLOCK, N_KV_HEADS, HEAD_DIM)
    vb = v.reshape(b, nb, BLOCK, N_KV_HEADS, HEAD_DIM)

    def with_prev(t):
        prev = jnp.concatenate([jnp.zeros_like(t[:, :1]), t[:, :-1]], axis=1)
        return jnp.concatenate([prev, t], axis=2)

    kk, vv = with_prev(kb), with_prev(vb)
    i = jnp.arange(BLOCK)[:, None]
    j = jnp.arange(2 * BLOCK)[None, :]
    dist = BLOCK + i - j
    kpos = jnp.arange(nb)[:, None, None] * BLOCK - BLOCK + j[None]
    valid = (dist >= 0) & (dist <= WINDOW) & (kpos >= 0)
    o = _sink_attend(qb, kk, vv, dist.astype(jnp.float32), valid, sinks)
    return o.reshape(b, s_len, Q_WIDTH), k[:, -WINDOW:], v[:, -WINDOW:]


def _make_attend_sample(k_buf, v_buf):
    def attend(q, k, v, sinks):
        b, t = q.shape[0], q.shape[1]
        lb = k_buf.shape[1]
        kk = jnp.concatenate([k_buf.astype(k.dtype), k], axis=1)
        vv = jnp.concatenate([v_buf.astype(v.dtype), v], axis=1)
        i = jnp.arange(t)[:, None]
        j = jnp.arange(lb + t)[None, :]
        dist = lb + i - j
        valid = (dist >= 0) & (dist <= WINDOW)
        o = _sink_attend(q, kk, vv, dist.astype(jnp.float32), valid, sinks)
        return o.reshape(b, t, Q_WIDTH), kk[:, t:], vv[:, t:]
    return attend


def _layer(x, c, conv_buf, ffn_buf, attend, norm1_g, norm2_g, w_ada, b_ada, w_in, conv_w,
           w_conv_out, attn_sinks, w_attn_out, w_mix_out, w_up, ffn_conv_w, w_down):
    b, t = x.shape[0], x.shape[1]
    mod = jax.nn.silu(c) @ w_ada + b_ada
    sh1, sc1, g1, sh2, sc2, g2 = [m[:, None, :] for m in jnp.split(mod, 6, axis=-1)]

    h = _rmsnorm(x, norm1_g) * (1.0 + sc1) + sh1
    proj = h @ w_in
    cuts = np.cumsum([D_CONV, D_CONV, D_CONV, Q_WIDTH, KV_WIDTH, KV_WIDTH, D_MODEL]).tolist()
    b_gate, c_gate, xa, q, k, v, ga, gb = jnp.split(proj, cuts, axis=-1)
    u = c_gate * xa
    uc, new_conv = _causal_conv(u, conv_buf, conv_w)
    ya = (b_gate * uc) @ w_conv_out
    q = q.reshape(b, t, N_KV_HEADS, GROUP, HEAD_DIM)
    k = k.reshape(b, t, N_KV_HEADS, HEAD_DIM)
    v = v.reshape(b, t, N_KV_HEADS, HEAD_DIM)
    o, k_state, v_state = attend(q, k, v, attn_sinks.reshape(N_KV_HEADS, GROUP))
    yb = o @ w_attn_out
    mixed = jax.nn.sigmoid(ga) * ya + jax.nn.sigmoid(gb) * yb
    x = x + g1 * (mixed @ w_mix_out)

    h2 = _rmsnorm(x, norm2_g) * (1.0 + sc2) + sh2
    a, val = jnp.split(h2 @ w_up, 2, axis=-1)
    ac, new_ffn = _causal_conv(a, ffn_buf, ffn_conv_w)
    x = x + g2 * ((jax.nn.gelu(ac, approximate=False) * val) @ w_down)
    return x, new_conv, k_state, v_state, new_ffn


def setup_inputs(seed: int = 0) -> dict:
    key = jax.random.key(seed)
    ks = jax.random.split(key, 24)
    f32 = jnp.float32

    def nrm(k, shape, scale=1.0):
        return jax.random.normal(k, shape, f32) * scale

    buf_len = min(WINDOW, PAST_LEN)
    return {
        'x_prompt': nrm(ks[0], (BATCH, SEQ, D_MODEL)),
        'x_sample': nrm(ks[1], (DEC_BATCH, DEC_SEQ, D_MODEL)),
        'c_prompt': nrm(ks[2], (BATCH, D_MODEL)),
        'c_sample': nrm(ks[3], (DEC_BATCH, D_MODEL)),
        'state_conv': nrm(ks[4], (DEPTH, DEC_BATCH, CONV_WIDTH - 1, D_CONV)),
        'cache_k_win': nrm(ks[5], (DEPTH, DEC_BATCH, buf_len, N_KV_HEADS, HEAD_DIM)),
        'cache_v_win': nrm(ks[6], (DEPTH, DEC_BATCH, buf_len, N_KV_HEADS, HEAD_DIM)),
        'state_ffn_conv': nrm(ks[7], (DEPTH, DEC_BATCH, CONV_WIDTH - 1, D_FF)),
        'norm1_g': 1.0 + nrm(ks[8], (DEPTH, D_MODEL), 0.01),
        'norm2_g': 1.0 + nrm(ks[9], (DEPTH, D_MODEL), 0.01),
        'w_ada': nrm(ks[10], (DEPTH, D_MODEL, 6 * D_MODEL), 0.3 * D_MODEL ** -0.5),
        'b_ada': nrm(ks[11], (DEPTH, 6 * D_MODEL), 0.01),
        'w_in': nrm(ks[12], (DEPTH, D_MODEL, IN_WIDTH), D_MODEL ** -0.5),
        'conv_w': nrm(ks[13], (DEPTH, CONV_WIDTH, D_CONV), CONV_WIDTH ** -0.5),
        'w_conv_out': nrm(ks[14], (DEPTH, D_CONV, D_MODEL), D_CONV ** -0.5),
        'attn_sinks': nrm(ks[15], (DEPTH, N_HEADS)),
        'w_attn_out': nrm(ks[16], (DEPTH, Q_WIDTH, D_MODEL), Q_WIDTH ** -0.5),
        'w_mix_out': nrm(ks[17], (DEPTH, D_MODEL, D_MODEL), D_MODEL ** -0.5),
        'w_up': nrm(ks[18], (DEPTH, D_MODEL, 2 * D_FF), D_MODEL ** -0.5),
        'ffn_conv_w': nrm(ks[19], (DEPTH, CONV_WIDTH, D_FF), CONV_WIDTH ** -0.5),
        'w_down': nrm(ks[20], (DEPTH, D_FF, D_MODEL), D_FF ** -0.5),
        'final_g': 1.0 + nrm(ks[21], (D_MODEL,), 0.01),
    }


def reference(x_prompt, x_sample, c_prompt, c_sample, state_conv, cache_k_win, cache_v_win,
              state_ffn_conv, norm1_g, norm2_g, w_ada, b_ada, w_in, conv_w, w_conv_out,
              attn_sinks, w_attn_out, w_mix_out, w_up, ffn_conv_w, w_down, final_g):
    xp, xs = x_prompt, x_sample
    conv_p, k_p, v_p, ffn_p = [], [], [], []
    conv_s, k_s, v_s, ffn_s = [], [], [], []
    for layer in range(DEPTH):
        params = (norm1_g[layer], norm2_g[layer], w_ada[layer], b_ada[layer], w_in[layer],
                  conv_w[layer], w_conv_out[layer], attn_sinks[layer], w_attn_out[layer],
                  w_mix_out[layer], w_up[layer], ffn_conv_w[layer], w_down[layer])
        zero_conv = jnp.zeros((xp.shape[0], CONV_WIDTH - 1, D_CONV), xp.dtype)
        zero_ffn = jnp.zeros((xp.shape[0], CONV_WIDTH - 1, D_FF), xp.dtype)
        xp, cp_, kp_, vp_, fp_ = _layer(xp, c_prompt, zero_conv, zero_ffn, _attend_prompt, *params)
        attend_s = _make_attend_sample(cache_k_win[layer], cache_v_win[layer])
        xs, cs_, ks_, vs_, fs_ = _layer(xs, c_sample, state_conv[layer], state_ffn_conv[layer],
                                        attend_s, *params)
        conv_p.append(cp_); k_p.append(kp_); v_p.append(vp_); ffn_p.append(fp_)
        conv_s.append(cs_); k_s.append(ks_); v_s.append(vs_); ffn_s.append(fs_)
    y_prompt = _rmsnorm(xp, final_g)
    y_sample = _rmsnorm(xs, final_g)
    return (y_prompt, y_sample,
            jnp.stack(conv_p, axis=0), jnp.stack(k_p, axis=0), jnp.stack(v_p, axis=0),
            jnp.stack(ffn_p, axis=0),
            jnp.stack(conv_s, axis=0), jnp.stack(k_s, axis=0), jnp.stack(v_s, axis=0),
            jnp.stack(ffn_s, axis=0))
```

```python
import functools

import jax
import jax.numpy as jnp
from jax import lax
from jax.experimental import pallas as pl
from jax.experimental.pallas import tpu as pltpu

D_MODEL = 1024
HEAD_DIM = 64
N_HEADS = 16
N_KV_HEADS = 4
WINDOW = 128
D_FF = 2816
KV_WIDTH = N_KV_HEADS * HEAD_DIM
IN_WIDTH = 3 * D_MODEL + D_MODEL + 2 * KV_WIDTH + 2 * D_MODEL
RMS_EPS = 1e-6
NEG_INF = -1e30
LANES = 128
HEADS_PER_COL = LANES // HEAD_DIM
N_QCOLS = N_HEADS * HEAD_DIM // LANES
CARRY_ROWS = 8
VMEM_LIMIT = 56 * 1024 * 1024

_OFF_B, _OFF_C, _OFF_X = 0, D_MODEL, 2 * D_MODEL
_OFF_Q = 3 * D_MODEL
_OFF_K = _OFF_Q + D_MODEL
_OFF_V = _OFF_K + KV_WIDTH
_OFF_GA = _OFF_V + KV_WIDTH
_OFF_GB = _OFF_GA + D_MODEL

_BF = jnp.bfloat16
_F32 = jnp.float32
_NT = (((1,), (1,)), ((), ()))


def _slope(head):
    return 2.0 ** (-8.0 * (head + 1) / N_HEADS)


def _rms(x, g):
    return x * lax.rsqrt(jnp.mean(x * x, axis=-1, keepdims=True) + RMS_EPS) * g


def _dot(a, b):
    return jnp.dot(a, b, preferred_element_type=_F32)


def _gelu(x):
    return 0.5 * x * (1.0 + lax.erf(x * (0.5 ** 0.5)))


def _split_heads(t2d, lo):
    rolled = pltpu.roll(t2d, HEAD_DIM, 1)
    zero = jnp.zeros_like(t2d)
    return (jnp.where(lo, t2d, zero), jnp.where(lo, zero, rolled),
            jnp.where(lo, rolled, zero), jnp.where(lo, zero, t2d))


def _softmax_parts(s, sink):
    m = jnp.maximum(jnp.max(s, axis=-1, keepdims=True), sink)
    p = jnp.exp(s - m)
    denom = jnp.sum(p, axis=-1, keepdims=True) + jnp.exp(sink - m)
    return p, 1.0 / denom


def _mod_kernel(c_ref, w_ref, b_ref, o_ref):
    c = c_ref[...]
    a = (c * jax.nn.sigmoid(c)).astype(_BF)
    o_ref[...] = _dot(a, w_ref[...]) + b_ref[...]


def _mod_call(c_all, w_ada, b_ada):
    n = c_all.shape[0]
    return pl.pallas_call(
        _mod_kernel,
        grid=(6,),
        in_specs=[pl.BlockSpec((n, D_MODEL), lambda j: (0, 0)),
                  pl.BlockSpec((D_MODEL, D_MODEL), lambda j: (0, j)),
                  pl.BlockSpec((1, D_MODEL), lambda j: (0, j))],
        out_specs=pl.BlockSpec((n, D_MODEL), lambda j: (0, j)),
        out_shape=jax.ShapeDtypeStruct((n, 6 * D_MODEL), _F32),
        name="adaln_mod",
    )(c_all, w_ada, b_ada)


def _mix_prompt_kernel(x_ref, sh_ref, sc_ref, g_ref, n1g_ref, win_ref, cw_ref, wco_ref,
                       sinks_ref, wao_ref, wmo_ref,
                       x1_ref, conv_ref, k_ref, v_ref,
                       u_ext, k_ext, v_ext, o_scr, bias_scr):
    b = pl.program_id(0)
    i = pl.program_id(1)
    t = x_ref.shape[0]
    nblk = t // WINDOW

    @pl.when((b == 0) & (i == 0))
    def _():
        row = lax.broadcasted_iota(jnp.int32, (WINDOW, 2 * WINDOW), 0)
        col = lax.broadcasted_iota(jnp.int32, (WINDOW, 2 * WINDOW), 1)
        dist = WINDOW + row - col
        valid = (dist >= 0) & (dist <= WINDOW)
        distf = dist.astype(_F32)
        for head in range(N_HEADS):
            bias_scr[head] = jnp.where(valid, -(_slope(head) * distf), NEG_INF)

    @pl.when(i == 0)
    def _():
        u_ext[0:CARRY_ROWS, :] = jnp.zeros((CARRY_ROWS, D_MODEL), _F32)
        k_ext[:, 0:WINDOW, :] = jnp.zeros((2 * N_KV_HEADS, WINDOW, LANES), _BF)
        v_ext[:, 0:WINDOW, :] = jnp.zeros((2 * N_KV_HEADS, WINDOW, LANES), _BF)

    x = x_ref[...]
    h = (_rms(x, n1g_ref[...]) * (1.0 + sc_ref[...]) + sh_ref[...]).astype(_BF)

    def proj(lo, width):
        return _dot(h, win_ref[:, lo:lo + width])

    pb = proj(_OFF_B, D_MODEL)
    u = proj(_OFF_C, D_MODEL) * proj(_OFF_X, D_MODEL)
    u_ext[CARRY_ROWS:CARRY_ROWS + t, :] = u
    cw = cw_ref[...]
    uc = (u_ext[CARRY_ROWS - 2:CARRY_ROWS - 2 + t, :] * cw[0:1, :]
          + u_ext[CARRY_ROWS - 1:CARRY_ROWS - 1 + t, :] * cw[1:2, :]
          + u * cw[2:3, :])
    conv_ref[...] = u[t - 2:t, :]
    u_ext[0:CARRY_ROWS, :] = u[t - CARRY_ROWS:t, :]
    ya = _dot((pb * uc).astype(_BF), wco_ref[...])

    q = (proj(_OFF_Q, D_MODEL) * (HEAD_DIM ** -0.5)).astype(_BF)
    kf = proj(_OFF_K, KV_WIDTH)
    vf = proj(_OFF_V, KV_WIDTH)
    k_ref[...] = kf[t - WINDOW:t, :]
    v_ref[...] = vf[t - WINDOW:t, :]
    lo = lax.broadcasted_iota(jnp.int32, (t, LANES), 1) < HEAD_DIM
    for kc in range(KV_WIDTH // LANES):
        ks = _split_heads(kf[:, kc * LANES:(kc + 1) * LANES], lo)
        vs = _split_heads(vf[:, kc * LANES:(kc + 1) * LANES], lo)
        for j in range(4):
            k_ext[4 * kc + j, WINDOW:WINDOW + t, :] = ks[j].astype(_BF)
            v_ext[4 * kc + j, WINDOW:WINDOW + t, :] = vs[j].astype(_BF)

    first_cols = lax.broadcasted_iota(jnp.int32, (WINDOW, 2 * WINDOW), 1) < WINDOW
    no_prev = first_cols & (i == 0)
    for blk in range(nblk):
        r0 = blk * WINDOW
        for c in range(N_QCOLS):
            hkv = c // 2
            qc = q[r0:r0 + WINDOW, c * LANES:(c + 1) * LANES]
            o_c = None
            for par in range(HEADS_PER_COL):
                head = HEADS_PER_COL * c + par
                kk = k_ext[2 * hkv + par, r0:r0 + 2 * WINDOW, :]
                s = lax.dot_general(qc, kk, _NT, preferred_element_type=_F32) + bias_scr[head]
                if blk == 0:
                    s = jnp.where(no_prev, NEG_INF, s)
                p, inv = _softmax_parts(s, sinks_ref[head])
                vv = v_ext[2 * hkv + par, r0:r0 + 2 * WINDOW, :]
                o_h = _dot(p.astype(_BF), vv) * inv
                o_c = o_h if o_c is None else o_c + o_h
            o_scr[r0:r0 + WINDOW, c * LANES:(c + 1) * LANES] = o_c.astype(_BF)
    k_ext[:, 0:WINDOW, :] = k_ext[:, t:t + WINDOW, :]
    v_ext[:, 0:WINDOW, :] = v_ext[:, t:t + WINDOW, :]
    yb = _dot(o_scr[...], wao_ref[...])

    mixed = (jax.nn.sigmoid(proj(_OFF_GA, D_MODEL)) * ya
             + jax.nn.sigmoid(proj(_OFF_GB, D_MODEL)) * yb).astype(_BF)
    x1_ref[...] = x + g_ref[...] * _dot(mixed, wmo_ref[...])


def _const_spec(shape):
    nd = len(shape)
    return pl.BlockSpec(shape, lambda *_: (0,) * nd, pipeline_mode=pl.Buffered(1))


def _mix_prompt_call(x, mod3, mod_row0, n1g, win, cw, wco, sinks, wao, wmo, tile):
    bsz, seq, _ = x.shape

    def mod_spec(chunk):
        return pl.BlockSpec((None, 1, D_MODEL), lambda b, i: (mod_row0 + b, 0, chunk))

    return pl.pallas_call(
        _mix_prompt_kernel,
        grid=(bsz, seq // tile),
        in_specs=[pl.BlockSpec((None, tile, D_MODEL), lambda b, i: (b, i, 0)),
                  mod_spec(0), mod_spec(1), mod_spec(2),
                  _const_spec((1, D_MODEL)),
                  _const_spec((D_MODEL, IN_WIDTH)),
                  _const_spec((3, D_MODEL)),
                  _const_spec((D_MODEL, D_MODEL)),
                  pl.BlockSpec(memory_space=pltpu.SMEM),
                  _const_spec((D_MODEL, D_MODEL)),
                  _const_spec((D_MODEL, D_MODEL))],
        out_specs=[pl.BlockSpec((None, tile, D_MODEL), lambda b, i: (b, i, 0)),
                   pl.BlockSpec((None, 2, D_MODEL), lambda b, i: (b, 0, 0)),
                   pl.BlockSpec((None, WINDOW, KV_WIDTH), lambda b, i: (b, 0, 0)),
                   pl.BlockSpec((None, WINDOW, KV_WIDTH), lambda b, i: (b, 0, 0))],
        out_shape=[jax.ShapeDtypeStruct((bsz, seq, D_MODEL), _F32),
                   jax.ShapeDtypeStruct((bsz, 2, D_MODEL), _F32),
                   jax.ShapeDtypeStruct((bsz, WINDOW, KV_WIDTH), _F32),
                   jax.ShapeDtypeStruct((bsz, WINDOW, KV_WIDTH), _F32)],
        scratch_shapes=[pltpu.VMEM((CARRY_ROWS + tile, D_MODEL), _F32),
                        pltpu.VMEM((2 * N_KV_HEADS, WINDOW + tile, LANES), _BF),
                        pltpu.VMEM((2 * N_KV_HEADS, WINDOW + tile, LANES), _BF),
                        pltpu.VMEM((tile, D_MODEL), _BF),
                        pltpu.VMEM((N_HEADS, WINDOW, 2 * WINDOW), _F32)],
        compiler_params=pltpu.CompilerParams(
            dimension_semantics=("arbitrary", "arbitrary"),
            vmem_limit_bytes=VMEM_LIMIT),
        name="mix_prompt",
    )(x, mod3, mod3, mod3, n1g, win, cw, wco, sinks, wao, wmo)


def _conv_seq(u2d, st, cw, nseq, tlen):
    width = u2d.shape[-1]
    shape3 = (nseq, tlen, width)
    u3 = u2d.reshape(shape3)
    r1 = pltpu.roll(u2d, 1, 0).reshape(shape3)
    r2 = pltpu.roll(u2d, 2, 0).reshape(shape3)
    tpos = lax.broadcasted_iota(jnp.int32, shape3, 1)
    st0 = jnp.broadcast_to(st[:, 0:1, :], shape3)
    st1 = jnp.broadcast_to(st[:, 1:2, :], shape3)
    um1 = jnp.where(tpos == 0, st1, r1)
    um2 = jnp.where(tpos == 0, st0, jnp.where(tpos == 1, st1, r2))
    y = um2 * cw[0:1, :] + um1 * cw[1:2, :] + u3 * cw[2:3, :]
    return y.reshape(nseq * tlen, width), u3[:, tlen - 2:tlen, :]


def _mix_sample_kernel(x_ref, sh_ref, sc_ref, g_ref, n1g_ref, win_ref, cw_ref, wco_ref,
                       sinks_ref, wao_ref, wmo_ref, st_ref, ck_ref, cv_ref,
                       x1_ref, conv_ref, k_ref, v_ref):
    nseq, tlen, _ = x_ref.shape
    rows = nseq * tlen
    nkeys = WINDOW + tlen

    x3 = x_ref[...]
    h3 = _rms(x3, n1g_ref[...]) * (1.0 + sc_ref[...]) + sh_ref[...]
    h = h3.reshape(rows, D_MODEL).astype(_BF)

    def proj(lo, width):
        return _dot(h, win_ref[:, lo:lo + width])

    pb = proj(_OFF_B, D_MODEL)
    u = proj(_OFF_C, D_MODEL) * proj(_OFF_X, D_MODEL)
    uc, new_st = _conv_seq(u, st_ref[...], cw_ref[...], nseq, tlen)
    conv_ref[...] = new_st
    ya = _dot((pb * uc).astype(_BF), wco_ref[...])

    qf = proj(_OFF_Q, D_MODEL) * (HEAD_DIM ** -0.5)
    kn = proj(_OFF_K, KV_WIDTH).reshape(nseq, tlen, KV_WIDTH)
    vn = proj(_OFF_V, KV_WIDTH).reshape(nseq, tlen, KV_WIDTH)
    ck = ck_ref[...]
    cv = cv_ref[...]
    k_ref[:, 0:WINDOW - tlen, :] = ck[:, tlen:WINDOW, :]
    k_ref[:, WINDOW - tlen:WINDOW, :] = kn
    v_ref[:, 0:WINDOW - tlen, :] = cv[:, tlen:WINDOW, :]
    v_ref[:, WINDOW - tlen:WINDOW, :] = vn
    kall = jnp.concatenate([ck, kn], axis=1)
    vall = jnp.concatenate([cv, vn], axis=1)

    lo = lax.broadcasted_iota(jnp.int32, (nseq * nkeys, LANES), 1) < HEAD_DIM
    row = lax.broadcasted_iota(jnp.int32, (2 * tlen, nkeys), 0)
    col = lax.broadcasted_iota(jnp.int32, (2 * tlen, nkeys), 1)
    upper = row >= tlen
    dist = WINDOW + jnp.where(upper, row - tlen, row) - col
    valid = (dist >= 0) & (dist <= WINDOW)
    distf = dist.astype(_F32)

    o_cols = [None] * N_QCOLS
    for kc in range(KV_WIDTH // LANES):
        ks = _split_heads(kall[:, :, kc * LANES:(kc + 1) * LANES].reshape(nseq * nkeys, LANES), lo)
        vs = _split_heads(vall[:, :, kc * LANES:(kc + 1) * LANES].reshape(nseq * nkeys, LANES), lo)
        for j in range(2):
            hkv = 2 * kc + j
            c0 = 2 * hkv
            qs = jnp.concatenate(
                [qf[:, c0 * LANES:(c0 + 1) * LANES].reshape(nseq, tlen, LANES),
                 qf[:, (c0 + 1) * LANES:(c0 + 2) * LANES].reshape(nseq, tlen, LANES)],
                axis=1).astype(_BF)
            o_pair = None
            for par in range(HEADS_PER_COL):
                head_a = HEADS_PER_COL * c0 + par
                head_b = HEADS_PER_COL * (c0 + 1) + par
                kk = ks[2 * j + par].astype(_BF).reshape(nseq, nkeys, LANES)
                vv = vs[2 * j + par].astype(_BF).reshape(nseq, nkeys, LANES)
                slope = jnp.where(upper, _slope(head_b), _slope(head_a))
                bias = jnp.where(valid, -(slope * distf), NEG_INF)
                sink = jnp.where(upper[:, 0:1], sinks_ref[head_b], sinks_ref[head_a])
                s = jnp.einsum('smd,sjd->smj', qs, kk, preferred_element_type=_F32) + bias
                p, inv = _softmax_parts(s, sink)
                o_h = jnp.einsum('smj,sjd->smd', p.astype(_BF), vv,
                                 preferred_element_type=_F32) * inv
                o_pair = o_h if o_pair is None else o_pair + o_h
            o_cols[c0] = o_pair[:, 0:tlen, :].reshape(rows, LANES)
            o_cols[c0 + 1] = o_pair[:, tlen:2 * tlen, :].reshape(rows, LANES)
    o = jnp.concatenate(o_cols, axis=1).astype(_BF)
    yb = _dot(o, wao_ref[...])

    mixed = (jax.nn.sigmoid(proj(_OFF_GA, D_MODEL)) * ya
             + jax.nn.sigmoid(proj(_OFF_GB, D_MODEL)) * yb).astype(_BF)
    y = _dot(mixed, wmo_ref[...]).reshape(nseq, tlen, D_MODEL)
    x1_ref[...] = x3 + g_ref[...] * y


def _mix_sample_call(x, mod3, n1g, win, cw, wco, sinks, wao, wmo, st, ck, cv, chunk):
    nseq, tlen, _ = x.shape

    def mod_spec(col):
        return pl.BlockSpec((chunk, 1, D_MODEL), lambda s: (s, 0, col))

    def seq_spec(d1, d2):
        return pl.BlockSpec((chunk, d1, d2), lambda s: (s, 0, 0))

    return pl.pallas_call(
        _mix_sample_kernel,
        grid=(nseq // chunk,),
        in_specs=[seq_spec(tlen, D_MODEL),
                  mod_spec(0), mod_spec(1), mod_spec(2),
                  _const_spec((1, D_MODEL)),
                  _const_spec((D_MODEL, IN_WIDTH)),
                  _const_spec((3, D_MODEL)),
                  _const_spec((D_MODEL, D_MODEL)),
                  pl.BlockSpec(memory_space=pltpu.SMEM),
                  _const_spec((D_MODEL, D_MODEL)),
                  _const_spec((D_MODEL, D_MODEL)),
                  seq_spec(2, D_MODEL),
                  seq_spec(WINDOW, KV_WIDTH),
                  seq_spec(WINDOW, KV_WIDTH)],
        out_specs=[seq_spec(tlen, D_MODEL),
                   seq_spec(2, D_MODEL),
                   seq_spec(WINDOW, KV_WIDTH),
                   seq_spec(WINDOW, KV_WIDTH)],
        out_shape=[jax.ShapeDtypeStruct((nseq, tlen, D_MODEL), _F32),
                   jax.ShapeDtypeStruct((nseq, 2, D_MODEL), _F32),
                   jax.ShapeDtypeStruct((nseq, WINDOW, KV_WIDTH), _F32),
                   jax.ShapeDtypeStruct((nseq, WINDOW, KV_WIDTH), _F32)],
        compiler_params=pltpu.CompilerParams(
            dimension_semantics=("arbitrary",),
            vmem_limit_bytes=VMEM_LIMIT),
        name="mix_sample",
    )(x, mod3, mod3, mod3, n1g, win, cw, wco, sinks, wao, wmo, st, ck, cv)


def _ffn_tail(x1, ac, val, g, wdown_ref, fg):
    hmid = (_gelu(ac) * val).astype(_BF)
    x2 = x1 + g * _dot(hmid, wdown_ref[...])
    return _rms(x2, fg)


def _ffn_prompt_kernel(x1_ref, sh_ref, sc_ref, g_ref, n2g_ref, wup_ref, fcw_ref, wdown_ref,
                       fg_ref, y_ref, ffn_ref, a_ext):
    i = pl.program_id(1)
    t = x1_ref.shape[0]

    @pl.when(i == 0)
    def _():
        a_ext[0:CARRY_ROWS, :] = jnp.zeros((CARRY_ROWS, D_FF), _F32)

    x1 = x1_ref[...]
    h2 = (_rms(x1, n2g_ref[...]) * (1.0 + sc_ref[...]) + sh_ref[...]).astype(_BF)
    a = _dot(h2, wup_ref[:, 0:D_FF])
    val = _dot(h2, wup_ref[:, D_FF:2 * D_FF])
    a_ext[CARRY_ROWS:CARRY_ROWS + t, :] = a
    fcw = fcw_ref[...]
    ac = (a_ext[CARRY_ROWS - 2:CARRY_ROWS - 2 + t, :] * fcw[0:1, :]
          + a_ext[CARRY_ROWS - 1:CARRY_ROWS - 1 + t, :] * fcw[1:2, :]
          + a * fcw[2:3, :])
    ffn_ref[...] = a[t - 2:t, :]
    a_ext[0:CARRY_ROWS, :] = a[t - CARRY_ROWS:t, :]
    y_ref[...] = _ffn_tail(x1, ac, val, g_ref[...], wdown_ref, fg_ref[...])


def _ffn_prompt_call(x1, mod3, mod_row0, n2g, wup, fcw, wdown, fg, tile):
    bsz, seq, _ = x1.shape

    def mod_spec(chunk):
        return pl.BlockSpec((None, 1, D_MODEL), lambda b, i: (mod_row0 + b, 0, chunk))

    return pl.pallas_call(
        _ffn_prompt_kernel,
        grid=(bsz, seq // tile),
        in_specs=[pl.BlockSpec((None, tile, D_MODEL), lambda b, i: (b, i, 0)),
                  mod_spec(3), mod_spec(4), mod_spec(5),
                  _const_spec((1, D_MODEL)),
                  _const_spec((D_MODEL, 2 * D_FF)),
                  _const_spec((3, D_FF)),
                  _const_spec((D_FF, D_MODEL)),
                  _const_spec((1, D_MODEL))],
        out_specs=[pl.BlockSpec((None, tile, D_MODEL), lambda b, i: (b, i, 0)),
                   pl.BlockSpec((None, 2, D_FF), lambda b, i: (b, 0, 0))],
        out_shape=[jax.ShapeDtypeStruct((bsz, seq, D_MODEL), _F32),
                   jax.ShapeDtypeStruct((bsz, 2, D_FF), _F32)],
        scratch_shapes=[pltpu.VMEM((CARRY_ROWS + tile, D_FF), _F32)],
        compiler_params=pltpu.CompilerParams(
            dimension_semantics=("arbitrary", "arbitrary"),
            vmem_limit_bytes=VMEM_LIMIT),
        name="ffn_prompt",
    )(x1, mod3, mod3, mod3, n2g, wup, fcw, wdown, fg)


def _ffn_sample_kernel(x1_ref, sh_ref, sc_ref, g_ref, n2g_ref, wup_ref, fcw_ref, wdown_ref,
                       fg_ref, st_ref, y_ref, ffn_ref):
    nseq, tlen, _ = x1_ref.shape
    rows = nseq * tlen
    x3 = x1_ref[...]
    h3 = _rms(x3, n2g_ref[...]) * (1.0 + sc_ref[...]) + sh_ref[...]
    h2 = h3.reshape(rows, D_MODEL).astype(_BF)
    a = _dot(h2, wup_ref[:, 0:D_FF])
    val = _dot(h2, wup_ref[:, D_FF:2 * D_FF])
    ac, new_st = _conv_seq(a, st_ref[...], fcw_ref[...], nseq, tlen)
    ffn_ref[...] = new_st
    hmid = (_gelu(ac) * val).astype(_BF)
    y = _dot(hmid, wdown_ref[...]).reshape(nseq, tlen, D_MODEL)
    x2 = x3 + g_ref[...] * y
    y_ref[...] = _rms(x2, fg_ref[...])


def _ffn_sample_call(x1, mod3, n2g, wup, fcw, wdown, fg, st, chunk):
    nseq, tlen, _ = x1.shape

    def mod_spec(col):
        return pl.BlockSpec((chunk, 1, D_MODEL), lambda s: (s, 0, col))

    def seq_spec(d1, d2):
        return pl.BlockSpec((chunk, d1, d2), lambda s: (s, 0, 0))

    return pl.pallas_call(
        _ffn_sample_kernel,
        grid=(nseq // chunk,),
        in_specs=[seq_spec(tlen, D_MODEL),
                  mod_spec(3), mod_spec(4), mod_spec(5),
                  _const_spec((1, D_MODEL)),
                  _const_spec((D_MODEL, 2 * D_FF)),
                  _const_spec((3, D_FF)),
                  _const_spec((D_FF, D_MODEL)),
                  _const_spec((1, D_MODEL)),
                  seq_spec(2, D_FF)],
        out_specs=[seq_spec(tlen, D_MODEL), seq_spec(2, D_FF)],
        out_shape=[jax.ShapeDtypeStruct((nseq, tlen, D_MODEL), _F32),
                   jax.ShapeDtypeStruct((nseq, 2, D_FF), _F32)],
        compiler_params=pltpu.CompilerParams(
            dimension_semantics=("arbitrary",),
            vmem_limit_bytes=VMEM_LIMIT),
        name="ffn_sample",
    )(x1, mod3, mod3, mod3, n2g, wup, fcw, wdown, fg, st)


PROMPT_TILE = 256
SAMPLE_CHUNK = 16


def kernel(x_prompt, x_sample, c_prompt, c_sample, state_conv, cache_k_win, cache_v_win,
           state_ffn_conv, norm1_g, norm2_g, w_ada, b_ada, w_in, conv_w, w_conv_out,
           attn_sinks, w_attn_out, w_mix_out, w_up, ffn_conv_w, w_down, final_g):
    depth = w_in.shape[0]
    assert depth == 1, "the final norm is fused into the (single) layer's FFN call"
    nsamp = x_sample.shape[0]
    bsz = x_prompt.shape[0]
    xp, xs = x_prompt, x_sample
    c_all = jnp.concatenate([c_sample, c_prompt], axis=0)
    fg = final_g.reshape(1, D_MODEL)
    outs = [[] for _ in range(8)]
    for layer in range(depth):
        mod = _mod_call(c_all, w_ada[layer].astype(_BF), b_ada[layer].reshape(1, -1))
        mod3 = mod.reshape(nsamp + bsz, 1, 6 * D_MODEL)
        n1g = norm1_g[layer].reshape(1, D_MODEL)
        n2g = norm2_g[layer].reshape(1, D_MODEL)
        win = w_in[layer].astype(_BF)
        wco = w_conv_out[layer].astype(_BF)
        wao = w_attn_out[layer].astype(_BF)
        wmo = w_mix_out[layer].astype(_BF)
        wup = w_up[layer].astype(_BF)
        wdown = w_down[layer].astype(_BF)
        cw = conv_w[layer]
        fcw = ffn_conv_w[layer]
        sinks = attn_sinks[layer]
        fg_l = fg
        ck =cache_k_win[layer].reshape(nsamp, WINDOW, KV_WIDTH)
        cv = cache_v_win[layer].reshape(nsamp, WINDOW, KV_WIDTH)

        x1p, conv_p, k_p, v_p = _mix_prompt_call(
            xp, mod3, nsamp, n1g, win, cw, wco, sinks, wao, wmo, PROMPT_TILE)
        x1s, conv_s, k_s, v_s = _mix_sample_call(
            xs, mod3, n1g, win, cw, wco, sinks, wao, wmo, state_conv[layer], ck, cv,
            SAMPLE_CHUNK)
        xp, ffn_p = _ffn_prompt_call(x1p, mod3, nsamp, n2g, wup, fcw, wdown, fg_l, PROMPT_TILE)
        xs, ffn_s = _ffn_sample_call(x1s, mod3, n2g, wup, fcw, wdown, fg_l,
                                     state_ffn_conv[layer], SAMPLE_CHUNK)
        kv_shape_p = (bsz, WINDOW, N_KV_HEADS, HEAD_DIM)
        kv_shape_s = (nsamp, WINDOW, N_KV_HEADS, HEAD_DIM)
        for lst, val in zip(outs, (conv_p, k_p.reshape(kv_shape_p), v_p.reshape(kv_shape_p),
                                   ffn_p, conv_s, k_s.reshape(kv_shape_s),
                                   v_s.reshape(kv_shape_s), ffn_s)):
            lst.append(val)
    return (xp, xs) + tuple(jnp.stack(lst, axis=0) for lst in outs)
```

```python
import functools

import jax
import jax.numpy as jnp
from jax import lax
from jax.experimental import pallas as pl
from jax.experimental.pallas import tpu as pltpu

D_MODEL = 1024
HEAD_DIM = 64
N_HEADS = 16
N_KV_HEADS = 4
WINDOW = 128
D_FF = 2816
KV_WIDTH = N_KV_HEADS * HEAD_DIM
IN_WIDTH = 3 * D_MODEL + D_MODEL + 2 * KV_WIDTH + 2 * D_MODEL
RMS_EPS = 1e-6
NEG_INF = -1e30
LANES = 128
HEADS_PER_COL = LANES // HEAD_DIM
N_QCOLS = N_HEADS * HEAD_DIM // LANES
CARRY_ROWS = 8
VMEM_LIMIT = 56 * 1024 * 1024

_OFF_B, _OFF_C, _OFF_X = 0, D_MODEL, 2 * D_MODEL
_OFF_Q = 3 * D_MODEL
_OFF_K = _OFF_Q + D_MODEL
_OFF_V = _OFF_K + KV_WIDTH
_OFF_GA = _OFF_V + KV_WIDTH
_OFF_GB = _OFF_GA + D_MODEL

_BF = jnp.bfloat16
_F32 = jnp.float32
_NT = (((1,), (1,)), ((), ()))


LOG2E = 1.4426950408889634
Q_SCALE = HEAD_DIM ** -0.5 * LOG2E


def _slope(head):
    return 2.0 ** (-8.0 * (head + 1) / N_HEADS) * LOG2E


def _rms(x, g):
    return x * lax.rsqrt(jnp.mean(x * x, axis=-1, keepdims=True) + RMS_EPS) * g


def _dot(a, b):
    return jnp.dot(a, b, preferred_element_type=_F32)


def _gelu(x):
    return 0.5 * x * (1.0 + lax.erf(x * (0.5 ** 0.5)))


def _split_heads(t2d, lo):
    rolled = pltpu.roll(t2d, HEAD_DIM, 1)
    zero = jnp.zeros_like(t2d)
    return (jnp.where(lo, t2d, zero), jnp.where(lo, zero, rolled),
            jnp.where(lo, rolled, zero), jnp.where(lo, zero, t2d))


def _softmax_parts(s, sink):
    m = jnp.maximum(jnp.max(s, axis=-1, keepdims=True), sink)
    p = jnp.exp2(s - m)
    denom = jnp.sum(p, axis=-1, keepdims=True) + jnp.exp2(sink - m)
    return p, 1.0 / denom


def _mod_kernel(c_ref, w_ref, b_ref, o_ref):
    c = c_ref[...]
    a = (c * jax.nn.sigmoid(c)).astype(_BF)
    o_ref[...] = _dot(a, w_ref[...]) + b_ref[...]


def _mod_call(c_all, w_ada, b_ada):
    n = c_all.shape[0]
    return pl.pallas_call(
        _mod_kernel,
        grid=(6,),
        in_specs=[pl.BlockSpec((n, D_MODEL), lambda j: (0, 0)),
                  pl.BlockSpec((D_MODEL, D_MODEL), lambda j: (0, j)),
                  pl.BlockSpec((1, D_MODEL), lambda j: (0, j))],
        out_specs=pl.BlockSpec((n, D_MODEL), lambda j: (0, j)),
        out_shape=jax.ShapeDtypeStruct((n, 6 * D_MODEL), _F32),
        name="adaln_mod",
    )(c_all, w_ada, b_ada)


def _mix_prompt_kernel(x_ref, sh_ref, sc_ref, g_ref, n1g_ref, win_ref, cw_ref, wco_ref,
                       sinks_ref, wao_ref, wmo_ref,
                       x1_ref, conv_ref, k_ref, v_ref,
                       u_ext, k_ext, v_ext, o_scr, bias_scr):
    b = pl.program_id(0)
    i = pl.program_id(1)
    t = x_ref.shape[0]
    nblk = t // WINDOW

    @pl.when((b == 0) & (i == 0))
    def _():
        row = lax.broadcasted_iota(jnp.int32, (WINDOW, 2 * WINDOW), 0)
        col = lax.broadcasted_iota(jnp.int32, (WINDOW, 2 * WINDOW), 1)
        dist = WINDOW + row - col
        valid = (dist >= 0) & (dist <= WINDOW)
        distf = dist.astype(_F32)
        for head in range(N_HEADS):
            bias_scr[head] = jnp.where(valid, -(_slope(head) * distf), NEG_INF)

    @pl.when(i == 0)
    def _():
        u_ext[0:CARRY_ROWS, :] = jnp.zeros((CARRY_ROWS, D_MODEL), _F32)
        k_ext[:, 0:WINDOW, :] = jnp.zeros((2 * N_KV_HEADS, WINDOW, LANES), _BF)
        v_ext[:, 0:WINDOW, :] = jnp.zeros((2 * N_KV_HEADS, WINDOW, LANES), _BF)

    x = x_ref[...]
    h = (_rms(x, n1g_ref[...]) * (1.0 + sc_ref[...]) + sh_ref[...]).astype(_BF)

    def proj(lo, width):
        return _dot(h, win_ref[:, lo:lo + width])

    pb = proj(_OFF_B, D_MODEL)
    u = proj(_OFF_C, D_MODEL) * proj(_OFF_X, D_MODEL)
    u_ext[CARRY_ROWS:CARRY_ROWS + t, :] = u
    cw = cw_ref[...]
    uc = (u_ext[CARRY_ROWS - 2:CARRY_ROWS - 2 + t, :] * cw[0:1, :]
          + u_ext[CARRY_ROWS - 1:CARRY_ROWS - 1 + t, :] * cw[1:2, :]
          + u * cw[2:3, :])
    conv_ref[...] = u[t - 2:t, :]
    u_ext[0:CARRY_ROWS, :] = u[t - CARRY_ROWS:t, :]
    ya = _dot((pb * uc).astype(_BF), wco_ref[...])

    q = (proj(_OFF_Q, D_MODEL) * Q_SCALE).astype(_BF)
    kf = proj(_OFF_K, KV_WIDTH)
    vf = proj(_OFF_V, KV_WIDTH)
    k_ref[...] = kf[t - WINDOW:t, :].T
    v_ref[...] = vf[t - WINDOW:t, :].T
    lo = lax.broadcasted_iota(jnp.int32, (t, LANES), 1) < HEAD_DIM
    for kc in range(KV_WIDTH // LANES):
        ks = _split_heads(kf[:, kc * LANES:(kc + 1) * LANES], lo)
        vs = _split_heads(vf[:, kc * LANES:(kc + 1) * LANES], lo)
        for j in range(4):
            k_ext[4 * kc + j, WINDOW:WINDOW + t, :] = ks[j].astype(_BF)
            v_ext[4 * kc + j, WINDOW:WINDOW + t, :] = vs[j].astype(_BF)

    first_cols = lax.broadcasted_iota(jnp.int32, (WINDOW, 2 * WINDOW), 1) < WINDOW
    no_prev = first_cols & (i == 0)
    for blk in range(nblk):
        r0 = blk * WINDOW
        for c in range(N_QCOLS):
            hkv = c // 2
            qc = q[r0:r0 + WINDOW, c * LANES:(c + 1) * LANES]
            o_c = None
            for par in range(HEADS_PER_COL):
                head = HEADS_PER_COL * c + par
                kk = k_ext[2 * hkv + par, r0:r0 + 2 * WINDOW, :]
                s = lax.dot_general(qc, kk, _NT, preferred_element_type=_F32) + bias_scr[head]
                if blk == 0:
                    s = jnp.where(no_prev, NEG_INF, s)
                p, inv = _softmax_parts(s, sinks_ref[head] * LOG2E)
                vv = v_ext[2 * hkv + par, r0:r0 + 2 * WINDOW, :]
                o_h = _dot(p.astype(_BF), vv) * inv
                o_c = o_h if o_c is None else o_c + o_h
            o_scr[r0:r0 + WINDOW, c * LANES:(c + 1) * LANES] = o_c.astype(_BF)
    k_ext[:, 0:WINDOW, :] = k_ext[:, t:t + WINDOW, :]
    v_ext[:, 0:WINDOW, :] = v_ext[:, t:t + WINDOW, :]
    yb = _dot(o_scr[...], wao_ref[...])

    mixed = (jax.nn.sigmoid(proj(_OFF_GA, D_MODEL)) * ya
             + jax.nn.sigmoid(proj(_OFF_GB, D_MODEL)) * yb).astype(_BF)
    x1_ref[...] = x + g_ref[...] * _dot(mixed, wmo_ref[...])


def _const_spec(shape):
    nd = len(shape)
    return pl.BlockSpec(shape, lambda *_: (0,) * nd, pipeline_mode=pl.Buffered(1))


def _mix_prompt_call(x, mod3, mod_row0, n1g, win, cw, wco, sinks, wao, wmo, tile):
    bsz, seq, _ = x.shape

    def mod_spec(chunk):
        return pl.BlockSpec((None, 1, D_MODEL), lambda b, i: (mod_row0 + b, 0, chunk))

    return pl.pallas_call(
        _mix_prompt_kernel,
        grid=(bsz, seq // tile),
        in_specs=[pl.BlockSpec((None, tile, D_MODEL), lambda b, i: (b, i, 0)),
                  mod_spec(0), mod_spec(1), mod_spec(2),
                  _const_spec((1, D_MODEL)),
                  _const_spec((D_MODEL, IN_WIDTH)),
                  _const_spec((3, D_MODEL)),
                  _const_spec((D_MODEL, D_MODEL)),
                  pl.BlockSpec(memory_space=pltpu.SMEM),
                  _const_spec((D_MODEL, D_MODEL)),
                  _const_spec((D_MODEL, D_MODEL))],
        out_specs=[pl.BlockSpec((None, tile, D_MODEL), lambda b, i: (b, i, 0)),
                   pl.BlockSpec((None, 2, D_MODEL), lambda b, i: (b, 0, 0)),
                   pl.BlockSpec((None, KV_WIDTH, WINDOW), lambda b, i: (b, 0, 0)),
                   pl.BlockSpec((None, KV_WIDTH, WINDOW), lambda b, i: (b, 0, 0))],
        out_shape=[jax.ShapeDtypeStruct((bsz, seq, D_MODEL), _F32),
                   jax.ShapeDtypeStruct((bsz, 2, D_MODEL), _F32),
                   jax.ShapeDtypeStruct((bsz, KV_WIDTH, WINDOW), _F32),
                   jax.ShapeDtypeStruct((bsz, KV_WIDTH, WINDOW), _F32)],
        scratch_shapes=[pltpu.VMEM((CARRY_ROWS + tile, D_MODEL), _F32),
                        pltpu.VMEM((2 * N_KV_HEADS, WINDOW + tile, LANES), _BF),
                        pltpu.VMEM((2 * N_KV_HEADS, WINDOW + tile, LANES), _BF),
                        pltpu.VMEM((tile, D_MODEL), _BF),
                        pltpu.VMEM((N_HEADS, WINDOW, 2 * WINDOW), _F32)],
        compiler_params=pltpu.CompilerParams(
            dimension_semantics=("arbitrary", "arbitrary"),
            vmem_limit_bytes=VMEM_LIMIT),
        name="mix_prompt",
    )(x, mod3, mod3, mod3, n1g, win, cw, wco, sinks, wao, wmo)


def _conv_seq(u2d, st, cw, nseq, tlen):
    width = u2d.shape[-1]
    shape3 = (nseq, tlen, width)
    u3 = u2d.reshape(shape3)
    r1 = pltpu.roll(u2d, 1, 0).reshape(shape3)
    r2 = pltpu.roll(u2d, 2, 0).reshape(shape3)
    tpos = lax.broadcasted_iota(jnp.int32, shape3, 1)
    st0 = jnp.broadcast_to(st[:, 0:1, :], shape3)
    st1 = jnp.broadcast_to(st[:, 1:2, :], shape3)
    um1 = jnp.where(tpos == 0, st1, r1)
    um2 = jnp.where(tpos == 0, st0, jnp.where(tpos == 1, st1, r2))
    y = um2 * cw[0:1, :] + um1 * cw[1:2, :] + u3 * cw[2:3, :]
    return y.reshape(nseq * tlen, width), u3[:, tlen - 2:tlen, :]


def _mix_sample_kernel(x_ref, sh_ref, sc_ref, g_ref, n1g_ref, win_ref, cw_ref, wco_ref,
                       sinks_ref, wao_ref, wmo_ref, st_ref, ck_ref, cv_ref,
                       x1_ref, conv_ref, k_ref, v_ref,
                       kmain, kdrop, vmain, vdrop):
    nseq, tlen, _ = x_ref.shape
    rows = nseq * tlen

    x3 = x_ref[...]
    h3 = _rms(x3, n1g_ref[...]) * (1.0 + sc_ref[...]) + sh_ref[...]
    h = h3.reshape(rows, D_MODEL).astype(_BF)

    def proj(lo, width):
        return _dot(h, win_ref[:, lo:lo + width])

    pb = proj(_OFF_B, D_MODEL)
    u = proj(_OFF_C, D_MODEL) * proj(_OFF_X, D_MODEL)
    uc, new_st = _conv_seq(u, st_ref[...], cw_ref[...], nseq, tlen)
    conv_ref[...] = new_st
    ya = _dot((pb * uc).astype(_BF), wco_ref[...])

    keep = WINDOW - tlen
    kvt_new = proj(_OFF_K, 2 * KV_WIDTH).T
    keep_old = lax.broadcasted_iota(jnp.int32, (KV_WIDTH, WINDOW), 1) < keep
    seq_per_tile = LANES // tlen
    for s in range(nseq):
        tile = s // seq_per_tile
        shift = (keep - (s % seq_per_tile) * tlen) % LANES
        for cache_ref, out_ref, main_scr, drop_scr, base in (
                (ck_ref, k_ref, kmain, kdrop, 0), (cv_ref, v_ref, vmain, vdrop, KV_WIDTH)):
            rolled = pltpu.roll(cache_ref[s], keep, 1)
            new = pltpu.roll(kvt_new[base:base + KV_WIDTH, tile * LANES:(tile + 1) * LANES], shift, 1)
            win = jnp.where(keep_old, rolled, new)
            out_ref[s] = win
            main_scr[s] = win.astype(_BF)
            drop_scr[s] = rolled.astype(_BF)

    qf = proj(_OFF_Q, D_MODEL) * Q_SCALE
    lo = lax.broadcasted_iota(jnp.int32, (rows, LANES), 1) < HEAD_DIM
    zero = jnp.zeros((rows, LANES), _F32)
    heads_per_tile = N_HEADS // (KV_WIDTH // LANES)
    m_rows = heads_per_tile * tlen
    row = lax.broadcasted_iota(jnp.int32, (m_rows, 2 * WINDOW), 0)
    col = lax.broadcasted_iota(jnp.int32, (m_rows, 2 * WINDOW), 1)
    tq = row % tlen
    in_main = col < WINDOW
    dist = jnp.where(in_main, keep + tq - col, 2 * WINDOW + keep + tq - col)
    valid = (in_main & (dist >= 0)) | ((col >= WINDOW + keep) & (dist <= WINDOW))
    distf = dist.astype(_F32)
    row1 = lax.broadcasted_iota(jnp.int32, (m_rows, 1), 0)

    o_cols = [None] * N_QCOLS
    for kc in range(KV_WIDTH // LANES):
        parts = []
        slope = jnp.zeros((m_rows, 1), _F32)
        sink = jnp.zeros((m_rows, 1), _F32)
        for j in range(heads_per_tile):
            head = heads_per_tile * kc + j
            c, par, half = head // HEADS_PER_COL, j % HEADS_PER_COL, j // (heads_per_tile // 2)
            qc = qf[:, c * LANES:(c + 1) * LANES]
            src = qc if par == half else pltpu.roll(qc, HEAD_DIM, 1)
            part = jnp.where(lo, src, zero) if half == 0 else jnp.where(lo, zero, src)
            parts.append(part.reshape(nseq, tlen, LANES))
            mine = (row1 >= j * tlen) & (row1 < (j + 1) * tlen)
            slope = jnp.where(mine, _slope(head), slope)
            sink = jnp.where(mine, sinks_ref[head] * LOG2E, sink)
        lhs = jnp.concatenate(parts, axis=1).astype(_BF)
        bias = jnp.where(valid, -(slope * distf), NEG_INF)
        rsl = slice(kc * LANES, (kc + 1) * LANES)
        wk = jnp.concatenate([kmain[:, rsl, :], kdrop[:, rsl, :]], axis=2)
        wv = jnp.concatenate([vmain[:, rsl, :], vdrop[:, rsl, :]], axis=2)
        sc = jnp.einsum('smk,skn->smn', lhs, wk, preferred_element_type=_F32) + bias
        p, inv = _softmax_parts(sc, sink)
        r = jnp.einsum('smk,snk->smn', p.astype(_BF), wv, preferred_element_type=_F32) * inv
        for m2 in range(heads_per_tile // HEADS_PER_COL):
            c = (heads_per_tile // HEADS_PER_COL) * kc + m2
            half = m2 // (heads_per_tile // (2 * HEADS_PER_COL))
            r0 = r[:, (2 * m2) * tlen:(2 * m2 + 1) * tlen, :].reshape(rows, LANES)
            r1 = r[:, (2 * m2 + 1) * tlen:(2 * m2 + 2) * tlen, :].reshape(rows, LANES)
            if half == 0:
                o_cols[c] = jnp.where(lo, r0, pltpu.roll(r1, HEAD_DIM, 1))
            else:
                o_cols[c] = jnp.where(lo, pltpu.roll(r0, HEAD_DIM, 1), r1)
    o = jnp.concatenate(o_cols, axis=1).astype(_BF)
    yb = _dot(o, wao_ref[...])

    mixed = (jax.nn.sigmoid(proj(_OFF_GA, D_MODEL)) * ya
             + jax.nn.sigmoid(proj(_OFF_GB, D_MODEL)) * yb).astype(_BF)
    y = _dot(mixed, wmo_ref[...]).reshape(nseq, tlen, D_MODEL)
    x1_ref[...] = x3 + g_ref[...] * y


def _mix_sample_call(x, mod3, n1g, win, cw, wco, sinks, wao, wmo, st, ck, cv, chunk):
    nseq, tlen, _ = x.shape
    assert LANES % tlen == 0 and chunk % (LANES // tlen) == 0

    def mod_spec(col):
        return pl.BlockSpec((chunk, 1, D_MODEL), lambda s: (s, 0, col))

    def seq_spec(d1, d2):
        return pl.BlockSpec((chunk, d1, d2), lambda s: (s, 0, 0))

    return pl.pallas_call(
        _mix_sample_kernel,
        grid=(nseq // chunk,),
        in_specs=[seq_spec(tlen, D_MODEL),
                  mod_spec(0), mod_spec(1), mod_spec(2),
                  _const_spec((1, D_MODEL)),
                  _const_spec((D_MODEL, IN_WIDTH)),
                  _const_spec((3, D_MODEL)),
                  _const_spec((D_MODEL, D_MODEL)),
                  pl.BlockSpec(memory_space=pltpu.SMEM),
                  _const_spec((D_MODEL, D_MODEL)),
                  _const_spec((D_MODEL, D_MODEL)),
                  seq_spec(2, D_MODEL),
                  seq_spec(KV_WIDTH, WINDOW),
                  seq_spec(KV_WIDTH, WINDOW)],
        out_specs=[seq_spec(tlen, D_MODEL),
                   seq_spec(2, D_MODEL),
                   seq_spec(KV_WIDTH, WINDOW),
                   seq_spec(KV_WIDTH, WINDOW)],
        out_shape=[jax.ShapeDtypeStruct((nseq, tlen, D_MODEL), _F32),
                   jax.ShapeDtypeStruct((nseq, 2, D_MODEL), _F32),
                   jax.ShapeDtypeStruct((nseq, KV_WIDTH, WINDOW), _F32),
                   jax.ShapeDtypeStruct((nseq, KV_WIDTH, WINDOW), _F32)],
        scratch_shapes=[pltpu.VMEM((chunk, KV_WIDTH, WINDOW), _BF) for _ in range(4)],
        compiler_params=pltpu.CompilerParams(
            dimension_semantics=("arbitrary",),
            vmem_limit_bytes=VMEM_LIMIT),
        name="mix_sample",
    )(x, mod3, mod3, mod3, n1g, win, cw, wco, sinks, wao, wmo, st, ck, cv)


def _ffn_tail(x1, ac, val, g, wdown_ref, fg):
    hmid = (_gelu(ac) * val).astype(_BF)
    x2 = x1 + g * _dot(hmid, wdown_ref[...])
    return _rms(x2, fg)


def _ffn_prompt_kernel(x1_ref, sh_ref, sc_ref, g_ref, n2g_ref, wup_ref, fcw_ref, wdown_ref,
                       fg_ref, y_ref, ffn_ref, a_ext):
    i = pl.program_id(1)
    t = x1_ref.shape[0]

    @pl.when(i == 0)
    def _():
        a_ext[0:CARRY_ROWS, :] = jnp.zeros((CARRY_ROWS, D_FF), _F32)

    x1 = x1_ref[...]
    h2 = (_rms(x1, n2g_ref[...]) * (1.0 + sc_ref[...]) + sh_ref[...]).astype(_BF)
    a = _dot(h2, wup_ref[:, 0:D_FF])
    val = _dot(h2, wup_ref[:, D_FF:2 * D_FF])
    a_ext[CARRY_ROWS:CARRY_ROWS + t, :] = a
    fcw = fcw_ref[...]
    ac = (a_ext[CARRY_ROWS - 2:CARRY_ROWS - 2 + t, :] * fcw[0:1, :]
          + a_ext[CARRY_ROWS - 1:CARRY_ROWS - 1 + t, :] * fcw[1:2, :]
          + a * fcw[2:3, :])
    ffn_ref[...] = a[t - 2:t, :]
    a_ext[0:CARRY_ROWS, :] = a[t - CARRY_ROWS:t, :]
    y_ref[...] = _ffn_tail(x1, ac, val, g_ref[...], wdown_ref, fg_ref[...])


def _ffn_prompt_call(x1, mod3, mod_row0, n2g, wup, fcw, wdown, fg, tile):
    bsz, seq, _ = x1.shape

    def mod_spec(chunk):
        return pl.BlockSpec((None, 1, D_MODEL), lambda b, i: (mod_row0 + b, 0, chunk))

    return pl.pallas_call(
        _ffn_prompt_kernel,
        grid=(bsz, seq // tile),
        in_specs=[pl.BlockSpec((None, tile, D_MODEL), lambda b, i: (b, i, 0)),
                  mod_spec(3), mod_spec(4), mod_spec(5),
                  _const_spec((1, D_MODEL)),
                  _const_spec((D_MODEL, 2 * D_FF)),
                  _const_spec((3, D_FF)),
                  _const_spec((D_FF, D_MODEL)),
                  _const_spec((1, D_MODEL))],
        out_specs=[pl.BlockSpec((None, tile, D_MODEL), lambda b, i: (b, i, 0)),
                   pl.BlockSpec((None, 2, D_FF), lambda b, i: (b, 0, 0))],
        out_shape=[jax.ShapeDtypeStruct((bsz, seq, D_MODEL), _F32),
                   jax.ShapeDtypeStruct((bsz, 2, D_FF), _F32)],
        scratch_shapes=[pltpu.VMEM((CARRY_ROWS + tile, D_FF), _F32)],
        compiler_params=pltpu.CompilerParams(
            dimension_semantics=("arbitrary", "arbitrary"),
            vmem_limit_bytes=VMEM_LIMIT),
        name="ffn_prompt",
    )(x1, mod3, mod3, mod3, n2g, wup, fcw, wdown, fg)


def _ffn_sample_kernel(x1_ref, sh_ref, sc_ref, g_ref, n2g_ref, wup_ref, fcw_ref, wdown_ref,
                       fg_ref, st_ref, y_ref, ffn_ref):
    nseq, tlen, _ = x1_ref.shape
    rows = nseq * tlen
    x3 = x1_ref[...]
    h3 = _rms(x3, n2g_ref[...]) * (1.0 + sc_ref[...]) + sh_ref[...]
    h2 = h3.reshape(rows, D_MODEL).astype(_BF)
    a = _dot(h2, wup_ref[:, 0:D_FF])
    val = _dot(h2, wup_ref[:, D_FF:2 * D_FF])
    ac, new_st = _conv_seq(a, st_ref[...], fcw_ref[...], nseq, tlen)
    ffn_ref[...] = new_st
    hmid = (_gelu(ac) * val).astype(_BF)
    y = _dot(hmid, wdown_ref[...]).reshape(nseq, tlen, D_MODEL)
    x2 = x3 + g_ref[...] * y
    y_ref[...] = _rms(x2, fg_ref[...])


def _ffn_sample_call(x1, mod3, n2g, wup, fcw, wdown, fg, st, chunk):
    nseq, tlen, _ = x1.shape

    def mod_spec(col):
        return pl.BlockSpec((chunk, 1, D_MODEL), lambda s: (s, 0, col))

    def seq_spec(d1, d2):
        return pl.BlockSpec((chunk, d1, d2), lambda s: (s, 0, 0))

    return pl.pallas_call(
        _ffn_sample_kernel,
        grid=(nseq // chunk,),
        in_specs=[seq_spec(tlen, D_MODEL),
                  mod_spec(3), mod_spec(4), mod_spec(5),
                  _const_spec((1, D_MODEL)),
                  _const_spec((D_MODEL, 2 * D_FF)),
                  _const_spec((3, D_FF)),
                  _const_spec((D_FF, D_MODEL)),
                  _const_spec((1, D_MODEL)),
                  seq_spec(2, D_FF)],
        out_specs=[seq_spec(tlen, D_MODEL), seq_spec(2, D_FF)],
        out_shape=[jax.ShapeDtypeStruct((nseq, tlen, D_MODEL), _F32),
                   jax.ShapeDtypeStruct((nseq, 2, D_FF), _F32)],
        compiler_params=pltpu.CompilerParams(
            dimension_semantics=("arbitrary",),
            vmem_limit_bytes=VMEM_LIMIT),
        name="ffn_sample",
    )(x1, mod3, mod3, mod3, n2g, wup, fcw, wdown, fg, st)


PROMPT_TILE = 512
SAMPLE_CHUNK = 16


def kernel(x_prompt, x_sample, c_prompt, c_sample, state_conv, cache_k_win, cache_v_win,
           state_ffn_conv, norm1_g, norm2_g, w_ada, b_ada, w_in, conv_w, w_conv_out,
           attn_sinks, w_attn_out, w_mix_out, w_up, ffn_conv_w, w_down, final_g):
    depth = w_in.shape[0]
    assert depth == 1, "the final norm is fused into the (single) layer's FFN call"
    nsamp = x_sample.shape[0]
    bsz = x_prompt.shape[0]
    xp, xs = x_prompt, x_sample
    c_all = jnp.concatenate([c_sample, c_prompt], axis=0)
    fg = final_g.reshape(1, D_MODEL)
    outs = [[] for _ in range(8)]
    for layer in range(depth):
        mod = _mod_call(c_all, w_ada[layer].astype(_BF), b_ada[layer].reshape(1, -1))
        mod3 = mod.reshape(nsamp + bsz, 1, 6 * D_MODEL)
        n1g = norm1_g[layer].reshape(1, D_MODEL)
        n2g = norm2_g[layer].reshape(1, D_MODEL)
        win = w_in[layer].astype(_BF)
        wco = w_conv_out[layer].astype(_BF)
        wao = w_attn_out[layer].astype(_BF)
        wmo = w_mix_out[layer].astype(_BF)
        wup = w_up[layer].astype(_BF)
        wdown = w_down[layer].astype(_BF)
        cw = conv_w[layer]
        fcw = ffn_conv_w[layer]
        sinks = attn_sinks[layer]
        fg_l = fg
        def to_dim_major(win_state):
            return jnp.transpose(win_state, (0, 2, 3, 1)).reshape(-1, KV_WIDTH, WINDOW)

        def to_pos_major(t3):
            return jnp.transpose(t3.reshape(-1, N_KV_HEADS, HEAD_DIM, WINDOW), (0, 3, 1, 2))

        x1p, conv_p, k_p, v_p = _mix_prompt_call(
            xp, mod3, nsamp, n1g, win, cw, wco, sinks, wao, wmo, PROMPT_TILE)
        x1s, conv_s, k_s, v_s = _mix_sample_call(
            xs, mod3, n1g, win, cw, wco, sinks, wao, wmo, state_conv[layer],
            to_dim_major(cache_k_win[layer]), to_dim_major(cache_v_win[layer]), SAMPLE_CHUNK)
        xp, ffn_p = _ffn_prompt_call(x1p, mod3, nsamp, n2g, wup, fcw, wdown, fg_l, PROMPT_TILE)
        xs, ffn_s = _ffn_sample_call(x1s, mod3, n2g, wup, fcw, wdown, fg_l,
                                     state_ffn_conv[layer], SAMPLE_CHUNK)
        for lst, val in zip(outs, (conv_p, to_pos_major(k_p), to_pos_major(v_p), ffn_p,
                                   conv_s, to_pos_major(k_s), to_pos_major(v_s), ffn_s)):
            lst.append(val)
    return (xp, xs) + tuple(jnp.stack(lst, axis=0) for lst in outs)
```

```python
import functools

import jax
import jax.numpy as jnp
from jax import lax
from jax.experimental import pallas as pl
from jax.experimental.pallas import tpu as pltpu

D_MODEL = 1024
HEAD_DIM = 64
N_HEADS = 16
N_KV_HEADS = 4
WINDOW = 128
D_FF = 2816
KV_WIDTH = N_KV_HEADS * HEAD_DIM
IN_WIDTH = 3 * D_MODEL + D_MODEL + 2 * KV_WIDTH + 2 * D_MODEL
RMS_EPS = 1e-6
NEG_INF = -1e30
LANES = 128
HEADS_PER_COL = LANES // HEAD_DIM
N_QCOLS = N_HEADS * HEAD_DIM // LANES
CARRY_ROWS = 8
VMEM_LIMIT = 56 * 1024 * 1024

_OFF_B, _OFF_C, _OFF_X = 0, D_MODEL, 2 * D_MODEL
_OFF_Q = 3 * D_MODEL
_OFF_K = _OFF_Q + D_MODEL
_OFF_V = _OFF_K + KV_WIDTH
_OFF_GA = _OFF_V + KV_WIDTH
_OFF_GB = _OFF_GA + D_MODEL

_BF = jnp.bfloat16
_F32 = jnp.float32
_NT = (((1,), (1,)), ((), ()))


LOG2E = 1.4426950408889634
Q_SCALE = HEAD_DIM ** -0.5 * LOG2E


def _slope(head):
    return 2.0 ** (-8.0 * (head + 1) / N_HEADS) * LOG2E


def _rms(x, g):
    return x * lax.rsqrt(jnp.mean(x * x, axis=-1, keepdims=True) + RMS_EPS) * g


def _dot(a, b):
    return jnp.dot(a, b, preferred_element_type=_F32)


def _gelu(x):
    return 0.5 * x * (1.0 + lax.erf(x * (0.5 ** 0.5)))


def _split_heads(t2d, lo):
    rolled = pltpu.roll(t2d, HEAD_DIM, 1)
    zero = jnp.zeros_like(t2d)
    return (jnp.where(lo, t2d, zero), jnp.where(lo, zero, rolled),
            jnp.where(lo, rolled, zero), jnp.where(lo, zero, t2d))


def _softmax_parts(s, sink):
    m = jnp.maximum(jnp.max(s, axis=-1, keepdims=True), sink)
    p = jnp.exp2(s - m)
    denom = jnp.sum(p, axis=-1, keepdims=True) + jnp.exp2(sink - m)
    return p, 1.0 / denom


def _mod_kernel(c_ref, w_ref, b_ref, o_ref):
    c = c_ref[...]
    a = (c * jax.nn.sigmoid(c)).astype(_BF)
    res = _dot(a, w_ref[...].astype(_BF)) + b_ref[...]
    o_ref[...] = res.reshape(o_ref.shape)


def _mod_call(c_all, w_ada, b_ada):
    n = c_all.shape[0]
    return pl.pallas_call(
        _mod_kernel,
        grid=(6,),
        in_specs=[pl.BlockSpec((n, D_MODEL), lambda j: (0, 0)),
                  pl.BlockSpec((D_MODEL, D_MODEL), lambda j: (0, j)),
                  pl.BlockSpec((1, D_MODEL), lambda j: (0, j))],
        out_specs=pl.BlockSpec((n, 1, D_MODEL), lambda j: (0, 0, j)),
        out_shape=jax.ShapeDtypeStruct((n, 1, 6 * D_MODEL), _F32),
        name="adaln_mod",
    )(c_all, w_ada, b_ada)


def _mod_norm(x, g, sc, sh):
    return (_rms(x, g) * (1.0 + sc) + sh).astype(_BF)


def _mix_prompt_kernel(x_ref, sh_ref, sc_ref, g_ref, xn_ref, shn_ref, scn_ref, n1g_ref,
                       win_ref, cw_ref, wco_ref, sinks_ref, wao_ref, wmo_ref,
                       x1_ref, conv_ref, k_ref, v_ref,
                       u_ext, k_ext, v_ext, o_scr, bias_scr, h_scr, *, tiles_per_seq):
    step = pl.program_id(0)
    i = step % tiles_per_seq
    t = x_ref.shape[0]
    nblk = t // WINDOW

    @pl.when(step == 0)
    def _():
        h_scr[...] = _mod_norm(x_ref[...], n1g_ref[...], sc_ref[...], sh_ref[...])
        row = lax.broadcasted_iota(jnp.int32, (WINDOW, 2 * WINDOW), 0)
        col = lax.broadcasted_iota(jnp.int32, (WINDOW, 2 * WINDOW), 1)
        dist = WINDOW + row - col
        valid = (dist >= 0) & (dist <= WINDOW)
        distf = dist.astype(_F32)
        for head in range(N_HEADS):
            bias_scr[head] = jnp.where(valid, -(_slope(head) * distf), NEG_INF)

    @pl.when(i == 0)
    def _():
        u_ext[0:CARRY_ROWS, :] = jnp.zeros((CARRY_ROWS, D_MODEL), _F32)
        k_ext[:, 0:WINDOW, :] = jnp.zeros((2 * N_KV_HEADS, WINDOW, LANES), _BF)
        v_ext[:, 0:WINDOW, :] = jnp.zeros((2 * N_KV_HEADS, WINDOW, LANES), _BF)

    def proj(lo, width):
        return _dot(h_scr[...], win_ref[:, lo:lo + width])

    def next_tile_norm():
        h_scr[...] = _mod_norm(xn_ref[...], n1g_ref[...], scn_ref[...], shn_ref[...])

    def branch_a():
        pb = proj(_OFF_B, D_MODEL)
        u = proj(_OFF_C, D_MODEL) * proj(_OFF_X, D_MODEL)
        u_ext[CARRY_ROWS:CARRY_ROWS + t, :] = u
        cw = cw_ref[...]
        uc = (u_ext[CARRY_ROWS - 2:CARRY_ROWS - 2 + t, :] * cw[0:1, :]
              + u_ext[CARRY_ROWS - 1:CARRY_ROWS - 1 + t, :] * cw[1:2, :]
              + u * cw[2:3, :])
        conv_ref[...] = u[t - 2:t, :]
        u_ext[0:CARRY_ROWS, :] = u[t - CARRY_ROWS:t, :]
        return _dot((pb * uc).astype(_BF), wco_ref[...])

    side_jobs = [branch_a,
                 lambda: jax.nn.sigmoid(proj(_OFF_GA, D_MODEL)),
                 lambda: jax.nn.sigmoid(proj(_OFF_GB, D_MODEL)),
                 next_tile_norm]
    side_out = []

    q = (proj(_OFF_Q, D_MODEL) * Q_SCALE).astype(_BF)
    kf = proj(_OFF_K, KV_WIDTH)
    vf = proj(_OFF_V, KV_WIDTH)
    k_ref[...] = kf[t - WINDOW:t, :].T
    v_ref[...] = vf[t - WINDOW:t, :].T
    lo = lax.broadcasted_iota(jnp.int32, (t, LANES), 1) < HEAD_DIM
    for kc in range(KV_WIDTH // LANES):
        ks = _split_heads(kf[:, kc * LANES:(kc + 1) * LANES], lo)
        vs = _split_heads(vf[:, kc * LANES:(kc + 1) * LANES], lo)
        for j in range(4):
            k_ext[4 * kc + j, WINDOW:WINDOW + t, :] = ks[j].astype(_BF)
            v_ext[4 * kc + j, WINDOW:WINDOW + t, :] = vs[j].astype(_BF)

    first_cols = lax.broadcasted_iota(jnp.int32, (WINDOW, 2 * WINDOW), 1) < WINDOW
    no_prev = first_cols & (i == 0)
    for blk in range(nblk):
        r0 = blk * WINDOW
        for c in range(N_QCOLS):
            hkv = c // 2
            qc = q[r0:r0 + WINDOW, c * LANES:(c + 1) * LANES]
            o_c = None
            for par in range(HEADS_PER_COL):
                head = HEADS_PER_COL * c + par
                kk = k_ext[2 * hkv + par, r0:r0 + 2 * WINDOW, :]
                s = lax.dot_general(qc, kk, _NT, preferred_element_type=_F32) + bias_scr[head]
                if blk == 0:
                    s = jnp.where(no_prev, NEG_INF, s)
                p, inv = _softmax_parts(s, sinks_ref[head] * LOG2E)
                vv = v_ext[2 * hkv + par, r0:r0 + 2 * WINDOW, :]
                o_h = _dot(p.astype(_BF), vv) * inv
                o_c = o_h if o_c is None else o_c + o_h
            o_scr[r0:r0 + WINDOW, c * LANES:(c + 1) * LANES] = o_c.astype(_BF)
        if side_jobs:
            side_out.append(side_jobs.pop(0)())
    while side_jobs:
        side_out.append(side_jobs.pop(0)())
    ya, gate_a, gate_b, _ = side_out
    k_ext[:, 0:WINDOW, :] = k_ext[:, t:t + WINDOW, :]
    v_ext[:, 0:WINDOW, :] = v_ext[:, t:t + WINDOW, :]
    yb = _dot(o_scr[...], wao_ref[...])

    mixed = (gate_a * ya + gate_b * yb).astype(_BF)
    x1_ref[...] = x_ref[...] + g_ref[...] * _dot(mixed, wmo_ref[...])


def _const_spec(shape):
    nd = len(shape)
    return pl.BlockSpec(shape, lambda *_: (0,) * nd, pipeline_mode=pl.Buffered(1))


def _tile_specs(tile, nt, steps, mod_row0):
    def nxt(s):
        return jnp.minimum(s + 1, steps - 1)

    x_spec = pl.BlockSpec((None, tile, D_MODEL), lambda s: (s // nt, s % nt, 0))
    xn_spec = pl.BlockSpec((None, tile, D_MODEL), lambda s: (nxt(s) // nt, nxt(s) % nt, 0))

    def mod_spec(chunk):
        return pl.BlockSpec((None, 1, D_MODEL), lambda s: (mod_row0 + s // nt, 0, chunk))

    def modn_spec(chunk):
        return pl.BlockSpec((None, 1, D_MODEL), lambda s: (mod_row0 + nxt(s) // nt, 0, chunk))

    return x_spec, xn_spec, mod_spec, modn_spec


def _mix_prompt_call(x, mod3, mod_row0, n1g, win, cw, wco, sinks, wao, wmo, tile):
    bsz, seq, _ = x.shape
    nt = seq // tile
    steps = bsz * nt
    x_spec, xn_spec, mod_spec, modn_spec = _tile_specs(tile, nt, steps, mod_row0)

    def state_spec(d1, d2):
        return pl.BlockSpec((None, d1, d2), lambda s: (s // nt, 0, 0))

    return pl.pallas_call(
        functools.partial(_mix_prompt_kernel, tiles_per_seq=nt),
        grid=(steps,),
        in_specs=[x_spec, mod_spec(0), mod_spec(1), mod_spec(2),
                  xn_spec, modn_spec(0), modn_spec(1),
                  _const_spec((1, D_MODEL)),
                  _const_spec((D_MODEL, IN_WIDTH)),
                  _const_spec((3, D_MODEL)),
                  _const_spec((D_MODEL, D_MODEL)),
                  pl.BlockSpec(memory_space=pltpu.SMEM),
                  _const_spec((D_MODEL, D_MODEL)),
                  _const_spec((D_MODEL, D_MODEL))],
        out_specs=[x_spec,
                   state_spec(2, D_MODEL),
                   state_spec(KV_WIDTH, WINDOW),
                   state_spec(KV_WIDTH, WINDOW)],
        out_shape=[jax.ShapeDtypeStruct((bsz, seq, D_MODEL), _F32),
                   jax.ShapeDtypeStruct((bsz, 2, D_MODEL), _F32),
                   jax.ShapeDtypeStruct((bsz, KV_WIDTH, WINDOW), _F32),
                   jax.ShapeDtypeStruct((bsz, KV_WIDTH, WINDOW), _F32)],
        scratch_shapes=[pltpu.VMEM((CARRY_ROWS + tile, D_MODEL), _F32),
                        pltpu.VMEM((2 * N_KV_HEADS, WINDOW + tile, LANES), _BF),
                        pltpu.VMEM((2 * N_KV_HEADS, WINDOW + tile, LANES), _BF),
                        pltpu.VMEM((tile, D_MODEL), _BF),
                        pltpu.VMEM((N_HEADS, WINDOW, 2 * WINDOW), _F32),
                        pltpu.VMEM((tile, D_MODEL), _BF)],
        compiler_params=pltpu.CompilerParams(
            dimension_semantics=("arbitrary",),
            vmem_limit_bytes=VMEM_LIMIT),
        name="mix_prompt",
    )(x, mod3, mod3, mod3, x, mod3, mod3, n1g, win, cw, wco, sinks, wao, wmo)


def _conv_seq(u2d, st, cw, nseq, tlen):
    width = u2d.shape[-1]
    shape3 = (nseq, tlen, width)
    u3 = u2d.reshape(shape3)
    r1 = pltpu.roll(u2d, 1, 0).reshape(shape3)
    r2 = pltpu.roll(u2d, 2, 0).reshape(shape3)
    tpos = lax.broadcasted_iota(jnp.int32, shape3, 1)
    st0 = jnp.broadcast_to(st[:, 0:1, :], shape3)
    st1 = jnp.broadcast_to(st[:, 1:2, :], shape3)
    um1 = jnp.where(tpos == 0, st1, r1)
    um2 = jnp.where(tpos == 0, st0, jnp.where(tpos == 1, st1, r2))
    y = um2 * cw[0:1, :] + um1 * cw[1:2, :] + u3 * cw[2:3, :]
    return y.reshape(nseq * tlen, width), u3[:, tlen - 2:tlen, :]


def _mix_sample_kernel(x_ref, sh_ref, sc_ref, g_ref, n1g_ref, win_ref, cw_ref, wco_ref,
                       sinks_ref, wao_ref, wmo_ref, st_ref, ck_ref, cv_ref,
                       x1_ref, conv_ref, k_ref, v_ref,
                       kmain, kdrop, vmain, vdrop):
    nseq, tlen, _ = x_ref.shape
    rows = nseq * tlen

    x3 = x_ref[...]
    h3 = _rms(x3, n1g_ref[...]) * (1.0 + sc_ref[...]) + sh_ref[...]
    h = h3.reshape(rows, D_MODEL).astype(_BF)

    def proj(lo, width):
        return _dot(h, win_ref[:, lo:lo + width])

    pb = proj(_OFF_B, D_MODEL)
    u = proj(_OFF_C, D_MODEL) * proj(_OFF_X, D_MODEL)
    uc, new_st = _conv_seq(u, st_ref[...], cw_ref[...], nseq, tlen)
    conv_ref[...] = new_st
    ya = _dot((pb * uc).astype(_BF), wco_ref[...])

    keep = WINDOW - tlen
    kvt_new = proj(_OFF_K, 2 * KV_WIDTH).T
    keep_old = lax.broadcasted_iota(jnp.int32, (KV_WIDTH, WINDOW), 1) < keep
    seq_per_tile = LANES // tlen
    for s in range(nseq):
        tile = s // seq_per_tile
        shift = (keep - (s % seq_per_tile) * tlen) % LANES
        for cache_ref, out_ref, main_scr, drop_scr, base in (
                (ck_ref, k_ref, kmain, kdrop, 0), (cv_ref, v_ref, vmain, vdrop, KV_WIDTH)):
            rolled = pltpu.roll(cache_ref[s], keep, 1)
            new = pltpu.roll(kvt_new[base:base + KV_WIDTH, tile * LANES:(tile + 1) * LANES], shift, 1)
            win = jnp.where(keep_old, rolled, new)
            out_ref[s] = win
            main_scr[s] = win.astype(_BF)
            drop_scr[s] = rolled.astype(_BF)

    qf = proj(_OFF_Q, D_MODEL) * Q_SCALE
    lo = lax.broadcasted_iota(jnp.int32, (rows, LANES), 1) < HEAD_DIM
    zero = jnp.zeros((rows, LANES), _F32)
    heads_per_tile = N_HEADS // (KV_WIDTH // LANES)
    m_rows = heads_per_tile * tlen
    row = lax.broadcasted_iota(jnp.int32, (m_rows, 2 * WINDOW), 0)
    col = lax.broadcasted_iota(jnp.int32, (m_rows, 2 * WINDOW), 1)
    tq = row % tlen
    in_main = col < WINDOW
    dist = jnp.where(in_main, keep + tq - col, 2 * WINDOW + keep + tq - col)
    valid = (in_main & (dist >= 0)) | ((col >= WINDOW + keep) & (dist <= WINDOW))
    distf = dist.astype(_F32)
    row1 = lax.broadcasted_iota(jnp.int32, (m_rows, 1), 0)

    o_cols = [None] * N_QCOLS
    for kc in range(KV_WIDTH // LANES):
        parts = []
        slope = jnp.zeros((m_rows, 1), _F32)
        sink = jnp.zeros((m_rows, 1), _F32)
        for j in range(heads_per_tile):
            head = heads_per_tile * kc + j
            c, par, half = head // HEADS_PER_COL, j % HEADS_PER_COL, j // (heads_per_tile // 2)
            qc = qf[:, c * LANES:(c + 1) * LANES]
            src = qc if par == half else pltpu.roll(qc, HEAD_DIM, 1)
            part = jnp.where(lo, src, zero) if half == 0 else jnp.where(lo, zero, src)
            parts.append(part.reshape(nseq, tlen, LANES))
            mine = (row1 >= j * tlen) & (row1 < (j + 1) * tlen)
            slope = jnp.where(mine, _slope(head), slope)
            sink = jnp.where(mine, sinks_ref[head] * LOG2E, sink)
        lhs = jnp.concatenate(parts, axis=1).astype(_BF)
        bias = jnp.where(valid, -(slope * distf), NEG_INF)
        rsl = slice(kc * LANES, (kc + 1) * LANES)
        wk = jnp.concatenate([kmain[:, rsl, :], kdrop[:, rsl, :]], axis=2)
        wv = jnp.concatenate([vmain[:, rsl, :], vdrop[:, rsl, :]], axis=2)
        sc = jnp.einsum('smk,skn->smn', lhs, wk, preferred_element_type=_F32) + bias
        p, inv = _softmax_parts(sc, sink)
        r = jnp.einsum('smk,snk->smn', p.astype(_BF), wv, preferred_element_type=_F32) * inv
        for m2 in range(heads_per_tile // HEADS_PER_COL):
            c = (heads_per_tile // HEADS_PER_COL) * kc + m2
            half = m2 // (heads_per_tile // (2 * HEADS_PER_COL))
            r0 = r[:, (2 * m2) * tlen:(2 * m2 + 1) * tlen, :].reshape(rows, LANES)
            r1 = r[:, (2 * m2 + 1) * tlen:(2 * m2 + 2) * tlen, :].reshape(rows, LANES)
            if half == 0:
                o_cols[c] = jnp.where(lo, r0, pltpu.roll(r1, HEAD_DIM, 1))
            else:
                o_cols[c] = jnp.where(lo, pltpu.roll(r0, HEAD_DIM, 1), r1)
    o = jnp.concatenate(o_cols, axis=1).astype(_BF)
    yb = _dot(o, wao_ref[...])

    mixed = (jax.nn.sigmoid(proj(_OFF_GA, D_MODEL)) * ya
             + jax.nn.sigmoid(proj(_OFF_GB, D_MODEL)) * yb).astype(_BF)
    y = _dot(mixed, wmo_ref[...]).reshape(nseq, tlen, D_MODEL)
    x1_ref[...] = x3 + g_ref[...] * y


def _mix_sample_call(x, mod3, n1g, win, cw, wco, sinks, wao, wmo, st, ck, cv, chunk):
    nseq, tlen, _ = x.shape
    assert LANES % tlen == 0 and chunk % (LANES // tlen) == 0

    def mod_spec(col):
        return pl.BlockSpec((chunk, 1, D_MODEL), lambda s: (s, 0, col))

    def seq_spec(d1, d2):
        return pl.BlockSpec((chunk, d1, d2), lambda s: (s, 0, 0))

    return pl.pallas_call(
        _mix_sample_kernel,
        grid=(nseq // chunk,),
        in_specs=[seq_spec(tlen, D_MODEL),
                  mod_spec(0), mod_spec(1), mod_spec(2),
                  _const_spec((1, D_MODEL)),
                  _const_spec((D_MODEL, IN_WIDTH)),
                  _const_spec((3, D_MODEL)),
                  _const_spec((D_MODEL, D_MODEL)),
                  pl.BlockSpec(memory_space=pltpu.SMEM),
                  _const_spec((D_MODEL, D_MODEL)),
                  _const_spec((D_MODEL, D_MODEL)),
                  seq_spec(2, D_MODEL),
                  seq_spec(KV_WIDTH, WINDOW),
                  seq_spec(KV_WIDTH, WINDOW)],
        out_specs=[seq_spec(tlen, D_MODEL),
                   seq_spec(2, D_MODEL),
                   seq_spec(KV_WIDTH, WINDOW),
                   seq_spec(KV_WIDTH, WINDOW)],
        out_shape=[jax.ShapeDtypeStruct((nseq, tlen, D_MODEL), _F32),
                   jax.ShapeDtypeStruct((nseq, 2, D_MODEL), _F32),
                   jax.ShapeDtypeStruct((nseq, KV_WIDTH, WINDOW), _F32),
                   jax.ShapeDtypeStruct((nseq, KV_WIDTH, WINDOW), _F32)],
        scratch_shapes=[pltpu.VMEM((chunk, KV_WIDTH, WINDOW), _BF) for _ in range(4)],
        compiler_params=pltpu.CompilerParams(
            dimension_semantics=("arbitrary",),
            vmem_limit_bytes=VMEM_LIMIT),
        name="mix_sample",
    )(x, mod3, mod3, mod3, n1g, win, cw, wco, sinks, wao, wmo, st, ck, cv)


def _ffn_tail(x1, ac, val, g, wdown_ref, fg):
    hmid = (_gelu(ac) * val).astype(_BF)
    x2 = x1 + g * _dot(hmid, wdown_ref[...])
    return _rms(x2, fg)


def _ffn_prompt_kernel(x1_ref, sh_ref, sc_ref, g_ref, xn_ref, shn_ref, scn_ref, n2g_ref,
                       wup_ref, fcw_ref, wdown_ref, fg_ref, y_ref, ffn_ref, a_ext, h_scr,
                       *, tiles_per_seq):
    step = pl.program_id(0)
    i = step % tiles_per_seq
    t = x1_ref.shape[0]

    @pl.when(step == 0)
    def _():
        h_scr[...] = _mod_norm(x1_ref[...], n2g_ref[...], sc_ref[...], sh_ref[...])

    @pl.when(i == 0)
    def _():
        a_ext[0:CARRY_ROWS, :] = jnp.zeros((CARRY_ROWS, D_FF), _F32)

    x1 = x1_ref[...]
    a = _dot(h_scr[...], wup_ref[:, 0:D_FF])
    val = _dot(h_scr[...], wup_ref[:, D_FF:2 * D_FF])
    h_scr[...] = _mod_norm(xn_ref[...], n2g_ref[...], scn_ref[...], shn_ref[...])
    a_ext[CARRY_ROWS:CARRY_ROWS + t, :] = a
    fcw = fcw_ref[...]
    ac = (a_ext[CARRY_ROWS - 2:CARRY_ROWS - 2 + t, :] * fcw[0:1, :]
          + a_ext[CARRY_ROWS - 1:CARRY_ROWS - 1 + t, :] * fcw[1:2, :]
          + a * fcw[2:3, :])
    ffn_ref[...] = a[t - 2:t, :]
    a_ext[0:CARRY_ROWS, :] = a[t - CARRY_ROWS:t, :]
    y_ref[...] = _ffn_tail(x1, ac, val, g_ref[...], wdown_ref, fg_ref[...])


def _ffn_prompt_call(x1, mod3, mod_row0, n2g, wup, fcw, wdown, fg, tile):
    bsz, seq, _ = x1.shape
    nt = seq // tile
    steps = bsz * nt
    x_spec, xn_spec, mod_spec, modn_spec = _tile_specs(tile, nt, steps, mod_row0)

    return pl.pallas_call(
        functools.partial(_ffn_prompt_kernel, tiles_per_seq=nt),
        grid=(steps,),
        in_specs=[x_spec, mod_spec(3), mod_spec(4), mod_spec(5),
                  xn_spec, modn_spec(3), modn_spec(4),
                  _const_spec((1, D_MODEL)),
                  _const_spec((D_MODEL, 2 * D_FF)),
                  _const_spec((3, D_FF)),
                  _const_spec((D_FF, D_MODEL)),
                  _const_spec((1, D_MODEL))],
        out_specs=[x_spec,
                   pl.BlockSpec((None, 2, D_FF), lambda s: (s // nt, 0, 0))],
        out_shape=[jax.ShapeDtypeStruct((bsz, seq, D_MODEL), _F32),
                   jax.ShapeDtypeStruct((bsz, 2, D_FF), _F32)],
        scratch_shapes=[pltpu.VMEM((CARRY_ROWS + tile, D_FF), _F32),
                        pltpu.VMEM((tile, D_MODEL), _BF)],
        compiler_params=pltpu.CompilerParams(
            dimension_semantics=("arbitrary",),
            vmem_limit_bytes=VMEM_LIMIT),
        name="ffn_prompt",
    )(x1, mod3, mod3, mod3, x1, mod3, mod3, n2g, wup, fcw, wdown, fg)


def _ffn_sample_kernel(x1_ref, sh_ref, sc_ref, g_ref, n2g_ref, wup_ref, fcw_ref, wdown_ref,
                       fg_ref, st_ref, y_ref, ffn_ref):
    nseq, tlen, _ = x1_ref.shape
    rows = nseq * tlen
    x3 = x1_ref[...]
    h3 = _rms(x3, n2g_ref[...]) * (1.0 + sc_ref[...]) + sh_ref[...]
    h2 = h3.reshape(rows, D_MODEL).astype(_BF)
    a = _dot(h2, wup_ref[:, 0:D_FF])
    val = _dot(h2, wup_ref[:, D_FF:2 * D_FF])
    ac, new_st = _conv_seq(a, st_ref[...], fcw_ref[...], nseq, tlen)
    ffn_ref[...] = new_st
    hmid = (_gelu(ac) * val).astype(_BF)
    y = _dot(hmid, wdown_ref[...]).reshape(nseq, tlen, D_MODEL)
    x2 = x3 + g_ref[...] * y
    y_ref[...] = _rms(x2, fg_ref[...])


def _ffn_sample_call(x1, mod3, n2g, wup, fcw, wdown, fg, st, chunk):
    nseq, tlen, _ = x1.shape

    def mod_spec(col):
        return pl.BlockSpec((chunk, 1, D_MODEL), lambda s: (s, 0, col))

    def seq_spec(d1, d2):
        return pl.BlockSpec((chunk, d1, d2), lambda s: (s, 0, 0))

    return pl.pallas_call(
        _ffn_sample_kernel,
        grid=(nseq // chunk,),
        in_specs=[seq_spec(tlen, D_MODEL),
                  mod_spec(3), mod_spec(4), mod_spec(5),
                  _const_spec((1, D_MODEL)),
                  _const_spec((D_MODEL, 2 * D_FF)),
                  _const_spec((3, D_FF)),
                  _const_spec((D_FF, D_MODEL)),
                  _const_spec((1, D_MODEL)),
                  seq_spec(2, D_FF)],
        out_specs=[seq_spec(tlen, D_MODEL), seq_spec(2, D_FF)],
        out_shape=[jax.ShapeDtypeStruct((nseq, tlen, D_MODEL), _F32),
                   jax.ShapeDtypeStruct((nseq, 2, D_FF), _F32)],
        compiler_params=pltpu.CompilerParams(
            dimension_semantics=("arbitrary",),
            vmem_limit_bytes=VMEM_LIMIT),
        name="ffn_sample",
    )(x1, mod3, mod3, mod3, n2g, wup, fcw, wdown, fg, st)


PROMPT_TILE = 512
SAMPLE_CHUNK = 16
FFN_SAMPLE_CHUNK = 64


def kernel(x_prompt, x_sample, c_prompt, c_sample, state_conv, cache_k_win, cache_v_win,
           state_ffn_conv, norm1_g, norm2_g, w_ada, b_ada, w_in, conv_w, w_conv_out,
           attn_sinks, w_attn_out, w_mix_out, w_up, ffn_conv_w, w_down, final_g):
    depth = w_in.shape[0]
    assert depth == 1, "the final norm is fused into the (single) layer's FFN call"
    nsamp = x_sample.shape[0]
    bsz = x_prompt.shape[0]
    xp, xs = x_prompt, x_sample
    c_all = jnp.concatenate([c_sample, c_prompt], axis=0)
    fg = final_g.reshape(1, D_MODEL)
    outs = [[] for _ in range(8)]
    for layer in range(depth):
        mod3 = _mod_call(c_all, w_ada[layer], b_ada[layer].reshape(1, -1))
        n1g = norm1_g[layer].reshape(1, D_MODEL)
        n2g = norm2_g[layer].reshape(1, D_MODEL)
        win = w_in[layer].astype(_BF)
        wco = w_conv_out[layer].astype(_BF)
        wao = w_attn_out[layer].astype(_BF)
        wmo = w_mix_out[layer].astype(_BF)
        wup = w_up[layer].astype(_BF)
        wdown = w_down[layer].astype(_BF)
        cw = conv_w[layer]
        fcw = ffn_conv_w[layer]
        sinks = attn_sinks[layer]
        fg_l = fg
        def to_dim_major(win_state):
            return jnp.transpose(win_state, (0, 2, 3, 1)).reshape(-1, KV_WIDTH, WINDOW)

        def to_pos_major(t3):
            return jnp.transpose(t3.reshape(-1, N_KV_HEADS, HEAD_DIM, WINDOW), (0, 3, 1, 2))

        x1p, conv_p, k_p, v_p = _mix_prompt_call(
            xp, mod3, nsamp, n1g, win, cw, wco, sinks, wao, wmo, PROMPT_TILE)
        x1s, conv_s, k_s, v_s = _mix_sample_call(
            xs, mod3, n1g, win, cw, wco, sinks, wao, wmo, state_conv[layer],
            to_dim_major(cache_k_win[layer]), to_dim_major(cache_v_win[layer]), SAMPLE_CHUNK)
        xp, ffn_p = _ffn_prompt_call(x1p, mod3, nsamp, n2g, wup, fcw, wdown, fg_l, PROMPT_TILE)
        xs, ffn_s = _ffn_sample_call(x1s, mod3, n2g, wup, fcw, wdown, fg_l,
                                     state_ffn_conv[layer], FFN_SAMPLE_CHUNK)
        for lst, val in zip(outs, (conv_p, to_pos_major(k_p), to_pos_major(v_p), ffn_p,
                                   conv_s, to_pos_major(k_s), to_pos_major(v_s), ffn_s)):
            lst.append(val)
    return (xp, xs) + tuple(jnp.stack(lst, axis=0) for lst in outs)
```

```python
import functools

import jax
import jax.numpy as jnp
from jax import lax
from jax.experimental import pallas as pl
from jax.experimental.pallas import tpu as pltpu

D_MODEL = 1024
HEAD_DIM = 64
N_HEADS = 16
N_KV_HEADS = 4
WINDOW = 128
D_FF = 2816
KV_WIDTH = N_KV_HEADS * HEAD_DIM
IN_WIDTH = 3 * D_MODEL + D_MODEL + 2 * KV_WIDTH + 2 * D_MODEL
RMS_EPS = 1e-6
NEG_INF = -1e30
LANES = 128
HEADS_PER_COL = LANES // HEAD_DIM
N_QCOLS = N_HEADS * HEAD_DIM // LANES
CARRY_ROWS = 8
VMEM_LIMIT = 56 * 1024 * 1024

_OFF_B, _OFF_C, _OFF_X = 0, D_MODEL, 2 * D_MODEL
_OFF_Q = 3 * D_MODEL
_OFF_K = _OFF_Q + D_MODEL
_OFF_V = _OFF_K + KV_WIDTH
_OFF_GA = _OFF_V + KV_WIDTH
_OFF_GB = _OFF_GA + D_MODEL

_BF = jnp.bfloat16
_F32 = jnp.float32
_NT = (((1,), (1,)), ((), ()))


LOG2E = 1.4426950408889634
Q_SCALE = HEAD_DIM ** -0.5 * LOG2E


def _slope(head):
    return 2.0 ** (-8.0 * (head + 1) / N_HEADS) * LOG2E


def _rms(x, g):
    return x * lax.rsqrt(jnp.mean(x * x, axis=-1, keepdims=True) + RMS_EPS) * g


def _dot(a, b):
    return jnp.dot(a, b, preferred_element_type=_F32)


def _gelu(x):
    return 0.5 * x * (1.0 + lax.erf(x * (0.5 ** 0.5)))


def _split_heads(t2d, lo):
    rolled = pltpu.roll(t2d, HEAD_DIM, 1)
    zero = jnp.zeros_like(t2d)
    return (jnp.where(lo, t2d, zero), jnp.where(lo, zero, rolled),
            jnp.where(lo, rolled, zero), jnp.where(lo, zero, t2d))


def _softmax_parts(s, sink):
    m = jnp.maximum(jnp.max(s, axis=-1, keepdims=True), sink)
    p = jnp.exp2(s - m)
    denom = jnp.sum(p, axis=-1, keepdims=True) + jnp.exp2(sink - m)
    return p, 1.0 / denom


def _mod_kernel(c_ref, w_ref, b_ref, o_ref):
    c = c_ref[...]
    a = (c * jax.nn.sigmoid(c)).astype(_BF)
    res = _dot(a, w_ref[...].astype(_BF)) + b_ref[...]
    o_ref[...] = res.reshape(o_ref.shape)


def _mod_call(c_all, w_ada, b_ada):
    n = c_all.shape[0]
    return pl.pallas_call(
        _mod_kernel,
        grid=(6,),
        in_specs=[pl.BlockSpec((n, D_MODEL), lambda j: (0, 0)),
                  pl.BlockSpec((D_MODEL, D_MODEL), lambda j: (0, j)),
                  pl.BlockSpec((1, D_MODEL), lambda j: (0, j))],
        out_specs=pl.BlockSpec((n, 1, D_MODEL), lambda j: (0, 0, j)),
        out_shape=jax.ShapeDtypeStruct((n, 1, 6 * D_MODEL), _F32),
        name="adaln_mod",
    )(c_all, w_ada, b_ada)


def _mod_norm(x, g, sc, sh):
    return (_rms(x, g) * (1.0 + sc) + sh).astype(_BF)


def _mix_prompt_kernel(x_ref, sh_ref, sc_ref, g_ref, xn_ref, shn_ref, scn_ref, n1g_ref,
                       win_ref, cw_ref, wco_ref, sinks_ref, wao_ref, wmo_ref,
                       x1_ref, conv_ref, k_ref, v_ref,
                       u_ext, k_ext, v_ext, o_scr, bias_scr, h_scr, *, tiles_per_seq):
    step = pl.program_id(0)
    i = step % tiles_per_seq
    t = x_ref.shape[0]
    nblk = t // WINDOW

    @pl.when(step == 0)
    def _():
        h_scr[...] = _mod_norm(x_ref[...], n1g_ref[...], sc_ref[...], sh_ref[...])
        row = lax.broadcasted_iota(jnp.int32, (WINDOW, 2 * WINDOW), 0)
        col = lax.broadcasted_iota(jnp.int32, (WINDOW, 2 * WINDOW), 1)
        dist = WINDOW + row - col
        valid = (dist >= 0) & (dist <= WINDOW)
        distf = dist.astype(_F32)
        for head in range(N_HEADS):
            bias_scr[head] = jnp.where(valid, -(_slope(head) * distf), NEG_INF)

    @pl.when(i == 0)
    def _():
        u_ext[0:CARRY_ROWS, :] = jnp.zeros((CARRY_ROWS, D_MODEL), _F32)
        k_ext[:, 0:WINDOW, :] = jnp.zeros((2 * N_KV_HEADS, WINDOW, LANES), _BF)
        v_ext[:, 0:WINDOW, :] = jnp.zeros((2 * N_KV_HEADS, WINDOW, LANES), _BF)

    def proj(lo, width):
        return _dot(h_scr[...], win_ref[:, lo:lo + width])

    def next_tile_norm():
        h_scr[...] = _mod_norm(xn_ref[...], n1g_ref[...], scn_ref[...], shn_ref[...])

    side = {}

    def conv_in():
        u = proj(_OFF_C, D_MODEL) * proj(_OFF_X, D_MODEL)
        u_ext[CARRY_ROWS:CARRY_ROWS + t, :] = u
        cw = cw_ref[...]
        side["uc"] = (u_ext[CARRY_ROWS - 2:CARRY_ROWS - 2 + t, :] * cw[0:1, :]
                      + u_ext[CARRY_ROWS - 1:CARRY_ROWS - 1 + t, :] * cw[1:2, :]
                      + u * cw[2:3, :])
        conv_ref[...] = u[t - 2:t, :]
        u_ext[0:CARRY_ROWS, :] = u[t - CARRY_ROWS:t, :]

    def conv_out():
        pb = proj(_OFF_B, D_MODEL)
        side["ya"] = _dot((pb * side.pop("uc")).astype(_BF), wco_ref[...])

    def gate_a():
        side["ga"] = jax.nn.sigmoid(proj(_OFF_GA, D_MODEL))

    def gate_b():
        side["gb"] = jax.nn.sigmoid(proj(_OFF_GB, D_MODEL))
        next_tile_norm()

    side_jobs = [conv_in, conv_out, gate_a, gate_b]

    q = (proj(_OFF_Q, D_MODEL) * Q_SCALE).astype(_BF)
    kf = proj(_OFF_K, KV_WIDTH)
    vf = proj(_OFF_V, KV_WIDTH)
    k_ref[...] = kf[t - WINDOW:t, :].T
    v_ref[...] = vf[t - WINDOW:t, :].T
    lane_t = lax.broadcasted_iota(jnp.int32, (t, LANES), 1)
    lo = lane_t < HEAD_DIM
    ones_lane = (HEAD_DIM, 0)
    for kc in range(KV_WIDTH // LANES):
        ks = _split_heads(kf[:, kc * LANES:(kc + 1) * LANES], lo)
        vs = _split_heads(vf[:, kc * LANES:(kc + 1) * LANES], lo)
        for j in range(4):
            k_ext[4 * kc + j, WINDOW:WINDOW + t, :] = ks[j].astype(_BF)
            v_one = jnp.where(lane_t == ones_lane[j % HEADS_PER_COL], 1.0, vs[j])
            v_ext[4 * kc + j, WINDOW:WINDOW + t, :] = v_one.astype(_BF)

    first_cols = lax.broadcasted_iota(jnp.int32, (WINDOW, 2 * WINDOW), 1) < WINDOW
    no_prev = first_cols & (i == 0)
    lo_q = lax.broadcasted_iota(jnp.int32, (WINDOW, LANES), 1) < HEAD_DIM
    for blk in range(nblk):
        r0 = blk * WINDOW
        for c in range(N_QCOLS):
            hkv = c // 2
            qc = q[r0:r0 + WINDOW, c * LANES:(c + 1) * LANES]
            o_pair = []
            for par in range(HEADS_PER_COL):
                head = HEADS_PER_COL * c + par
                kk = k_ext[2 * hkv + par, r0:r0 + 2 * WINDOW, :]
                s = lax.dot_general(qc, kk, _NT, preferred_element_type=_F32) + bias_scr[head]
                if blk == 0:
                    s = jnp.where(no_prev, NEG_INF, s)
                sink = sinks_ref[head] * LOG2E
                m = jnp.maximum(jnp.max(s, axis=-1, keepdims=True), sink)
                p = jnp.exp2(s - m).astype(_BF)
                o_h = _dot(p, v_ext[2 * hkv + par, r0:r0 + 2 * WINDOW, :])
                col = ones_lane[par]
                denom = o_h[:, col:col + 1] + jnp.exp2(sink - m)
                o_pair.append(o_h * (1.0 / denom))
            o_c = jnp.where(lo_q, o_pair[0], o_pair[1])
            o_scr[r0:r0 + WINDOW, c * LANES:(c + 1) * LANES] = o_c.astype(_BF)
        for _ in range(-(-len(side_jobs) // (nblk - blk))):
            side_jobs.pop(0)()
    k_ext[:, 0:WINDOW, :] = k_ext[:, t:t + WINDOW, :]
    v_ext[:, 0:WINDOW, :] = v_ext[:, t:t + WINDOW, :]
    yb = _dot(o_scr[...], wao_ref[...])

    mixed = (side["ga"] * side["ya"] + side["gb"] * yb).astype(_BF)
    x1_ref[...] = x_ref[...] + g_ref[...] * _dot(mixed, wmo_ref[...])


def _const_spec(shape):
    nd = len(shape)
    return pl.BlockSpec(shape, lambda *_: (0,) * nd, pipeline_mode=pl.Buffered(1))


def _tile_specs(tile, nt, steps, mod_row0):
    def nxt(s):
        return jnp.minimum(s + 1, steps - 1)

    x_spec = pl.BlockSpec((None, tile, D_MODEL), lambda s: (s // nt, s % nt, 0))
    xn_spec = pl.BlockSpec((None, tile, D_MODEL), lambda s: (nxt(s) // nt, nxt(s) % nt, 0))

    def mod_spec(chunk):
        return pl.BlockSpec((None, 1, D_MODEL), lambda s: (mod_row0 + s // nt, 0, chunk))

    def modn_spec(chunk):
        return pl.BlockSpec((None, 1, D_MODEL), lambda s: (mod_row0 + nxt(s) // nt, 0, chunk))

    return x_spec, xn_spec, mod_spec, modn_spec


def _mix_prompt_call(x, mod3, mod_row0, n1g, win, cw, wco, sinks, wao, wmo, tile):
    bsz, seq, _ = x.shape
    nt = seq // tile
    steps = bsz * nt
    x_spec, xn_spec, mod_spec, modn_spec = _tile_specs(tile, nt, steps, mod_row0)

    def state_spec(d1, d2):
        return pl.BlockSpec((None, d1, d2), lambda s: (s // nt, 0, 0))

    return pl.pallas_call(
        functools.partial(_mix_prompt_kernel, tiles_per_seq=nt),
        grid=(steps,),
        in_specs=[x_spec, mod_spec(0), mod_spec(1), mod_spec(2),
                  xn_spec, modn_spec(0), modn_spec(1),
                  _const_spec((1, D_MODEL)),
                  _const_spec((D_MODEL, IN_WIDTH)),
                  _const_spec((3, D_MODEL)),
                  _const_spec((D_MODEL, D_MODEL)),
                  pl.BlockSpec(memory_space=pltpu.SMEM),
                  _const_spec((D_MODEL, D_MODEL)),
                  _const_spec((D_MODEL, D_MODEL))],
        out_specs=[x_spec,
                   state_spec(2, D_MODEL),
                   state_spec(KV_WIDTH, WINDOW),
                   state_spec(KV_WIDTH, WINDOW)],
        out_shape=[jax.ShapeDtypeStruct((bsz, seq, D_MODEL), _F32),
                   jax.ShapeDtypeStruct((bsz, 2, D_MODEL), _F32),
                   jax.ShapeDtypeStruct((bsz, KV_WIDTH, WINDOW), _F32),
                   jax.ShapeDtypeStruct((bsz, KV_WIDTH, WINDOW), _F32)],
        scratch_shapes=[pltpu.VMEM((CARRY_ROWS + tile, D_MODEL), _F32),
                        pltpu.VMEM((2 * N_KV_HEADS, WINDOW + tile, LANES), _BF),
                        pltpu.VMEM((2 * N_KV_HEADS, WINDOW + tile, LANES), _BF),
                        pltpu.VMEM((tile, D_MODEL), _BF),
                        pltpu.VMEM((N_HEADS, WINDOW, 2 * WINDOW), _F32),
                        pltpu.VMEM((tile, D_MODEL), _BF)],
        compiler_params=pltpu.CompilerParams(
            dimension_semantics=("arbitrary",),
            vmem_limit_bytes=VMEM_LIMIT),
        name="mix_prompt",
    )(x, mod3, mod3, mod3, x, mod3, mod3, n1g, win, cw, wco, sinks, wao, wmo)


def _conv_seq(u2d, st, cw, nseq, tlen):
    width = u2d.shape[-1]
    shape3 = (nseq, tlen, width)
    u3 = u2d.reshape(shape3)
    r1 = pltpu.roll(u2d, 1, 0).reshape(shape3)
    r2 = pltpu.roll(u2d, 2, 0).reshape(shape3)
    tpos = lax.broadcasted_iota(jnp.int32, shape3, 1)
    st0 = jnp.broadcast_to(st[:, 0:1, :], shape3)
    st1 = jnp.broadcast_to(st[:, 1:2, :], shape3)
    um1 = jnp.where(tpos == 0, st1, r1)
    um2 = jnp.where(tpos == 0, st0, jnp.where(tpos == 1, st1, r2))
    y = um2 * cw[0:1, :] + um1 * cw[1:2, :] + u3 * cw[2:3, :]
    return y.reshape(nseq * tlen, width), u3[:, tlen - 2:tlen, :]


def _mix_sample_kernel(x_ref, sh_ref, sc_ref, g_ref, n1g_ref, win_ref, cw_ref, wco_ref,
                       sinks_ref, wao_ref, wmo_ref, st_ref, ck_ref, cv_ref,
                       x1_ref, conv_ref, k_ref, v_ref,
                       kmain, kdrop, vmain, vdrop):
    nseq, tlen, _ = x_ref.shape
    rows = nseq * tlen

    x3 = x_ref[...]
    h3 = _rms(x3, n1g_ref[...]) * (1.0 + sc_ref[...]) + sh_ref[...]
    h = h3.reshape(rows, D_MODEL).astype(_BF)

    def proj(lo, width):
        return _dot(h, win_ref[:, lo:lo + width])

    pb = proj(_OFF_B, D_MODEL)
    u = proj(_OFF_C, D_MODEL) * proj(_OFF_X, D_MODEL)
    uc, new_st = _conv_seq(u, st_ref[...], cw_ref[...], nseq, tlen)
    conv_ref[...] = new_st
    ya = _dot((pb * uc).astype(_BF), wco_ref[...])

    keep = WINDOW - tlen
    kvt_new = proj(_OFF_K, 2 * KV_WIDTH).T
    keep_old = lax.broadcasted_iota(jnp.int32, (KV_WIDTH, WINDOW), 1) < keep
    seq_per_tile = LANES // tlen
    for s in range(nseq):
        tile = s // seq_per_tile
        shift = (keep - (s % seq_per_tile) * tlen) % LANES
        for cache_ref, out_ref, main_scr, drop_scr, base in (
                (ck_ref, k_ref, kmain, kdrop, 0), (cv_ref, v_ref, vmain, vdrop, KV_WIDTH)):
            rolled = pltpu.roll(cache_ref[s], keep, 1)
            new = pltpu.roll(kvt_new[base:base + KV_WIDTH, tile * LANES:(tile + 1) * LANES], shift, 1)
            win = jnp.where(keep_old, rolled, new)
            out_ref[s] = win
            main_scr[s] = win.astype(_BF)
            drop_scr[s] = rolled.astype(_BF)

    qf = proj(_OFF_Q, D_MODEL) * Q_SCALE
    lo = lax.broadcasted_iota(jnp.int32, (rows, LANES), 1) < HEAD_DIM
    zero = jnp.zeros((rows, LANES), _F32)
    heads_per_tile = N_HEADS // (KV_WIDTH // LANES)
    m_rows = heads_per_tile * tlen
    row = lax.broadcasted_iota(jnp.int32, (m_rows, 2 * WINDOW), 0)
    col = lax.broadcasted_iota(jnp.int32, (m_rows, 2 * WINDOW), 1)
    tq = row % tlen
    in_main = col < WINDOW
    dist = jnp.where(in_main, keep + tq - col, 2 * WINDOW + keep + tq - col)
    valid = (in_main & (dist >= 0)) | ((col >= WINDOW + keep) & (dist <= WINDOW))
    distf = dist.astype(_F32)
    row1 = lax.broadcasted_iota(jnp.int32, (m_rows, 1), 0)

    o_cols = [None] * N_QCOLS
    for kc in range(KV_WIDTH // LANES):
        parts = []
        slope = jnp.zeros((m_rows, 1), _F32)
        sink = jnp.zeros((m_rows, 1), _F32)
        for j in range(heads_per_tile):
            head = heads_per_tile * kc + j
            c, par, half = head // HEADS_PER_COL, j % HEADS_PER_COL, j // (heads_per_tile // 2)
            qc = qf[:, c * LANES:(c + 1) * LANES]
            src = qc if par == half else pltpu.roll(qc, HEAD_DIM, 1)
            part = jnp.where(lo, src, zero) if half == 0 else jnp.where(lo, zero, src)
            parts.append(part.reshape(nseq, tlen, LANES))
            mine = (row1 >= j * tlen) & (row1 < (j + 1) * tlen)
            slope = jnp.where(mine, _slope(head), slope)
            sink = jnp.where(mine, sinks_ref[head] * LOG2E, sink)
        lhs = jnp.concatenate(parts, axis=1).astype(_BF)
        bias = jnp.where(valid, -(slope * distf), NEG_INF)
        rsl = slice(kc * LANES, (kc + 1) * LANES)
        wk = jnp.concatenate([kmain[:, rsl, :], kdrop[:, rsl, :]], axis=2)
        wv = jnp.concatenate([vmain[:, rsl, :], vdrop[:, rsl, :]], axis=2)
        sc = jnp.einsum('smk,skn->smn', lhs, wk, preferred_element_type=_F32) + bias
        p, inv = _softmax_parts(sc, sink)
        r = jnp.einsum('smk,snk->smn', p.astype(_BF), wv, preferred_element_type=_F32) * inv
        for m2 in range(heads_per_tile // HEADS_PER_COL):
            c = (heads_per_tile // HEADS_PER_COL) * kc + m2
            half = m2 // (heads_per_tile // (2 * HEADS_PER_COL))
            r0 = r[:, (2 * m2) * tlen:(2 * m2 + 1) * tlen, :].reshape(rows, LANES)
            r1 = r[:, (2 * m2 + 1) * tlen:(2 * m2 + 2) * tlen, :].reshape(rows, LANES)
            if half == 0:
                o_cols[c] = jnp.where(lo, r0, pltpu.roll(r1, HEAD_DIM, 1))
            else:
                o_cols[c] = jnp.where(lo, pltpu.roll(r0, HEAD_DIM, 1), r1)
    o = jnp.concatenate(o_cols, axis=1).astype(_BF)
    yb = _dot(o, wao_ref[...])

    mixed = (jax.nn.sigmoid(proj(_OFF_GA, D_MODEL)) * ya
             + jax.nn.sigmoid(proj(_OFF_GB, D_MODEL)) * yb).astype(_BF)
    y = _dot(mixed, wmo_ref[...]).reshape(nseq, tlen, D_MODEL)
    x1_ref[...] = x3 + g_ref[...] * y


def _mix_sample_call(x, mod3, n1g, win, cw, wco, sinks, wao, wmo, st, ck, cv, chunk):
    nseq, tlen, _ = x.shape
    assert LANES % tlen == 0 and chunk % (LANES // tlen) == 0

    def mod_spec(col):
        return pl.BlockSpec((chunk, 1, D_MODEL), lambda s: (s, 0, col))

    def seq_spec(d1, d2):
        return pl.BlockSpec((chunk, d1, d2), lambda s: (s, 0, 0))

    return pl.pallas_call(
        _mix_sample_kernel,
        grid=(nseq // chunk,),
        in_specs=[seq_spec(tlen, D_MODEL),
                  mod_spec(0), mod_spec(1), mod_spec(2),
                  _const_spec((1, D_MODEL)),
                  _const_spec((D_MODEL, IN_WIDTH)),
                  _const_spec((3, D_MODEL)),
                  _const_spec((D_MODEL, D_MODEL)),
                  pl.BlockSpec(memory_space=pltpu.SMEM),
                  _const_spec((D_MODEL, D_MODEL)),
                  _const_spec((D_MODEL, D_MODEL)),
                  seq_spec(2, D_MODEL),
                  seq_spec(KV_WIDTH, WINDOW),
                  seq_spec(KV_WIDTH, WINDOW)],
        out_specs=[seq_spec(tlen, D_MODEL),
                   seq_spec(2, D_MODEL),
                   seq_spec(KV_WIDTH, WINDOW),
                   seq_spec(KV_WIDTH, WINDOW)],
        out_shape=[jax.ShapeDtypeStruct((nseq, tlen, D_MODEL), _F32),
                   jax.ShapeDtypeStruct((nseq, 2, D_MODEL), _F32),
                   jax.ShapeDtypeStruct((nseq, KV_WIDTH, WINDOW), _F32),
                   jax.ShapeDtypeStruct((nseq, KV_WIDTH, WINDOW), _F32)],
        scratch_shapes=[pltpu.VMEM((chunk, KV_WIDTH, WINDOW), _BF) for _ in range(4)],
        compiler_params=pltpu.CompilerParams(
            dimension_semantics=("arbitrary",),
            vmem_limit_bytes=VMEM_LIMIT),
        name="mix_sample",
    )(x, mod3, mod3, mod3, n1g, win, cw, wco, sinks, wao, wmo, st, ck, cv)


def _ffn_tail(x1, ac, val, g, wdown_ref, fg):
    hmid = (_gelu(ac) * val).astype(_BF)
    x2 = x1 + g * _dot(hmid, wdown_ref[...])
    return _rms(x2, fg)


def _ffn_prompt_kernel(x1_ref, sh_ref, sc_ref, g_ref, n2g_ref, wup_ref, fcw_ref, wdown_ref,
                       fg_ref, y_ref, ffn_ref, a_ext, *, parts):
    i = pl.program_id(1)
    t = x1_ref.shape[0]
    rows = t // parts

    @pl.when(i == 0)
    def _():
        a_ext[0:CARRY_ROWS, :] = jnp.zeros((CARRY_ROWS, D_FF), _F32)

    fcw = fcw_ref[...]

    def up(part):
        r0 = part * rows
        e0 = CARRY_ROWS + r0
        h2 = _mod_norm(x1_ref[r0:r0 + rows, :], n2g_ref[...], sc_ref[...], sh_ref[...])
        a = _dot(h2, wup_ref[:, 0:D_FF])
        val = _dot(h2, wup_ref[:, D_FF:2 * D_FF])
        a_ext[e0:e0 + rows, :] = a
        ac = (a_ext[e0 - 2:e0 - 2 + rows, :] * fcw[0:1, :]
              + a_ext[e0 - 1:e0 - 1 + rows, :] * fcw[1:2, :]
              + a * fcw[2:3, :])
        return ac, val

    def down(part, ac, val):
        r0 = part * rows
        y_ref[r0:r0 + rows, :] = _ffn_tail(x1_ref[r0:r0 + rows, :], ac, val, g_ref[...],
                                           wdown_ref, fg_ref[...])

    for part in range(parts):
        down(part, *up(part))
    ffn_ref[...] = a_ext[CARRY_ROWS + t - 2:CARRY_ROWS + t, :]
    a_ext[0:CARRY_ROWS, :] = a_ext[t:t + CARRY_ROWS, :]


def _ffn_prompt_call(x1, mod3, mod_row0, n2g, wup, fcw, wdown, fg, tile, parts):
    bsz, seq, _ = x1.shape

    def mod_spec(chunk):
        return pl.BlockSpec((None, 1, D_MODEL), lambda b, i: (mod_row0 + b, 0, chunk))

    x_spec = pl.BlockSpec((None, tile, D_MODEL), lambda b, i: (b, i, 0))
    return pl.pallas_call(
        functools.partial(_ffn_prompt_kernel, parts=parts),
        grid=(bsz, seq // tile),
        in_specs=[x_spec, mod_spec(3), mod_spec(4), mod_spec(5),
                  _const_spec((1, D_MODEL)),
                  _const_spec((D_MODEL, 2 * D_FF)),
                  _const_spec((3, D_FF)),
                  _const_spec((D_FF, D_MODEL)),
                  _const_spec((1, D_MODEL))],
        out_specs=[x_spec,
                   pl.BlockSpec((None, 2, D_FF), lambda b, i: (b, 0, 0))],
        out_shape=[jax.ShapeDtypeStruct((bsz, seq, D_MODEL), _F32),
                   jax.ShapeDtypeStruct((bsz, 2, D_FF), _F32)],
        scratch_shapes=[pltpu.VMEM((CARRY_ROWS + tile, D_FF), _F32)],
        compiler_params=pltpu.CompilerParams(
            dimension_semantics=("arbitrary", "arbitrary"),
            vmem_limit_bytes=VMEM_LIMIT),
        name="ffn_prompt",
    )(x1, mod3, mod3, mod3, n2g, wup, fcw, wdown, fg)


def _ffn_sample_kernel(x1_ref, sh_ref, sc_ref, g_ref, n2g_ref, wup_ref, fcw_ref, wdown_ref,
                       fg_ref, st_ref, y_ref, ffn_ref):
    nseq, tlen, _ = x1_ref.shape
    rows = nseq * tlen
    x3 = x1_ref[...]
    h3 = _rms(x3, n2g_ref[...]) * (1.0 + sc_ref[...]) + sh_ref[...]
    h2 = h3.reshape(rows, D_MODEL).astype(_BF)
    a = _dot(h2, wup_ref[:, 0:D_FF])
    val = _dot(h2, wup_ref[:, D_FF:2 * D_FF])
    ac, new_st = _conv_seq(a, st_ref[...], fcw_ref[...], nseq, tlen)
    ffn_ref[...] = new_st
    hmid = (_gelu(ac) * val).astype(_BF)
    y = _dot(hmid, wdown_ref[...]).reshape(nseq, tlen, D_MODEL)
    x2 = x3 + g_ref[...] * y
    y_ref[...] = _rms(x2, fg_ref[...])


def _ffn_sample_call(x1, mod3, n2g, wup, fcw, wdown, fg, st, chunk):
    nseq, tlen, _ = x1.shape

    def mod_spec(col):
        return pl.BlockSpec((chunk, 1, D_MODEL), lambda s: (s, 0, col))

    def seq_spec(d1, d2):
        return pl.BlockSpec((chunk, d1, d2), lambda s: (s, 0, 0))

    return pl.pallas_call(
        _ffn_sample_kernel,
        grid=(nseq // chunk,),
        in_specs=[seq_spec(tlen, D_MODEL),
                  mod_spec(3), mod_spec(4), mod_spec(5),
                  _const_spec((1, D_MODEL)),
                  _const_spec((D_MODEL, 2 * D_FF)),
                  _const_spec((3, D_FF)),
                  _const_spec((D_FF, D_MODEL)),
                  _const_spec((1, D_MODEL)),
                  seq_spec(2, D_FF)],
        out_specs=[seq_spec(tlen, D_MODEL), seq_spec(2, D_FF)],
        out_shape=[jax.ShapeDtypeStruct((nseq, tlen, D_MODEL), _F32),
                   jax.ShapeDtypeStruct((nseq, 2, D_FF), _F32)],
        compiler_params=pltpu.CompilerParams(
            dimension_semantics=("arbitrary",),
            vmem_limit_bytes=VMEM_LIMIT),
        name="ffn_sample",
    )(x1, mod3, mod3, mod3, n2g, wup, fcw, wdown, fg, st)


PROMPT_TILE = 512
FFN_TILE = 1024
FFN_PARTS = 2
SAMPLE_CHUNK = 16
FFN_SAMPLE_CHUNK = 64


def kernel(x_prompt, x_sample, c_prompt, c_sample, state_conv, cache_k_win, cache_v_win,
           state_ffn_conv, norm1_g, norm2_g, w_ada, b_ada, w_in, conv_w, w_conv_out,
           attn_sinks, w_attn_out, w_mix_out, w_up, ffn_conv_w, w_down, final_g):
    depth = w_in.shape[0]
    assert depth == 1, "the final norm is fused into the (single) layer's FFN call"
    nsamp = x_sample.shape[0]
    bsz = x_prompt.shape[0]
    xp, xs = x_prompt, x_sample
    c_all = jnp.concatenate([c_sample, c_prompt], axis=0)
    fg = final_g.reshape(1, D_MODEL)
    outs = [[] for _ in range(8)]
    for layer in range(depth):
        mod3 = _mod_call(c_all, w_ada[layer], b_ada[layer].reshape(1, -1))
        n1g = norm1_g[layer].reshape(1, D_MODEL)
        n2g = norm2_g[layer].reshape(1, D_MODEL)
        win = w_in[layer].astype(_BF)
        wco = w_conv_out[layer].astype(_BF)
        wao = w_attn_out[layer].astype(_BF)
        wmo = w_mix_out[layer].astype(_BF)
        wup = w_up[layer].astype(_BF)
        wdown = w_down[layer].astype(_BF)
        cw = conv_w[layer]
        fcw = ffn_conv_w[layer]
        sinks = attn_sinks[layer]
        fg_l = fg
        def to_dim_major(win_state):
            return jnp.transpose(win_state, (0, 2, 3, 1)).reshape(-1, KV_WIDTH, WINDOW)

        def to_pos_major(t3):
            return jnp.transpose(t3.reshape(-1, N_KV_HEADS, HEAD_DIM, WINDOW), (0, 3, 1, 2))

        x1p, conv_p, k_p, v_p = _mix_prompt_call(
            xp, mod3, nsamp, n1g, win, cw, wco, sinks, wao, wmo, PROMPT_TILE)
        x1s, conv_s, k_s, v_s = _mix_sample_call(
            xs, mod3, n1g, win, cw, wco, sinks, wao, wmo, state_conv[layer],
            to_dim_major(cache_k_win[layer]), to_dim_major(cache_v_win[layer]), SAMPLE_CHUNK)
        xp, ffn_p = _ffn_prompt_call(x1p, mod3, nsamp, n2g, wup, fcw, wdown, fg_l,
                                     FFN_TILE, FFN_PARTS)
        xs, ffn_s = _ffn_sample_call(x1s, mod3, n2g, wup, fcw, wdown, fg_l,
                                     state_ffn_conv[layer], FFN_SAMPLE_CHUNK)
        for lst, val in zip(outs, (conv_p, to_pos_major(k_p), to_pos_major(v_p), ffn_p,
                                   conv_s, to_pos_major(k_s), to_pos_major(v_s), ffn_s)):
            lst.append(val)
    return (xp, xs) + tuple(jnp.stack(lst, axis=0) for lst in outs)
```

```python
import functools

import jax
import jax.numpy as jnp
from jax import lax
from jax.experimental import pallas as pl
from jax.experimental.pallas import tpu as pltpu

D_MODEL = 1024
HEAD_DIM = 64
N_HEADS = 16
N_KV_HEADS = 4
WINDOW = 128
D_FF = 2816
KV_WIDTH = N_KV_HEADS * HEAD_DIM
IN_WIDTH = 3 * D_MODEL + D_MODEL + 2 * KV_WIDTH + 2 * D_MODEL
RMS_EPS = 1e-6
NEG_INF = -1e30
LANES = 128
HEADS_PER_COL = LANES // HEAD_DIM
N_QCOLS = N_HEADS * HEAD_DIM // LANES
CARRY_ROWS = 8
VMEM_LIMIT = 60 * 1024 * 1024

_OFF_B, _OFF_C, _OFF_X = 0, D_MODEL, 2 * D_MODEL
_OFF_Q = 3 * D_MODEL
_OFF_K = _OFF_Q + D_MODEL
_OFF_V = _OFF_K + KV_WIDTH
_OFF_GA = _OFF_V + KV_WIDTH
_OFF_GB = _OFF_GA + D_MODEL

_BF = jnp.bfloat16
_F32 = jnp.float32
_NT = (((1,), (1,)), ((), ()))


LOG2E = 1.4426950408889634
Q_SCALE = HEAD_DIM ** -0.5 * LOG2E


def _slope(head):
    return 2.0 ** (-8.0 * (head + 1) / N_HEADS) * LOG2E


def _rms(x, g):
    return x * lax.rsqrt(jnp.mean(x * x, axis=-1, keepdims=True) + RMS_EPS) * g


def _dot(a, b):
    return jnp.dot(a, b, preferred_element_type=_F32)


def _gelu(x):
    return 0.5 * x * (1.0 + lax.erf(x * (0.5 ** 0.5)))


def _split_heads(t2d, lo):
    rolled = pltpu.roll(t2d, HEAD_DIM, 1)
    zero = jnp.zeros_like(t2d)
    return (jnp.where(lo, t2d, zero), jnp.where(lo, zero, rolled),
            jnp.where(lo, rolled, zero), jnp.where(lo, zero, t2d))


def _softmax_parts(s, sink):
    m = jnp.maximum(jnp.max(s, axis=-1, keepdims=True), sink)
    p = jnp.exp2(s - m)
    denom = jnp.sum(p, axis=-1, keepdims=True) + jnp.exp2(sink - m)
    return p, 1.0 / denom


def _mod_kernel(c_ref, w_ref, b_ref, o_ref):
    c = c_ref[...]
    a = (c * jax.nn.sigmoid(c)).astype(_BF)
    res = _dot(a, w_ref[...].astype(_BF)) + b_ref[...]
    o_ref[...] = res.reshape(o_ref.shape)


def _mod_call(c_all, w_ada, b_ada):
    n = c_all.shape[0]
    return pl.pallas_call(
        _mod_kernel,
        grid=(6,),
        in_specs=[pl.BlockSpec((n, D_MODEL), lambda j: (0, 0)),
                  pl.BlockSpec((D_MODEL, D_MODEL), lambda j: (0, j)),
                  pl.BlockSpec((1, D_MODEL), lambda j: (0, j))],
        out_specs=pl.BlockSpec((n, 1, D_MODEL), lambda j: (0, 0, j)),
        out_shape=jax.ShapeDtypeStruct((n, 1, 6 * D_MODEL), _F32),
        name="adaln_mod",
    )(c_all, w_ada, b_ada)


def _mod_norm(x, g, sc, sh):
    return (_rms(x, g) * (1.0 + sc) + sh).astype(_BF)


def _mix_prompt_kernel(x_ref, sh_ref, sc_ref, g_ref, xn_ref, shn_ref, scn_ref, n1g_ref,
                       win_ref, cw_ref, wco_ref, sinks_ref, wao_ref, wmo_ref, *rest,
                       tiles_per_seq, n_cast):
    cast_in, rest = rest[:n_cast], rest[n_cast:]
    (x1_ref, conv_ref, k_ref, v_ref), rest = rest[:4], rest[4:]
    cast_out, (u_ext, k_ext, v_ext, o_scr, bias_scr, h_scr) = rest[:n_cast], rest[n_cast:]
    step = pl.program_id(0)
    i = step % tiles_per_seq
    t = x_ref.shape[0]
    nblk = t // WINDOW

    for src, dst in zip(cast_in, cast_out):
        dst[...] = src[...].astype(_BF)

    @pl.when(step == 0)
    def _():
        h_scr[...] = _mod_norm(x_ref[...], n1g_ref[...], sc_ref[...], sh_ref[...])
        row = lax.broadcasted_iota(jnp.int32, (WINDOW, 2 * WINDOW), 0)
        col = lax.broadcasted_iota(jnp.int32, (WINDOW, 2 * WINDOW), 1)
        dist = WINDOW + row - col
        valid = (dist >= 0) & (dist <= WINDOW)
        distf = dist.astype(_F32)
        for head in range(N_HEADS):
            bias_scr[head] = jnp.where(valid, -(_slope(head) * distf), NEG_INF)

    @pl.when(i == 0)
    def _():
        u_ext[0:CARRY_ROWS, :] = jnp.zeros((CARRY_ROWS, D_MODEL), _F32)
        k_ext[:, 0:WINDOW, :] = jnp.zeros((2 * N_KV_HEADS, WINDOW, LANES), _BF)
        v_ext[:, 0:WINDOW, :] = jnp.zeros((2 * N_KV_HEADS, WINDOW, LANES), _BF)

    def proj(lo, width):
        return _dot(h_scr[...], win_ref[:, lo:lo + width])

    def next_tile_norm():
        h_scr[...] = _mod_norm(xn_ref[...], n1g_ref[...], scn_ref[...], shn_ref[...])

    side = {}

    def conv_in():
        u = proj(_OFF_C, D_MODEL) * proj(_OFF_X, D_MODEL)
        u_ext[CARRY_ROWS:CARRY_ROWS + t, :] = u
        cw = cw_ref[...]
        side["uc"] = (u_ext[CARRY_ROWS - 2:CARRY_ROWS - 2 + t, :] * cw[0:1, :]
                      + u_ext[CARRY_ROWS - 1:CARRY_ROWS - 1 + t, :] * cw[1:2, :]
                      + u * cw[2:3, :])
        conv_ref[...] = u[t - 2:t, :]
        u_ext[0:CARRY_ROWS, :] = u[t - CARRY_ROWS:t, :]

    def conv_out():
        pb = proj(_OFF_B, D_MODEL)
        side["ya"] = _dot((pb * side.pop("uc")).astype(_BF), wco_ref[...])

    def gate_a():
        side["ga"] = jax.nn.sigmoid(proj(_OFF_GA, D_MODEL))

    def gate_b():
        side["gb"] = jax.nn.sigmoid(proj(_OFF_GB, D_MODEL))
        next_tile_norm()

    side_jobs = [conv_in, conv_out, gate_a, gate_b]

    q = (proj(_OFF_Q, D_MODEL) * Q_SCALE).astype(_BF)
    kf = proj(_OFF_K, KV_WIDTH)
    vf = proj(_OFF_V, KV_WIDTH)
    k_ref[...] = kf[t - WINDOW:t, :].T
    v_ref[...] = vf[t - WINDOW:t, :].T
    lane_t = lax.broadcasted_iota(jnp.int32, (t, LANES), 1)
    lo = lane_t < HEAD_DIM
    ones_lane = (HEAD_DIM, 0)
    for kc in range(KV_WIDTH // LANES):
        ks = _split_heads(kf[:, kc * LANES:(kc + 1) * LANES], lo)
        vs = _split_heads(vf[:, kc * LANES:(kc + 1) * LANES], lo)
        for j in range(4):
            k_ext[4 * kc + j, WINDOW:WINDOW + t, :] = ks[j].astype(_BF)
            v_one = jnp.where(lane_t == ones_lane[j % HEADS_PER_COL], 1.0, vs[j])
            v_ext[4 * kc + j, WINDOW:WINDOW + t, :] = v_one.astype(_BF)

    first_cols = lax.broadcasted_iota(jnp.int32, (WINDOW, 2 * WINDOW), 1) < WINDOW
    no_prev = first_cols & (i == 0)
    lo_q = lax.broadcasted_iota(jnp.int32, (WINDOW, LANES), 1) < HEAD_DIM
    for blk in range(nblk):
        r0 = blk * WINDOW
        for c in range(N_QCOLS):
            hkv = c // 2
            qc = q[r0:r0 + WINDOW, c * LANES:(c + 1) * LANES]
            o_pair = []
            for par in range(HEADS_PER_COL):
                head = HEADS_PER_COL * c + par
                kk = k_ext[2 * hkv + par, r0:r0 + 2 * WINDOW, :]
                s = lax.dot_general(qc, kk, _NT, preferred_element_type=_F32) + bias_scr[head]
                if blk == 0:
                    s = jnp.where(no_prev, NEG_INF, s)
                sink = sinks_ref[head] * LOG2E
                m = jnp.maximum(jnp.max(s, axis=-1, keepdims=True), sink)
                p = jnp.exp2(s - m).astype(_BF)
                o_h = _dot(p, v_ext[2 * hkv + par, r0:r0 + 2 * WINDOW, :])
                col = ones_lane[par]
                denom = o_h[:, col:col + 1] + jnp.exp2(sink - m)
                o_pair.append(o_h * (1.0 / denom))
            o_c = jnp.where(lo_q, o_pair[0], o_pair[1])
            o_scr[r0:r0 + WINDOW, c * LANES:(c + 1) * LANES] = o_c.astype(_BF)
        for _ in range(-(-len(side_jobs) // (nblk - blk))):
            side_jobs.pop(0)()
    k_ext[:, 0:WINDOW, :] = k_ext[:, t:t + WINDOW, :]
    v_ext[:, 0:WINDOW, :] = v_ext[:, t:t + WINDOW, :]
    yb = _dot(o_scr[...], wao_ref[...])

    mixed = (side["ga"] * side["ya"] + side["gb"] * yb).astype(_BF)
    x1_ref[...] = x_ref[...] + g_ref[...] * _dot(mixed, wmo_ref[...])


def _const_spec(shape):
    nd = len(shape)
    return pl.BlockSpec(shape, lambda *_: (0,) * nd, pipeline_mode=pl.Buffered(1))


def _tile_specs(tile, nt, steps, mod_row0):
    def nxt(s):
        return jnp.minimum(s + 1, steps - 1)

    x_spec = pl.BlockSpec((None, tile, D_MODEL), lambda s: (s // nt, s % nt, 0))
    xn_spec = pl.BlockSpec((None, tile, D_MODEL), lambda s: (nxt(s) // nt, nxt(s) % nt, 0))

    def mod_spec(chunk):
        return pl.BlockSpec((None, 1, D_MODEL), lambda s: (mod_row0 + s // nt, 0, chunk))

    def modn_spec(chunk):
        return pl.BlockSpec((None, 1, D_MODEL), lambda s: (mod_row0 + nxt(s) // nt, 0, chunk))

    return x_spec, xn_spec, mod_spec, modn_spec


BF16_SUBLANES = 16


def _cast_stream_spec(shape, steps):
    rows, cols = shape
    for hold in range(1, steps + 1):
        nblocks, rem = divmod(steps, hold)
        if rem == 0 and rows % nblocks == 0 and (rows // nblocks) % BF16_SUBLANES == 0:
            return pl.BlockSpec((rows // nblocks, cols), lambda s: (s // hold, 0))
    raise ValueError(f"no bf16-aligned row split of {shape} over {steps} steps")


def _mix_prompt_call(x, mod3, mod_row0, n1g, win, cw, wco, sinks, wao, wmo, cast_along, tile):
    bsz, seq, _ = x.shape
    nt = seq // tile
    steps = bsz * nt
    x_spec, xn_spec, mod_spec, modn_spec = _tile_specs(tile, nt, steps, mod_row0)
    cast_specs = [_cast_stream_spec(w.shape, steps) for w in cast_along]

    def state_spec(d1, d2):
        return pl.BlockSpec((None, d1, d2), lambda s: (s // nt, 0, 0))

    return pl.pallas_call(
        functools.partial(_mix_prompt_kernel, tiles_per_seq=nt, n_cast=len(cast_along)),
        grid=(steps,),
        in_specs=[x_spec, mod_spec(0), mod_spec(1), mod_spec(2),
                  xn_spec, modn_spec(0), modn_spec(1),
                  _const_spec((1, D_MODEL)),
                  _const_spec((D_MODEL, IN_WIDTH)),
                  _const_spec((3, D_MODEL)),
                  _const_spec((D_MODEL, D_MODEL)),
                  pl.BlockSpec(memory_space=pltpu.SMEM),
                  _const_spec((D_MODEL, D_MODEL)),
                  _const_spec((D_MODEL, D_MODEL))] + cast_specs,
        out_specs=[x_spec,
                   state_spec(2, D_MODEL),
                   state_spec(KV_WIDTH, WINDOW),
                   state_spec(KV_WIDTH, WINDOW)] + cast_specs,
        out_shape=[jax.ShapeDtypeStruct((bsz, seq, D_MODEL), _F32),
                   jax.ShapeDtypeStruct((bsz, 2, D_MODEL), _F32),
                   jax.ShapeDtypeStruct((bsz, KV_WIDTH, WINDOW), _F32),
                   jax.ShapeDtypeStruct((bsz, KV_WIDTH, WINDOW), _F32)]
        + [jax.ShapeDtypeStruct(w.shape, _BF) for w in cast_along],
        scratch_shapes=[pltpu.VMEM((CARRY_ROWS + tile, D_MODEL), _F32),
                        pltpu.VMEM((2 * N_KV_HEADS, WINDOW + tile, LANES), _BF),
                        pltpu.VMEM((2 * N_KV_HEADS, WINDOW + tile, LANES), _BF),
                        pltpu.VMEM((tile, D_MODEL), _BF),
                        pltpu.VMEM((N_HEADS, WINDOW, 2 * WINDOW), _F32),
                        pltpu.VMEM((tile, D_MODEL), _BF)],
        compiler_params=pltpu.CompilerParams(
            dimension_semantics=("arbitrary",),
            vmem_limit_bytes=VMEM_LIMIT),
        name="mix_prompt",
    )(x, mod3, mod3, mod3, x, mod3, mod3, n1g, win, cw, wco, sinks, wao, wmo, *cast_along)


def _conv_seq(u2d, st, cw, nseq, tlen):
    width = u2d.shape[-1]
    shape3 = (nseq, tlen, width)
    u3 = u2d.reshape(shape3)
    r1 = pltpu.roll(u2d, 1, 0).reshape(shape3)
    r2 = pltpu.roll(u2d, 2, 0).reshape(shape3)
    tpos = lax.broadcasted_iota(jnp.int32, shape3, 1)
    st0 = jnp.broadcast_to(st[:, 0:1, :], shape3)
    st1 = jnp.broadcast_to(st[:, 1:2, :], shape3)
    um1 = jnp.where(tpos == 0, st1, r1)
    um2 = jnp.where(tpos == 0, st0, jnp.where(tpos == 1, st1, r2))
    y = um2 * cw[0:1, :] + um1 * cw[1:2, :] + u3 * cw[2:3, :]
    return y.reshape(nseq * tlen, width), u3[:, tlen - 2:tlen, :]


def _mix_sample_kernel(x_ref, sh_ref, sc_ref, g_ref, n1g_ref, win_ref, cw_ref, wco_ref,
                       sinks_ref, wao_ref, wmo_ref, st_ref, ck_ref, cv_ref,
                       x1_ref, conv_ref, k_ref, v_ref,
                       kmain, kdrop, vmain, vdrop):
    nseq, tlen, _ = x_ref.shape
    rows = nseq * tlen

    x3 = x_ref[...]
    h3 = _rms(x3, n1g_ref[...]) * (1.0 + sc_ref[...]) + sh_ref[...]
    h = h3.reshape(rows, D_MODEL).astype(_BF)

    def proj(lo, width):
        return _dot(h, win_ref[:, lo:lo + width])

    pb = proj(_OFF_B, D_MODEL)
    u = proj(_OFF_C, D_MODEL) * proj(_OFF_X, D_MODEL)
    uc, new_st = _conv_seq(u, st_ref[...], cw_ref[...], nseq, tlen)
    conv_ref[...] = new_st
    ya = _dot((pb * uc).astype(_BF), wco_ref[...])

    keep = WINDOW - tlen
    kvt_new = proj(_OFF_K, 2 * KV_WIDTH).T
    keep_old = lax.broadcasted_iota(jnp.int32, (KV_WIDTH, WINDOW), 1) < keep
    seq_per_tile = LANES // tlen
    for s in range(nseq):
        tile = s // seq_per_tile
        shift = (keep - (s % seq_per_tile) * tlen) % LANES
        for cache_ref, out_ref, main_scr, drop_scr, base in (
                (ck_ref, k_ref, kmain, kdrop, 0), (cv_ref, v_ref, vmain, vdrop, KV_WIDTH)):
            rolled = pltpu.roll(cache_ref[s], keep, 1)
            new = pltpu.roll(kvt_new[base:base + KV_WIDTH, tile * LANES:(tile + 1) * LANES], shift, 1)
            win = jnp.where(keep_old, rolled, new)
            out_ref[s] = win
            main_scr[s] = win.astype(_BF)
            drop_scr[s] = rolled.astype(_BF)

    qf = proj(_OFF_Q, D_MODEL) * Q_SCALE
    lo = lax.broadcasted_iota(jnp.int32, (rows, LANES), 1) < HEAD_DIM
    zero = jnp.zeros((rows, LANES), _F32)
    heads_per_tile = N_HEADS // (KV_WIDTH // LANES)
    m_rows = heads_per_tile * tlen
    row = lax.broadcasted_iota(jnp.int32, (m_rows, 2 * WINDOW), 0)
    col = lax.broadcasted_iota(jnp.int32, (m_rows, 2 * WINDOW), 1)
    tq = row % tlen
    in_main = col < WINDOW
    dist = jnp.where(in_main, keep + tq - col, 2 * WINDOW + keep + tq - col)
    valid = (in_main & (dist >= 0)) | ((col >= WINDOW + keep) & (dist <= WINDOW))
    distf = dist.astype(_F32)
    row1 = lax.broadcasted_iota(jnp.int32, (m_rows, 1), 0)

    o_cols = [None] * N_QCOLS
    for kc in range(KV_WIDTH // LANES):
        parts = []
        slope = jnp.zeros((m_rows, 1), _F32)
        sink = jnp.zeros((m_rows, 1), _F32)
        for j in range(heads_per_tile):
            head = heads_per_tile * kc + j
            c, par, half = head // HEADS_PER_COL, j % HEADS_PER_COL, j // (heads_per_tile // 2)
            qc = qf[:, c * LANES:(c + 1) * LANES]
            src = qc if par == half else pltpu.roll(qc, HEAD_DIM, 1)
            part = jnp.where(lo, src, zero) if half == 0 else jnp.where(lo, zero, src)
            parts.append(part.reshape(nseq, tlen, LANES))
            mine = (row1 >= j * tlen) & (row1 < (j + 1) * tlen)
            slope = jnp.where(mine, _slope(head), slope)
            sink = jnp.where(mine, sinks_ref[head] * LOG2E, sink)
        lhs = jnp.concatenate(parts, axis=1).astype(_BF)
        bias = jnp.where(valid, -(slope * distf), NEG_INF)
        rsl = slice(kc * LANES, (kc + 1) * LANES)
        wk = jnp.concatenate([kmain[:, rsl, :], kdrop[:, rsl, :]], axis=2)
        wv = jnp.concatenate([vmain[:, rsl, :], vdrop[:, rsl, :]], axis=2)
        sc = jnp.einsum('smk,skn->smn', lhs, wk, preferred_element_type=_F32) + bias
        p, inv = _softmax_parts(sc, sink)
        r = jnp.einsum('smk,snk->smn', p.astype(_BF), wv, preferred_element_type=_F32) * inv
        for m2 in range(heads_per_tile // HEADS_PER_COL):
            c = (heads_per_tile // HEADS_PER_COL) * kc + m2
            half = m2 // (heads_per_tile // (2 * HEADS_PER_COL))
            r0 = r[:, (2 * m2) * tlen:(2 * m2 + 1) * tlen, :].reshape(rows, LANES)
            r1 = r[:, (2 * m2 + 1) * tlen:(2 * m2 + 2) * tlen, :].reshape(rows, LANES)
            if half == 0:
                o_cols[c] = jnp.where(lo, r0, pltpu.roll(r1, HEAD_DIM, 1))
            else:
                o_cols[c] = jnp.where(lo, pltpu.roll(r0, HEAD_DIM, 1), r1)
    o = jnp.concatenate(o_cols, axis=1).astype(_BF)
    yb = _dot(o, wao_ref[...])

    mixed = (jax.nn.sigmoid(proj(_OFF_GA, D_MODEL)) * ya
             + jax.nn.sigmoid(proj(_OFF_GB, D_MODEL)) * yb).astype(_BF)
    y = _dot(mixed, wmo_ref[...]).reshape(nseq, tlen, D_MODEL)
    x1_ref[...] = x3 + g_ref[...] * y


def _mix_sample_call(x, mod3, n1g, win, cw, wco, sinks, wao, wmo, st, ck, cv, chunk):
    nseq, tlen, _ = x.shape
    assert LANES % tlen == 0 and chunk % (LANES // tlen) == 0

    def mod_spec(col):
        return pl.BlockSpec((chunk, 1, D_MODEL), lambda s: (s, 0, col))

    def seq_spec(d1, d2):
        return pl.BlockSpec((chunk, d1, d2), lambda s: (s, 0, 0))

    return pl.pallas_call(
        _mix_sample_kernel,
        grid=(nseq // chunk,),
        in_specs=[seq_spec(tlen, D_MODEL),
                  mod_spec(0), mod_spec(1), mod_spec(2),
                  _const_spec((1, D_MODEL)),
                  _const_spec((D_MODEL, IN_WIDTH)),
                  _const_spec((3, D_MODEL)),
                  _const_spec((D_MODEL, D_MODEL)),
                  pl.BlockSpec(memory_space=pltpu.SMEM),
                  _const_spec((D_MODEL, D_MODEL)),
                  _const_spec((D_MODEL, D_MODEL)),
                  seq_spec(2, D_MODEL),
                  seq_spec(KV_WIDTH, WINDOW),
                  seq_spec(KV_WIDTH, WINDOW)],
        out_specs=[seq_spec(tlen, D_MODEL),
                   seq_spec(2, D_MODEL),
                   seq_spec(KV_WIDTH, WINDOW),
                   seq_spec(KV_WIDTH, WINDOW)],
        out_shape=[jax.ShapeDtypeStruct((nseq, tlen, D_MODEL), _F32),
                   jax.ShapeDtypeStruct((nseq, 2, D_MODEL), _F32),
                   jax.ShapeDtypeStruct((nseq, KV_WIDTH, WINDOW), _F32),
                   jax.ShapeDtypeStruct((nseq, KV_WIDTH, WINDOW), _F32)],
        scratch_shapes=[pltpu.VMEM((chunk, KV_WIDTH, WINDOW), _BF) for _ in range(4)],
        compiler_params=pltpu.CompilerParams(
            dimension_semantics=("arbitrary",),
            vmem_limit_bytes=VMEM_LIMIT),
        name="mix_sample",
    )(x, mod3, mod3, mod3, n1g, win, cw, wco, sinks, wao, wmo, st, ck, cv)


def _ffn_tail(x1, ac, val, g, wdown_ref, fg):
    hmid = (_gelu(ac) * val).astype(_BF)
    x2 = x1 + g * _dot(hmid, wdown_ref[...])
    return _rms(x2, fg)


def _ffn_prompt_kernel(x1_ref, sh_ref, sc_ref, g_ref, n2g_ref, wup_ref, fcw_ref, wdown_ref,
                       fg_ref, y_ref, ffn_ref, a_ext, *, parts):
    i = pl.program_id(1)
    t = x1_ref.shape[0]
    rows = t // parts

    @pl.when(i == 0)
    def _():
        a_ext[0:CARRY_ROWS, :] = jnp.zeros((CARRY_ROWS, D_FF), _F32)

    fcw = fcw_ref[...]

    def up(part):
        r0 = part * rows
        e0 = CARRY_ROWS + r0
        h2 = _mod_norm(x1_ref[r0:r0 + rows, :], n2g_ref[...], sc_ref[...], sh_ref[...])
        a = _dot(h2, wup_ref[:, 0:D_FF])
        val = _dot(h2, wup_ref[:, D_FF:2 * D_FF])
        a_ext[e0:e0 + rows, :] = a
        ac = (a_ext[e0 - 2:e0 - 2 + rows, :] * fcw[0:1, :]
              + a_ext[e0 - 1:e0 - 1 + rows, :] * fcw[1:2, :]
              + a * fcw[2:3, :])
        return ac, val

    def down(part, ac, val):
        r0 = part * rows
        y_ref[r0:r0 + rows, :] = _ffn_tail(x1_ref[r0:r0 + rows, :], ac, val, g_ref[...],
                                           wdown_ref, fg_ref[...])

    for part in range(parts):
        down(part, *up(part))
    ffn_ref[...] = a_ext[CARRY_ROWS + t - 2:CARRY_ROWS + t, :]
    a_ext[0:CARRY_ROWS, :] = a_ext[t:t + CARRY_ROWS, :]


def _ffn_prompt_call(x1, mod3, mod_row0, n2g, wup, fcw, wdown, fg, tile, parts):
    bsz, seq, _ = x1.shape

    def mod_spec(chunk):
        return pl.BlockSpec((None, 1, D_MODEL), lambda b, i: (mod_row0 + b, 0, chunk))

    x_spec = pl.BlockSpec((None, tile, D_MODEL), lambda b, i: (b, i, 0))
    return pl.pallas_call(
        functools.partial(_ffn_prompt_kernel, parts=parts),
        grid=(bsz, seq // tile),
        in_specs=[x_spec, mod_spec(3), mod_spec(4), mod_spec(5),
                  _const_spec((1, D_MODEL)),
                  _const_spec((D_MODEL, 2 * D_FF)),
                  _const_spec((3, D_FF)),
                  _const_spec((D_FF, D_MODEL)),
                  _const_spec((1, D_MODEL))],
        out_specs=[x_spec,
                   pl.BlockSpec((None, 2, D_FF), lambda b, i: (b, 0, 0))],
        out_shape=[jax.ShapeDtypeStruct((bsz, seq, D_MODEL), _F32),
                   jax.ShapeDtypeStruct((bsz, 2, D_FF), _F32)],
        scratch_shapes=[pltpu.VMEM((CARRY_ROWS + tile, D_FF), _F32)],
        compiler_params=pltpu.CompilerParams(
            dimension_semantics=("arbitrary", "arbitrary"),
            vmem_limit_bytes=VMEM_LIMIT),
        name="ffn_prompt",
    )(x1, mod3, mod3, mod3, n2g, wup, fcw, wdown, fg)


def _ffn_sample_kernel(x1_ref, sh_ref, sc_ref, g_ref, n2g_ref, wup_ref, fcw_ref, wdown_ref,
                       fg_ref, st_ref, y_ref, ffn_ref):
    nseq, tlen, _ = x1_ref.shape
    rows = nseq * tlen
    x3 = x1_ref[...]
    h3 = _rms(x3, n2g_ref[...]) * (1.0 + sc_ref[...]) + sh_ref[...]
    h2 = h3.reshape(rows, D_MODEL).astype(_BF)
    a = _dot(h2, wup_ref[:, 0:D_FF])
    val = _dot(h2, wup_ref[:, D_FF:2 * D_FF])
    ac, new_st = _conv_seq(a, st_ref[...], fcw_ref[...], nseq, tlen)
    ffn_ref[...] = new_st
    hmid = (_gelu(ac) * val).astype(_BF)
    y = _dot(hmid, wdown_ref[...]).reshape(nseq, tlen, D_MODEL)
    x2 = x3 + g_ref[...] * y
    y_ref[...] = _rms(x2, fg_ref[...])


def _ffn_sample_call(x1, mod3, n2g, wup, fcw, wdown, fg, st, chunk):
    nseq, tlen, _ = x1.shape

    def mod_spec(col):
        return pl.BlockSpec((chunk, 1, D_MODEL), lambda s: (s, 0, col))

    def seq_spec(d1, d2):
        return pl.BlockSpec((chunk, d1, d2), lambda s: (s, 0, 0))

    return pl.pallas_call(
        _ffn_sample_kernel,
        grid=(nseq // chunk,),
        in_specs=[seq_spec(tlen, D_MODEL),
                  mod_spec(3), mod_spec(4), mod_spec(5),
                  _const_spec((1, D_MODEL)),
                  _const_spec((D_MODEL, 2 * D_FF)),
                  _const_spec((3, D_FF)),
                  _const_spec((D_FF, D_MODEL)),
                  _const_spec((1, D_MODEL)),
                  seq_spec(2, D_FF)],
        out_specs=[seq_spec(tlen, D_MODEL), seq_spec(2, D_FF)],
        out_shape=[jax.ShapeDtypeStruct((nseq, tlen, D_MODEL), _F32),
                   jax.ShapeDtypeStruct((nseq, 2, D_FF), _F32)],
        compiler_params=pltpu.CompilerParams(
            dimension_semantics=("arbitrary",),
            vmem_limit_bytes=VMEM_LIMIT),
        name="ffn_sample",
    )(x1, mod3, mod3, mod3, n2g, wup, fcw, wdown, fg, st)


PROMPT_TILE = 512
FFN_TILE = 1024
FFN_PARTS = 2
SAMPLE_CHUNK = 16
FFN_SAMPLE_CHUNK = 64


def kernel(x_prompt, x_sample, c_prompt, c_sample, state_conv, cache_k_win, cache_v_win,
           state_ffn_conv, norm1_g, norm2_g, w_ada, b_ada, w_in, conv_w, w_conv_out,
           attn_sinks, w_attn_out, w_mix_out, w_up, ffn_conv_w, w_down, final_g):
    depth = w_in.shape[0]
    assert depth == 1, "the final norm is fused into the (single) layer's FFN call"
    nsamp = x_sample.shape[0]
    bsz = x_prompt.shape[0]
    xp, xs = x_prompt, x_sample
    c_all = jnp.concatenate([c_sample, c_prompt], axis=0)
    fg = final_g.reshape(1, D_MODEL)
    outs = [[] for _ in range(8)]
    for layer in range(depth):
        mod3 = _mod_call(c_all, w_ada[layer], b_ada[layer].reshape(1, -1))
        n1g = norm1_g[layer].reshape(1, D_MODEL)
        n2g = norm2_g[layer].reshape(1, D_MODEL)
        win = w_in[layer].astype(_BF)
        wco = w_conv_out[layer].astype(_BF)
        wao = w_attn_out[layer].astype(_BF)
        wmo = w_mix_out[layer].astype(_BF)
        cw = conv_w[layer]
        fcw = ffn_conv_w[layer]
        sinks = attn_sinks[layer]
        fg_l = fg
        def to_dim_major(win_state):
            return jnp.transpose(win_state, (0, 2, 3, 1)).reshape(-1, KV_WIDTH, WINDOW)

        def to_pos_major(t3):
            return jnp.transpose(t3.reshape(-1, N_KV_HEADS, HEAD_DIM, WINDOW), (0, 3, 1, 2))

        x1p, conv_p, k_p, v_p, wup, wdown = _mix_prompt_call(
            xp, mod3, nsamp, n1g, win, cw, wco, sinks, wao, wmo,
            (w_up[layer], w_down[layer]), PROMPT_TILE)
        x1s, conv_s, k_s, v_s = _mix_sample_call(
            xs, mod3, n1g, win, cw, wco, sinks, wao, wmo, state_conv[layer],
            to_dim_major(cache_k_win[layer]), to_dim_major(cache_v_win[layer]), SAMPLE_CHUNK)
        xp, ffn_p = _ffn_prompt_call(x1p, mod3, nsamp, n2g, wup, fcw, wdown, fg_l,
                                     FFN_TILE, FFN_PARTS)
        xs, ffn_s = _ffn_sample_call(x1s, mod3, n2g, wup, fcw, wdown, fg_l,
                                     state_ffn_conv[layer], FFN_SAMPLE_CHUNK)
        for lst, val in zip(outs, (conv_p, to_pos_major(k_p), to_pos_major(v_p), ffn_p,
                                   conv_s, to_pos_major(k_s), to_pos_major(v_s), ffn_s)):
            lst.append(val)
    return (xp, xs) + tuple(jnp.stack(lst, axis=0) for lst in outs)
```

```python
import functools

import jax
import jax.numpy as jnp
from jax import lax
from jax.experimental import pallas as pl
from jax.experimental.pallas import tpu as pltpu

D_MODEL = 1024
HEAD_DIM = 64
N_HEADS = 16
N_KV_HEADS = 4
WINDOW = 128
D_FF = 2816
KV_WIDTH = N_KV_HEADS * HEAD_DIM
IN_WIDTH = 3 * D_MODEL + D_MODEL + 2 * KV_WIDTH + 2 * D_MODEL
RMS_EPS = 1e-6
NEG_INF = -1e30
LANES = 128
HEADS_PER_COL = LANES // HEAD_DIM
N_QCOLS = N_HEADS * HEAD_DIM // LANES
CARRY_ROWS = 8
VMEM_LIMIT = 60 * 1024 * 1024

_OFF_B, _OFF_C, _OFF_X = 0, D_MODEL, 2 * D_MODEL
_OFF_Q = 3 * D_MODEL
_OFF_K = _OFF_Q + D_MODEL
_OFF_V = _OFF_K + KV_WIDTH
_OFF_GA = _OFF_V + KV_WIDTH
_OFF_GB = _OFF_GA + D_MODEL

_BF = jnp.bfloat16
_F32 = jnp.float32
_NT = (((1,), (1,)), ((), ()))


LOG2E = 1.4426950408889634
Q_SCALE = HEAD_DIM ** -0.5 * LOG2E


def _slope(head):
    return 2.0 ** (-8.0 * (head + 1) / N_HEADS) * LOG2E


def _rms(x, g):
    return x * lax.rsqrt(jnp.mean(x * x, axis=-1, keepdims=True) + RMS_EPS) * g


def _dot(a, b):
    return jnp.dot(a, b, preferred_element_type=_F32)


def _gelu(x):
    return 0.5 * x * (1.0 + lax.erf(x * (0.5 ** 0.5)))


def _split_heads(t2d, lo):
    rolled = pltpu.roll(t2d, HEAD_DIM, 1)
    zero = jnp.zeros_like(t2d)
    return (jnp.where(lo, t2d, zero), jnp.where(lo, zero, rolled),
            jnp.where(lo, rolled, zero), jnp.where(lo, zero, t2d))


def _softmax_parts(s, sink):
    m = jnp.maximum(jnp.max(s, axis=-1, keepdims=True), sink)
    p = jnp.exp2(s - m)
    denom = jnp.sum(p, axis=-1, keepdims=True) + jnp.exp2(sink - m)
    return p, 1.0 / denom


def _mod_kernel(c_ref, w_ref, b_ref, o_ref):
    c = c_ref[...]
    a = (c * jax.nn.sigmoid(c)).astype(_BF)
    res = _dot(a, w_ref[...].astype(_BF)) + b_ref[...]
    o_ref[...] = res.reshape(o_ref.shape)


def _mod_call(c_all, w_ada, b_ada):
    n = c_all.shape[0]
    return pl.pallas_call(
        _mod_kernel,
        grid=(6,),
        in_specs=[pl.BlockSpec((n, D_MODEL), lambda j: (0, 0)),
                  pl.BlockSpec((D_MODEL, D_MODEL), lambda j: (0, j)),
                  pl.BlockSpec((1, D_MODEL), lambda j: (0, j))],
        out_specs=pl.BlockSpec((n, 1, D_MODEL), lambda j: (0, 0, j)),
        out_shape=jax.ShapeDtypeStruct((n, 1, 6 * D_MODEL), _F32),
        name="adaln_mod",
    )(c_all, w_ada, b_ada)


def _mod_norm(x, g, sc, sh):
    return (_rms(x, g) * (1.0 + sc) + sh).astype(_BF)


def _mix_prompt_kernel(x_ref, sh_ref, sc_ref, g_ref, xn_ref, shn_ref, scn_ref, n1g_ref,
                       win_ref, cw_ref, wco_ref, sinks_ref, wao_ref, wmo_ref, *rest,
                       tiles_per_seq, n_cast):
    cast_in, rest = rest[:n_cast], rest[n_cast:]
    (x1_ref, conv_ref, k_ref, v_ref), rest = rest[:4], rest[4:]
    cast_out, (u_ext, k_ext, v_ext, o_scr, bias_scr, h_scr) = rest[:n_cast], rest[n_cast:]
    step = pl.program_id(0)
    i = step % tiles_per_seq
    t = x_ref.shape[0]
    nblk = t // WINDOW

    for src, dst in zip(cast_in, cast_out):
        dst[...] = src[...].astype(_BF)

    @pl.when(step == 0)
    def _():
        h_scr[...] = _mod_norm(x_ref[...], n1g_ref[...], sc_ref[...], sh_ref[...])
        row = lax.broadcasted_iota(jnp.int32, (WINDOW, 2 * WINDOW), 0)
        col = lax.broadcasted_iota(jnp.int32, (WINDOW, 2 * WINDOW), 1)
        dist = WINDOW + row - col
        valid = (dist >= 0) & (dist <= WINDOW)
        distf = dist.astype(_F32)
        for head in range(N_HEADS):
            bias_scr[head] = jnp.where(valid, -(_slope(head) * distf), NEG_INF)

    @pl.when(i == 0)
    def _():
        u_ext[0:CARRY_ROWS, :] = jnp.zeros((CARRY_ROWS, D_MODEL), _F32)
        k_ext[:, 0:WINDOW, :] = jnp.zeros((2 * N_KV_HEADS, WINDOW, LANES), _BF)
        v_ext[:, 0:WINDOW, :] = jnp.zeros((2 * N_KV_HEADS, WINDOW, LANES), _BF)

    def proj(lo, width):
        return _dot(h_scr[...], win_ref[:, lo:lo + width])

    def next_tile_norm():
        h_scr[...] = _mod_norm(xn_ref[...], n1g_ref[...], scn_ref[...], shn_ref[...])

    side = {}

    def conv_in():
        u = proj(_OFF_C, D_MODEL) * proj(_OFF_X, D_MODEL)
        u_ext[CARRY_ROWS:CARRY_ROWS + t, :] = u
        cw = cw_ref[...]
        side["uc"] = (u_ext[CARRY_ROWS - 2:CARRY_ROWS - 2 + t, :] * cw[0:1, :]
                      + u_ext[CARRY_ROWS - 1:CARRY_ROWS - 1 + t, :] * cw[1:2, :]
                      + u * cw[2:3, :])
        conv_ref[...] = u[t - 2:t, :]
        u_ext[0:CARRY_ROWS, :] = u[t - CARRY_ROWS:t, :]

    def conv_out():
        pb = proj(_OFF_B, D_MODEL)
        side["ya"] = _dot((pb * side.pop("uc")).astype(_BF), wco_ref[...])

    def gate_a():
        side["ga"] = jax.nn.sigmoid(proj(_OFF_GA, D_MODEL))

    def gate_b():
        side["gb"] = jax.nn.sigmoid(proj(_OFF_GB, D_MODEL))
        next_tile_norm()

    side_jobs = [conv_in, conv_out, gate_a, gate_b]

    q = (proj(_OFF_Q, D_MODEL) * Q_SCALE).astype(_BF)
    kf = proj(_OFF_K, KV_WIDTH)
    vf = proj(_OFF_V, KV_WIDTH)
    k_ref[...] = kf[t - WINDOW:t, :].T
    v_ref[...] = vf[t - WINDOW:t, :].T
    lane_t = lax.broadcasted_iota(jnp.int32, (t, LANES), 1)
    lo = lane_t < HEAD_DIM
    ones_lane = (HEAD_DIM, 0)
    for kc in range(KV_WIDTH // LANES):
        ks = _split_heads(kf[:, kc * LANES:(kc + 1) * LANES], lo)
        vs = _split_heads(vf[:, kc * LANES:(kc + 1) * LANES], lo)
        for j in range(4):
            k_ext[4 * kc + j, WINDOW:WINDOW + t, :] = ks[j].astype(_BF)
            v_one = jnp.where(lane_t == ones_lane[j % HEADS_PER_COL], 1.0, vs[j])
            v_ext[4 * kc + j, WINDOW:WINDOW + t, :] = v_one.astype(_BF)

    first_cols = lax.broadcasted_iota(jnp.int32, (WINDOW, 2 * WINDOW), 1) < WINDOW
    no_prev = first_cols & (i == 0)
    lo_q = lax.broadcasted_iota(jnp.int32, (WINDOW, LANES), 1) < HEAD_DIM
    for blk in range(nblk):
        r0 = blk * WINDOW
        for c in range(N_QCOLS):
            hkv = c // 2
            qc = q[r0:r0 + WINDOW, c * LANES:(c + 1) * LANES]
            o_pair = []
            for par in range(HEADS_PER_COL):
                head = HEADS_PER_COL * c + par
                kk = k_ext[2 * hkv + par, r0:r0 + 2 * WINDOW, :]
                s = lax.dot_general(qc, kk, _NT, preferred_element_type=_F32) + bias_scr[head]
                if blk == 0:
                    s = jnp.where(no_prev, NEG_INF, s)
                sink = sinks_ref[head] * LOG2E
                m = jnp.maximum(jnp.max(s, axis=-1, keepdims=True), sink)
                p = jnp.exp2(s - m).astype(_BF)
                o_h = _dot(p, v_ext[2 * hkv + par, r0:r0 + 2 * WINDOW, :])
                col = ones_lane[par]
                denom = o_h[:, col:col + 1] + jnp.exp2(sink - m)
                o_pair.append(o_h * (1.0 / denom))
            o_c = jnp.where(lo_q, o_pair[0], o_pair[1])
            o_scr[r0:r0 + WINDOW, c * LANES:(c + 1) * LANES] = o_c.astype(_BF)
        for _ in range(-(-len(side_jobs) // (nblk - blk))):
            side_jobs.pop(0)()
    k_ext[:, 0:WINDOW, :] = k_ext[:, t:t + WINDOW, :]
    v_ext[:, 0:WINDOW, :] = v_ext[:, t:t + WINDOW, :]
    yb = _dot(o_scr[...], wao_ref[...])

    mixed = (side["ga"] * side["ya"] + side["gb"] * yb).astype(_BF)
    x1_ref[...] = x_ref[...] + g_ref[...] * _dot(mixed, wmo_ref[...])


def _const_spec(shape):
    nd = len(shape)
    return pl.BlockSpec(shape, lambda *_: (0,) * nd, pipeline_mode=pl.Buffered(1))


def _tile_specs(tile, nt, steps, mod_row0):
    def nxt(s):
        return jnp.minimum(s + 1, steps - 1)

    x_spec = pl.BlockSpec((None, tile, D_MODEL), lambda s: (s // nt, s % nt, 0))
    xn_spec = pl.BlockSpec((None, tile, D_MODEL), lambda s: (nxt(s) // nt, nxt(s) % nt, 0))

    def mod_spec(chunk):
        return pl.BlockSpec((None, 1, D_MODEL), lambda s: (mod_row0 + s // nt, 0, chunk))

    def modn_spec(chunk):
        return pl.BlockSpec((None, 1, D_MODEL), lambda s: (mod_row0 + nxt(s) // nt, 0, chunk))

    return x_spec, xn_spec, mod_spec, modn_spec


BF16_SUBLANES = 16


def _cast_stream_spec(shape, steps):
    rows, cols = shape
    for hold in range(1, steps + 1):
        nblocks, rem = divmod(steps, hold)
        if rem == 0 and rows % nblocks == 0 and (rows // nblocks) % BF16_SUBLANES == 0:
            return pl.BlockSpec((rows // nblocks, cols), lambda s: (s // hold, 0))
    raise ValueError(f"no bf16-aligned row split of {shape} over {steps} steps")


def _mix_prompt_call(x, mod3, mod_row0, n1g, win, cw, wco, sinks, wao, wmo, cast_along, tile):
    bsz, seq, _ = x.shape
    nt = seq // tile
    steps = bsz * nt
    x_spec, xn_spec, mod_spec, modn_spec = _tile_specs(tile, nt, steps, mod_row0)
    cast_specs = [_cast_stream_spec(w.shape, steps) for w in cast_along]

    def state_spec(d1, d2):
        return pl.BlockSpec((None, d1, d2), lambda s: (s // nt, 0, 0))

    return pl.pallas_call(
        functools.partial(_mix_prompt_kernel, tiles_per_seq=nt, n_cast=len(cast_along)),
        grid=(steps,),
        in_specs=[x_spec, mod_spec(0), mod_spec(1), mod_spec(2),
                  xn_spec, modn_spec(0), modn_spec(1),
                  _const_spec((1, D_MODEL)),
                  _const_spec((D_MODEL, IN_WIDTH)),
                  _const_spec((3, D_MODEL)),
                  _const_spec((D_MODEL, D_MODEL)),
                  pl.BlockSpec(memory_space=pltpu.SMEM),
                  _const_spec((D_MODEL, D_MODEL)),
                  _const_spec((D_MODEL, D_MODEL))] + cast_specs,
        out_specs=[x_spec,
                   state_spec(2, D_MODEL),
                   state_spec(KV_WIDTH, WINDOW),
                   state_spec(KV_WIDTH, WINDOW)] + cast_specs,
        out_shape=[jax.ShapeDtypeStruct((bsz, seq, D_MODEL), _F32),
                   jax.ShapeDtypeStruct((bsz, 2, D_MODEL), _F32),
                   jax.ShapeDtypeStruct((bsz, KV_WIDTH, WINDOW), _F32),
                   jax.ShapeDtypeStruct((bsz, KV_WIDTH, WINDOW), _F32)]
        + [jax.ShapeDtypeStruct(w.shape, _BF) for w in cast_along],
        scratch_shapes=[pltpu.VMEM((CARRY_ROWS + tile, D_MODEL), _F32),
                        pltpu.VMEM((2 * N_KV_HEADS, WINDOW + tile, LANES), _BF),
                        pltpu.VMEM((2 * N_KV_HEADS, WINDOW + tile, LANES), _BF),
                        pltpu.VMEM((tile, D_MODEL), _BF),
                        pltpu.VMEM((N_HEADS, WINDOW, 2 * WINDOW), _F32),
                        pltpu.VMEM((tile, D_MODEL), _BF)],
        compiler_params=pltpu.CompilerParams(
            dimension_semantics=("arbitrary",),
            vmem_limit_bytes=VMEM_LIMIT),
        name="mix_prompt",
    )(x, mod3, mod3, mod3, x, mod3, mod3, n1g, win, cw, wco, sinks, wao, wmo, *cast_along)


def _conv_seq(u2d, st, cw, nseq, tlen):
    width = u2d.shape[-1]
    shape3 = (nseq, tlen, width)
    u3 = u2d.reshape(shape3)
    r1 = pltpu.roll(u2d, 1, 0).reshape(shape3)
    r2 = pltpu.roll(u2d, 2, 0).reshape(shape3)
    tpos = lax.broadcasted_iota(jnp.int32, shape3, 1)
    st0 = jnp.broadcast_to(st[:, 0:1, :], shape3)
    st1 = jnp.broadcast_to(st[:, 1:2, :], shape3)
    um1 = jnp.where(tpos == 0, st1, r1)
    um2 = jnp.where(tpos == 0, st0, jnp.where(tpos == 1, st1, r2))
    y = um2 * cw[0:1, :] + um1 * cw[1:2, :] + u3 * cw[2:3, :]
    return y.reshape(nseq * tlen, width), u3[:, tlen - 2:tlen, :]


def _mix_sample_kernel(x_ref, sh_ref, sc_ref, g_ref, n1g_ref, win_ref, cw_ref, wco_ref,
                       sinks_ref, wao_ref, wmo_ref, st_ref, ck_ref, cv_ref,
                       x1_ref, conv_ref, k_ref, v_ref,
                       kmain, kdrop, vmain, vdrop):
    nseq, tlen, _ = x_ref.shape
    rows = nseq * tlen

    x3 = x_ref[...]
    h3 = _rms(x3, n1g_ref[...]) * (1.0 + sc_ref[...]) + sh_ref[...]
    hf = h3.reshape(rows, D_MODEL)
    h_now = [hf.astype(_BF)]

    def proj(lo, width):
        return _dot(h_now[0], win_ref[:, lo:lo + width])

    def order_after(tile_f32):
        bits = pltpu.bitcast(tile_f32[0:8, :], jnp.uint32)
        zero = ((bits >> 16) >> 16).astype(_F32)
        h_now[0] = (hf + jnp.tile(zero, (rows // 8, D_MODEL // LANES))).astype(_BF)

    side = {}

    def job_q():
        side["q"] = proj(_OFF_Q, D_MODEL) * Q_SCALE

    def job_conv_in():
        u = proj(_OFF_C, D_MODEL) * proj(_OFF_X, D_MODEL)
        side["uc"], new_st = _conv_seq(u, st_ref[...], cw_ref[...], nseq, tlen)
        conv_ref[...] = new_st

    def job_gate_b():
        side["pb"] = proj(_OFF_B, D_MODEL)

    def job_conv_out():
        side["ya"] = _dot((side.pop("pb") * side.pop("uc")).astype(_BF), wco_ref[...])

    def job_ga():
        side["ga"] = jax.nn.sigmoid(proj(_OFF_GA, D_MODEL))

    def job_gb():
        side["gb"] = jax.nn.sigmoid(proj(_OFF_GB, D_MODEL))

    keep = WINDOW - tlen
    kvt_new = proj(_OFF_K, 2 * KV_WIDTH).T
    keep_old = lax.broadcasted_iota(jnp.int32, (KV_WIDTH, WINDOW), 1) < keep
    seq_per_tile = LANES // tlen

    def window_update(s):
        tile = s // seq_per_tile
        shift = (keep - (s % seq_per_tile) * tlen) % LANES
        for cache_ref, out_ref, main_scr, drop_scr, base in (
                (ck_ref, k_ref, kmain, kdrop, 0), (cv_ref, v_ref, vmain, vdrop, KV_WIDTH)):
            rolled = pltpu.roll(cache_ref[s], keep, 1)
            new = pltpu.roll(kvt_new[base:base + KV_WIDTH, tile * LANES:(tile + 1) * LANES], shift, 1)
            win = jnp.where(keep_old, rolled, new)
            out_ref[s] = win
            main_scr[s] = win.astype(_BF)
            drop_scr[s] = rolled.astype(_BF)
        return win

    window_jobs = [job_q, job_conv_in, job_gate_b, job_conv_out]
    group = -(-nseq // len(window_jobs))
    for s in range(nseq):
        last_win = window_update(s)
        if window_jobs and (s + 1) % group == 0:
            order_after(last_win)
            window_jobs.pop(0)()
    while window_jobs:
        window_jobs.pop(0)()
    job_ga()
    attn_jobs = [job_gb]

    qf = side.pop("q")
    lo = lax.broadcasted_iota(jnp.int32, (rows, LANES), 1) < HEAD_DIM
    zero = jnp.zeros((rows, LANES), _F32)
    heads_per_tile = N_HEADS // (KV_WIDTH // LANES)
    m_rows = heads_per_tile * tlen
    row = lax.broadcasted_iota(jnp.int32, (m_rows, 2 * WINDOW), 0)
    col = lax.broadcasted_iota(jnp.int32, (m_rows, 2 * WINDOW), 1)
    tq = row % tlen
    in_main = col < WINDOW
    dist = jnp.where(in_main, keep + tq - col, 2 * WINDOW + keep + tq - col)
    valid = (in_main & (dist >= 0)) | ((col >= WINDOW + keep) & (dist <= WINDOW))
    distf = dist.astype(_F32)
    row1 = lax.broadcasted_iota(jnp.int32, (m_rows, 1), 0)

    o_cols = [None] * N_QCOLS
    for kc in range(KV_WIDTH // LANES):
        parts = []
        slope = jnp.zeros((m_rows, 1), _F32)
        sink = jnp.zeros((m_rows, 1), _F32)
        for j in range(heads_per_tile):
            head = heads_per_tile * kc + j
            c, par, half = head // HEADS_PER_COL, j % HEADS_PER_COL, j // (heads_per_tile // 2)
            qc = qf[:, c * LANES:(c + 1) * LANES]
            src = qc if par == half else pltpu.roll(qc, HEAD_DIM, 1)
            part = jnp.where(lo, src, zero) if half == 0 else jnp.where(lo, zero, src)
            parts.append(part.reshape(nseq, tlen, LANES))
            mine = (row1 >= j * tlen) & (row1 < (j + 1) * tlen)
            slope = jnp.where(mine, _slope(head), slope)
            sink = jnp.where(mine, sinks_ref[head] * LOG2E, sink)
        lhs = jnp.concatenate(parts, axis=1).astype(_BF)
        bias = jnp.where(valid, -(slope * distf), NEG_INF)
        rsl = slice(kc * LANES, (kc + 1) * LANES)
        wk = jnp.concatenate([kmain[:, rsl, :], kdrop[:, rsl, :]], axis=2)
        wv = jnp.concatenate([vmain[:, rsl, :], vdrop[:, rsl, :]], axis=2)
        sc = jnp.einsum('smk,skn->smn', lhs, wk, preferred_element_type=_F32) + bias
        p, inv = _softmax_parts(sc, sink)
        r = jnp.einsum('smk,snk->smn', p.astype(_BF), wv, preferred_element_type=_F32) * inv
        for m2 in range(heads_per_tile // HEADS_PER_COL):
            c = (heads_per_tile // HEADS_PER_COL) * kc + m2
            half = m2 // (heads_per_tile // (2 * HEADS_PER_COL))
            r0 = r[:, (2 * m2) * tlen:(2 * m2 + 1) * tlen, :].reshape(rows, LANES)
            r1 = r[:, (2 * m2 + 1) * tlen:(2 * m2 + 2) * tlen, :].reshape(rows, LANES)
            if half == 0:
                o_cols[c] = jnp.where(lo, r0, pltpu.roll(r1, HEAD_DIM, 1))
            else:
                o_cols[c] = jnp.where(lo, pltpu.roll(r0, HEAD_DIM, 1), r1)
        if attn_jobs:
            order_after(o_cols[c])
            attn_jobs.pop(0)()
    while attn_jobs:
        attn_jobs.pop(0)()
    o = jnp.concatenate(o_cols, axis=1).astype(_BF)
    yb = _dot(o, wao_ref[...])

    mixed = (side["ga"] * side["ya"] + side["gb"] * yb).astype(_BF)
    y = _dot(mixed, wmo_ref[...]).reshape(nseq, tlen, D_MODEL)
    x1_ref[...] = x3 + g_ref[...] * y


def _mix_sample_call(x, mod3, n1g, win, cw, wco, sinks, wao, wmo, st, ck, cv, chunk):
    nseq, tlen, _ = x.shape
    assert LANES % tlen == 0 and chunk % (LANES // tlen) == 0

    def mod_spec(col):
        return pl.BlockSpec((chunk, 1, D_MODEL), lambda s: (s, 0, col))

    def seq_spec(d1, d2):
        return pl.BlockSpec((chunk, d1, d2), lambda s: (s, 0, 0))

    return pl.pallas_call(
        _mix_sample_kernel,
        grid=(nseq // chunk,),
        in_specs=[seq_spec(tlen, D_MODEL),
                  mod_spec(0), mod_spec(1), mod_spec(2),
                  _const_spec((1, D_MODEL)),
                  _const_spec((D_MODEL, IN_WIDTH)),
                  _const_spec((3, D_MODEL)),
                  _const_spec((D_MODEL, D_MODEL)),
                  pl.BlockSpec(memory_space=pltpu.SMEM),
                  _const_spec((D_MODEL, D_MODEL)),
                  _const_spec((D_MODEL, D_MODEL)),
                  seq_spec(2, D_MODEL),
                  seq_spec(KV_WIDTH, WINDOW),
                  seq_spec(KV_WIDTH, WINDOW)],
        out_specs=[seq_spec(tlen, D_MODEL),
                   seq_spec(2, D_MODEL),
                   seq_spec(KV_WIDTH, WINDOW),
                   seq_spec(KV_WIDTH, WINDOW)],
        out_shape=[jax.ShapeDtypeStruct((nseq, tlen, D_MODEL), _F32),
                   jax.ShapeDtypeStruct((nseq, 2, D_MODEL), _F32),
                   jax.ShapeDtypeStruct((nseq, KV_WIDTH, WINDOW), _F32),
                   jax.ShapeDtypeStruct((nseq, KV_WIDTH, WINDOW), _F32)],
        scratch_shapes=[pltpu.VMEM((chunk, KV_WIDTH, WINDOW), _BF) for _ in range(4)],
        compiler_params=pltpu.CompilerParams(
            dimension_semantics=("arbitrary",),
            vmem_limit_bytes=VMEM_LIMIT),
        name="mix_sample",
    )(x, mod3, mod3, mod3, n1g, win, cw, wco, sinks, wao, wmo, st, ck, cv)


def _ffn_tail(x1, ac, val, g, wdown_ref, fg):
    hmid = (_gelu(ac) * val).astype(_BF)
    x2 = x1 + g * _dot(hmid, wdown_ref[...])
    return _rms(x2, fg)


def _ffn_prompt_kernel(x1_ref, sh_ref, sc_ref, g_ref, n2g_ref, wup_ref, fcw_ref, wdown_ref,
                       fg_ref, y_ref, ffn_ref, a_ext, *, parts):
    i = pl.program_id(1)
    t = x1_ref.shape[0]
    rows = t // parts

    @pl.when(i == 0)
    def _():
        a_ext[0:CARRY_ROWS, :] = jnp.zeros((CARRY_ROWS, D_FF), _F32)

    fcw = fcw_ref[...]

    def up(part):
        r0 = part * rows
        e0 = CARRY_ROWS + r0
        h2 = _mod_norm(x1_ref[r0:r0 + rows, :], n2g_ref[...], sc_ref[...], sh_ref[...])
        a = _dot(h2, wup_ref[:, 0:D_FF])
        val = _dot(h2, wup_ref[:, D_FF:2 * D_FF])
        a_ext[e0:e0 + rows, :] = a
        ac = (a_ext[e0 - 2:e0 - 2 + rows, :] * fcw[0:1, :]
              + a_ext[e0 - 1:e0 - 1 + rows, :] * fcw[1:2, :]
              + a * fcw[2:3, :])
        return ac, val

    def down(part, ac, val):
        r0 = part * rows
        y_ref[r0:r0 + rows, :] = _ffn_tail(x1_ref[r0:r0 + rows, :], ac, val, g_ref[...],
                                           wdown_ref, fg_ref[...])

    for part in range(parts):
        down(part, *up(part))
    ffn_ref[...] = a_ext[CARRY_ROWS + t - 2:CARRY_ROWS + t, :]
    a_ext[0:CARRY_ROWS, :] = a_ext[t:t + CARRY_ROWS, :]


def _ffn_prompt_call(x1, mod3, mod_row0, n2g, wup, fcw, wdown, fg, tile, parts):
    bsz, seq, _ = x1.shape

    def mod_spec(chunk):
        return pl.BlockSpec((None, 1, D_MODEL), lambda b, i: (mod_row0 + b, 0, chunk))

    x_spec = pl.BlockSpec((None, tile, D_MODEL), lambda b, i: (b, i, 0))
    return pl.pallas_call(
        functools.partial(_ffn_prompt_kernel, parts=parts),
        grid=(bsz, seq // tile),
        in_specs=[x_spec, mod_spec(3), mod_spec(4), mod_spec(5),
                  _const_spec((1, D_MODEL)),
                  _const_spec((D_MODEL, 2 * D_FF)),
                  _const_spec((3, D_FF)),
                  _const_spec((D_FF, D_MODEL)),
                  _const_spec((1, D_MODEL))],
        out_specs=[x_spec,
                   pl.BlockSpec((None, 2, D_FF), lambda b, i: (b, 0, 0))],
        out_shape=[jax.ShapeDtypeStruct((bsz, seq, D_MODEL), _F32),
                   jax.ShapeDtypeStruct((bsz, 2, D_FF), _F32)],
        scratch_shapes=[pltpu.VMEM((CARRY_ROWS + tile, D_FF), _F32)],
        compiler_params=pltpu.CompilerParams(
            dimension_semantics=("arbitrary", "arbitrary"),
            vmem_limit_bytes=VMEM_LIMIT),
        name="ffn_prompt",
    )(x1, mod3, mod3, mod3, n2g, wup, fcw, wdown, fg)


def _ffn_sample_kernel(x1_ref, sh_ref, sc_ref, g_ref, n2g_ref, wup_ref, fcw_ref, wdown_ref,
                       fg_ref, st_ref, y_ref, ffn_ref):
    nseq, tlen, _ = x1_ref.shape
    rows = nseq * tlen
    x3 = x1_ref[...]
    h3 = _rms(x3, n2g_ref[...]) * (1.0 + sc_ref[...]) + sh_ref[...]
    h2 = h3.reshape(rows, D_MODEL).astype(_BF)
    a = _dot(h2, wup_ref[:, 0:D_FF])
    val = _dot(h2, wup_ref[:, D_FF:2 * D_FF])
    ac, new_st = _conv_seq(a, st_ref[...], fcw_ref[...], nseq, tlen)
    ffn_ref[...] = new_st
    hmid = (_gelu(ac) * val).astype(_BF)
    y = _dot(hmid, wdown_ref[...]).reshape(nseq, tlen, D_MODEL)
    x2 = x3 + g_ref[...] * y
    y_ref[...] = _rms(x2, fg_ref[...])


def _ffn_sample_call(x1, mod3, n2g, wup, fcw, wdown, fg, st, chunk):
    nseq, tlen, _ = x1.shape

    def mod_spec(col):
        return pl.BlockSpec((chunk, 1, D_MODEL), lambda s: (s, 0, col))

    def seq_spec(d1, d2):
        return pl.BlockSpec((chunk, d1, d2), lambda s: (s, 0, 0))

    return pl.pallas_call(
        _ffn_sample_kernel,
        grid=(nseq // chunk,),
        in_specs=[seq_spec(tlen, D_MODEL),
                  mod_spec(3), mod_spec(4), mod_spec(5),
                  _const_spec((1, D_MODEL)),
                  _const_spec((D_MODEL, 2 * D_FF)),
                  _const_spec((3, D_FF)),
                  _const_spec((D_FF, D_MODEL)),
                  _const_spec((1, D_MODEL)),
                  seq_spec(2, D_FF)],
        out_specs=[seq_spec(tlen, D_MODEL), seq_spec(2, D_FF)],
        out_shape=[jax.ShapeDtypeStruct((nseq, tlen, D_MODEL), _F32),
                   jax.ShapeDtypeStruct((nseq, 2, D_FF), _F32)],
        compiler_params=pltpu.CompilerParams(
            dimension_semantics=("arbitrary",),
            vmem_limit_bytes=VMEM_LIMIT),
        name="ffn_sample",
    )(x1, mod3, mod3, mod3, n2g, wup, fcw, wdown, fg, st)


PROMPT_TILE = 512
FFN_TILE = 1024
FFN_PARTS = 2
SAMPLE_CHUNK = 16
FFN_SAMPLE_CHUNK = 64


def kernel(x_prompt, x_sample, c_prompt, c_sample, state_conv, cache_k_win, cache_v_win,
           state_ffn_conv, norm1_g, norm2_g, w_ada, b_ada, w_in, conv_w, w_conv_out,
           attn_sinks, w_attn_out, w_mix_out, w_up, ffn_conv_w, w_down, final_g):
    depth = w_in.shape[0]
    assert depth == 1, "the final norm is fused into the (single) layer's FFN call"
    nsamp = x_sample.shape[0]
    bsz = x_prompt.shape[0]
    xp, xs = x_prompt, x_sample
    c_all = jnp.concatenate([c_sample, c_prompt], axis=0)
    fg = final_g.reshape(1, D_MODEL)
    outs = [[] for _ in range(8)]
    for layer in range(depth):
        mod3 = _mod_call(c_all, w_ada[layer], b_ada[layer].reshape(1, -1))
        n1g = norm1_g[layer].reshape(1, D_MODEL)
        n2g = norm2_g[layer].reshape(1, D_MODEL)
        win = w_in[layer].astype(_BF)
        wco = w_conv_out[layer].astype(_BF)
        wao = w_attn_out[layer].astype(_BF)
        wmo = w_mix_out[layer].astype(_BF)
        cw = conv_w[layer]
        fcw = ffn_conv_w[layer]
        sinks = attn_sinks[layer]
        fg_l = fg
        def to_dim_major(win_state):
            return jnp.transpose(win_state, (0, 2, 3, 1)).reshape(-1, KV_WIDTH, WINDOW)

        def to_pos_major(t3):
            return jnp.transpose(t3.reshape(-1, N_KV_HEADS, HEAD_DIM, WINDOW), (0, 3, 1, 2))

        x1p, conv_p, k_p, v_p, wup, wdown = _mix_prompt_call(
            xp, mod3, nsamp, n1g, win, cw, wco, sinks, wao, wmo,
            (w_up[layer], w_down[layer]), PROMPT_TILE)
        x1s, conv_s, k_s, v_s = _mix_sample_call(
            xs, mod3, n1g, win, cw, wco, sinks, wao, wmo, state_conv[layer],
            to_dim_major(cache_k_win[layer]), to_dim_major(cache_v_win[layer]), SAMPLE_CHUNK)
        xp, ffn_p = _ffn_prompt_call(x1p, mod3, nsamp, n2g, wup, fcw, wdown, fg_l,
                                     FFN_TILE, FFN_PARTS)
        xs, ffn_s = _ffn_sample_call(x1s, mod3, n2g, wup, fcw, wdown, fg_l,
                                     state_ffn_conv[layer], FFN_SAMPLE_CHUNK)
        for lst, val in zip(outs, (conv_p, to_pos_major(k_p), to_pos_major(v_p), ffn_p,
                                   conv_s, to_pos_major(k_s), to_pos_major(v_s), ffn_s)):
            lst.append(val)
    return (xp, xs) + tuple(jnp.stack(lst, axis=0) for lst in outs)
```

```python
import functools

import jax
import jax.numpy as jnp
from jax import lax
from jax.experimental import pallas as pl
from jax.experimental.pallas import tpu as pltpu

D_MODEL = 1024
HEAD_DIM = 64
N_HEADS = 16
N_KV_HEADS = 4
WINDOW = 128
D_FF = 2816
KV_WIDTH = N_KV_HEADS * HEAD_DIM
IN_WIDTH = 3 * D_MODEL + D_MODEL + 2 * KV_WIDTH + 2 * D_MODEL
RMS_EPS = 1e-6
NEG_INF = -1e30
LANES = 128
HEADS_PER_COL = LANES // HEAD_DIM
N_QCOLS = N_HEADS * HEAD_DIM // LANES
CARRY_ROWS = 8
VMEM_LIMIT = 60 * 1024 * 1024

_OFF_B, _OFF_C, _OFF_X = 0, D_MODEL, 2 * D_MODEL
_OFF_Q = 3 * D_MODEL
_OFF_K = _OFF_Q + D_MODEL
_OFF_V = _OFF_K + KV_WIDTH
_OFF_GA = _OFF_V + KV_WIDTH
_OFF_GB = _OFF_GA + D_MODEL

_BF = jnp.bfloat16
_F32 = jnp.float32
_NT = (((1,), (1,)), ((), ()))


LOG2E = 1.4426950408889634
Q_SCALE = HEAD_DIM ** -0.5 * LOG2E


def _slope(head):
    return 2.0 ** (-8.0 * (head + 1) / N_HEADS) * LOG2E


def _rms(x, g):
    return x * lax.rsqrt(jnp.mean(x * x, axis=-1, keepdims=True) + RMS_EPS) * g


def _dot(a, b):
    return jnp.dot(a, b, preferred_element_type=_F32)


def _gelu(x):
    return 0.5 * x * (1.0 + lax.erf(x * (0.5 ** 0.5)))


def _split_heads(t2d, lo):
    rolled = pltpu.roll(t2d, HEAD_DIM, 1)
    zero = jnp.zeros_like(t2d)
    return (jnp.where(lo, t2d, zero), jnp.where(lo, zero, rolled),
            jnp.where(lo, rolled, zero), jnp.where(lo, zero, t2d))


def _softmax_parts(s, sink):
    m = jnp.maximum(jnp.max(s, axis=-1, keepdims=True), sink)
    p = jnp.exp2(s - m)
    denom = jnp.sum(p, axis=-1, keepdims=True) + jnp.exp2(sink - m)
    return p, 1.0 / denom


MOD_COLS = 512


def _mod_kernel(cs_ref, cp_ref, w_ref, b_ref, o_ref):
    w = w_ref[...].astype(_BF)
    n0 = cs_ref.shape[0]
    for c_ref, r0 in ((cs_ref, 0), (cp_ref, n0)):
        c = c_ref[...]
        a = (c * jax.nn.sigmoid(c)).astype(_BF)
        res = _dot(a, w) + b_ref[...]
        n = c.shape[0]
        o_ref[r0:r0 + n] = res.reshape(n, 1, res.shape[-1])


def _mod_call(c_sample, c_prompt, w_ada, b_ada):
    ns, npr = c_sample.shape[0], c_prompt.shape[0]
    n = ns + npr
    return pl.pallas_call(
        _mod_kernel,
        grid=(6 * D_MODEL // MOD_COLS,),
        in_specs=[pl.BlockSpec((ns, D_MODEL), lambda j: (0, 0)),
                  pl.BlockSpec((npr, D_MODEL), lambda j: (0, 0)),
                  pl.BlockSpec((D_MODEL, MOD_COLS), lambda j: (0, j)),
                  pl.BlockSpec((1, MOD_COLS), lambda j: (0, j))],
        out_specs=pl.BlockSpec((n, 1, MOD_COLS), lambda j: (0, 0, j)),
        out_shape=jax.ShapeDtypeStruct((n, 1, 6 * D_MODEL), _F32),
        name="adaln_mod",
    )(c_sample, c_prompt, w_ada, b_ada)


def _mod_norm(x, g, sc, sh):
    return (_rms(x, g) * (1.0 + sc) + sh).astype(_BF)


def _mix_prompt_kernel(x_ref, sh_ref, sc_ref, g_ref, xn_ref, shn_ref, scn_ref, n1g_ref,
                       win_ref, cw_ref, wco_ref, sinks_ref, wao_ref, wmo_ref, *rest,
                       tiles_per_seq, n_cast):
    cast_in, rest = rest[:n_cast], rest[n_cast:]
    (x1_ref, conv_ref, k_ref, v_ref), rest = rest[:4], rest[4:]
    cast_out, (u_ext, k_ext, v_ext, o_scr, bias_scr, h_scr) = rest[:n_cast], rest[n_cast:]
    step = pl.program_id(0)
    i = step % tiles_per_seq
    t = x_ref.shape[0]
    nblk = t // WINDOW

    for src, dst in zip(cast_in, cast_out):
        dst[...] = src[...].astype(_BF)

    @pl.when(step == 0)
    def _():
        h_scr[...] = _mod_norm(x_ref[...], n1g_ref[...], sc_ref[...], sh_ref[...])
        row = lax.broadcasted_iota(jnp.int32, (WINDOW, 2 * WINDOW), 0)
        col = lax.broadcasted_iota(jnp.int32, (WINDOW, 2 * WINDOW), 1)
        dist = WINDOW + row - col
        valid = (dist >= 0) & (dist <= WINDOW)
        distf = dist.astype(_F32)
        for head in range(N_HEADS):
            bias_scr[head] = jnp.where(valid, -(_slope(head) * distf), NEG_INF)

    @pl.when(i == 0)
    def _():
        u_ext[0:CARRY_ROWS, :] = jnp.zeros((CARRY_ROWS, D_MODEL), _F32)
        k_ext[:, 0:WINDOW, :] = jnp.zeros((2 * N_KV_HEADS, WINDOW, LANES), _BF)
        v_ext[:, 0:WINDOW, :] = jnp.zeros((2 * N_KV_HEADS, WINDOW, LANES), _BF)

    def proj(lo, width):
        return _dot(h_scr[...], win_ref[:, lo:lo + width])

    def next_tile_norm():
        h_scr[...] = _mod_norm(xn_ref[...], n1g_ref[...], scn_ref[...], shn_ref[...])

    side = {}

    def conv_in():
        u = proj(_OFF_C, D_MODEL) * proj(_OFF_X, D_MODEL)
        u_ext[CARRY_ROWS:CARRY_ROWS + t, :] = u
        cw = cw_ref[...]
        side["uc"] = (u_ext[CARRY_ROWS - 2:CARRY_ROWS - 2 + t, :] * cw[0:1, :]
                      + u_ext[CARRY_ROWS - 1:CARRY_ROWS - 1 + t, :] * cw[1:2, :]
                      + u * cw[2:3, :])
        conv_ref[...] = u[t - 2:t, :]
        u_ext[0:CARRY_ROWS, :] = u[t - CARRY_ROWS:t, :]

    def conv_out():
        pb = proj(_OFF_B, D_MODEL)
        side["ya"] = _dot((pb * side.pop("uc")).astype(_BF), wco_ref[...])

    def gate_a():
        side["ga"] = jax.nn.sigmoid(proj(_OFF_GA, D_MODEL))

    def gate_b():
        side["gb"] = jax.nn.sigmoid(proj(_OFF_GB, D_MODEL))
        next_tile_norm()

    side_jobs = [conv_in, conv_out, gate_a, gate_b]

    q = (proj(_OFF_Q, D_MODEL) * Q_SCALE).astype(_BF)
    kf = proj(_OFF_K, KV_WIDTH)
    vf = proj(_OFF_V, KV_WIDTH)
    k_ref[...] = kf[t - WINDOW:t, :].T
    v_ref[...] = vf[t - WINDOW:t, :].T
    lane_t = lax.broadcasted_iota(jnp.int32, (t, LANES), 1)
    lo = lane_t < HEAD_DIM
    ones_lane = (HEAD_DIM, 0)
    for kc in range(KV_WIDTH // LANES):
        ks = _split_heads(kf[:, kc * LANES:(kc + 1) * LANES], lo)
        vs = _split_heads(vf[:, kc * LANES:(kc + 1) * LANES], lo)
        for j in range(4):
            k_ext[4 * kc + j, WINDOW:WINDOW + t, :] = ks[j].astype(_BF)
            v_one = jnp.where(lane_t == ones_lane[j % HEADS_PER_COL], 1.0, vs[j])
            v_ext[4 * kc + j, WINDOW:WINDOW + t, :] = v_one.astype(_BF)

    first_cols = lax.broadcasted_iota(jnp.int32, (WINDOW, 2 * WINDOW), 1) < WINDOW
    no_prev = first_cols & (i == 0)
    lo_q = lax.broadcasted_iota(jnp.int32, (WINDOW, LANES), 1) < HEAD_DIM
    for blk in range(nblk):
        r0 = blk * WINDOW
        for c in range(N_QCOLS):
            hkv = c // 2
            qc = q[r0:r0 + WINDOW, c * LANES:(c + 1) * LANES]
            o_pair = []
            for par in range(HEADS_PER_COL):
                head = HEADS_PER_COL * c + par
                kk = k_ext[2 * hkv + par, r0:r0 + 2 * WINDOW, :]
                s = lax.dot_general(qc, kk, _NT, preferred_element_type=_F32) + bias_scr[head]
                if blk == 0:
                    s = jnp.where(no_prev, NEG_INF, s)
                sink = sinks_ref[head] * LOG2E
                m = jnp.maximum(jnp.max(s, axis=-1, keepdims=True), sink)
                p = jnp.exp2(s - m).astype(_BF)
                o_h = _dot(p, v_ext[2 * hkv + par, r0:r0 + 2 * WINDOW, :])
                col = ones_lane[par]
                denom = o_h[:, col:col + 1] + jnp.exp2(sink - m)
                o_pair.append(o_h * (1.0 / denom))
            o_c = jnp.where(lo_q, o_pair[0], o_pair[1])
            o_scr[r0:r0 + WINDOW, c * LANES:(c + 1) * LANES] = o_c.astype(_BF)
        for _ in range(-(-len(side_jobs) // (nblk - blk))):
            side_jobs.pop(0)()
    k_ext[:, 0:WINDOW, :] = k_ext[:, t:t + WINDOW, :]
    v_ext[:, 0:WINDOW, :] = v_ext[:, t:t + WINDOW, :]
    yb = _dot(o_scr[...], wao_ref[...])

    mixed = (side["ga"] * side["ya"] + side["gb"] * yb).astype(_BF)
    x1_ref[...] = x_ref[...] + g_ref[...] * _dot(mixed, wmo_ref[...])


def _const_spec(shape):
    nd = len(shape)
    return pl.BlockSpec(shape, lambda *_: (0,) * nd, pipeline_mode=pl.Buffered(1))


def _tile_specs(tile, nt, steps, mod_row0):
    def nxt(s):
        return jnp.minimum(s + 1, steps - 1)

    x_spec = pl.BlockSpec((None, tile, D_MODEL), lambda s: (s // nt, s % nt, 0))
    xn_spec = pl.BlockSpec((None, tile, D_MODEL), lambda s: (nxt(s) // nt, nxt(s) % nt, 0))

    def mod_spec(chunk):
        return pl.BlockSpec((None, 1, D_MODEL), lambda s: (mod_row0 + s // nt, 0, chunk))

    def modn_spec(chunk):
        return pl.BlockSpec((None, 1, D_MODEL), lambda s: (mod_row0 + nxt(s) // nt, 0, chunk))

    return x_spec, xn_spec, mod_spec, modn_spec


BF16_SUBLANES = 16


def _cast_stream_spec(shape, steps):
    rows, cols = shape
    for hold in range(1, steps + 1):
        nblocks, rem = divmod(steps, hold)
        if rem == 0 and rows % nblocks == 0 and (rows // nblocks) % BF16_SUBLANES == 0:
            return pl.BlockSpec((rows // nblocks, cols), lambda s: (s // hold, 0))
    raise ValueError(f"no bf16-aligned row split of {shape} over {steps} steps")


def _mix_prompt_call(x, mod3, mod_row0, n1g, win, cw, wco, sinks, wao, wmo, cast_along, tile):
    bsz, seq, _ = x.shape
    nt = seq // tile
    steps = bsz * nt
    x_spec, xn_spec, mod_spec, modn_spec = _tile_specs(tile, nt, steps, mod_row0)
    cast_specs = [_cast_stream_spec(w.shape, steps) for w in cast_along]

    def state_spec(d1, d2):
        return pl.BlockSpec((None, d1, d2), lambda s: (s // nt, 0, 0))

    return pl.pallas_call(
        functools.partial(_mix_prompt_kernel, tiles_per_seq=nt, n_cast=len(cast_along)),
        grid=(steps,),
        in_specs=[x_spec, mod_spec(0), mod_spec(1), mod_spec(2),
                  xn_spec, modn_spec(0), modn_spec(1),
                  _const_spec((1, D_MODEL)),
                  _const_spec((D_MODEL, IN_WIDTH)),
                  _const_spec((3, D_MODEL)),
                  _const_spec((D_MODEL, D_MODEL)),
                  pl.BlockSpec(memory_space=pltpu.SMEM),
                  _const_spec((D_MODEL, D_MODEL)),
                  _const_spec((D_MODEL, D_MODEL))] + cast_specs,
        out_specs=[x_spec,
                   state_spec(2, D_MODEL),
                   state_spec(KV_WIDTH, WINDOW),
                   state_spec(KV_WIDTH, WINDOW)] + cast_specs,
        out_shape=[jax.ShapeDtypeStruct((bsz, seq, D_MODEL), _F32),
                   jax.ShapeDtypeStruct((bsz, 2, D_MODEL), _F32),
                   jax.ShapeDtypeStruct((bsz, KV_WIDTH, WINDOW), _F32),
                   jax.ShapeDtypeStruct((bsz, KV_WIDTH, WINDOW), _F32)]
        + [jax.ShapeDtypeStruct(w.shape, _BF) for w in cast_along],
        scratch_shapes=[pltpu.VMEM((CARRY_ROWS + tile, D_MODEL), _F32),
                        pltpu.VMEM((2 * N_KV_HEADS, WINDOW + tile, LANES), _BF),
                        pltpu.VMEM((2 * N_KV_HEADS, WINDOW + tile, LANES), _BF),
                        pltpu.VMEM((tile, D_MODEL), _BF),
                        pltpu.VMEM((N_HEADS, WINDOW, 2 * WINDOW), _F32),
                        pltpu.VMEM((tile, D_MODEL), _BF)],
        compiler_params=pltpu.CompilerParams(
            dimension_semantics=("arbitrary",),
            vmem_limit_bytes=VMEM_LIMIT),
        name="mix_prompt",
    )(x, mod3, mod3, mod3, x, mod3, mod3, n1g, win, cw, wco, sinks, wao, wmo, *cast_along)


def _conv_seq(u2d, st, cw, nseq, tlen):
    width = u2d.shape[-1]
    shape3 = (nseq, tlen, width)
    u3 = u2d.reshape(shape3)
    r1 = pltpu.roll(u2d, 1, 0).reshape(shape3)
    r2 = pltpu.roll(u2d, 2, 0).reshape(shape3)
    tpos = lax.broadcasted_iota(jnp.int32, shape3, 1)
    st0 = jnp.broadcast_to(st[:, 0:1, :], shape3)
    st1 = jnp.broadcast_to(st[:, 1:2, :], shape3)
    um1 = jnp.where(tpos == 0, st1, r1)
    um2 = jnp.where(tpos == 0, st0, jnp.where(tpos == 1, st1, r2))
    y = um2 * cw[0:1, :] + um1 * cw[1:2, :] + u3 * cw[2:3, :]
    return y.reshape(nseq * tlen, width), u3[:, tlen - 2:tlen, :]


def _mix_sample_kernel(x_ref, sh_ref, sc_ref, g_ref, n1g_ref, win_ref, cw_ref, wco_ref,
                       sinks_ref, wao_ref, wmo_ref, st_ref, ck_ref, cv_ref,
                       x1_ref, conv_ref, k_ref, v_ref,
                       kmain, kdrop, vmain, vdrop):
    nseq, tlen, _ = x_ref.shape
    rows = nseq * tlen

    x3 = x_ref[...]
    h3 = _rms(x3, n1g_ref[...]) * (1.0 + sc_ref[...]) + sh_ref[...]
    hf = h3.reshape(rows, D_MODEL)
    h_now = [hf.astype(_BF)]

    def proj(lo, width):
        return _dot(h_now[0], win_ref[:, lo:lo + width])

    def order_after(tile_f32):
        bits = pltpu.bitcast(tile_f32[0:8, :], jnp.uint32)
        zero = ((bits >> 16) >> 16).astype(_F32)
        h_now[0] = (hf + jnp.tile(zero, (rows // 8, D_MODEL // LANES))).astype(_BF)

    side = {}

    def job_q():
        side["q"] = proj(_OFF_Q, D_MODEL) * Q_SCALE

    def job_conv_in():
        u = proj(_OFF_C, D_MODEL) * proj(_OFF_X, D_MODEL)
        side["uc"], new_st = _conv_seq(u, st_ref[...], cw_ref[...], nseq, tlen)
        conv_ref[...] = new_st

    def job_gate_b():
        side["pb"] = proj(_OFF_B, D_MODEL)

    def job_conv_out():
        side["ya"] = _dot((side.pop("pb") * side.pop("uc")).astype(_BF), wco_ref[...])

    def job_ga():
        side["ga"] = jax.nn.sigmoid(proj(_OFF_GA, D_MODEL))

    def job_gb():
        side["gb"] = jax.nn.sigmoid(proj(_OFF_GB, D_MODEL))

    keep = WINDOW - tlen
    kvt_new = proj(_OFF_K, 2 * KV_WIDTH).T
    keep_old = lax.broadcasted_iota(jnp.int32, (KV_WIDTH, WINDOW), 1) < keep
    seq_per_tile = LANES // tlen

    def window_update(s):
        tile = s // seq_per_tile
        shift = (keep - (s % seq_per_tile) * tlen) % LANES
        for cache_ref, out_ref, main_scr, drop_scr, base in (
                (ck_ref, k_ref, kmain, kdrop, 0), (cv_ref, v_ref, vmain, vdrop, KV_WIDTH)):
            rolled = pltpu.roll(cache_ref[s], keep, 1)
            new = pltpu.roll(kvt_new[base:base + KV_WIDTH, tile * LANES:(tile + 1) * LANES], shift, 1)
            win = jnp.where(keep_old, rolled, new)
            out_ref[s] = win
            main_scr[s] = win.astype(_BF)
            drop_scr[s] = rolled.astype(_BF)
        return win

    window_jobs = [job_q, job_conv_in, job_gate_b, job_conv_out]
    group = -(-nseq // len(window_jobs))
    for s in range(nseq):
        last_win = window_update(s)
        if window_jobs and (s + 1) % group == 0:
            order_after(last_win)
            window_jobs.pop(0)()
    while window_jobs:
        window_jobs.pop(0)()
    job_ga()
    attn_jobs = [job_gb]

    qf = side.pop("q")
    lo = lax.broadcasted_iota(jnp.int32, (rows, LANES), 1) < HEAD_DIM
    zero = jnp.zeros((rows, LANES), _F32)
    heads_per_tile = N_HEADS // (KV_WIDTH // LANES)
    m_rows = heads_per_tile * tlen
    row = lax.broadcasted_iota(jnp.int32, (m_rows, 2 * WINDOW), 0)
    col = lax.broadcasted_iota(jnp.int32, (m_rows, 2 * WINDOW), 1)
    tq = row % tlen
    in_main = col < WINDOW
    dist = jnp.where(in_main, keep + tq - col, 2 * WINDOW + keep + tq - col)
    valid = (in_main & (dist >= 0)) | ((col >= WINDOW + keep) & (dist <= WINDOW))
    distf = dist.astype(_F32)
    row1 = lax.broadcasted_iota(jnp.int32, (m_rows, 1), 0)

    o_cols = [None] * N_QCOLS
    for kc in range(KV_WIDTH // LANES):
        parts = []
        slope = jnp.zeros((m_rows, 1), _F32)
        sink = jnp.zeros((m_rows, 1), _F32)
        for j in range(heads_per_tile):
            head = heads_per_tile * kc + j
            c, par, half = head // HEADS_PER_COL, j % HEADS_PER_COL, j // (heads_per_tile // 2)
            qc = qf[:, c * LANES:(c + 1) * LANES]
            src = qc if par == half else pltpu.roll(qc, HEAD_DIM, 1)
            part = jnp.where(lo, src, zero) if half == 0 else jnp.where(lo, zero, src)
            parts.append(part.reshape(nseq, tlen, LANES))
            mine = (row1 >= j * tlen) & (row1 < (j + 1) * tlen)
            slope = jnp.where(mine, _slope(head), slope)
            sink = jnp.where(mine, sinks_ref[head] * LOG2E, sink)
        lhs = jnp.concatenate(parts, axis=1).astype(_BF)
        bias = jnp.where(valid, -(slope * distf), NEG_INF)
        rsl = slice(kc * LANES, (kc + 1) * LANES)
        wk = jnp.concatenate([kmain[:, rsl, :], kdrop[:, rsl, :]], axis=2)
        wv = jnp.concatenate([vmain[:, rsl, :], vdrop[:, rsl, :]], axis=2)
        sc = jnp.einsum('smk,skn->smn', lhs, wk, preferred_element_type=_F32) + bias
        p, inv = _softmax_parts(sc, sink)
        r = jnp.einsum('smk,snk->smn', p.astype(_BF), wv, preferred_element_type=_F32) * inv
        for m2 in range(heads_per_tile // HEADS_PER_COL):
            c = (heads_per_tile // HEADS_PER_COL) * kc + m2
            half = m2 // (heads_per_tile // (2 * HEADS_PER_COL))
            r0 = r[:, (2 * m2) * tlen:(2 * m2 + 1) * tlen, :].reshape(rows, LANES)
            r1 = r[:, (2 * m2 + 1) * tlen:(2 * m2 + 2) * tlen, :].reshape(rows, LANES)
            if half == 0:
                o_cols[c] = jnp.where(lo, r0, pltpu.roll(r1, HEAD_DIM, 1))
            else:
                o_cols[c] = jnp.where(lo, pltpu.roll(r0, HEAD_DIM, 1), r1)
        if attn_jobs:
            order_after(o_cols[c])
            attn_jobs.pop(0)()
    while attn_jobs:
        attn_jobs.pop(0)()
    o = jnp.concatenate(o_cols, axis=1).astype(_BF)
    yb = _dot(o, wao_ref[...])

    mixed = (side["ga"] * side["ya"] + side["gb"] * yb).astype(_BF)
    y = _dot(mixed, wmo_ref[...]).reshape(nseq, tlen, D_MODEL)
    x1_ref[...] = x3 + g_ref[...] * y


def _mix_sample_call(x, mod3, n1g, win, cw, wco, sinks, wao, wmo, st, ck, cv, chunk):
    nseq, tlen, _ = x.shape
    assert LANES % tlen == 0 and chunk % (LANES // tlen) == 0

    def mod_spec(col):
        return pl.BlockSpec((chunk, 1, D_MODEL), lambda s: (s, 0, col))

    def seq_spec(d1, d2):
        return pl.BlockSpec((chunk, d1, d2), lambda s: (s, 0, 0))

    return pl.pallas_call(
        _mix_sample_kernel,
        grid=(nseq // chunk,),
        in_specs=[seq_spec(tlen, D_MODEL),
                  mod_spec(0), mod_spec(1), mod_spec(2),
                  _const_spec((1, D_MODEL)),
                  _const_spec((D_MODEL, IN_WIDTH)),
                  _const_spec((3, D_MODEL)),
                  _const_spec((D_MODEL, D_MODEL)),
                  pl.BlockSpec(memory_space=pltpu.SMEM),
                  _const_spec((D_MODEL, D_MODEL)),
                  _const_spec((D_MODEL, D_MODEL)),
                  seq_spec(2, D_MODEL),
                  seq_spec(KV_WIDTH, WINDOW),
                  seq_spec(KV_WIDTH, WINDOW)],
        out_specs=[seq_spec(tlen, D_MODEL),
                   seq_spec(2, D_MODEL),
                   seq_spec(KV_WIDTH, WINDOW),
                   seq_spec(KV_WIDTH, WINDOW)],
        out_shape=[jax.ShapeDtypeStruct((nseq, tlen, D_MODEL), _F32),
                   jax.ShapeDtypeStruct((nseq, 2, D_MODEL), _F32),
                   jax.ShapeDtypeStruct((nseq, KV_WIDTH, WINDOW), _F32),
                   jax.ShapeDtypeStruct((nseq, KV_WIDTH, WINDOW), _F32)],
        scratch_shapes=[pltpu.VMEM((chunk, KV_WIDTH, WINDOW), _BF) for _ in range(4)],
        compiler_params=pltpu.CompilerParams(
            dimension_semantics=("arbitrary",),
            vmem_limit_bytes=VMEM_LIMIT),
        name="mix_sample",
    )(x, mod3, mod3, mod3, n1g, win, cw, wco, sinks, wao, wmo, st, ck, cv)


def _ffn_tail(x1, ac, val, g, wdown_ref, fg):
    hmid = (_gelu(ac) * val).astype(_BF)
    x2 = x1 + g * _dot(hmid, wdown_ref[...])
    return _rms(x2, fg)


def _zero_after(ref_bf16):
    words = pltpu.bitcast(ref_bf16[...], jnp.uint32)
    nrow, ncol = words.shape
    acc = words[0:8, :]
    for r in range(8, nrow, 8):
        acc = acc | words[r:r + 8, :]
    out = acc[:, 0:LANES]
    for c in range(LANES, ncol, LANES):
        out = out | acc[:, c:c + LANES]
    return ((out >> 16) >> 16).astype(_F32)


FFN_LAST_K = 256


def _ffn_prompt_kernel(x1_ref, sh_ref, sc_ref, g_ref, xn_ref, shn_ref, scn_ref, n2g_ref,
                       wup_ref, fcw_ref, wdown_ref, fg_ref, y_ref, ffn_ref, a_ext, h_scr,
                       *, tiles_per_seq):
    step = pl.program_id(0)
    i = step % tiles_per_seq
    t = x1_ref.shape[0]

    @pl.when(step == 0)
    def _():
        h_scr[...] = _mod_norm(x1_ref[...], n2g_ref[...], sc_ref[...], sh_ref[...])

    @pl.when(i == 0)
    def _():
        a_ext[0:CARRY_ROWS, :] = jnp.zeros((CARRY_ROWS, D_FF), _F32)

    a = _dot(h_scr[...], wup_ref[:, 0:D_FF])
    val = _dot(h_scr[...], wup_ref[:, D_FF:2 * D_FF])
    h_scr[...] = _mod_norm(xn_ref[...], n2g_ref[...], scn_ref[...], shn_ref[...])
    a_ext[CARRY_ROWS:CARRY_ROWS + t, :] = a
    fcw = fcw_ref[...]
    ac = (a_ext[CARRY_ROWS - 2:CARRY_ROWS - 2 + t, :] * fcw[0:1, :]
          + a_ext[CARRY_ROWS - 1:CARRY_ROWS - 1 + t, :] * fcw[1:2, :]
          + a * fcw[2:3, :])
    ffn_ref[...] = a[t - 2:t, :]
    a_ext[0:CARRY_ROWS, :] = a[t - CARRY_ROWS:t, :]
    hmid = _gelu(ac) * val
    k0 = D_FF - FFN_LAST_K
    zero = jnp.tile(_zero_after(h_scr), (t // 8, FFN_LAST_K // LANES))
    d = (_dot(hmid[:, 0:k0].astype(_BF), wdown_ref[0:k0, :])
         + _dot((hmid[:, k0:] + zero).astype(_BF), wdown_ref[k0:, :]))
    x2 = x1_ref[...] + g_ref[...] * d
    y_ref[...] = _rms(x2, fg_ref[...])


def _ffn_prompt_call(x1, mod3, mod_row0, n2g, wup, fcw, wdown, fg, tile):
    bsz, seq, _ = x1.shape
    nt = seq // tile
    steps = bsz * nt
    x_spec, xn_spec, mod_spec, modn_spec = _tile_specs(tile, nt, steps, mod_row0)

    return pl.pallas_call(
        functools.partial(_ffn_prompt_kernel, tiles_per_seq=nt),
        grid=(steps,),
        in_specs=[x_spec, mod_spec(3), mod_spec(4), mod_spec(5),
                  xn_spec, modn_spec(3), modn_spec(4),
                  _const_spec((1, D_MODEL)),
                  _const_spec((D_MODEL, 2 * D_FF)),
                  _const_spec((3, D_FF)),
                  _const_spec((D_FF, D_MODEL)),
                  _const_spec((1, D_MODEL))],
        out_specs=[x_spec,
                   pl.BlockSpec((None, 2, D_FF), lambda s: (s // nt, 0, 0))],
        out_shape=[jax.ShapeDtypeStruct((bsz, seq, D_MODEL), _F32),
                   jax.ShapeDtypeStruct((bsz, 2, D_FF), _F32)],
        scratch_shapes=[pltpu.VMEM((CARRY_ROWS + tile, D_FF), _F32),
                        pltpu.VMEM((tile, D_MODEL), _BF)],
        compiler_params=pltpu.CompilerParams(
            dimension_semantics=("arbitrary",),
            vmem_limit_bytes=VMEM_LIMIT),
        name="ffn_prompt",
    )(x1, mod3, mod3, mod3, x1, mod3, mod3, n2g, wup, fcw, wdown, fg)


def _ffn_sample_kernel(x1_ref, sh_ref, sc_ref, g_ref, n2g_ref, wup_ref, fcw_ref, wdown_ref,
                       fg_ref, st_ref, y_ref, ffn_ref):
    nseq, tlen, _ = x1_ref.shape
    rows = nseq * tlen
    x3 = x1_ref[...]
    h3 = _rms(x3, n2g_ref[...]) * (1.0 + sc_ref[...]) + sh_ref[...]
    h2 = h3.reshape(rows, D_MODEL).astype(_BF)
    a = _dot(h2, wup_ref[:, 0:D_FF])
    val = _dot(h2, wup_ref[:, D_FF:2 * D_FF])
    ac, new_st = _conv_seq(a, st_ref[...], fcw_ref[...], nseq, tlen)
    ffn_ref[...] = new_st
    hmid = (_gelu(ac) * val).astype(_BF)
    y = _dot(hmid, wdown_ref[...]).reshape(nseq, tlen, D_MODEL)
    x2 = x3 + g_ref[...] * y
    y_ref[...] = _rms(x2, fg_ref[...])


def _ffn_sample_call(x1, mod3, n2g, wup, fcw, wdown, fg, st, chunk):
    nseq, tlen, _ = x1.shape

    def mod_spec(col):
        return pl.BlockSpec((chunk, 1, D_MODEL), lambda s: (s, 0, col))

    def seq_spec(d1, d2):
        return pl.BlockSpec((chunk, d1, d2), lambda s: (s, 0, 0))

    return pl.pallas_call(
        _ffn_sample_kernel,
        grid=(nseq // chunk,),
        in_specs=[seq_spec(tlen, D_MODEL),
                  mod_spec(3), mod_spec(4), mod_spec(5),
                  _const_spec((1, D_MODEL)),
                  _const_spec((D_MODEL, 2 * D_FF)),
                  _const_spec((3, D_FF)),
                  _const_spec((D_FF, D_MODEL)),
                  _const_spec((1, D_MODEL)),
                  seq_spec(2, D_FF)],
        out_specs=[seq_spec(tlen, D_MODEL), seq_spec(2, D_FF)],
        out_shape=[jax.ShapeDtypeStruct((nseq, tlen, D_MODEL), _F32),
                   jax.ShapeDtypeStruct((nseq, 2, D_FF), _F32)],
        compiler_params=pltpu.CompilerParams(
            dimension_semantics=("arbitrary",),
            vmem_limit_bytes=VMEM_LIMIT),
        name="ffn_sample",
    )(x1, mod3, mod3, mod3, n2g, wup, fcw, wdown, fg, st)


PROMPT_TILE = 512
SAMPLE_CHUNK = 16
FFN_SAMPLE_CHUNK = 64


def kernel(x_prompt, x_sample, c_prompt, c_sample, state_conv, cache_k_win, cache_v_win,
           state_ffn_conv, norm1_g, norm2_g, w_ada, b_ada, w_in, conv_w, w_conv_out,
           attn_sinks, w_attn_out, w_mix_out, w_up, ffn_conv_w, w_down, final_g):
    depth = w_in.shape[0]
    assert depth == 1, "the final norm is fused into the (single) layer's FFN call"
    nsamp = x_sample.shape[0]
    bsz = x_prompt.shape[0]
    xp, xs = x_prompt, x_sample
    fg = final_g.reshape(1, D_MODEL)
    outs = [[] for _ in range(8)]
    for layer in range(depth):
        mod3 = _mod_call(c_sample, c_prompt, w_ada[layer], b_ada[layer].reshape(1, -1))
        n1g = norm1_g[layer].reshape(1, D_MODEL)
        n2g = norm2_g[layer].reshape(1, D_MODEL)
        win = w_in[layer].astype(_BF)
        wco = w_conv_out[layer].astype(_BF)
        wao = w_attn_out[layer].astype(_BF)
        wmo = w_mix_out[layer].astype(_BF)
        cw = conv_w[layer]
        fcw = ffn_conv_w[layer]
        sinks = attn_sinks[layer]
        fg_l = fg
        def to_dim_major(win_state):
            return jnp.transpose(win_state, (0, 2, 3, 1)).reshape(-1, KV_WIDTH, WINDOW)

        def to_pos_major(t3):
            return jnp.transpose(t3.reshape(-1, N_KV_HEADS, HEAD_DIM, WINDOW), (0, 3, 1, 2))

        x1p, conv_p, k_p, v_p, wup, wdown = _mix_prompt_call(
            xp, mod3, nsamp, n1g, win, cw, wco, sinks, wao, wmo,
            (w_up[layer], w_down[layer]), PROMPT_TILE)
        x1s, conv_s, k_s, v_s = _mix_sample_call(
            xs, mod3, n1g, win, cw, wco, sinks, wao, wmo, state_conv[layer],
            to_dim_major(cache_k_win[layer]), to_dim_major(cache_v_win[layer]), SAMPLE_CHUNK)
        xp, ffn_p = _ffn_prompt_call(x1p, mod3, nsamp, n2g, wup, fcw, wdown, fg_l,
                                     PROMPT_TILE)
        xs, ffn_s = _ffn_sample_call(x1s, mod3, n2g, wup, fcw, wdown, fg_l,
                                     state_ffn_conv[layer], FFN_SAMPLE_CHUNK)
        for lst, val in zip(outs, (conv_p, to_pos_major(k_p), to_pos_major(v_p), ffn_p,
                                   conv_s, to_pos_major(k_s), to_pos_major(v_s), ffn_s)):
            lst.append(val)
    return (xp, xs) + tuple(jnp.stack(lst, axis=0) for lst in outs)
```

```python
import functools

import jax
import jax.numpy as jnp
from jax import lax
from jax.experimental import pallas as pl
from jax.experimental.pallas import tpu as pltpu

D_MODEL = 1024
HEAD_DIM = 64
N_HEADS = 16
N_KV_HEADS = 4
WINDOW = 128
D_FF = 2816
KV_WIDTH = N_KV_HEADS * HEAD_DIM
IN_WIDTH = 3 * D_MODEL + D_MODEL + 2 * KV_WIDTH + 2 * D_MODEL
RMS_EPS = 1e-6
NEG_INF = -1e30
LANES = 128
HEADS_PER_COL = LANES // HEAD_DIM
N_QCOLS = N_HEADS * HEAD_DIM // LANES
CARRY_ROWS = 8
VMEM_LIMIT = 60 * 1024 * 1024

_OFF_B, _OFF_C, _OFF_X = 0, D_MODEL, 2 * D_MODEL
_OFF_Q = 3 * D_MODEL
_OFF_K = _OFF_Q + D_MODEL
_OFF_V = _OFF_K + KV_WIDTH
_OFF_GA = _OFF_V + KV_WIDTH
_OFF_GB = _OFF_GA + D_MODEL

_BF = jnp.bfloat16
_F32 = jnp.float32
_NT = (((1,), (1,)), ((), ()))


LOG2E = 1.4426950408889634
Q_SCALE = HEAD_DIM ** -0.5 * LOG2E


def _slope(head):
    return 2.0 ** (-8.0 * (head + 1) / N_HEADS) * LOG2E


def _rms(x, g):
    return x * lax.rsqrt(jnp.mean(x * x, axis=-1, keepdims=True) + RMS_EPS) * g


def _dot(a, b):
    return jnp.dot(a, b, preferred_element_type=_F32)


def _gelu(x):
    return 0.5 * x * (1.0 + lax.erf(x * (0.5 ** 0.5)))


def _split_heads(t2d, lo):
    rolled = pltpu.roll(t2d, HEAD_DIM, 1)
    zero = jnp.zeros_like(t2d)
    return (jnp.where(lo, t2d, zero), jnp.where(lo, zero, rolled),
            jnp.where(lo, rolled, zero), jnp.where(lo, zero, t2d))


def _softmax_parts(s, sink):
    m = jnp.maximum(jnp.max(s, axis=-1, keepdims=True), sink)
    p = jnp.exp2(s - m)
    denom = jnp.sum(p, axis=-1, keepdims=True) + jnp.exp2(sink - m)
    return p, 1.0 / denom


MOD_COLS = 2048


def _mod_kernel(cs_ref, cp_ref, w_ref, b_ref, o_ref):
    w = w_ref[...].astype(_BF)
    n0 = cs_ref.shape[0]
    for c_ref, r0 in ((cs_ref, 0), (cp_ref, n0)):
        c = c_ref[...]
        a = (c * jax.nn.sigmoid(c)).astype(_BF)
        res = _dot(a, w) + b_ref[...]
        n = c.shape[0]
        o_ref[r0:r0 + n] = res.reshape(n, 1, res.shape[-1])


def _mod_call(c_sample, c_prompt, w_ada, b_ada):
    ns, npr = c_sample.shape[0], c_prompt.shape[0]
    n = ns + npr
    return pl.pallas_call(
        _mod_kernel,
        grid=(6 * D_MODEL // MOD_COLS,),
        in_specs=[pl.BlockSpec((ns, D_MODEL), lambda j: (0, 0)),
                  pl.BlockSpec((npr, D_MODEL), lambda j: (0, 0)),
                  pl.BlockSpec((D_MODEL, MOD_COLS), lambda j: (0, j)),
                  pl.BlockSpec((1, MOD_COLS), lambda j: (0, j))],
        out_specs=pl.BlockSpec((n, 1, MOD_COLS), lambda j: (0, 0, j)),
        out_shape=jax.ShapeDtypeStruct((n, 1, 6 * D_MODEL), _F32),
        name="adaln_mod",
    )(c_sample, c_prompt, w_ada, b_ada)


def _mod_norm(x, g, sc, sh):
    return (_rms(x, g) * (1.0 + sc) + sh).astype(_BF)


def _mix_prompt_kernel(x_ref, sh_ref, sc_ref, g_ref, xn_ref, shn_ref, scn_ref, n1g_ref,
                       win_ref, cw_ref, wco_ref, sinks_ref, wao_ref, wmo_ref, *rest,
                       tiles_per_seq, n_cast):
    cast_in, rest = rest[:n_cast], rest[n_cast:]
    (x1_ref, conv_ref, k_ref, v_ref), rest = rest[:4], rest[4:]
    cast_out, (u_ext, k_ext, v_ext, o_scr, bias_scr, h_scr) = rest[:n_cast], rest[n_cast:]
    step = pl.program_id(0)
    i = step % tiles_per_seq
    t = x_ref.shape[0]
    nblk = t // WINDOW

    for src, dst in zip(cast_in, cast_out):
        dst[...] = src[...].astype(_BF)

    @pl.when(step == 0)
    def _():
        h_scr[...] = _mod_norm(x_ref[...], n1g_ref[...], sc_ref[...], sh_ref[...])
        row = lax.broadcasted_iota(jnp.int32, (WINDOW, 2 * WINDOW), 0)
        col = lax.broadcasted_iota(jnp.int32, (WINDOW, 2 * WINDOW), 1)
        dist = WINDOW + row - col
        valid = (dist >= 0) & (dist <= WINDOW)
        distf = dist.astype(_F32)
        for head in range(N_HEADS):
            bias_scr[head] = jnp.where(valid, -(_slope(head) * distf), NEG_INF)

    @pl.when(i == 0)
    def _():
        u_ext[0:CARRY_ROWS, :] = jnp.zeros((CARRY_ROWS, D_MODEL), _F32)
        k_ext[:, 0:WINDOW, :] = jnp.zeros((2 * N_KV_HEADS, WINDOW, LANES), _BF)
        v_ext[:, 0:WINDOW, :] = jnp.zeros((2 * N_KV_HEADS, WINDOW, LANES), _BF)

    def proj(lo, width):
        return _dot(h_scr[...], win_ref[:, lo:lo + width])

    def next_tile_norm():
        h_scr[...] = _mod_norm(xn_ref[...], n1g_ref[...], scn_ref[...], shn_ref[...])

    side = {}

    def conv_in():
        u = proj(_OFF_C, D_MODEL) * proj(_OFF_X, D_MODEL)
        u_ext[CARRY_ROWS:CARRY_ROWS + t, :] = u
        cw = cw_ref[...]
        side["uc"] = (u_ext[CARRY_ROWS - 2:CARRY_ROWS - 2 + t, :] * cw[0:1, :]
                      + u_ext[CARRY_ROWS - 1:CARRY_ROWS - 1 + t, :] * cw[1:2, :]
                      + u * cw[2:3, :])
        conv_ref[...] = u[t - 2:t, :]
        u_ext[0:CARRY_ROWS, :] = u[t - CARRY_ROWS:t, :]

    def conv_out():
        pb = proj(_OFF_B, D_MODEL)
        side["ya"] = _dot((pb * side.pop("uc")).astype(_BF), wco_ref[...])

    def gate_a():
        side["ga"] = jax.nn.sigmoid(proj(_OFF_GA, D_MODEL))

    def gate_b():
        side["gb"] = jax.nn.sigmoid(proj(_OFF_GB, D_MODEL))
        next_tile_norm()

    side_jobs = [conv_in, conv_out, gate_a, gate_b]

    q = (proj(_OFF_Q, D_MODEL) * Q_SCALE).astype(_BF)
    kf = proj(_OFF_K, KV_WIDTH)
    vf = proj(_OFF_V, KV_WIDTH)
    k_ref[...] = kf[t - WINDOW:t, :].T
    v_ref[...] = vf[t - WINDOW:t, :].T
    lane_t = lax.broadcasted_iota(jnp.int32, (t, LANES), 1)
    lo = lane_t < HEAD_DIM
    ones_lane = (HEAD_DIM, 0)
    for kc in range(KV_WIDTH // LANES):
        ks = _split_heads(kf[:, kc * LANES:(kc + 1) * LANES], lo)
        vs = _split_heads(vf[:, kc * LANES:(kc + 1) * LANES], lo)
        for j in range(4):
            k_ext[4 * kc + j, WINDOW:WINDOW + t, :] = ks[j].astype(_BF)
            v_one = jnp.where(lane_t == ones_lane[j % HEADS_PER_COL], 1.0, vs[j])
            v_ext[4 * kc + j, WINDOW:WINDOW + t, :] = v_one.astype(_BF)

    first_cols = lax.broadcasted_iota(jnp.int32, (WINDOW, 2 * WINDOW), 1) < WINDOW
    no_prev = first_cols & (i == 0)
    lo_q = lax.broadcasted_iota(jnp.int32, (WINDOW, LANES), 1) < HEAD_DIM
    for blk in range(nblk):
        r0 = blk * WINDOW
        for c in range(N_QCOLS):
            hkv = c // 2
            qc = q[r0:r0 + WINDOW, c * LANES:(c + 1) * LANES]
            o_pair = []
            for par in range(HEADS_PER_COL):
                head = HEADS_PER_COL * c + par
                kk = k_ext[2 * hkv + par, r0:r0 + 2 * WINDOW, :]
                s = lax.dot_general(qc, kk, _NT, preferred_element_type=_F32) + bias_scr[head]
                if blk == 0:
                    s = jnp.where(no_prev, NEG_INF, s)
                sink = sinks_ref[head] * LOG2E
                m = jnp.maximum(jnp.max(s, axis=-1, keepdims=True), sink)
                p = jnp.exp2(s - m).astype(_BF)
                o_h = _dot(p, v_ext[2 * hkv + par, r0:r0 + 2 * WINDOW, :])
                col = ones_lane[par]
                denom = o_h[:, col:col + 1] + jnp.exp2(sink - m)
                o_pair.append(o_h * (1.0 / denom))
            o_c = jnp.where(lo_q, o_pair[0], o_pair[1])
            o_scr[r0:r0 + WINDOW, c * LANES:(c + 1) * LANES] = o_c.astype(_BF)
        for _ in range(-(-len(side_jobs) // (nblk - blk))):
            side_jobs.pop(0)()
    k_ext[:, 0:WINDOW, :] = k_ext[:, t:t + WINDOW, :]
    v_ext[:, 0:WINDOW, :] = v_ext[:, t:t + WINDOW, :]
    yb = _dot(o_scr[...], wao_ref[...])

    mixed = (side["ga"] * side["ya"] + side["gb"] * yb).astype(_BF)
    x1_ref[...] = x_ref[...] + g_ref[...] * _dot(mixed, wmo_ref[...])


def _const_spec(shape):
    nd = len(shape)
    return pl.BlockSpec(shape, lambda *_: (0,) * nd, pipeline_mode=pl.Buffered(1))


def _tile_specs(tile, nt, steps, mod_row0):
    def nxt(s):
        return jnp.minimum(s + 1, steps - 1)

    x_spec = pl.BlockSpec((None, tile, D_MODEL), lambda s: (s // nt, s % nt, 0))
    xn_spec = pl.BlockSpec((None, tile, D_MODEL), lambda s: (nxt(s) // nt, nxt(s) % nt, 0))

    def mod_spec(chunk):
        return pl.BlockSpec((None, 1, D_MODEL), lambda s: (mod_row0 + s // nt, 0, chunk))

    def modn_spec(chunk):
        return pl.BlockSpec((None, 1, D_MODEL), lambda s: (mod_row0 + nxt(s) // nt, 0, chunk))

    return x_spec, xn_spec, mod_spec, modn_spec


BF16_SUBLANES = 16


def _cast_stream_spec(shape, steps):
    rows, cols = shape
    for hold in range(1, steps + 1):
        nblocks, rem = divmod(steps, hold)
        if rem == 0 and rows % nblocks == 0 and (rows // nblocks) % BF16_SUBLANES == 0:
            return pl.BlockSpec((rows // nblocks, cols), lambda s: (s // hold, 0))
    raise ValueError(f"no bf16-aligned row split of {shape} over {steps} steps")


def _mix_prompt_call(x, mod3, mod_row0, n1g, win, cw, wco, sinks, wao, wmo, cast_along, tile):
    bsz, seq, _ = x.shape
    nt = seq // tile
    steps = bsz * nt
    x_spec, xn_spec, mod_spec, modn_spec = _tile_specs(tile, nt, steps, mod_row0)
    cast_specs = [_cast_stream_spec(w.shape, steps) for w in cast_along]

    def state_spec(d1, d2):
        return pl.BlockSpec((None, d1, d2), lambda s: (s // nt, 0, 0))

    return pl.pallas_call(
        functools.partial(_mix_prompt_kernel, tiles_per_seq=nt, n_cast=len(cast_along)),
        grid=(steps,),
        in_specs=[x_spec, mod_spec(0), mod_spec(1), mod_spec(2),
                  xn_spec, modn_spec(0), modn_spec(1),
                  _const_spec((1, D_MODEL)),
                  _const_spec((D_MODEL, IN_WIDTH)),
                  _const_spec((3, D_MODEL)),
                  _const_spec((D_MODEL, D_MODEL)),
                  pl.BlockSpec(memory_space=pltpu.SMEM),
                  _const_spec((D_MODEL, D_MODEL)),
                  _const_spec((D_MODEL, D_MODEL))] + cast_specs,
        out_specs=[x_spec,
                   state_spec(2, D_MODEL),
                   state_spec(KV_WIDTH, WINDOW),
                   state_spec(KV_WIDTH, WINDOW)] + cast_specs,
        out_shape=[jax.ShapeDtypeStruct((bsz, seq, D_MODEL), _F32),
                   jax.ShapeDtypeStruct((bsz, 2, D_MODEL), _F32),
                   jax.ShapeDtypeStruct((bsz, KV_WIDTH, WINDOW), _F32),
                   jax.ShapeDtypeStruct((bsz, KV_WIDTH, WINDOW), _F32)]
        + [jax.ShapeDtypeStruct(w.shape, _BF) for w in cast_along],
        scratch_shapes=[pltpu.VMEM((CARRY_ROWS + tile, D_MODEL), _F32),
                        pltpu.VMEM((2 * N_KV_HEADS, WINDOW + tile, LANES), _BF),
                        pltpu.VMEM((2 * N_KV_HEADS, WINDOW + tile, LANES), _BF),
                        pltpu.VMEM((tile, D_MODEL), _BF),
                        pltpu.VMEM((N_HEADS, WINDOW, 2 * WINDOW), _F32),
                        pltpu.VMEM((tile, D_MODEL), _BF)],
        compiler_params=pltpu.CompilerParams(
            dimension_semantics=("arbitrary",),
            vmem_limit_bytes=VMEM_LIMIT),
        name="mix_prompt",
    )(x, mod3, mod3, mod3, x, mod3, mod3, n1g, win, cw, wco, sinks, wao, wmo, *cast_along)


def _conv_seq(u2d, st, cw, nseq, tlen):
    width = u2d.shape[-1]
    shape3 = (nseq, tlen, width)
    u3 = u2d.reshape(shape3)
    r1 = pltpu.roll(u2d, 1, 0).reshape(shape3)
    r2 = pltpu.roll(u2d, 2, 0).reshape(shape3)
    tpos = lax.broadcasted_iota(jnp.int32, shape3, 1)
    st0 = jnp.broadcast_to(st[:, 0:1, :], shape3)
    st1 = jnp.broadcast_to(st[:, 1:2, :], shape3)
    um1 = jnp.where(tpos == 0, st1, r1)
    um2 = jnp.where(tpos == 0, st0, jnp.where(tpos == 1, st1, r2))
    y = um2 * cw[0:1, :] + um1 * cw[1:2, :] + u3 * cw[2:3, :]
    return y.reshape(nseq * tlen, width), u3[:, tlen - 2:tlen, :]


def _mix_sample_kernel(x_ref, sh_ref, sc_ref, g_ref, n1g_ref, win_ref, cw_ref, wco_ref,
                       sinks_ref, wao_ref, wmo_ref, st_ref, ck_ref, cv_ref,
                       x1_ref, conv_ref, k_ref, v_ref,
                       kmain, kdrop, vmain, vdrop):
    nseq, tlen, _ = x_ref.shape
    rows = nseq * tlen

    x3 = x_ref[...]
    h3 = _rms(x3, n1g_ref[...]) * (1.0 + sc_ref[...]) + sh_ref[...]
    hf = h3.reshape(rows, D_MODEL)
    h_now = [hf.astype(_BF)]

    def proj(lo, width):
        return _dot(h_now[0], win_ref[:, lo:lo + width])

    def order_after(tile_f32):
        bits = pltpu.bitcast(tile_f32[0:8, :], jnp.uint32)
        zero = ((bits >> 16) >> 16).astype(_F32)
        h_now[0] = (hf + jnp.tile(zero, (rows // 8, D_MODEL // LANES))).astype(_BF)

    side = {}

    def job_q():
        side["q"] = proj(_OFF_Q, D_MODEL) * Q_SCALE

    def job_conv_in():
        u = proj(_OFF_C, D_MODEL) * proj(_OFF_X, D_MODEL)
        side["uc"], new_st = _conv_seq(u, st_ref[...], cw_ref[...], nseq, tlen)
        conv_ref[...] = new_st

    def job_gate_b():
        side["pb"] = proj(_OFF_B, D_MODEL)

    def job_conv_out():
        side["ya"] = _dot((side.pop("pb") * side.pop("uc")).astype(_BF), wco_ref[...])

    def job_ga():
        side["ga"] = jax.nn.sigmoid(proj(_OFF_GA, D_MODEL))

    def job_gb():
        side["gb"] = jax.nn.sigmoid(proj(_OFF_GB, D_MODEL))

    keep = WINDOW - tlen
    kvt_new = proj(_OFF_K, 2 * KV_WIDTH).T
    keep_old = lax.broadcasted_iota(jnp.int32, (KV_WIDTH, WINDOW), 1) < keep
    seq_per_tile = LANES // tlen

    def window_update(s):
        tile = s // seq_per_tile
        shift = (keep - (s % seq_per_tile) * tlen) % LANES
        for cache_ref, out_ref, main_scr, drop_scr, base in (
                (ck_ref, k_ref, kmain, kdrop, 0), (cv_ref, v_ref, vmain, vdrop, KV_WIDTH)):
            rolled = pltpu.roll(cache_ref[s], keep, 1)
            new = pltpu.roll(kvt_new[base:base + KV_WIDTH, tile * LANES:(tile + 1) * LANES], shift, 1)
            win = jnp.where(keep_old, rolled, new)
            out_ref[s] = win
            main_scr[s] = win.astype(_BF)
            drop_scr[s] = rolled.astype(_BF)
        return win

    window_jobs = [job_q, job_conv_in, job_gate_b, job_conv_out]
    group = -(-nseq // len(window_jobs))
    for s in range(nseq):
        last_win = window_update(s)
        if window_jobs and (s + 1) % group == 0:
            order_after(last_win)
            window_jobs.pop(0)()
    while window_jobs:
        window_jobs.pop(0)()
    job_ga()
    attn_jobs = [job_gb]

    qf = side.pop("q")
    lo = lax.broadcasted_iota(jnp.int32, (rows, LANES), 1) < HEAD_DIM
    zero = jnp.zeros((rows, LANES), _F32)
    heads_per_tile = N_HEADS // (KV_WIDTH // LANES)
    m_rows = heads_per_tile * tlen
    row = lax.broadcasted_iota(jnp.int32, (m_rows, 2 * WINDOW), 0)
    col = lax.broadcasted_iota(jnp.int32, (m_rows, 2 * WINDOW), 1)
    tq = row % tlen
    in_main = col < WINDOW
    dist = jnp.where(in_main, keep + tq - col, 2 * WINDOW + keep + tq - col)
    valid = (in_main & (dist >= 0)) | ((col >= WINDOW + keep) & (dist <= WINDOW))
    distf = dist.astype(_F32)
    row1 = lax.broadcasted_iota(jnp.int32, (m_rows, 1), 0)

    o_cols = [None] * N_QCOLS
    for kc in range(KV_WIDTH // LANES):
        parts = []
        slope = jnp.zeros((m_rows, 1), _F32)
        sink = jnp.zeros((m_rows, 1), _F32)
        for j in range(heads_per_tile):
            head = heads_per_tile * kc + j
            c, par, half = head // HEADS_PER_COL, j % HEADS_PER_COL, j // (heads_per_tile // 2)
            qc = qf[:, c * LANES:(c + 1) * LANES]
            src = qc if par == half else pltpu.roll(qc, HEAD_DIM, 1)
            part = jnp.where(lo, src, zero) if half == 0 else jnp.where(lo, zero, src)
            parts.append(part.reshape(nseq, tlen, LANES))
            mine = (row1 >= j * tlen) & (row1 < (j + 1) * tlen)
            slope = jnp.where(mine, _slope(head), slope)
            sink = jnp.where(mine, sinks_ref[head] * LOG2E, sink)
        lhs = jnp.concatenate(parts, axis=1).astype(_BF)
        bias = jnp.where(valid, -(slope * distf), NEG_INF)
        rsl = slice(kc * LANES, (kc + 1) * LANES)
        wk = jnp.concatenate([kmain[:, rsl, :], kdrop[:, rsl, :]], axis=2)
        wv = jnp.concatenate([vmain[:, rsl, :], vdrop[:, rsl, :]], axis=2)
        sc = jnp.einsum('smk,skn->smn', lhs, wk, preferred_element_type=_F32) + bias
        p, inv = _softmax_parts(sc, sink)
        r = jnp.einsum('smk,snk->smn', p.astype(_BF), wv, preferred_element_type=_F32) * inv
        for m2 in range(heads_per_tile // HEADS_PER_COL):
            c = (heads_per_tile // HEADS_PER_COL) * kc + m2
            half = m2 // (heads_per_tile // (2 * HEADS_PER_COL))
            r0 = r[:, (2 * m2) * tlen:(2 * m2 + 1) * tlen, :].reshape(rows, LANES)
            r1 = r[:, (2 * m2 + 1) * tlen:(2 * m2 + 2) * tlen, :].reshape(rows, LANES)
            if half == 0:
                o_cols[c] = jnp.where(lo, r0, pltpu.roll(r1, HEAD_DIM, 1))
            else:
                o_cols[c] = jnp.where(lo, pltpu.roll(r0, HEAD_DIM, 1), r1)
        if attn_jobs:
            order_after(o_cols[c])
            attn_jobs.pop(0)()
    while attn_jobs:
        attn_jobs.pop(0)()
    o = jnp.concatenate(o_cols, axis=1).astype(_BF)
    yb = _dot(o, wao_ref[...])

    mixed = (side["ga"] * side["ya"] + side["gb"] * yb).astype(_BF)
    y = _dot(mixed, wmo_ref[...]).reshape(nseq, tlen, D_MODEL)
    x1_ref[...] = x3 + g_ref[...] * y


def _mix_sample_call(x, mod3, n1g, win, cw, wco, sinks, wao, wmo, st, ck, cv, chunk):
    nseq, tlen, _ = x.shape
    assert LANES % tlen == 0 and chunk % (LANES // tlen) == 0

    def mod_spec(col):
        return pl.BlockSpec((chunk, 1, D_MODEL), lambda s: (s, 0, col))

    def seq_spec(d1, d2):
        return pl.BlockSpec((chunk, d1, d2), lambda s: (s, 0, 0))

    return pl.pallas_call(
        _mix_sample_kernel,
        grid=(nseq // chunk,),
        in_specs=[seq_spec(tlen, D_MODEL),
                  mod_spec(0), mod_spec(1), mod_spec(2),
                  _const_spec((1, D_MODEL)),
                  _const_spec((D_MODEL, IN_WIDTH)),
                  _const_spec((3, D_MODEL)),
                  _const_spec((D_MODEL, D_MODEL)),
                  pl.BlockSpec(memory_space=pltpu.SMEM),
                  _const_spec((D_MODEL, D_MODEL)),
                  _const_spec((D_MODEL, D_MODEL)),
                  seq_spec(2, D_MODEL),
                  seq_spec(KV_WIDTH, WINDOW),
                  seq_spec(KV_WIDTH, WINDOW)],
        out_specs=[seq_spec(tlen, D_MODEL),
                   seq_spec(2, D_MODEL),
                   seq_spec(KV_WIDTH, WINDOW),
                   seq_spec(KV_WIDTH, WINDOW)],
        out_shape=[jax.ShapeDtypeStruct((nseq, tlen, D_MODEL), _F32),
                   jax.ShapeDtypeStruct((nseq, 2, D_MODEL), _F32),
                   jax.ShapeDtypeStruct((nseq, KV_WIDTH, WINDOW), _F32),
                   jax.ShapeDtypeStruct((nseq, KV_WIDTH, WINDOW), _F32)],
        scratch_shapes=[pltpu.VMEM((chunk, KV_WIDTH, WINDOW), _BF) for _ in range(4)],
        compiler_params=pltpu.CompilerParams(
            dimension_semantics=("arbitrary",),
            vmem_limit_bytes=VMEM_LIMIT),
        name="mix_sample",
    )(x, mod3, mod3, mod3, n1g, win, cw, wco, sinks, wao, wmo, st, ck, cv)


def _ffn_tail(x1, ac, val, g, wdown_ref, fg):
    hmid = (_gelu(ac) * val).astype(_BF)
    x2 = x1 + g * _dot(hmid, wdown_ref[...])
    return _rms(x2, fg)


def _ffn_prompt_kernel(x1_ref, sh_ref, sc_ref, g_ref, n2g_ref, wup_ref, fcw_ref, wdown_ref,
                       fg_ref, y_ref, ffn_ref, a_ext, *, parts):
    i = pl.program_id(1)
    t = x1_ref.shape[0]
    rows = t // parts

    @pl.when(i == 0)
    def _():
        a_ext[0:CARRY_ROWS, :] = jnp.zeros((CARRY_ROWS, D_FF), _F32)

    fcw = fcw_ref[...]

    def up(part):
        r0 = part * rows
        e0 = CARRY_ROWS + r0
        h2 = _mod_norm(x1_ref[r0:r0 + rows, :], n2g_ref[...], sc_ref[...], sh_ref[...])
        a = _dot(h2, wup_ref[:, 0:D_FF])
        val = _dot(h2, wup_ref[:, D_FF:2 * D_FF])
        a_ext[e0:e0 + rows, :] = a
        ac = (a_ext[e0 - 2:e0 - 2 + rows, :] * fcw[0:1, :]
              + a_ext[e0 - 1:e0 - 1 + rows, :] * fcw[1:2, :]
              + a * fcw[2:3, :])
        return ac, val

    def down(part, ac, val):
        r0 = part * rows
        y_ref[r0:r0 + rows, :] = _ffn_tail(x1_ref[r0:r0 + rows, :], ac, val, g_ref[...],
                                           wdown_ref, fg_ref[...])

    for part in range(parts):
        down(part, *up(part))
    ffn_ref[...] = a_ext[CARRY_ROWS + t - 2:CARRY_ROWS + t, :]
    a_ext[0:CARRY_ROWS, :] = a_ext[t:t + CARRY_ROWS, :]


def _ffn_prompt_call(x1, mod3, mod_row0, n2g, wup, fcw, wdown, fg, tile, parts):
    bsz, seq, _ = x1.shape

    def mod_spec(chunk):
        return pl.BlockSpec((None, 1, D_MODEL), lambda b, i: (mod_row0 + b, 0, chunk))

    x_spec = pl.BlockSpec((None, tile, D_MODEL), lambda b, i: (b, i, 0))
    return pl.pallas_call(
        functools.partial(_ffn_prompt_kernel, parts=parts),
        grid=(bsz, seq // tile),
        in_specs=[x_spec, mod_spec(3), mod_spec(4), mod_spec(5),
                  _const_spec((1, D_MODEL)),
                  _const_spec((D_MODEL, 2 * D_FF)),
                  _const_spec((3, D_FF)),
                  _const_spec((D_FF, D_MODEL)),
                  _const_spec((1, D_MODEL))],
        out_specs=[x_spec,
                   pl.BlockSpec((None, 2, D_FF), lambda b, i: (b, 0, 0))],
        out_shape=[jax.ShapeDtypeStruct((bsz, seq, D_MODEL), _F32),
                   jax.ShapeDtypeStruct((bsz, 2, D_FF), _F32)],
        scratch_shapes=[pltpu.VMEM((CARRY_ROWS + tile, D_FF), _F32)],
        compiler_params=pltpu.CompilerParams(
            dimension_semantics=("arbitrary", "arbitrary"),
            vmem_limit_bytes=VMEM_LIMIT),
        name="ffn_prompt",
    )(x1, mod3, mod3, mod3, n2g, wup, fcw, wdown, fg)


def _ffn_sample_kernel(x1_ref, sh_ref, sc_ref, g_ref, n2g_ref, wup_ref, fcw_ref, wdown_ref,
                       fg_ref, st_ref, y_ref, ffn_ref):
    nseq, tlen, _ = x1_ref.shape
    rows = nseq * tlen
    x3 = x1_ref[...]
    h3 = _rms(x3, n2g_ref[...]) * (1.0 + sc_ref[...]) + sh_ref[...]
    h2 = h3.reshape(rows, D_MODEL).astype(_BF)
    a = _dot(h2, wup_ref[:, 0:D_FF])
    val = _dot(h2, wup_ref[:, D_FF:2 * D_FF])
    ac, new_st = _conv_seq(a, st_ref[...], fcw_ref[...], nseq, tlen)
    ffn_ref[...] = new_st
    hmid = (_gelu(ac) * val).astype(_BF)
    y = _dot(hmid, wdown_ref[...]).reshape(nseq, tlen, D_MODEL)
    x2 = x3 + g_ref[...] * y
    y_ref[...] = _rms(x2, fg_ref[...])


def _ffn_sample_call(x1, mod3, n2g, wup, fcw, wdown, fg, st, chunk):
    nseq, tlen, _ = x1.shape

    def mod_spec(col):
        return pl.BlockSpec((chunk, 1, D_MODEL), lambda s: (s, 0, col))

    def seq_spec(d1, d2):
        return pl.BlockSpec((chunk, d1, d2), lambda s: (s, 0, 0))

    return pl.pallas_call(
        _ffn_sample_kernel,
        grid=(nseq // chunk,),
        in_specs=[seq_spec(tlen, D_MODEL),
                  mod_spec(3), mod_spec(4), mod_spec(5),
                  _const_spec((1, D_MODEL)),
                  _const_spec((D_MODEL, 2 * D_FF)),
                  _const_spec((3, D_FF)),
                  _const_spec((D_FF, D_MODEL)),
                  _const_spec((1, D_MODEL)),
                  seq_spec(2, D_FF)],
        out_specs=[seq_spec(tlen, D_MODEL), seq_spec(2, D_FF)],
        out_shape=[jax.ShapeDtypeStruct((nseq, tlen, D_MODEL), _F32),
                   jax.ShapeDtypeStruct((nseq, 2, D_FF), _F32)],
        compiler_params=pltpu.CompilerParams(
            dimension_semantics=("arbitrary",),
            vmem_limit_bytes=VMEM_LIMIT),
        name="ffn_sample",
    )(x1, mod3, mod3, mod3, n2g, wup, fcw, wdown, fg, st)


PROMPT_TILE = 512
FFN_TILE = 1024
FFN_PARTS = 2
SAMPLE_CHUNK = 16
FFN_SAMPLE_CHUNK = 64


def kernel(x_prompt, x_sample, c_prompt, c_sample, state_conv, cache_k_win, cache_v_win,
           state_ffn_conv, norm1_g, norm2_g, w_ada, b_ada, w_in, conv_w, w_conv_out,
           attn_sinks, w_attn_out, w_mix_out, w_up, ffn_conv_w, w_down, final_g):
    depth = w_in.shape[0]
    assert depth == 1, "the final norm is fused into the (single) layer's FFN call"
    nsamp = x_sample.shape[0]
    bsz = x_prompt.shape[0]
    xp, xs = x_prompt, x_sample
    fg = final_g.reshape(1, D_MODEL)
    outs = [[] for _ in range(8)]
    for layer in range(depth):
        mod3 = _mod_call(c_sample, c_prompt, w_ada[layer], b_ada[layer].reshape(1, -1))
        n1g = norm1_g[layer].reshape(1, D_MODEL)
        n2g = norm2_g[layer].reshape(1, D_MODEL)
        win = w_in[layer].astype(_BF)
        wco = w_conv_out[layer].astype(_BF)
        wao = w_attn_out[layer].astype(_BF)
        wmo = w_mix_out[layer].astype(_BF)
        cw = conv_w[layer]
        fcw = ffn_conv_w[layer]
        sinks = attn_sinks[layer]
        fg_l = fg
        def to_dim_major(win_state):
            return jnp.transpose(win_state, (0, 2, 3, 1)).reshape(-1, KV_WIDTH, WINDOW)

        def to_pos_major(t3):
            return jnp.transpose(t3.reshape(-1, N_KV_HEADS, HEAD_DIM, WINDOW), (0, 3, 1, 2))

        x1p, conv_p, k_p, v_p, wup, wdown = _mix_prompt_call(
            xp, mod3, nsamp, n1g, win, cw, wco, sinks, wao, wmo,
            (w_up[layer], w_down[layer]), PROMPT_TILE)
        x1s, conv_s, k_s, v_s = _mix_sample_call(
            xs, mod3, n1g, win, cw, wco, sinks, wao, wmo, state_conv[layer],
            to_dim_major(cache_k_win[layer]), to_dim_major(cache_v_win[layer]), SAMPLE_CHUNK)
        xp, ffn_p = _ffn_prompt_call(x1p, mod3, nsamp, n2g, wup, fcw, wdown, fg_l,
                                     FFN_TILE, FFN_PARTS)
        xs, ffn_s = _ffn_sample_call(x1s, mod3, n2g, wup, fcw, wdown, fg_l,
                                     state_ffn_conv[layer], FFN_SAMPLE_CHUNK)
        for lst, val in zip(outs, (conv_p, to_pos_major(k_p), to_pos_major(v_p), ffn_p,
                                   conv_s, to_pos_major(k_s), to_pos_major(v_s), ffn_s)):
            lst.append(val)
    return (xp, xs) + tuple(jnp.stack(lst, axis=0) for lst in outs)
```

```python
import functools

import jax
import jax.numpy as jnp
from jax import lax
from jax.experimental import pallas as pl
from jax.experimental.pallas import tpu as pltpu

D_MODEL = 1024
HEAD_DIM = 64
N_HEADS = 16
N_KV_HEADS = 4
WINDOW = 128
D_FF = 2816
KV_WIDTH = N_KV_HEADS * HEAD_DIM
IN_WIDTH = 3 * D_MODEL + D_MODEL + 2 * KV_WIDTH + 2 * D_MODEL
RMS_EPS = 1e-6
NEG_INF = -1e30
LANES = 128
HEADS_PER_COL = LANES // HEAD_DIM
N_QCOLS = N_HEADS * HEAD_DIM // LANES
CARRY_ROWS = 8
VMEM_LIMIT = 60 * 1024 * 1024

_OFF_B, _OFF_C, _OFF_X = 0, D_MODEL, 2 * D_MODEL
_OFF_Q = 3 * D_MODEL
_OFF_K = _OFF_Q + D_MODEL
_OFF_V = _OFF_K + KV_WIDTH
_OFF_GA = _OFF_V + KV_WIDTH
_OFF_GB = _OFF_GA + D_MODEL

_BF = jnp.bfloat16
_F32 = jnp.float32
_NT = (((1,), (1,)), ((), ()))


LOG2E = 1.4426950408889634
Q_SCALE = HEAD_DIM ** -0.5 * LOG2E


def _slope(head):
    return 2.0 ** (-8.0 * (head + 1) / N_HEADS) * LOG2E


def _rms(x, g):
    return x * lax.rsqrt(jnp.mean(x * x, axis=-1, keepdims=True) + RMS_EPS) * g


def _dot(a, b):
    return jnp.dot(a, b, preferred_element_type=_F32)


def _gelu(x):
    return 0.5 * x * (1.0 + lax.erf(x * (0.5 ** 0.5)))


def _split_heads(t2d, lo):
    rolled = pltpu.roll(t2d, HEAD_DIM, 1)
    zero = jnp.zeros_like(t2d)
    return (jnp.where(lo, t2d, zero), jnp.where(lo, zero, rolled),
            jnp.where(lo, rolled, zero), jnp.where(lo, zero, t2d))


def _softmax_parts(s, sink):
    m = jnp.maximum(jnp.max(s, axis=-1, keepdims=True), sink)
    p = jnp.exp2(s - m)
    denom = jnp.sum(p, axis=-1, keepdims=True) + jnp.exp2(sink - m)
    return p, 1.0 / denom


MOD_COLS = 2048


def _mod_kernel(cs_ref, cp_ref, w_ref, b_ref, o_ref):
    w = w_ref[...].astype(_BF)
    n0 = cs_ref.shape[0]
    for c_ref, r0 in ((cs_ref, 0), (cp_ref, n0)):
        c = c_ref[...]
        a = (c * jax.nn.sigmoid(c)).astype(_BF)
        res = _dot(a, w) + b_ref[...]
        n = c.shape[0]
        o_ref[r0:r0 + n] = res.reshape(n, 1, res.shape[-1])


def _mod_call(c_sample, c_prompt, w_ada, b_ada):
    ns, npr = c_sample.shape[0], c_prompt.shape[0]
    n = ns + npr
    return pl.pallas_call(
        _mod_kernel,
        grid=(6 * D_MODEL // MOD_COLS,),
        in_specs=[pl.BlockSpec((ns, D_MODEL), lambda j: (0, 0)),
                  pl.BlockSpec((npr, D_MODEL), lambda j: (0, 0)),
                  pl.BlockSpec((D_MODEL, MOD_COLS), lambda j: (0, j)),
                  pl.BlockSpec((1, MOD_COLS), lambda j: (0, j))],
        out_specs=pl.BlockSpec((n, 1, MOD_COLS), lambda j: (0, 0, j)),
        out_shape=jax.ShapeDtypeStruct((n, 1, 6 * D_MODEL), _F32),
        name="adaln_mod",
    )(c_sample, c_prompt, w_ada, b_ada)


def _mod_norm(x, g, sc, sh):
    return (_rms(x, g) * (1.0 + sc) + sh).astype(_BF)


def _mix_prompt_kernel(x_ref, sh_ref, sc_ref, g_ref, xn_ref, shn_ref, scn_ref, n1g_ref,
                       win_ref, cw_ref, wco_ref, sinks_ref, wao_ref, wmo_ref, *rest,
                       tiles_per_seq, n_cast):
    cast_in, rest = rest[:n_cast], rest[n_cast:]
    (x1_ref, conv_ref, k_ref, v_ref), rest = rest[:4], rest[4:]
    cast_out, (u_ext, k_ext, v_ext, o_scr, bias_scr, h_scr) = rest[:n_cast], rest[n_cast:]
    step = pl.program_id(0)
    i = step % tiles_per_seq
    t = x_ref.shape[0]
    nblk = t // WINDOW

    for src, dst in zip(cast_in, cast_out):
        dst[...] = src[...].astype(_BF)

    @pl.when(step == 0)
    def _():
        h_scr[...] = _mod_norm(x_ref[...], n1g_ref[...], sc_ref[...], sh_ref[...])
        row = lax.broadcasted_iota(jnp.int32, (WINDOW, 2 * WINDOW), 0)
        col = lax.broadcasted_iota(jnp.int32, (WINDOW, 2 * WINDOW), 1)
        dist = WINDOW + row - col
        valid = (dist >= 0) & (dist <= WINDOW)
        distf = dist.astype(_F32)
        for head in range(N_HEADS):
            bias_scr[head] = jnp.where(valid, -(_slope(head) * distf), NEG_INF)

    @pl.when(i == 0)
    def _():
        u_ext[0:CARRY_ROWS, :] = jnp.zeros((CARRY_ROWS, D_MODEL), _F32)
        k_ext[:, 0:WINDOW, :] = jnp.zeros((2 * N_KV_HEADS, WINDOW, LANES), _BF)
        v_ext[:, 0:WINDOW, :] = jnp.zeros((2 * N_KV_HEADS, WINDOW, LANES), _BF)

    def proj(lo, width):
        return _dot(h_scr[...], win_ref[:, lo:lo + width])

    def next_tile_norm():
        h_scr[...] = _mod_norm(xn_ref[...], n1g_ref[...], scn_ref[...], shn_ref[...])

    side = {}

    def conv_in():
        u = proj(_OFF_C, D_MODEL) * proj(_OFF_X, D_MODEL)
        u_ext[CARRY_ROWS:CARRY_ROWS + t, :] = u
        cw = cw_ref[...]
        side["uc"] = (u_ext[CARRY_ROWS - 2:CARRY_ROWS - 2 + t, :] * cw[0:1, :]
                      + u_ext[CARRY_ROWS - 1:CARRY_ROWS - 1 + t, :] * cw[1:2, :]
                      + u * cw[2:3, :])
        conv_ref[...] = u[t - 2:t, :]
        u_ext[0:CARRY_ROWS, :] = u[t - CARRY_ROWS:t, :]

    def conv_out():
        pb = proj(_OFF_B, D_MODEL)
        side["ya"] = _dot((pb * side.pop("uc")).astype(_BF), wco_ref[...])

    def gate_a():
        side["ga"] = jax.nn.sigmoid(proj(_OFF_GA, D_MODEL))

    def gate_b():
        side["gb"] = jax.nn.sigmoid(proj(_OFF_GB, D_MODEL))
        next_tile_norm()

    side_jobs = [conv_in, conv_out, gate_a, gate_b]

    q = (proj(_OFF_Q, D_MODEL) * Q_SCALE).astype(_BF)
    kf = proj(_OFF_K, KV_WIDTH)
    vf = proj(_OFF_V, KV_WIDTH)
    k_ref[...] = kf[t - WINDOW:t, :].T
    v_ref[...] = vf[t - WINDOW:t, :].T
    lane_t = lax.broadcasted_iota(jnp.int32, (t, LANES), 1)
    lo = lane_t < HEAD_DIM
    ones_lane = (HEAD_DIM, 0)
    for kc in range(KV_WIDTH // LANES):
        ks = _split_heads(kf[:, kc * LANES:(kc + 1) * LANES], lo)
        vs = _split_heads(vf[:, kc * LANES:(kc + 1) * LANES], lo)
        for j in range(4):
            k_ext[4 * kc + j, WINDOW:WINDOW + t, :] = ks[j].astype(_BF)
            v_one = jnp.where(lane_t == ones_lane[j % HEADS_PER_COL], 1.0, vs[j])
            v_ext[4 * kc + j, WINDOW:WINDOW + t, :] = v_one.astype(_BF)

    first_cols = lax.broadcasted_iota(jnp.int32, (WINDOW, 2 * WINDOW), 1) < WINDOW
    no_prev = first_cols & (i == 0)
    lo_q = lax.broadcasted_iota(jnp.int32, (WINDOW, LANES), 1) < HEAD_DIM
    for blk in range(nblk):
        r0 = blk * WINDOW
        for c in range(N_QCOLS):
            hkv = c // 2
            qc = q[r0:r0 + WINDOW, c * LANES:(c + 1) * LANES]
            o_pair = []
            for par in range(HEADS_PER_COL):
                head = HEADS_PER_COL * c + par
                kk = k_ext[2 * hkv + par, r0:r0 + 2 * WINDOW, :]
                s = lax.dot_general(qc, kk, _NT, preferred_element_type=_F32) + bias_scr[head]
                if blk == 0:
                    s = jnp.where(no_prev, NEG_INF, s)
                sink = sinks_ref[head] * LOG2E
                m = jnp.maximum(jnp.max(s, axis=-1, keepdims=True), sink)
                p = jnp.exp2(s - m).astype(_BF)
                o_h = _dot(p, v_ext[2 * hkv + par, r0:r0 + 2 * WINDOW, :])
                col = ones_lane[par]
                denom = o_h[:, col:col + 1] + jnp.exp2(sink - m)
                o_pair.append(o_h * (1.0 / denom))
            o_c = jnp.where(lo_q, o_pair[0], o_pair[1])
            o_scr[r0:r0 + WINDOW, c * LANES:(c + 1) * LANES] = o_c.astype(_BF)
        for _ in range(-(-len(side_jobs) // (nblk - blk))):
            side_jobs.pop(0)()
    k_ext[:, 0:WINDOW, :] = k_ext[:, t:t + WINDOW, :]
    v_ext[:, 0:WINDOW, :] = v_ext[:, t:t + WINDOW, :]
    yb = _dot(o_scr[...], wao_ref[...])

    mixed = (side["ga"] * side["ya"] + side["gb"] * yb).astype(_BF)
    x1_ref[...] = x_ref[...] + g_ref[...] * _dot(mixed, wmo_ref[...])


def _const_spec(shape):
    nd = len(shape)
    return pl.BlockSpec(shape, lambda *_: (0,) * nd, pipeline_mode=pl.Buffered(1))


def _tile_specs(tile, nt, steps, mod_row0):
    def nxt(s):
        return jnp.minimum(s + 1, steps - 1)

    x_spec = pl.BlockSpec((None, tile, D_MODEL), lambda s: (s // nt, s % nt, 0))
    xn_spec = pl.BlockSpec((None, tile, D_MODEL), lambda s: (nxt(s) // nt, nxt(s) % nt, 0))

    def mod_spec(chunk):
        return pl.BlockSpec((None, 1, D_MODEL), lambda s: (mod_row0 + s // nt, 0, chunk))

    def modn_spec(chunk):
        return pl.BlockSpec((None, 1, D_MODEL), lambda s: (mod_row0 + nxt(s) // nt, 0, chunk))

    return x_spec, xn_spec, mod_spec, modn_spec


BF16_SUBLANES = 16


def _cast_stream_spec(shape, steps):
    rows, cols = shape
    for hold in range(1, steps + 1):
        nblocks, rem = divmod(steps, hold)
        if rem == 0 and rows % nblocks == 0 and (rows // nblocks) % BF16_SUBLANES == 0:
            return pl.BlockSpec((rows // nblocks, cols), lambda s: (s // hold, 0))
    raise ValueError(f"no bf16-aligned row split of {shape} over {steps} steps")


def _mix_prompt_call(x, mod3, mod_row0, n1g, win, cw, wco, sinks, wao, wmo, cast_along, tile):
    bsz, seq, _ = x.shape
    nt = seq // tile
    steps = bsz * nt
    x_spec, xn_spec, mod_spec, modn_spec = _tile_specs(tile, nt, steps, mod_row0)
    cast_specs = [_cast_stream_spec(w.shape, steps) for w in cast_along]

    def state_spec(d1, d2):
        return pl.BlockSpec((None, d1, d2), lambda s: (s // nt, 0, 0))

    return pl.pallas_call(
        functools.partial(_mix_prompt_kernel, tiles_per_seq=nt, n_cast=len(cast_along)),
        grid=(steps,),
        in_specs=[x_spec, mod_spec(0), mod_spec(1), mod_spec(2),
                  xn_spec, modn_spec(0), modn_spec(1),
                  _const_spec((1, D_MODEL)),
                  _const_spec((D_MODEL, IN_WIDTH)),
                  _const_spec((3, D_MODEL)),
                  _const_spec((D_MODEL, D_MODEL)),
                  pl.BlockSpec(memory_space=pltpu.SMEM),
                  _const_spec((D_MODEL, D_MODEL)),
                  _const_spec((D_MODEL, D_MODEL))] + cast_specs,
        out_specs=[x_spec,
                   state_spec(2, D_MODEL),
                   state_spec(KV_WIDTH, WINDOW),
                   state_spec(KV_WIDTH, WINDOW)] + cast_specs,
        out_shape=[jax.ShapeDtypeStruct((bsz, seq, D_MODEL), _F32),
                   jax.ShapeDtypeStruct((bsz, 2, D_MODEL), _F32),
                   jax.ShapeDtypeStruct((bsz, KV_WIDTH, WINDOW), _F32),
                   jax.ShapeDtypeStruct((bsz, KV_WIDTH, WINDOW), _F32)]
        + [jax.ShapeDtypeStruct(w.shape, _BF) for w in cast_along],
        scratch_shapes=[pltpu.VMEM((CARRY_ROWS + tile, D_MODEL), _F32),
                        pltpu.VMEM((2 * N_KV_HEADS, WINDOW + tile, LANES), _BF),
                        pltpu.VMEM((2 * N_KV_HEADS, WINDOW + tile, LANES), _BF),
                        pltpu.VMEM((tile, D_MODEL), _BF),
                        pltpu.VMEM((N_HEADS, WINDOW, 2 * WINDOW), _F32),
                        pltpu.VMEM((tile, D_MODEL), _BF)],
        compiler_params=pltpu.CompilerParams(
            dimension_semantics=("arbitrary",),
            vmem_limit_bytes=VMEM_LIMIT),
        name="mix_prompt",
    )(x, mod3, mod3, mod3, x, mod3, mod3, n1g, win, cw, wco, sinks, wao, wmo, *cast_along)


def _conv_seq(u2d, st, cw, nseq, tlen):
    width = u2d.shape[-1]
    shape3 = (nseq, tlen, width)
    u3 = u2d.reshape(shape3)
    r1 = pltpu.roll(u2d, 1, 0).reshape(shape3)
    r2 = pltpu.roll(u2d, 2, 0).reshape(shape3)
    tpos = lax.broadcasted_iota(jnp.int32, shape3, 1)
    st0 = jnp.broadcast_to(st[:, 0:1, :], shape3)
    st1 = jnp.broadcast_to(st[:, 1:2, :], shape3)
    um1 = jnp.where(tpos == 0, st1, r1)
    um2 = jnp.where(tpos == 0, st0, jnp.where(tpos == 1, st1, r2))
    y = um2 * cw[0:1, :] + um1 * cw[1:2, :] + u3 * cw[2:3, :]
    return y.reshape(nseq * tlen, width), u3[:, tlen - 2:tlen, :]


def _mix_sample_kernel(x_ref, sh_ref, sc_ref, g_ref, n1g_ref, win_ref, cw_ref, wco_ref,
                       sinks_ref, wao_ref, wmo_ref, st_ref, ck_ref, cv_ref,
                       x1_ref, conv_ref, k_ref, v_ref):
    nseq, tlen, _ = x_ref.shape
    rows = nseq * tlen

    assert rows == LANES, "one step's new keys must fill exactly one lane tile"

    x3 = x_ref[...]
    h3 = _rms(x3, n1g_ref[...]) * (1.0 + sc_ref[...]) + sh_ref[...]
    hf = h3.reshape(rows, D_MODEL)
    h_now = [hf.astype(_BF)]

    def proj(lo, width):
        return _dot(h_now[0], win_ref[:, lo:lo + width])

    def order_after(tile_f32):
        bits = pltpu.bitcast(tile_f32[0:8, :], jnp.uint32)
        zero = ((bits >> 16) >> 16).astype(_F32)
        h_now[0] = (hf + jnp.tile(zero, (rows // 8, D_MODEL // LANES))).astype(_BF)

    side = {}

    def job_conv_in():
        u = proj(_OFF_C, D_MODEL) * proj(_OFF_X, D_MODEL)
        side["uc"], new_st = _conv_seq(u, st_ref[...], cw_ref[...], nseq, tlen)
        conv_ref[...] = new_st

    def job_gate_b():
        side["pb"] = proj(_OFF_B, D_MODEL)

    def job_conv_out():
        side["ya"] = _dot((side.pop("pb") * side.pop("uc")).astype(_BF), wco_ref[...])

    def job_ga():
        side["ga"] = jax.nn.sigmoid(proj(_OFF_GA, D_MODEL))

    def job_gb():
        side["gb"] = jax.nn.sigmoid(proj(_OFF_GB, D_MODEL))

    keep = WINDOW - tlen
    qf = proj(_OFF_Q, D_MODEL) * Q_SCALE
    kvt_new = proj(_OFF_K, 2 * KV_WIDTH).T
    keep_old = lax.broadcasted_iota(jnp.int32, (KV_WIDTH, WINDOW), 1) < keep

    def window_update(s):
        shift = (keep - s * tlen) % LANES
        for cache_ref, out_ref, base in ((ck_ref, k_ref, 0), (cv_ref, v_ref, KV_WIDTH)):
            rolled = pltpu.roll(cache_ref[s], keep, 1)
            new = pltpu.roll(kvt_new[base:base + KV_WIDTH, :], shift, 1)
            win = jnp.where(keep_old, rolled, new)
            out_ref[s] = win
        return win

    lo = lax.broadcasted_iota(jnp.int32, (rows, LANES), 1) < HEAD_DIM
    zero = jnp.zeros((rows, LANES), _F32)
    n_tiles = KV_WIDTH // LANES
    heads_per_tile = N_HEADS // n_tiles
    m_rows = heads_per_tile * tlen
    row = lax.broadcasted_iota(jnp.int32, (m_rows, WINDOW), 0)
    col = lax.broadcasted_iota(jnp.int32, (m_rows, WINDOW), 1)
    dist_c = WINDOW + row % tlen - col
    valid_c = dist_c <= WINDOW
    shape3 = (nseq, m_rows, LANES)
    seq3 = lax.broadcasted_iota(jnp.int32, shape3, 0)
    row3 = lax.broadcasted_iota(jnp.int32, shape3, 1)
    col3 = lax.broadcasted_iota(jnp.int32, shape3, 2)
    dist_n = row3 % tlen - col3 % tlen
    valid_n = (col3 // tlen == seq3) & (dist_n >= 0)
    row1 = lax.broadcasted_iota(jnp.int32, (m_rows, 1), 0)

    def attention(kc):
        parts = []
        slope = jnp.zeros((m_rows, 1), _F32)
        sink = jnp.zeros((m_rows, 1), _F32)
        for j in range(heads_per_tile):
            head = heads_per_tile * kc + j
            c, par, half = head // HEADS_PER_COL, j % HEADS_PER_COL, j // (heads_per_tile // 2)
            qc = qf[:, c * LANES:(c + 1) * LANES]
            src = qc if par == half else pltpu.roll(qc, HEAD_DIM, 1)
            part = jnp.where(lo, src, zero) if half == 0 else jnp.where(lo, zero, src)
            parts.append(part.reshape(nseq, tlen, LANES))
            mine = (row1 >= j * tlen) & (row1 < (j + 1) * tlen)
            slope = jnp.where(mine, _slope(head), slope)
            sink = jnp.where(mine, sinks_ref[head] * LOG2E, sink)
        lhs = jnp.concatenate(parts, axis=1).astype(_BF)
        lhs2 = lhs.reshape(nseq * m_rows, LANES)
        rsl = slice(kc * LANES, (kc + 1) * LANES)
        k_new = kvt_new[rsl, :].astype(_BF)
        v_new = kvt_new[KV_WIDTH + kc * LANES:KV_WIDTH + (kc + 1) * LANES, :].astype(_BF)
        s_c = jnp.einsum('smk,skn->smn', lhs, ck_ref[:, rsl, :].astype(_BF),
                         preferred_element_type=_F32)
        s_n = _dot(lhs2, k_new).reshape(shape3)
        sc = jnp.concatenate(
            [s_c + jnp.where(valid_c, -(slope * dist_c.astype(_F32)), NEG_INF),
             s_n + jnp.where(valid_n, -(slope * dist_n.astype(_F32)), NEG_INF)], axis=2)
        p, inv = _softmax_parts(sc, sink)
        p = p.astype(_BF)
        r_c = jnp.einsum('smk,snk->smn', p[:, :, 0:WINDOW], cv_ref[:, rsl, :].astype(_BF),
                         preferred_element_type=_F32)
        r_n = lax.dot_general(p[:, :, WINDOW:].reshape(nseq * m_rows, LANES), v_new, _NT,
                              preferred_element_type=_F32).reshape(shape3)
        r = (r_c + r_n) * inv
        cols = {}
        for m2 in range(heads_per_tile // HEADS_PER_COL):
            c = (heads_per_tile // HEADS_PER_COL) * kc + m2
            half = m2 // (heads_per_tile // (2 * HEADS_PER_COL))
            r0 = r[:, (2 * m2) * tlen:(2 * m2 + 1) * tlen, :].reshape(rows, LANES)
            r1 = r[:, (2 * m2 + 1) * tlen:(2 * m2 + 2) * tlen, :].reshape(rows, LANES)
            if half == 0:
                cols[c] = jnp.where(lo, r0, pltpu.roll(r1, HEAD_DIM, 1))
            else:
                cols[c] = jnp.where(lo, pltpu.roll(r0, HEAD_DIM, 1), r1)
        return cols

    o_cols = {}
    job_conv_in()
    window_jobs = [job_gate_b, job_conv_out, job_ga, job_gb]
    group = -(-nseq // len(window_jobs))
    for s in range(nseq):
        if s == ATTN_AFTER_WINDOWS:
            for kc in range(n_tiles):
                o_cols.update(attention(kc))
        last_win = window_update(s)
        if window_jobs and (s + 1) % group == 0:
            window_jobs.pop(0)()
            order_after(last_win)
    while window_jobs:
        window_jobs.pop(0)()
    o = jnp.concatenate([o_cols[c] for c in range(N_QCOLS)], axis=1).astype(_BF)
    yb = _dot(o, wao_ref[...])

    mixed = (side["ga"] * side["ya"] + side["gb"] * yb).astype(_BF)
    y = _dot(mixed, wmo_ref[...]).reshape(nseq, tlen, D_MODEL)
    x1_ref[...] = x3 + g_ref[...] * y


def _mix_sample_call(x, mod3, n1g, win, cw, wco, sinks, wao, wmo, st, ck, cv, chunk):
    nseq, tlen, _ = x.shape
    assert LANES % tlen == 0 and chunk % (LANES // tlen) == 0

    def mod_spec(col):
        return pl.BlockSpec((chunk, 1, D_MODEL), lambda s: (s, 0, col))

    def seq_spec(d1, d2):
        return pl.BlockSpec((chunk, d1, d2), lambda s: (s, 0, 0))

    return pl.pallas_call(
        _mix_sample_kernel,
        grid=(nseq // chunk,),
        in_specs=[seq_spec(tlen, D_MODEL),
                  mod_spec(0), mod_spec(1), mod_spec(2),
                  _const_spec((1, D_MODEL)),
                  _const_spec((D_MODEL, IN_WIDTH)),
                  _const_spec((3, D_MODEL)),
                  _const_spec((D_MODEL, D_MODEL)),
                  pl.BlockSpec(memory_space=pltpu.SMEM),
                  _const_spec((D_MODEL, D_MODEL)),
                  _const_spec((D_MODEL, D_MODEL)),
                  seq_spec(2, D_MODEL),
                  seq_spec(KV_WIDTH, WINDOW),
                  seq_spec(KV_WIDTH, WINDOW)],
        out_specs=[seq_spec(tlen, D_MODEL),
                   seq_spec(2, D_MODEL),
                   seq_spec(KV_WIDTH, WINDOW),
                   seq_spec(KV_WIDTH, WINDOW)],
        out_shape=[jax.ShapeDtypeStruct((nseq, tlen, D_MODEL), _F32),
                   jax.ShapeDtypeStruct((nseq, 2, D_MODEL), _F32),
                   jax.ShapeDtypeStruct((nseq, KV_WIDTH, WINDOW), _F32),
                   jax.ShapeDtypeStruct((nseq, KV_WIDTH, WINDOW), _F32)],
        compiler_params=pltpu.CompilerParams(
            dimension_semantics=("arbitrary",),
            vmem_limit_bytes=VMEM_LIMIT),
        name="mix_sample",
    )(x, mod3, mod3, mod3, n1g, win, cw, wco, sinks, wao, wmo, st, ck, cv)


def _ffn_tail(x1, ac, val, g, wdown_ref, fg):
    hmid = (_gelu(ac) * val).astype(_BF)
    x2 = x1 + g * _dot(hmid, wdown_ref[...])
    return _rms(x2, fg)


def _ffn_prompt_kernel(x1_ref, sh_ref, sc_ref, g_ref, n2g_ref, wup_ref, fcw_ref, wdown_ref,
                       fg_ref, y_ref, ffn_ref, a_ext, *, parts):
    i = pl.program_id(1)
    t = x1_ref.shape[0]
    rows = t // parts

    @pl.when(i == 0)
    def _():
        a_ext[0:CARRY_ROWS, :] = jnp.zeros((CARRY_ROWS, D_FF), _F32)

    fcw = fcw_ref[...]

    def up(part):
        r0 = part * rows
        e0 = CARRY_ROWS + r0
        h2 = _mod_norm(x1_ref[r0:r0 + rows, :], n2g_ref[...], sc_ref[...], sh_ref[...])
        a = _dot(h2, wup_ref[:, 0:D_FF])
        val = _dot(h2, wup_ref[:, D_FF:2 * D_FF])
        a_ext[e0:e0 + rows, :] = a
        ac = (a_ext[e0 - 2:e0 - 2 + rows, :] * fcw[0:1, :]
              + a_ext[e0 - 1:e0 - 1 + rows, :] * fcw[1:2, :]
              + a * fcw[2:3, :])
        return ac, val

    def down(part, ac, val):
        r0 = part * rows
        y_ref[r0:r0 + rows, :] = _ffn_tail(x1_ref[r0:r0 + rows, :], ac, val, g_ref[...],
                                           wdown_ref, fg_ref[...])

    for part in range(parts):
        down(part, *up(part))
    ffn_ref[...] = a_ext[CARRY_ROWS + t - 2:CARRY_ROWS + t, :]
    a_ext[0:CARRY_ROWS, :] = a_ext[t:t + CARRY_ROWS, :]


def _ffn_prompt_call(x1, mod3, mod_row0, n2g, wup, fcw, wdown, fg, tile, parts):
    bsz, seq, _ = x1.shape

    def mod_spec(chunk):
        return pl.BlockSpec((None, 1, D_MODEL), lambda b, i: (mod_row0 + b, 0, chunk))

    x_spec = pl.BlockSpec((None, tile, D_MODEL), lambda b, i: (b, i, 0))
    return pl.pallas_call(
        functools.partial(_ffn_prompt_kernel, parts=parts),
        grid=(bsz, seq // tile),
        in_specs=[x_spec, mod_spec(3), mod_spec(4), mod_spec(5),
                  _const_spec((1, D_MODEL)),
                  _const_spec((D_MODEL, 2 * D_FF)),
                  _const_spec((3, D_FF)),
                  _const_spec((D_FF, D_MODEL)),
                  _const_spec((1, D_MODEL))],
        out_specs=[x_spec,
                   pl.BlockSpec((None, 2, D_FF), lambda b, i: (b, 0, 0))],
        out_shape=[jax.ShapeDtypeStruct((bsz, seq, D_MODEL), _F32),
                   jax.ShapeDtypeStruct((bsz, 2, D_FF), _F32)],
        scratch_shapes=[pltpu.VMEM((CARRY_ROWS + tile, D_FF), _F32)],
        compiler_params=pltpu.CompilerParams(
            dimension_semantics=("arbitrary", "arbitrary"),
            vmem_limit_bytes=VMEM_LIMIT),
        name="ffn_prompt",
    )(x1, mod3, mod3, mod3, n2g, wup, fcw, wdown, fg)


def _ffn_sample_kernel(x1_ref, sh_ref, sc_ref, g_ref, n2g_ref, wup_ref, fcw_ref, wdown_ref,
                       fg_ref, st_ref, y_ref, ffn_ref):
    nseq, tlen, _ = x1_ref.shape
    rows = nseq * tlen
    x3 = x1_ref[...]
    h3 = _rms(x3, n2g_ref[...]) * (1.0 + sc_ref[...]) + sh_ref[...]
    h2 = h3.reshape(rows, D_MODEL).astype(_BF)
    a = _dot(h2, wup_ref[:, 0:D_FF])
    val = _dot(h2, wup_ref[:, D_FF:2 * D_FF])
    ac, new_st = _conv_seq(a, st_ref[...], fcw_ref[...], nseq, tlen)
    ffn_ref[...] = new_st
    hmid = (_gelu(ac) * val).astype(_BF)
    y = _dot(hmid, wdown_ref[...]).reshape(nseq, tlen, D_MODEL)
    x2 = x3 + g_ref[...] * y
    y_ref[...] = _rms(x2, fg_ref[...])


def _ffn_sample_call(x1, mod3, n2g, wup, fcw, wdown, fg, st, chunk):
    nseq, tlen, _ = x1.shape

    def mod_spec(col):
        return pl.BlockSpec((chunk, 1, D_MODEL), lambda s: (s, 0, col))

    def seq_spec(d1, d2):
        return pl.BlockSpec((chunk, d1, d2), lambda s: (s, 0, 0))

    return pl.pallas_call(
        _ffn_sample_kernel,
        grid=(nseq // chunk,),
        in_specs=[seq_spec(tlen, D_MODEL),
                  mod_spec(3), mod_spec(4), mod_spec(5),
                  _const_spec((1, D_MODEL)),
                  _const_spec((D_MODEL, 2 * D_FF)),
                  _const_spec((3, D_FF)),
                  _const_spec((D_FF, D_MODEL)),
                  _const_spec((1, D_MODEL)),
                  seq_spec(2, D_FF)],
        out_specs=[seq_spec(tlen, D_MODEL), seq_spec(2, D_FF)],
        out_shape=[jax.ShapeDtypeStruct((nseq, tlen, D_MODEL), _F32),
                   jax.ShapeDtypeStruct((nseq, 2, D_FF), _F32)],
        compiler_params=pltpu.CompilerParams(
            dimension_semantics=("arbitrary",),
            vmem_limit_bytes=VMEM_LIMIT),
        name="ffn_sample",
    )(x1, mod3, mod3, mod3, n2g, wup, fcw, wdown, fg, st)


PROMPT_TILE = 512
FFN_TILE = 1024
FFN_PARTS = 2
SAMPLE_CHUNK = 16
ATTN_AFTER_WINDOWS = 8
FFN_SAMPLE_CHUNK = 64


def kernel(x_prompt, x_sample, c_prompt, c_sample, state_conv, cache_k_win, cache_v_win,
           state_ffn_conv, norm1_g, norm2_g, w_ada, b_ada, w_in, conv_w, w_conv_out,
           attn_sinks, w_attn_out, w_mix_out, w_up, ffn_conv_w, w_down, final_g):
    depth = w_in.shape[0]
    assert depth == 1, "the final norm is fused into the (single) layer's FFN call"
    nsamp = x_sample.shape[0]
    bsz = x_prompt.shape[0]
    xp, xs = x_prompt, x_sample
    fg = final_g.reshape(1, D_MODEL)
    outs = [[] for _ in range(8)]
    for layer in range(depth):
        mod3 = _mod_call(c_sample, c_prompt, w_ada[layer], b_ada[layer].reshape(1, -1))
        n1g = norm1_g[layer].reshape(1, D_MODEL)
        n2g = norm2_g[layer].reshape(1, D_MODEL)
        win = w_in[layer].astype(_BF)
        wco = w_conv_out[layer].astype(_BF)
        wao = w_attn_out[layer].astype(_BF)
        wmo = w_mix_out[layer].astype(_BF)
        cw = conv_w[layer]
        fcw = ffn_conv_w[layer]
        sinks = attn_sinks[layer]
        fg_l = fg
        def to_dim_major(win_state):
            return jnp.transpose(win_state, (0, 2, 3, 1)).reshape(-1, KV_WIDTH, WINDOW)

        def to_pos_major(t3):
            return jnp.transpose(t3.reshape(-1, N_KV_HEADS, HEAD_DIM, WINDOW), (0, 3, 1, 2))

        x1p, conv_p, k_p, v_p, wup, wdown = _mix_prompt_call(
            xp, mod3, nsamp, n1g, win, cw, wco, sinks, wao, wmo,
            (w_up[layer], w_down[layer]), PROMPT_TILE)
        x1s, conv_s, k_s, v_s = _mix_sample_call(
            xs, mod3, n1g, win, cw, wco, sinks, wao, wmo, state_conv[layer],
            to_dim_major(cache_k_win[layer]), to_dim_major(cache_v_win[layer]), SAMPLE_CHUNK)
        xp, ffn_p = _ffn_prompt_call(x1p, mod3, nsamp, n2g, wup, fcw, wdown, fg_l,
                                     FFN_TILE, FFN_PARTS)
        xs, ffn_s = _ffn_sample_call(x1s, mod3, n2g, wup, fcw, wdown, fg_l,
                                     state_ffn_conv[layer], FFN_SAMPLE_CHUNK)
        for lst, val in zip(outs, (conv_p, to_pos_major(k_p), to_pos_major(v_p), ffn_p,
                                   conv_s, to_pos_major(k_s), to_pos_major(v_s), ffn_s)):
            lst.append(val)
    return (xp, xs) + tuple(jnp.stack(lst, axis=0) for lst in outs)
```

```python
import functools

import jax
import jax.numpy as jnp
from jax import lax
from jax.experimental import pallas as pl
from jax.experimental.pallas import tpu as pltpu

D_MODEL = 1024
HEAD_DIM = 64
N_HEADS = 16
N_KV_HEADS = 4
WINDOW = 128
D_FF = 2816
KV_WIDTH = N_KV_HEADS * HEAD_DIM
IN_WIDTH = 3 * D_MODEL + D_MODEL + 2 * KV_WIDTH + 2 * D_MODEL
RMS_EPS = 1e-6
NEG_INF = -1e30
LANES = 128
HEADS_PER_COL = LANES // HEAD_DIM
N_QCOLS = N_HEADS * HEAD_DIM // LANES
CARRY_ROWS = 8
VMEM_LIMIT = 60 * 1024 * 1024

_OFF_B, _OFF_C, _OFF_X = 0, D_MODEL, 2 * D_MODEL
_OFF_Q = 3 * D_MODEL
_OFF_K = _OFF_Q + D_MODEL
_OFF_V = _OFF_K + KV_WIDTH
_OFF_GA = _OFF_V + KV_WIDTH
_OFF_GB = _OFF_GA + D_MODEL

_BF = jnp.bfloat16
_F32 = jnp.float32
_NT = (((1,), (1,)), ((), ()))


LOG2E = 1.4426950408889634
Q_SCALE = HEAD_DIM ** -0.5 * LOG2E


def _slope(head):
    return 2.0 ** (-8.0 * (head + 1) / N_HEADS) * LOG2E


def _rms(x, g):
    return x * lax.rsqrt(jnp.mean(x * x, axis=-1, keepdims=True) + RMS_EPS) * g


def _dot(a, b):
    return jnp.dot(a, b, preferred_element_type=_F32)


def _gelu(x):
    return 0.5 * x * (1.0 + lax.erf(x * (0.5 ** 0.5)))


def _split_heads(t2d, lo):
    rolled = pltpu.roll(t2d, HEAD_DIM, 1)
    zero = jnp.zeros_like(t2d)
    return (jnp.where(lo, t2d, zero), jnp.where(lo, zero, rolled),
            jnp.where(lo, rolled, zero), jnp.where(lo, zero, t2d))


def _softmax_parts(s, sink):
    m = jnp.maximum(jnp.max(s, axis=-1, keepdims=True), sink)
    p = jnp.exp2(s - m)
    denom = jnp.sum(p, axis=-1, keepdims=True) + jnp.exp2(sink - m)
    return p, 1.0 / denom


MOD_COLS = 2048


def _mod_kernel(cs_ref, cp_ref, w_ref, b_ref, o_ref):
    w = w_ref[...].astype(_BF)
    n0 = cs_ref.shape[0]
    for c_ref, r0 in ((cs_ref, 0), (cp_ref, n0)):
        c = c_ref[...]
        a = (c * jax.nn.sigmoid(c)).astype(_BF)
        res = _dot(a, w) + b_ref[...]
        n = c.shape[0]
        o_ref[r0:r0 + n] = res.reshape(n, 1, res.shape[-1])


def _mod_call(c_sample, c_prompt, w_ada, b_ada):
    ns, npr = c_sample.shape[0], c_prompt.shape[0]
    n = ns + npr
    return pl.pallas_call(
        _mod_kernel,
        grid=(6 * D_MODEL // MOD_COLS,),
        in_specs=[pl.BlockSpec((ns, D_MODEL), lambda j: (0, 0)),
                  pl.BlockSpec((npr, D_MODEL), lambda j: (0, 0)),
                  pl.BlockSpec((D_MODEL, MOD_COLS), lambda j: (0, j)),
                  pl.BlockSpec((1, MOD_COLS), lambda j: (0, j))],
        out_specs=pl.BlockSpec((n, 1, MOD_COLS), lambda j: (0, 0, j)),
        out_shape=jax.ShapeDtypeStruct((n, 1, 6 * D_MODEL), _F32),
        name="adaln_mod",
    )(c_sample, c_prompt, w_ada, b_ada)


def _mod_norm(x, g, sc, sh):
    return (_rms(x, g) * (1.0 + sc) + sh).astype(_BF)


def _mix_prompt_kernel(x_ref, sh_ref, sc_ref, g_ref, xn_ref, shn_ref, scn_ref, n1g_ref,
                       win_ref, cw_ref, wco_ref, sinks_ref, wao_ref, wmo_ref, *rest,
                       tiles_per_seq, n_cast):
    cast_in, rest = rest[:n_cast], rest[n_cast:]
    (x1_ref, conv_ref, k_ref, v_ref), rest = rest[:4], rest[4:]
    cast_out, (u_ext, k_ext, v_ext, o_scr, bias_scr, h_scr) = rest[:n_cast], rest[n_cast:]
    step = pl.program_id(0)
    i = step % tiles_per_seq
    t = x_ref.shape[0]
    nblk = t // WINDOW

    for src, dst in zip(cast_in, cast_out):
        dst[...] = src[...].astype(_BF)

    @pl.when(step == 0)
    def _():
        h_scr[...] = _mod_norm(x_ref[...], n1g_ref[...], sc_ref[...], sh_ref[...])
        row = lax.broadcasted_iota(jnp.int32, (WINDOW, 2 * WINDOW), 0)
        col = lax.broadcasted_iota(jnp.int32, (WINDOW, 2 * WINDOW), 1)
        dist = WINDOW + row - col
        valid = (dist >= 0) & (dist <= WINDOW)
        distf = dist.astype(_F32)
        for head in range(N_HEADS):
            bias_scr[head] = jnp.where(valid, -(_slope(head) * distf), NEG_INF)

    @pl.when(i == 0)
    def _():
        u_ext[0:CARRY_ROWS, :] = jnp.zeros((CARRY_ROWS, D_MODEL), _F32)
        k_ext[:, 0:WINDOW, :] = jnp.zeros((2 * N_KV_HEADS, WINDOW, LANES), _BF)
        v_ext[:, 0:WINDOW, :] = jnp.zeros((2 * N_KV_HEADS, WINDOW, LANES), _BF)

    def proj(lo, width):
        return _dot(h_scr[...], win_ref[:, lo:lo + width])

    def next_tile_norm():
        h_scr[...] = _mod_norm(xn_ref[...], n1g_ref[...], scn_ref[...], shn_ref[...])

    side = {}

    def conv_in():
        u = proj(_OFF_C, D_MODEL) * proj(_OFF_X, D_MODEL)
        u_ext[CARRY_ROWS:CARRY_ROWS + t, :] = u
        cw = cw_ref[...]
        side["uc"] = (u_ext[CARRY_ROWS - 2:CARRY_ROWS - 2 + t, :] * cw[0:1, :]
                      + u_ext[CARRY_ROWS - 1:CARRY_ROWS - 1 + t, :] * cw[1:2, :]
                      + u * cw[2:3, :])
        conv_ref[...] = u[t - 2:t, :]
        u_ext[0:CARRY_ROWS, :] = u[t - CARRY_ROWS:t, :]

    def conv_out():
        pb = proj(_OFF_B, D_MODEL)
        side["ya"] = _dot((pb * side.pop("uc")).astype(_BF), wco_ref[...])

    def gate_a():
        side["ga"] = jax.nn.sigmoid(proj(_OFF_GA, D_MODEL))

    def gate_b():
        side["gb"] = jax.nn.sigmoid(proj(_OFF_GB, D_MODEL))
        next_tile_norm()

    side_jobs = [conv_in, conv_out, gate_a, gate_b]

    q = (proj(_OFF_Q, D_MODEL) * Q_SCALE).astype(_BF)
    kf = proj(_OFF_K, KV_WIDTH)
    vf = proj(_OFF_V, KV_WIDTH)
    k_ref[...] = kf[t - WINDOW:t, :].T
    v_ref[...] = vf[t - WINDOW:t, :].T
    lane_t = lax.broadcasted_iota(jnp.int32, (t, LANES), 1)
    lo = lane_t < HEAD_DIM
    ones_lane = (HEAD_DIM, 0)
    for kc in range(KV_WIDTH // LANES):
        ks = _split_heads(kf[:, kc * LANES:(kc + 1) * LANES], lo)
        vs = _split_heads(vf[:, kc * LANES:(kc + 1) * LANES], lo)
        for j in range(4):
            k_ext[4 * kc + j, WINDOW:WINDOW + t, :] = ks[j].astype(_BF)
            v_one = jnp.where(lane_t == ones_lane[j % HEADS_PER_COL], 1.0, vs[j])
            v_ext[4 * kc + j, WINDOW:WINDOW + t, :] = v_one.astype(_BF)

    first_cols = lax.broadcasted_iota(jnp.int32, (WINDOW, 2 * WINDOW), 1) < WINDOW
    no_prev = first_cols & (i == 0)
    lo_q = lax.broadcasted_iota(jnp.int32, (WINDOW, LANES), 1) < HEAD_DIM
    for blk in range(nblk):
        r0 = blk * WINDOW
        for c in range(N_QCOLS):
            hkv = c // 2
            qc = q[r0:r0 + WINDOW, c * LANES:(c + 1) * LANES]
            o_pair = []
            for par in range(HEADS_PER_COL):
                head = HEADS_PER_COL * c + par
                kk = k_ext[2 * hkv + par, r0:r0 + 2 * WINDOW, :]
                s = lax.dot_general(qc, kk, _NT, preferred_element_type=_F32) + bias_scr[head]
                if blk == 0:
                    s = jnp.where(no_prev, NEG_INF, s)
                sink = sinks_ref[head] * LOG2E
                m = jnp.maximum(jnp.max(s, axis=-1, keepdims=True), sink)
                p = jnp.exp2(s - m).astype(_BF)
                o_h = _dot(p, v_ext[2 * hkv + par, r0:r0 + 2 * WINDOW, :])
                col = ones_lane[par]
                denom = o_h[:, col:col + 1] + jnp.exp2(sink - m)
                o_pair.append(o_h * (1.0 / denom))
            o_c = jnp.where(lo_q, o_pair[0], o_pair[1])
            o_scr[r0:r0 + WINDOW, c * LANES:(c + 1) * LANES] = o_c.astype(_BF)
        for _ in range(-(-len(side_jobs) // (nblk - blk))):
            side_jobs.pop(0)()
    k_ext[:, 0:WINDOW, :] = k_ext[:, t:t + WINDOW, :]
    v_ext[:, 0:WINDOW, :] = v_ext[:, t:t + WINDOW, :]
    yb = _dot(o_scr[...], wao_ref[...])

    mixed = (side["ga"] * side["ya"] + side["gb"] * yb).astype(_BF)
    x1_ref[...] = x_ref[...] + g_ref[...] * _dot(mixed, wmo_ref[...])


def _const_spec(shape):
    nd = len(shape)
    return pl.BlockSpec(shape, lambda *_: (0,) * nd, pipeline_mode=pl.Buffered(1))


def _tile_specs(tile, nt, steps, mod_row0):
    def nxt(s):
        return jnp.minimum(s + 1, steps - 1)

    x_spec = pl.BlockSpec((None, tile, D_MODEL), lambda s: (s // nt, s % nt, 0))
    xn_spec = pl.BlockSpec((None, tile, D_MODEL), lambda s: (nxt(s) // nt, nxt(s) % nt, 0))

    def mod_spec(chunk):
        return pl.BlockSpec((None, 1, D_MODEL), lambda s: (mod_row0 + s // nt, 0, chunk))

    def modn_spec(chunk):
        return pl.BlockSpec((None, 1, D_MODEL), lambda s: (mod_row0 + nxt(s) // nt, 0, chunk))

    return x_spec, xn_spec, mod_spec, modn_spec


BF16_SUBLANES = 16


def _cast_stream_spec(shape, steps):
    rows, cols = shape
    for hold in range(1, steps + 1):
        nblocks, rem = divmod(steps, hold)
        if rem == 0 and rows % nblocks == 0 and (rows // nblocks) % BF16_SUBLANES == 0:
            return pl.BlockSpec((rows // nblocks, cols), lambda s: (s // hold, 0))
    raise ValueError(f"no bf16-aligned row split of {shape} over {steps} steps")


def _mix_prompt_call(x, mod3, mod_row0, n1g, win, cw, wco, sinks, wao, wmo, cast_along, tile):
    bsz, seq, _ = x.shape
    nt = seq // tile
    steps = bsz * nt
    x_spec, xn_spec, mod_spec, modn_spec = _tile_specs(tile, nt, steps, mod_row0)
    cast_specs = [_cast_stream_spec(w.shape, steps) for w in cast_along]

    def state_spec(d1, d2):
        return pl.BlockSpec((None, d1, d2), lambda s: (s // nt, 0, 0))

    return pl.pallas_call(
        functools.partial(_mix_prompt_kernel, tiles_per_seq=nt, n_cast=len(cast_along)),
        grid=(steps,),
        in_specs=[x_spec, mod_spec(0), mod_spec(1), mod_spec(2),
                  xn_spec, modn_spec(0), modn_spec(1),
                  _const_spec((1, D_MODEL)),
                  _const_spec((D_MODEL, IN_WIDTH)),
                  _const_spec((3, D_MODEL)),
                  _const_spec((D_MODEL, D_MODEL)),
                  pl.BlockSpec(memory_space=pltpu.SMEM),
                  _const_spec((D_MODEL, D_MODEL)),
                  _const_spec((D_MODEL, D_MODEL))] + cast_specs,
        out_specs=[x_spec,
                   state_spec(2, D_MODEL),
                   state_spec(KV_WIDTH, WINDOW),
                   state_spec(KV_WIDTH, WINDOW)] + cast_specs,
        out_shape=[jax.ShapeDtypeStruct((bsz, seq, D_MODEL), _F32),
                   jax.ShapeDtypeStruct((bsz, 2, D_MODEL), _F32),
                   jax.ShapeDtypeStruct((bsz, KV_WIDTH, WINDOW), _F32),
                   jax.ShapeDtypeStruct((bsz, KV_WIDTH, WINDOW), _F32)]
        + [jax.ShapeDtypeStruct(w.shape, _BF) for w in cast_along],
        scratch_shapes=[pltpu.VMEM((CARRY_ROWS + tile, D_MODEL), _F32),
                        pltpu.VMEM((2 * N_KV_HEADS, WINDOW + tile, LANES), _BF),
                        pltpu.VMEM((2 * N_KV_HEADS, WINDOW + tile, LANES), _BF),
                        pltpu.VMEM((tile, D_MODEL), _BF),
                        pltpu.VMEM((N_HEADS, WINDOW, 2 * WINDOW), _F32),
                        pltpu.VMEM((tile, D_MODEL), _BF)],
        compiler_params=pltpu.CompilerParams(
            dimension_semantics=("arbitrary",),
            vmem_limit_bytes=VMEM_LIMIT),
        name="mix_prompt",
    )(x, mod3, mod3, mod3, x, mod3, mod3, n1g, win, cw, wco, sinks, wao, wmo, *cast_along)


def _conv_seq(u2d, st, cw, nseq, tlen):
    width = u2d.shape[-1]
    shape3 = (nseq, tlen, width)
    u3 = u2d.reshape(shape3)
    r1 = pltpu.roll(u2d, 1, 0).reshape(shape3)
    r2 = pltpu.roll(u2d, 2, 0).reshape(shape3)
    tpos = lax.broadcasted_iota(jnp.int32, shape3, 1)
    st0 = jnp.broadcast_to(st[:, 0:1, :], shape3)
    st1 = jnp.broadcast_to(st[:, 1:2, :], shape3)
    um1 = jnp.where(tpos == 0, st1, r1)
    um2 = jnp.where(tpos == 0, st0, jnp.where(tpos == 1, st1, r2))
    y = um2 * cw[0:1, :] + um1 * cw[1:2, :] + u3 * cw[2:3, :]
    return y.reshape(nseq * tlen, width), u3[:, tlen - 2:tlen, :]


def _mix_sample_kernel(x_ref, sh_ref, sc_ref, g_ref, n1g_ref, win_ref, cw_ref, wco_ref,
                       sinks_ref, wao_ref, wmo_ref, st_ref, ck_ref, cv_ref,
                       x1_ref, conv_ref, k_ref, v_ref):
    nseq, tlen, _ = x_ref.shape
    rows = nseq * tlen

    assert rows == LANES, "one step's new keys must fill exactly one lane tile"

    x3 = x_ref[...]
    h3 = _rms(x3, n1g_ref[...]) * (1.0 + sc_ref[...]) + sh_ref[...]
    hf = h3.reshape(rows, D_MODEL)
    h_now = [hf.astype(_BF)]

    def proj(lo, width):
        return _dot(h_now[0], win_ref[:, lo:lo + width])

    def order_after(tile_f32):
        bits = pltpu.bitcast(tile_f32[0:8, :], jnp.uint32)
        zero = ((bits >> 16) >> 16).astype(_F32)
        h_now[0] = (hf + jnp.tile(zero, (rows // 8, D_MODEL // LANES))).astype(_BF)

    side = {}

    def job_conv_in():
        u = proj(_OFF_C, D_MODEL) * proj(_OFF_X, D_MODEL)
        side["uc"], new_st = _conv_seq(u, st_ref[...], cw_ref[...], nseq, tlen)
        conv_ref[...] = new_st

    def job_gate_b():
        side["pb"] = proj(_OFF_B, D_MODEL)

    def job_conv_out():
        side["ya"] = _dot((side.pop("pb") * side.pop("uc")).astype(_BF), wco_ref[...])

    def job_ga():
        side["ga"] = jax.nn.sigmoid(proj(_OFF_GA, D_MODEL))

    def job_gb():
        side["gb"] = jax.nn.sigmoid(proj(_OFF_GB, D_MODEL))

    keep = WINDOW - tlen
    qf = proj(_OFF_Q, D_MODEL) * Q_SCALE
    kvt_new = proj(_OFF_K, 2 * KV_WIDTH).T
    keep_old = lax.broadcasted_iota(jnp.int32, (KV_WIDTH, WINDOW), 1) < keep

    def window_update(s):
        shift = (keep - s * tlen) % LANES
        for cache_ref, out_ref, base in ((ck_ref, k_ref, 0), (cv_ref, v_ref, KV_WIDTH)):
            rolled = pltpu.roll(cache_ref[s], keep, 1)
            new = pltpu.roll(kvt_new[base:base + KV_WIDTH, :], shift, 1)
            win = jnp.where(keep_old, rolled, new)
            out_ref[s] = win
        return win

    lo = lax.broadcasted_iota(jnp.int32, (rows, LANES), 1) < HEAD_DIM
    zero = jnp.zeros((rows, LANES), _F32)
    n_tiles = KV_WIDTH // LANES
    heads_per_tile = N_HEADS // n_tiles
    m_rows = heads_per_tile * tlen
    row = lax.broadcasted_iota(jnp.int32, (m_rows, WINDOW), 0)
    col = lax.broadcasted_iota(jnp.int32, (m_rows, WINDOW), 1)
    dist_c = WINDOW + row % tlen - col
    valid_c = dist_c <= WINDOW
    shape3 = (nseq, m_rows, LANES)
    seq3 = lax.broadcasted_iota(jnp.int32, shape3, 0)
    row3 = lax.broadcasted_iota(jnp.int32, shape3, 1)
    col3 = lax.broadcasted_iota(jnp.int32, shape3, 2)
    dist_n = row3 % tlen - col3 % tlen
    valid_n = (col3 // tlen == seq3) & (dist_n >= 0)
    row1 = lax.broadcasted_iota(jnp.int32, (m_rows, 1), 0)

    def attention(kc):
        parts = []
        slope = jnp.zeros((m_rows, 1), _F32)
        sink = jnp.zeros((m_rows, 1), _F32)
        for j in range(heads_per_tile):
            head = heads_per_tile * kc + j
            c, par, half = head // HEADS_PER_COL, j % HEADS_PER_COL, j // (heads_per_tile // 2)
            qc = qf[:, c * LANES:(c + 1) * LANES]
            src = qc if par == half else pltpu.roll(qc, HEAD_DIM, 1)
            part = jnp.where(lo, src, zero) if half == 0 else jnp.where(lo, zero, src)
            parts.append(part.reshape(nseq, tlen, LANES))
            mine = (row1 >= j * tlen) & (row1 < (j + 1) * tlen)
            slope = jnp.where(mine, _slope(head), slope)
            sink = jnp.where(mine, sinks_ref[head] * LOG2E, sink)
        lhs = jnp.concatenate(parts, axis=1).astype(_BF)
        lhs2 = lhs.reshape(nseq * m_rows, LANES)
        rsl = slice(kc * LANES, (kc + 1) * LANES)
        k_new = kvt_new[rsl, :].astype(_BF)
        v_new = kvt_new[KV_WIDTH + kc * LANES:KV_WIDTH + (kc + 1) * LANES, :].astype(_BF)
        s_c = jnp.einsum('smk,skn->smn', lhs, ck_ref[:, rsl, :].astype(_BF),
                         preferred_element_type=_F32)
        s_n = _dot(lhs2, k_new).reshape(shape3)
        sc = jnp.concatenate(
            [s_c + jnp.where(valid_c, -(slope * dist_c.astype(_F32)), NEG_INF),
             s_n + jnp.where(valid_n, -(slope * dist_n.astype(_F32)), NEG_INF)], axis=2)
        p, inv = _softmax_parts(sc, sink)
        p = p.astype(_BF)
        r_c = jnp.einsum('smk,snk->smn', p[:, :, 0:WINDOW], cv_ref[:, rsl, :].astype(_BF),
                         preferred_element_type=_F32)
        r_n = lax.dot_general(p[:, :, WINDOW:].reshape(nseq * m_rows, LANES), v_new, _NT,
                              preferred_element_type=_F32).reshape(shape3)
        r = (r_c + r_n) * inv
        cols = {}
        for m2 in range(heads_per_tile // HEADS_PER_COL):
            c = (heads_per_tile // HEADS_PER_COL) * kc + m2
            half = m2 // (heads_per_tile // (2 * HEADS_PER_COL))
            r0 = r[:, (2 * m2) * tlen:(2 * m2 + 1) * tlen, :].reshape(rows, LANES)
            r1 = r[:, (2 * m2 + 1) * tlen:(2 * m2 + 2) * tlen, :].reshape(rows, LANES)
            if half == 0:
                cols[c] = jnp.where(lo, r0, pltpu.roll(r1, HEAD_DIM, 1))
            else:
                cols[c] = jnp.where(lo, pltpu.roll(r0, HEAD_DIM, 1), r1)
        return cols

    o_cols = {}
    job_conv_in()
    window_jobs = [job_gate_b, job_conv_out, job_ga, job_gb]
    group = -(-nseq // len(window_jobs))
    for s in range(nseq):
        if s == ATTN_AFTER_WINDOWS:
            for kc in range(n_tiles):
                o_cols.update(attention(kc))
        last_win = window_update(s)
        if window_jobs and (s + 1) % group == 0:
            window_jobs.pop(0)()
            order_after(last_win)
    while window_jobs:
        window_jobs.pop(0)()
    o = jnp.concatenate([o_cols[c] for c in range(N_QCOLS)], axis=1).astype(_BF)
    yb = _dot(o, wao_ref[...])

    mixed = (side["ga"] * side["ya"] + side["gb"] * yb).astype(_BF)
    y = _dot(mixed, wmo_ref[...]).reshape(nseq, tlen, D_MODEL)
    x1_ref[...] = x3 + g_ref[...] * y


def _mix_sample_call(x, mod3, n1g, win, cw, wco, sinks, wao, wmo, st, ck, cv, chunk):
    nseq, tlen, _ = x.shape
    assert LANES % tlen == 0 and chunk % (LANES // tlen) == 0

    def mod_spec(col):
        return pl.BlockSpec((chunk, 1, D_MODEL), lambda s: (s, 0, col))

    def seq_spec(d1, d2):
        return pl.BlockSpec((chunk, d1, d2), lambda s: (s, 0, 0))

    return pl.pallas_call(
        _mix_sample_kernel,
        grid=(nseq // chunk,),
        in_specs=[seq_spec(tlen, D_MODEL),
                  mod_spec(0), mod_spec(1), mod_spec(2),
                  _const_spec((1, D_MODEL)),
                  _const_spec((D_MODEL, IN_WIDTH)),
                  _const_spec((3, D_MODEL)),
                  _const_spec((D_MODEL, D_MODEL)),
                  pl.BlockSpec(memory_space=pltpu.SMEM),
                  _const_spec((D_MODEL, D_MODEL)),
                  _const_spec((D_MODEL, D_MODEL)),
                  seq_spec(2, D_MODEL),
                  seq_spec(KV_WIDTH, WINDOW),
                  seq_spec(KV_WIDTH, WINDOW)],
        out_specs=[seq_spec(tlen, D_MODEL),
                   seq_spec(2, D_MODEL),
                   seq_spec(KV_WIDTH, WINDOW),
                   seq_spec(KV_WIDTH, WINDOW)],
        out_shape=[jax.ShapeDtypeStruct((nseq, tlen, D_MODEL), _F32),
                   jax.ShapeDtypeStruct((nseq, 2, D_MODEL), _F32),
                   jax.ShapeDtypeStruct((nseq, KV_WIDTH, WINDOW), _F32),
                   jax.ShapeDtypeStruct((nseq, KV_WIDTH, WINDOW), _F32)],
        compiler_params=pltpu.CompilerParams(
            dimension_semantics=("arbitrary",),
            vmem_limit_bytes=VMEM_LIMIT),
        name="mix_sample",
    )(x, mod3, mod3, mod3, n1g, win, cw, wco, sinks, wao, wmo, st, ck, cv)


def _ffn_tail(x1, ac, val, g, wdown_ref, fg):
    hmid = (_gelu(ac) * val).astype(_BF)
    x2 = x1 + g * _dot(hmid, wdown_ref[...])
    return _rms(x2, fg)


def _ffn_prompt_body(i, x1_ref, sh_ref, sc_ref, g_ref, n2g_ref, wup_ref, fcw_ref, wdown_ref,
                     fg_ref, y_ref, ffn_ref, a_ext, parts):
    t = x1_ref.shape[0]
    rows = t // parts

    @pl.when(i == 0)
    def _():
        a_ext[0:CARRY_ROWS, :] = jnp.zeros((CARRY_ROWS, D_FF), _F32)

    fcw = fcw_ref[...]

    def up(part):
        r0 = part * rows
        e0 = CARRY_ROWS + r0
        h2 = _mod_norm(x1_ref[r0:r0 + rows, :], n2g_ref[...], sc_ref[...], sh_ref[...])
        a = _dot(h2, wup_ref[:, 0:D_FF])
        val = _dot(h2, wup_ref[:, D_FF:2 * D_FF])
        a_ext[e0:e0 + rows, :] = a
        ac = (a_ext[e0 - 2:e0 - 2 + rows, :] * fcw[0:1, :]
              + a_ext[e0 - 1:e0 - 1 + rows, :] * fcw[1:2, :]
              + a * fcw[2:3, :])
        return ac, val

    def down(part, ac, val):
        r0 = part * rows
        y_ref[r0:r0 + rows, :] = _ffn_tail(x1_ref[r0:r0 + rows, :], ac, val, g_ref[...],
                                           wdown_ref, fg_ref[...])

    for part in range(parts):
        down(part, *up(part))
    ffn_ref[...] = a_ext[CARRY_ROWS + t - 2:CARRY_ROWS + t, :]
    a_ext[0:CARRY_ROWS, :] = a_ext[t:t + CARRY_ROWS, :]


def _ffn_kernel(x1p_ref, shp_ref, scp_ref, gp_ref, x1s_ref, shs_ref, scs_ref, gs_ref, st_ref,
                n2g_ref, wup_ref, fcw_ref, wdown_ref, fg_ref,
                yp_ref, ffnp_ref, ys_ref, ffns_ref, a_ext, *, prompt_steps, tiles_per_seq, parts):
    step = pl.program_id(0)

    @pl.when(step < prompt_steps)
    def _():
        _ffn_prompt_body(step % tiles_per_seq, x1p_ref, shp_ref, scp_ref, gp_ref, n2g_ref,
                         wup_ref, fcw_ref, wdown_ref, fg_ref, yp_ref, ffnp_ref, a_ext, parts)

    @pl.when(step >= prompt_steps)
    def _():
        _ffn_sample_kernel(x1s_ref, shs_ref, scs_ref, gs_ref, n2g_ref, wup_ref, fcw_ref,
                           wdown_ref, fg_ref, st_ref, ys_ref, ffns_ref)


def _ffn_call(x1p, x1s, mod3, mod_row0, n2g, wup, fcw, wdown, fg, st, tile, parts, chunk):
    bsz, seq, _ = x1p.shape
    nseq, tlen, _ = x1s.shape
    nt = seq // tile
    psteps = bsz * nt
    ssteps = nseq // chunk

    def pidx(s):
        return jnp.minimum(s, psteps - 1)

    def sidx(s):
        return jnp.maximum(s - psteps, 0)

    def pmod_spec(col):
        return pl.BlockSpec((None, 1, D_MODEL), lambda s: (mod_row0 + pidx(s) // nt, 0, col))

    def smod_spec(col):
        return pl.BlockSpec((chunk, 1, D_MODEL), lambda s: (sidx(s), 0, col))

    def seq_spec(d1, d2):
        return pl.BlockSpec((chunk, d1, d2), lambda s: (sidx(s), 0, 0))

    xp_spec = pl.BlockSpec((None, tile, D_MODEL), lambda s: (pidx(s) // nt, pidx(s) % nt, 0))
    return pl.pallas_call(
        functools.partial(_ffn_kernel, prompt_steps=psteps, tiles_per_seq=nt, parts=parts),
        grid=(psteps + ssteps,),
        in_specs=[xp_spec, pmod_spec(3), pmod_spec(4), pmod_spec(5),
                  seq_spec(tlen, D_MODEL), smod_spec(3), smod_spec(4), smod_spec(5),
                  seq_spec(2, D_FF),
                  _const_spec((1, D_MODEL)),
                  _const_spec((D_MODEL, 2 * D_FF)),
                  _const_spec((3, D_FF)),
                  _const_spec((D_FF, D_MODEL)),
                  _const_spec((1, D_MODEL))],
        out_specs=[xp_spec,
                   pl.BlockSpec((None, 2, D_FF), lambda s: (pidx(s) // nt, 0, 0)),
                   seq_spec(tlen, D_MODEL),
                   seq_spec(2, D_FF)],
        out_shape=[jax.ShapeDtypeStruct((bsz, seq, D_MODEL), _F32),
                   jax.ShapeDtypeStruct((bsz, 2, D_FF), _F32),
                   jax.ShapeDtypeStruct((nseq, tlen, D_MODEL), _F32),
                   jax.ShapeDtypeStruct((nseq, 2, D_FF), _F32)],
        scratch_shapes=[pltpu.VMEM((CARRY_ROWS + tile, D_FF), _F32)],
        compiler_params=pltpu.CompilerParams(
            dimension_semantics=("arbitrary",),
            vmem_limit_bytes=VMEM_LIMIT),
        name="conv_ffn",
    )(x1p, mod3, mod3, mod3, x1s, mod3, mod3, mod3, st, n2g, wup, fcw, wdown, fg)


def _ffn_sample_kernel(x1_ref, sh_ref, sc_ref, g_ref, n2g_ref, wup_ref, fcw_ref, wdown_ref,
                       fg_ref, st_ref, y_ref, ffn_ref):
    nseq, tlen, _ = x1_ref.shape
    rows = nseq * tlen
    x3 = x1_ref[...]
    h3 = _rms(x3, n2g_ref[...]) * (1.0 + sc_ref[...]) + sh_ref[...]
    h2 = h3.reshape(rows, D_MODEL).astype(_BF)
    a = _dot(h2, wup_ref[:, 0:D_FF])
    val = _dot(h2, wup_ref[:, D_FF:2 * D_FF])
    ac, new_st = _conv_seq(a, st_ref[...], fcw_ref[...], nseq, tlen)
    ffn_ref[...] = new_st
    hmid = (_gelu(ac) * val).astype(_BF)
    y = _dot(hmid, wdown_ref[...]).reshape(nseq, tlen, D_MODEL)
    x2 = x3 + g_ref[...] * y
    y_ref[...] = _rms(x2, fg_ref[...])


PROMPT_TILE = 512
FFN_TILE = 512
FFN_PARTS = 1
SAMPLE_CHUNK = 16
ATTN_AFTER_WINDOWS = 8
FFN_SAMPLE_CHUNK = 64


def kernel(x_prompt, x_sample, c_prompt, c_sample, state_conv, cache_k_win, cache_v_win,
           state_ffn_conv, norm1_g, norm2_g, w_ada, b_ada, w_in, conv_w, w_conv_out,
           attn_sinks, w_attn_out, w_mix_out, w_up, ffn_conv_w, w_down, final_g):
    depth = w_in.shape[0]
    assert depth == 1, "the final norm is fused into the (single) layer's FFN call"
    nsamp = x_sample.shape[0]
    bsz = x_prompt.shape[0]
    xp, xs = x_prompt, x_sample
    fg = final_g.reshape(1, D_MODEL)
    outs = [[] for _ in range(8)]
    for layer in range(depth):
        mod3 = _mod_call(c_sample, c_prompt, w_ada[layer], b_ada[layer].reshape(1, -1))
        n1g = norm1_g[layer].reshape(1, D_MODEL)
        n2g = norm2_g[layer].reshape(1, D_MODEL)
        win = w_in[layer].astype(_BF)
        wco = w_conv_out[layer].astype(_BF)
        wao = w_attn_out[layer].astype(_BF)
        wmo = w_mix_out[layer].astype(_BF)
        cw = conv_w[layer]
        fcw = ffn_conv_w[layer]
        sinks = attn_sinks[layer]
        fg_l = fg
        def to_dim_major(win_state):
            return jnp.transpose(win_state, (0, 2, 3, 1)).reshape(-1, KV_WIDTH, WINDOW)

        def to_pos_major(t3):
            return jnp.transpose(t3.reshape(-1, N_KV_HEADS, HEAD_DIM, WINDOW), (0, 3, 1, 2))

        x1p, conv_p, k_p, v_p, wup, wdown = _mix_prompt_call(
            xp, mod3, nsamp, n1g, win, cw, wco, sinks, wao, wmo,
            (w_up[layer], w_down[layer]), PROMPT_TILE)
        x1s, conv_s, k_s, v_s = _mix_sample_call(
            xs, mod3, n1g, win, cw, wco, sinks, wao, wmo, state_conv[layer],
            to_dim_major(cache_k_win[layer]), to_dim_major(cache_v_win[layer]), SAMPLE_CHUNK)
        xp, ffn_p, xs, ffn_s = _ffn_call(
            x1p, x1s, mod3, nsamp, n2g, wup, fcw, wdown, fg_l, state_ffn_conv[layer],
            FFN_TILE, FFN_PARTS, FFN_SAMPLE_CHUNK)
        for lst, val in zip(outs, (conv_p, to_pos_major(k_p), to_pos_major(v_p), ffn_p,
                                   conv_s, to_pos_major(k_s), to_pos_major(v_s), ffn_s)):
            lst.append(val)
    return (xp, xs) + tuple(jnp.stack(lst, axis=0) for lst in outs)
```

```python
import functools

import jax
import jax.numpy as jnp
from jax import lax
from jax.experimental import pallas as pl
from jax.experimental.pallas import tpu as pltpu

D_MODEL = 1024
HEAD_DIM = 64
N_HEADS = 16
N_KV_HEADS = 4
WINDOW = 128
D_FF = 2816
KV_WIDTH = N_KV_HEADS * HEAD_DIM
IN_WIDTH = 3 * D_MODEL + D_MODEL + 2 * KV_WIDTH + 2 * D_MODEL
RMS_EPS = 1e-6
NEG_INF = -1e30
LANES = 128
HEADS_PER_COL = LANES // HEAD_DIM
N_QCOLS = N_HEADS * HEAD_DIM // LANES
CARRY_ROWS = 8
VMEM_LIMIT = 60 * 1024 * 1024

_OFF_B, _OFF_C, _OFF_X = 0, D_MODEL, 2 * D_MODEL
_OFF_Q = 3 * D_MODEL
_OFF_K = _OFF_Q + D_MODEL
_OFF_V = _OFF_K + KV_WIDTH
_OFF_GA = _OFF_V + KV_WIDTH
_OFF_GB = _OFF_GA + D_MODEL

_BF = jnp.bfloat16
_F32 = jnp.float32
_NT = (((1,), (1,)), ((), ()))


LOG2E = 1.4426950408889634
Q_SCALE = HEAD_DIM ** -0.5 * LOG2E


def _slope(head):
    return 2.0 ** (-8.0 * (head + 1) / N_HEADS) * LOG2E


def _rms(x, g):
    return x * lax.rsqrt(jnp.mean(x * x, axis=-1, keepdims=True) + RMS_EPS) * g


def _dot(a, b):
    return jnp.dot(a, b, preferred_element_type=_F32)


def _gelu(x):
    return 0.5 * x * (1.0 + lax.erf(x * (0.5 ** 0.5)))


def _split_heads(t2d, lo):
    rolled = pltpu.roll(t2d, HEAD_DIM, 1)
    zero = jnp.zeros_like(t2d)
    return (jnp.where(lo, t2d, zero), jnp.where(lo, zero, rolled),
            jnp.where(lo, rolled, zero), jnp.where(lo, zero, t2d))


def _softmax_parts(s, sink):
    m = jnp.maximum(jnp.max(s, axis=-1, keepdims=True), sink)
    p = jnp.exp2(s - m)
    denom = jnp.sum(p, axis=-1, keepdims=True) + jnp.exp2(sink - m)
    return p, 1.0 / denom


N_MOD = 6
MOD_COLS = 2048


def _mod_kernel(cs_ref, cp_ref, w_ref, b_ref, o_ref):
    w = w_ref[...].astype(_BF)
    n0 = cs_ref.shape[0]
    for c_ref, r0 in ((cs_ref, 0), (cp_ref, n0)):
        c = c_ref[...]
        a = (c * jax.nn.sigmoid(c)).astype(_BF)
        res = _dot(a, w) + b_ref[...]
        n = c.shape[0]
        o_ref[r0:r0 + n] = res.reshape(n, 1, res.shape[-1])


def _mod_call(c_sample, c_prompt, w_ada, b_ada):
    ns, npr = c_sample.shape[0], c_prompt.shape[0]
    n = ns + npr
    return pl.pallas_call(
        _mod_kernel,
        grid=(N_MOD * D_MODEL // MOD_COLS,),
        in_specs=[pl.BlockSpec((ns, D_MODEL), lambda j: (0, 0)),
                  pl.BlockSpec((npr, D_MODEL), lambda j: (0, 0)),
                  pl.BlockSpec((D_MODEL, MOD_COLS), lambda j: (0, j)),
                  pl.BlockSpec((1, MOD_COLS), lambda j: (0, j))],
        out_specs=pl.BlockSpec((n, 1, MOD_COLS), lambda j: (0, 0, j)),
        out_shape=jax.ShapeDtypeStruct((n, 1, N_MOD * D_MODEL), _F32),
        name="adaln_mod",
    )(c_sample, c_prompt, w_ada, b_ada)


def _mod_norm(x, g, sc, sh):
    return (_rms(x, g) * (1.0 + sc) + sh).astype(_BF)


def _mix_prompt_kernel(x_ref, sh_ref, sc_ref, g_ref, xn_ref, shn_ref, scn_ref, n1g_ref,
                       win_ref, cw_ref, wco_ref, sinks_ref, wao_ref, wmo_ref, *rest,
                       tiles_per_seq, n_cast):
    cast_in, rest = rest[:n_cast], rest[n_cast:]
    (x1_ref, conv_ref, k_ref, v_ref), rest = rest[:4], rest[4:]
    cast_out, (u_ext, k_ext, v_ext, o_scr, bias_scr, h_scr) = rest[:n_cast], rest[n_cast:]
    step = pl.program_id(0)
    i = step % tiles_per_seq
    t = x_ref.shape[0]
    nblk = t // WINDOW

    for src, dst in zip(cast_in, cast_out):
        dst[...] = src[...].astype(_BF)

    @pl.when(step == 0)
    def _():
        h_scr[...] = _mod_norm(x_ref[...], n1g_ref[...], sc_ref[...], sh_ref[...])
        row = lax.broadcasted_iota(jnp.int32, (WINDOW, 2 * WINDOW), 0)
        col = lax.broadcasted_iota(jnp.int32, (WINDOW, 2 * WINDOW), 1)
        dist = WINDOW + row - col
        valid = (dist >= 0) & (dist <= WINDOW)
        distf = dist.astype(_F32)
        for head in range(N_HEADS):
            bias_scr[head] = jnp.where(valid, -(_slope(head) * distf), NEG_INF)

    @pl.when(i == 0)
    def _():
        u_ext[0:CARRY_ROWS, :] = jnp.zeros((CARRY_ROWS, D_MODEL), _F32)
        k_ext[:, 0:WINDOW, :] = jnp.zeros((2 * N_KV_HEADS, WINDOW, LANES), _BF)
        v_ext[:, 0:WINDOW, :] = jnp.zeros((2 * N_KV_HEADS, WINDOW, LANES), _BF)

    def proj(lo, width):
        return _dot(h_scr[...], win_ref[:, lo:lo + width])

    def next_tile_norm():
        h_scr[...] = _mod_norm(xn_ref[...], n1g_ref[...], scn_ref[...], shn_ref[...])

    side = {}

    def conv_in():
        u = proj(_OFF_C, D_MODEL) * proj(_OFF_X, D_MODEL)
        u_ext[CARRY_ROWS:CARRY_ROWS + t, :] = u
        cw = cw_ref[...]
        side["uc"] = (u_ext[CARRY_ROWS - 2:CARRY_ROWS - 2 + t, :] * cw[0:1, :]
                      + u_ext[CARRY_ROWS - 1:CARRY_ROWS - 1 + t, :] * cw[1:2, :]
                      + u * cw[2:3, :])
        conv_ref[...] = u[t - 2:t, :]
        u_ext[0:CARRY_ROWS, :] = u[t - CARRY_ROWS:t, :]

    def conv_out():
        pb = proj(_OFF_B, D_MODEL)
        side["ya"] = _dot((pb * side.pop("uc")).astype(_BF), wco_ref[...])

    def gate_a():
        side["ga"] = jax.nn.sigmoid(proj(_OFF_GA, D_MODEL))

    def gate_b():
        side["gb"] = jax.nn.sigmoid(proj(_OFF_GB, D_MODEL))
        next_tile_norm()

    side_jobs = [conv_in, conv_out, gate_a, gate_b]

    q = (proj(_OFF_Q, D_MODEL) * Q_SCALE).astype(_BF)
    kf = proj(_OFF_K, KV_WIDTH)
    vf = proj(_OFF_V, KV_WIDTH)
    k_ref[...] = kf[t - WINDOW:t, :].T
    v_ref[...] = vf[t - WINDOW:t, :].T
    lane_t = lax.broadcasted_iota(jnp.int32, (t, LANES), 1)
    lo = lane_t < HEAD_DIM
    ones_lane = (HEAD_DIM, 0)
    for kc in range(KV_WIDTH // LANES):
        ks = _split_heads(kf[:, kc * LANES:(kc + 1) * LANES], lo)
        vs = _split_heads(vf[:, kc * LANES:(kc + 1) * LANES], lo)
        for j in range(4):
            k_ext[4 * kc + j, WINDOW:WINDOW + t, :] = ks[j].astype(_BF)
            v_one = jnp.where(lane_t == ones_lane[j % HEADS_PER_COL], 1.0, vs[j])
            v_ext[4 * kc + j, WINDOW:WINDOW + t, :] = v_one.astype(_BF)

    first_cols = lax.broadcasted_iota(jnp.int32, (WINDOW, 2 * WINDOW), 1) < WINDOW
    no_prev = first_cols & (i == 0)
    lo_q = lax.broadcasted_iota(jnp.int32, (WINDOW, LANES), 1) < HEAD_DIM
    for blk in range(nblk):
        r0 = blk * WINDOW
        for c in range(N_QCOLS):
            hkv = c // 2
            qc = q[r0:r0 + WINDOW, c * LANES:(c + 1) * LANES]
            o_pair = []
            for par in range(HEADS_PER_COL):
                head = HEADS_PER_COL * c + par
                kk = k_ext[2 * hkv + par, r0:r0 + 2 * WINDOW, :]
                s = lax.dot_general(qc, kk, _NT, preferred_element_type=_F32) + bias_scr[head]
                if blk == 0:
                    s = jnp.where(no_prev, NEG_INF, s)
                sink = sinks_ref[head] * LOG2E
                m = jnp.maximum(jnp.max(s, axis=-1, keepdims=True), sink)
                p = jnp.exp2(s - m).astype(_BF)
                o_h = _dot(p, v_ext[2 * hkv + par, r0:r0 + 2 * WINDOW, :])
                col = ones_lane[par]
                denom = o_h[:, col:col + 1] + jnp.exp2(sink - m)
                o_pair.append(o_h * (1.0 / denom))
            o_c = jnp.where(lo_q, o_pair[0], o_pair[1])
            o_scr[r0:r0 + WINDOW, c * LANES:(c + 1) * LANES] = o_c.astype(_BF)
        for _ in range(-(-len(side_jobs) // (nblk - blk))):
            side_jobs.pop(0)()
    k_ext[:, 0:WINDOW, :] = k_ext[:, t:t + WINDOW, :]
    v_ext[:, 0:WINDOW, :] = v_ext[:, t:t + WINDOW, :]
    yb = _dot(o_scr[...], wao_ref[...])

    mixed = (side["ga"] * side["ya"] + side["gb"] * yb).astype(_BF)
    x1_ref[...] = x_ref[...] + g_ref[...] * _dot(mixed, wmo_ref[...])


def _const_spec(shape):
    nd = len(shape)
    return pl.BlockSpec(shape, lambda *_: (0,) * nd, pipeline_mode=pl.Buffered(1))


def _tile_specs(tile, nt, steps, mod_row0):
    def nxt(s):
        return jnp.minimum(s + 1, steps - 1)

    x_spec = pl.BlockSpec((None, tile, D_MODEL), lambda s: (s // nt, s % nt, 0))
    xn_spec = pl.BlockSpec((None, tile, D_MODEL), lambda s: (nxt(s) // nt, nxt(s) % nt, 0))

    def mod_spec(chunk):
        return pl.BlockSpec((None, 1, D_MODEL), lambda s: (mod_row0 + s // nt, 0, chunk))

    def modn_spec(chunk):
        return pl.BlockSpec((None, 1, D_MODEL), lambda s: (mod_row0 + nxt(s) // nt, 0, chunk))

    return x_spec, xn_spec, mod_spec, modn_spec


BF16_SUBLANES = 16


def _cast_stream_spec(shape, steps):
    rows, cols = shape
    for hold in range(1, steps + 1):
        nblocks, rem = divmod(steps, hold)
        if rem == 0 and rows % nblocks == 0 and (rows // nblocks) % BF16_SUBLANES == 0:
            return pl.BlockSpec((rows // nblocks, cols), lambda s: (s // hold, 0))
    raise ValueError(f"no bf16-aligned row split of {shape} over {steps} steps")


def _mix_prompt_call(x, mod3, mod_row0, n1g, win, cw, wco, sinks, wao, wmo, cast_along, tile):
    bsz, seq, _ = x.shape
    nt = seq // tile
    steps = bsz * nt
    x_spec, xn_spec, mod_spec, modn_spec = _tile_specs(tile, nt, steps, mod_row0)
    cast_specs = [_cast_stream_spec(w.shape, steps) for w in cast_along]

    def state_spec(d1, d2):
        return pl.BlockSpec((None, d1, d2), lambda s: (s // nt, 0, 0))

    return pl.pallas_call(
        functools.partial(_mix_prompt_kernel, tiles_per_seq=nt, n_cast=len(cast_along)),
        grid=(steps,),
        in_specs=[x_spec, mod_spec(0), mod_spec(1), mod_spec(2),
                  xn_spec, modn_spec(0), modn_spec(1),
                  _const_spec((1, D_MODEL)),
                  _const_spec((D_MODEL, IN_WIDTH)),
                  _const_spec((3, D_MODEL)),
                  _const_spec((D_MODEL, D_MODEL)),
                  pl.BlockSpec(memory_space=pltpu.SMEM),
                  _const_spec((D_MODEL, D_MODEL)),
                  _const_spec((D_MODEL, D_MODEL))] + cast_specs,
        out_specs=[x_spec,
                   state_spec(2, D_MODEL),
                   state_spec(KV_WIDTH, WINDOW),
                   state_spec(KV_WIDTH, WINDOW)] + cast_specs,
        out_shape=[jax.ShapeDtypeStruct((bsz, seq, D_MODEL), _F32),
                   jax.ShapeDtypeStruct((bsz, 2, D_MODEL), _F32),
                   jax.ShapeDtypeStruct((bsz, KV_WIDTH, WINDOW), _F32),
                   jax.ShapeDtypeStruct((bsz, KV_WIDTH, WINDOW), _F32)]
        + [jax.ShapeDtypeStruct(w.shape, _BF) for w in cast_along],
        scratch_shapes=[pltpu.VMEM((CARRY_ROWS + tile, D_MODEL), _F32),
                        pltpu.VMEM((2 * N_KV_HEADS, WINDOW + tile, LANES), _BF),
                        pltpu.VMEM((2 * N_KV_HEADS, WINDOW + tile, LANES), _BF),
                        pltpu.VMEM((tile, D_MODEL), _BF),
                        pltpu.VMEM((N_HEADS, WINDOW, 2 * WINDOW), _F32),
                        pltpu.VMEM((tile, D_MODEL), _BF)],
        compiler_params=pltpu.CompilerParams(
            dimension_semantics=("arbitrary",),
            vmem_limit_bytes=VMEM_LIMIT),
        name="mix_prompt",
    )(x, mod3, mod3, mod3, x, mod3, mod3, n1g, win, cw, wco, sinks, wao, wmo, *cast_along)


def _conv_seq(u2d, st, cw, nseq, tlen):
    width = u2d.shape[-1]
    shape3 = (nseq, tlen, width)
    u3 = u2d.reshape(shape3)
    r1 = pltpu.roll(u2d, 1, 0).reshape(shape3)
    r2 = pltpu.roll(u2d, 2, 0).reshape(shape3)
    tpos = lax.broadcasted_iota(jnp.int32, shape3, 1)
    st0 = jnp.broadcast_to(st[:, 0:1, :], shape3)
    st1 = jnp.broadcast_to(st[:, 1:2, :], shape3)
    um1 = jnp.where(tpos == 0, st1, r1)
    um2 = jnp.where(tpos == 0, st0, jnp.where(tpos == 1, st1, r2))
    y = um2 * cw[0:1, :] + um1 * cw[1:2, :] + u3 * cw[2:3, :]
    return y.reshape(nseq * tlen, width), u3[:, tlen - 2:tlen, :]


def _mix_sample_kernel(x_ref, sh_ref, sc_ref, g_ref, n1g_ref, win_ref, cw_ref, wco_ref,
                       sinks_ref, wao_ref, wmo_ref, st_ref, ck_ref, cv_ref,
                       x1_ref, conv_ref, k_ref, v_ref):
    nseq, tlen, _ = x_ref.shape
    rows = nseq * tlen

    assert rows == LANES, "one step's new keys must fill exactly one lane tile"

    x3 = x_ref[...]
    h3 = _rms(x3, n1g_ref[...]) * (1.0 + sc_ref[...]) + sh_ref[...]
    hf = h3.reshape(rows, D_MODEL)
    h_now = [hf.astype(_BF)]

    def proj(lo, width):
        return _dot(h_now[0], win_ref[:, lo:lo + width])

    def order_after(tile_f32):
        bits = pltpu.bitcast(tile_f32[0:8, :], jnp.uint32)
        zero = ((bits >> 16) >> 16).astype(_F32)
        h_now[0] = (hf + jnp.tile(zero, (rows // 8, D_MODEL // LANES))).astype(_BF)

    side = {}

    def job_conv_in():
        u = proj(_OFF_C, D_MODEL) * proj(_OFF_X, D_MODEL)
        side["uc"], new_st = _conv_seq(u, st_ref[...], cw_ref[...], nseq, tlen)
        conv_ref[...] = new_st

    def job_gate_b():
        side["pb"] = proj(_OFF_B, D_MODEL)

    def job_conv_out():
        side["ya"] = _dot((side.pop("pb") * side.pop("uc")).astype(_BF), wco_ref[...])

    def job_ga():
        side["ga"] = jax.nn.sigmoid(proj(_OFF_GA, D_MODEL))

    def job_gb():
        side["gb"] = jax.nn.sigmoid(proj(_OFF_GB, D_MODEL))

    keep = WINDOW - tlen
    qf = proj(_OFF_Q, D_MODEL) * Q_SCALE
    kvt_new = proj(_OFF_K, 2 * KV_WIDTH).T
    keep_old = lax.broadcasted_iota(jnp.int32, (KV_WIDTH, WINDOW), 1) < keep

    def window_update(s):
        shift = (keep - s * tlen) % LANES
        for cache_ref, out_ref, base in ((ck_ref, k_ref, 0), (cv_ref, v_ref, KV_WIDTH)):
            rolled = pltpu.roll(cache_ref[s], keep, 1)
            new = pltpu.roll(kvt_new[base:base + KV_WIDTH, :], shift, 1)
            win = jnp.where(keep_old, rolled, new)
            out_ref[s] = win
        return win

    lo = lax.broadcasted_iota(jnp.int32, (rows, LANES), 1) < HEAD_DIM
    zero = jnp.zeros((rows, LANES), _F32)
    n_tiles = KV_WIDTH // LANES
    heads_per_tile = N_HEADS // n_tiles
    m_rows = heads_per_tile * tlen
    row = lax.broadcasted_iota(jnp.int32, (m_rows, WINDOW), 0)
    col = lax.broadcasted_iota(jnp.int32, (m_rows, WINDOW), 1)
    dist_c = WINDOW + row % tlen - col
    valid_c = dist_c <= WINDOW
    shape3 = (nseq, m_rows, LANES)
    seq3 = lax.broadcasted_iota(jnp.int32, shape3, 0)
    row3 = lax.broadcasted_iota(jnp.int32, shape3, 1)
    col3 = lax.broadcasted_iota(jnp.int32, shape3, 2)
    dist_n = row3 % tlen - col3 % tlen
    valid_n = (col3 // tlen == seq3) & (dist_n >= 0)
    row1 = lax.broadcasted_iota(jnp.int32, (m_rows, 1), 0)

    def attention(kc):
        parts = []
        slope = jnp.zeros((m_rows, 1), _F32)
        sink = jnp.zeros((m_rows, 1), _F32)
        for j in range(heads_per_tile):
            head = heads_per_tile * kc + j
            c, par, half = head // HEADS_PER_COL, j % HEADS_PER_COL, j // (heads_per_tile // 2)
            qc = qf[:, c * LANES:(c + 1) * LANES]
            src = qc if par == half else pltpu.roll(qc, HEAD_DIM, 1)
            part = jnp.where(lo, src, zero) if half == 0 else jnp.where(lo, zero, src)
            parts.append(part.reshape(nseq, tlen, LANES))
            mine = (row1 >= j * tlen) & (row1 < (j + 1) * tlen)
            slope = jnp.where(mine, _slope(head), slope)
            sink = jnp.where(mine, sinks_ref[head] * LOG2E, sink)
        lhs = jnp.concatenate(parts, axis=1).astype(_BF)
        lhs2 = lhs.reshape(nseq * m_rows, LANES)
        rsl = slice(kc * LANES, (kc + 1) * LANES)
        k_new = kvt_new[rsl, :].astype(_BF)
        v_new = kvt_new[KV_WIDTH + kc * LANES:KV_WIDTH + (kc + 1) * LANES, :].astype(_BF)
        s_c = jnp.einsum('smk,skn->smn', lhs, ck_ref[:, rsl, :].astype(_BF),
                         preferred_element_type=_F32)
        s_n = _dot(lhs2, k_new).reshape(shape3)
        sc = jnp.concatenate(
            [s_c + jnp.where(valid_c, -(slope * dist_c.astype(_F32)), NEG_INF),
             s_n + jnp.where(valid_n, -(slope * dist_n.astype(_F32)), NEG_INF)], axis=2)
        p, inv = _softmax_parts(sc, sink)
        p = p.astype(_BF)
        r_c = jnp.einsum('smk,snk->smn', p[:, :, 0:WINDOW], cv_ref[:, rsl, :].astype(_BF),
                         preferred_element_type=_F32)
        r_n = lax.dot_general(p[:, :, WINDOW:].reshape(nseq * m_rows, LANES), v_new, _NT,
                              preferred_element_type=_F32).reshape(shape3)
        r = (r_c + r_n) * inv
        cols = {}
        for m2 in range(heads_per_tile // HEADS_PER_COL):
            c = (heads_per_tile // HEADS_PER_COL) * kc + m2
            half = m2 // (heads_per_tile // (2 * HEADS_PER_COL))
            r0 = r[:, (2 * m2) * tlen:(2 * m2 + 1) * tlen, :].reshape(rows, LANES)
            r1 = r[:, (2 * m2 + 1) * tlen:(2 * m2 + 2) * tlen, :].reshape(rows, LANES)
            if half == 0:
                cols[c] = jnp.where(lo, r0, pltpu.roll(r1, HEAD_DIM, 1))
            else:
                cols[c] = jnp.where(lo, pltpu.roll(r0, HEAD_DIM, 1), r1)
        return cols

    o_cols = {}
    job_conv_in()
    window_jobs = [job_gate_b, job_conv_out, job_ga, job_gb]
    group = -(-nseq // len(window_jobs))
    for s in range(nseq):
        if s == ATTN_AFTER_WINDOWS:
            for kc in range(n_tiles):
                o_cols.update(attention(kc))
        last_win = window_update(s)
        if window_jobs and (s + 1) % group == 0:
            window_jobs.pop(0)()
            order_after(last_win)
    while window_jobs:
        window_jobs.pop(0)()
    o = jnp.concatenate([o_cols[c] for c in range(N_QCOLS)], axis=1).astype(_BF)
    yb = _dot(o, wao_ref[...])

    mixed = (side["ga"] * side["ya"] + side["gb"] * yb).astype(_BF)
    y = _dot(mixed, wmo_ref[...]).reshape(nseq, tlen, D_MODEL)
    x1_ref[...] = x3 + g_ref[...] * y


def _mix_sample_call(x, mod3, n1g, win, cw, wco, sinks, wao, wmo, st, ck, cv, chunk):
    nseq, tlen, _ = x.shape
    assert LANES % tlen == 0 and chunk % (LANES // tlen) == 0

    def mod_spec(col):
        return pl.BlockSpec((chunk, 1, D_MODEL), lambda s: (s, 0, col))

    def seq_spec(d1, d2):
        return pl.BlockSpec((chunk, d1, d2), lambda s: (s, 0, 0))

    return pl.pallas_call(
        _mix_sample_kernel,
        grid=(nseq // chunk,),
        in_specs=[seq_spec(tlen, D_MODEL),
                  mod_spec(0), mod_spec(1), mod_spec(2),
                  _const_spec((1, D_MODEL)),
                  _const_spec((D_MODEL, IN_WIDTH)),
                  _const_spec((3, D_MODEL)),
                  _const_spec((D_MODEL, D_MODEL)),
                  pl.BlockSpec(memory_space=pltpu.SMEM),
                  _const_spec((D_MODEL, D_MODEL)),
                  _const_spec((D_MODEL, D_MODEL)),
                  seq_spec(2, D_MODEL),
                  seq_spec(KV_WIDTH, WINDOW),
                  seq_spec(KV_WIDTH, WINDOW)],
        out_specs=[seq_spec(tlen, D_MODEL),
                   seq_spec(2, D_MODEL),
                   seq_spec(KV_WIDTH, WINDOW),
                   seq_spec(KV_WIDTH, WINDOW)],
        out_shape=[jax.ShapeDtypeStruct((nseq, tlen, D_MODEL), _F32),
                   jax.ShapeDtypeStruct((nseq, 2, D_MODEL), _F32),
                   jax.ShapeDtypeStruct((nseq, KV_WIDTH, WINDOW), _F32),
                   jax.ShapeDtypeStruct((nseq, KV_WIDTH, WINDOW), _F32)],
        compiler_params=pltpu.CompilerParams(
            dimension_semantics=("arbitrary",),
            vmem_limit_bytes=VMEM_LIMIT),
        name="mix_sample",
    )(x, mod3, mod3, mod3, n1g, win, cw, wco, sinks, wao, wmo, st, ck, cv)


def _ffn_tail(x1, ac, val, g, wdown_ref, fg):
    hmid = (_gelu(ac) * val).astype(_BF)
    x2 = x1 + g * _dot(hmid, wdown_ref[...])
    return _rms(x2, fg)


def _ffn_prompt_kernel(x1_ref, sh_ref, sc_ref, g_ref, n2g_ref, wup_ref, fcw_ref, wdown_ref,
                       fg_ref, y_ref, ffn_ref, a_ext, *, parts):
    i = pl.program_id(1)
    t = x1_ref.shape[0]
    rows = t // parts

    @pl.when(i == 0)
    def _():
        a_ext[0:CARRY_ROWS, :] = jnp.zeros((CARRY_ROWS, D_FF), _F32)

    fcw = fcw_ref[...]

    def up(part):
        r0 = part * rows
        e0 = CARRY_ROWS + r0
        h2 = _mod_norm(x1_ref[r0:r0 + rows, :], n2g_ref[...], sc_ref[...], sh_ref[...])
        a = _dot(h2, wup_ref[:, 0:D_FF])
        val = _dot(h2, wup_ref[:, D_FF:2 * D_FF])
        a_ext[e0:e0 + rows, :] = a
        ac = (a_ext[e0 - 2:e0 - 2 + rows, :] * fcw[0:1, :]
              + a_ext[e0 - 1:e0 - 1 + rows, :] * fcw[1:2, :]
              + a * fcw[2:3, :])
        return ac, val

    def down(part, ac, val):
        r0 = part * rows
        y_ref[r0:r0 + rows, :] = _ffn_tail(x1_ref[r0:r0 + rows, :], ac, val, g_ref[...],
                                           wdown_ref, fg_ref[...])

    for part in range(parts):
        down(part, *up(part))
    ffn_ref[...] = a_ext[CARRY_ROWS + t - 2:CARRY_ROWS + t, :]
    a_ext[0:CARRY_ROWS, :] = a_ext[t:t + CARRY_ROWS, :]


def _ffn_prompt_call(x1, mod3, mod_row0, n2g, wup, fcw, wdown, fg, tile, parts):
    bsz, seq, _ = x1.shape

    def mod_spec(chunk):
        return pl.BlockSpec((None, 1, D_MODEL), lambda b, i: (mod_row0 + b, 0, chunk))

    x_spec = pl.BlockSpec((None, tile, D_MODEL), lambda b, i: (b, i, 0))
    return pl.pallas_call(
        functools.partial(_ffn_prompt_kernel, parts=parts),
        grid=(bsz, seq // tile),
        in_specs=[x_spec, mod_spec(3), mod_spec(4), mod_spec(5),
                  _const_spec((1, D_MODEL)),
                  _const_spec((D_MODEL, 2 * D_FF)),
                  _const_spec((3, D_FF)),
                  _const_spec((D_FF, D_MODEL)),
                  _const_spec((1, D_MODEL))],
        out_specs=[x_spec,
                   pl.BlockSpec((None, 2, D_FF), lambda b, i: (b, 0, 0))],
        out_shape=[jax.ShapeDtypeStruct((bsz, seq, D_MODEL), _F32),
                   jax.ShapeDtypeStruct((bsz, 2, D_FF), _F32)],
        scratch_shapes=[pltpu.VMEM((CARRY_ROWS + tile, D_FF), _F32)],
        compiler_params=pltpu.CompilerParams(
            dimension_semantics=("arbitrary", "arbitrary"),
            vmem_limit_bytes=VMEM_LIMIT),
        name="ffn_prompt",
    )(x1, mod3, mod3, mod3, n2g, wup, fcw, wdown, fg)


def _ffn_sample_kernel(x1_ref, sh_ref, sc_ref, g_ref, n2g_ref, wup_ref, fcw_ref, wdown_ref,
                       fg_ref, st_ref, y_ref, ffn_ref):
    nseq, tlen, _ = x1_ref.shape
    rows = nseq * tlen
    x3 = x1_ref[...]
    h3 = _rms(x3, n2g_ref[...]) * (1.0 + sc_ref[...]) + sh_ref[...]
    h2 = h3.reshape(rows, D_MODEL).astype(_BF)
    a = _dot(h2, wup_ref[:, 0:D_FF])
    val = _dot(h2, wup_ref[:, D_FF:2 * D_FF])
    ac, new_st = _conv_seq(a, st_ref[...], fcw_ref[...], nseq, tlen)
    ffn_ref[...] = new_st
    hmid = (_gelu(ac) * val).astype(_BF)
    y = _dot(hmid, wdown_ref[...]).reshape(nseq, tlen, D_MODEL)
    x2 = x3 + g_ref[...] * y
    y_ref[...] = _rms(x2, fg_ref[...])


def _ffn_sample_call(x1, mod3, n2g, wup, fcw, wdown, fg, st, chunk):
    nseq, tlen, _ = x1.shape

    def mod_spec(col):
        return pl.BlockSpec((chunk, 1, D_MODEL), lambda s: (s, 0, col))

    def seq_spec(d1, d2):
        return pl.BlockSpec((chunk, d1, d2), lambda s: (s, 0, 0))

    return pl.pallas_call(
        _ffn_sample_kernel,
        grid=(nseq // chunk,),
        in_specs=[seq_spec(tlen, D_MODEL),
                  mod_spec(3), mod_spec(4), mod_spec(5),
                  _const_spec((1, D_MODEL)),
                  _const_spec((D_MODEL, 2 * D_FF)),
                  _const_spec((3, D_FF)),
                  _const_spec((D_FF, D_MODEL)),
                  _const_spec((1, D_MODEL)),
                  seq_spec(2, D_FF)],
        out_specs=[seq_spec(tlen, D_MODEL), seq_spec(2, D_FF)],
        out_shape=[jax.ShapeDtypeStruct((nseq, tlen, D_MODEL), _F32),
                   jax.ShapeDtypeStruct((nseq, 2, D_FF), _F32)],
        compiler_params=pltpu.CompilerParams(
            dimension_semantics=("arbitrary",),
            vmem_limit_bytes=VMEM_LIMIT),
        name="ffn_sample",
    )(x1, mod3, mod3, mod3, n2g, wup, fcw, wdown, fg, st)


PROMPT_TILE = 512
FFN_TILE = 1024
FFN_PARTS = 2
SAMPLE_CHUNK = 16
ATTN_AFTER_WINDOWS = 4
FFN_SAMPLE_CHUNK = 64


def kernel(x_prompt, x_sample, c_prompt, c_sample, state_conv, cache_k_win, cache_v_win,
           state_ffn_conv, norm1_g, norm2_g, w_ada, b_ada, w_in, conv_w, w_conv_out,
           attn_sinks, w_attn_out, w_mix_out, w_up, ffn_conv_w, w_down, final_g):
    depth = w_in.shape[0]
    assert depth == 1, "the final norm is fused into the (single) layer's FFN call"
    nsamp = x_sample.shape[0]
    bsz = x_prompt.shape[0]
    xp, xs = x_prompt, x_sample
    fg = final_g.reshape(1, D_MODEL)
    outs = [[] for _ in range(8)]
    for layer in range(depth):
        mod3 = _mod_call(c_sample, c_prompt, w_ada[layer], b_ada[layer].reshape(1, -1))
        n1g = norm1_g[layer].reshape(1, D_MODEL)
        n2g = norm2_g[layer].reshape(1, D_MODEL)
        win = w_in[layer].astype(_BF)
        wco = w_conv_out[layer].astype(_BF)
        wao = w_attn_out[layer].astype(_BF)
        wmo = w_mix_out[layer].astype(_BF)
        cw = conv_w[layer]
        fcw = ffn_conv_w[layer]
        sinks = attn_sinks[layer]
        fg_l = fg
        def to_dim_major(win_state):
            return jnp.transpose(win_state, (0, 2, 3, 1)).reshape(-1, KV_WIDTH, WINDOW)

        def to_pos_major(t3):
            return jnp.transpose(t3.reshape(-1, N_KV_HEADS, HEAD_DIM, WINDOW), (0, 3, 1, 2))

        x1p, conv_p, k_p, v_p, wup, wdown = _mix_prompt_call(
            xp, mod3, nsamp, n1g, win, cw, wco, sinks, wao, wmo,
            (w_up[layer], w_down[layer]), PROMPT_TILE)
        x1s, conv_s, k_s, v_s = _mix_sample_call(
            xs, mod3, n1g, win, cw, wco, sinks, wao, wmo, state_conv[layer],
            to_dim_major(cache_k_win[layer]), to_dim_major(cache_v_win[layer]), SAMPLE_CHUNK)
        xp, ffn_p = _ffn_prompt_call(x1p, mod3, nsamp, n2g, wup, fcw, wdown, fg_l,
                                     FFN_TILE, FFN_PARTS)
        xs, ffn_s = _ffn_sample_call(x1s, mod3, n2g, wup, fcw, wdown, fg_l,
                                     state_ffn_conv[layer], FFN_SAMPLE_CHUNK)
        for lst, val in zip(outs, (conv_p, to_pos_major(k_p), to_pos_major(v_p), ffn_p,
                                   conv_s, to_pos_major(k_s), to_pos_major(v_s), ffn_s)):
            lst.append(val)
    return (xp, xs) + tuple(jnp.stack(lst, axis=0) for lst in outs)
```

```python
import functools

import jax
import jax.numpy as jnp
from jax import lax
from jax.experimental import pallas as pl
from jax.experimental.pallas import tpu as pltpu

D_MODEL = 1024
HEAD_DIM = 64
N_HEADS = 16
N_KV_HEADS = 4
WINDOW = 128
D_FF = 2816
KV_WIDTH = N_KV_HEADS * HEAD_DIM
IN_WIDTH = 3 * D_MODEL + D_MODEL + 2 * KV_WIDTH + 2 * D_MODEL
RMS_EPS = 1e-6
NEG_INF = -1e30
LANES = 128
HEADS_PER_COL = LANES // HEAD_DIM
N_QCOLS = N_HEADS * HEAD_DIM // LANES
CARRY_ROWS = 8
VMEM_LIMIT = 60 * 1024 * 1024

_OFF_B, _OFF_C, _OFF_X = 0, D_MODEL, 2 * D_MODEL
_OFF_Q = 3 * D_MODEL
_OFF_K = _OFF_Q + D_MODEL
_OFF_V = _OFF_K + KV_WIDTH
_OFF_GA = _OFF_V + KV_WIDTH
_OFF_GB = _OFF_GA + D_MODEL

_BF = jnp.bfloat16
_F32 = jnp.float32
_NT = (((1,), (1,)), ((), ()))


LOG2E = 1.4426950408889634
Q_SCALE = HEAD_DIM ** -0.5 * LOG2E


def _slope(head):
    return 2.0 ** (-8.0 * (head + 1) / N_HEADS) * LOG2E


def _rms(x, g):
    return x * lax.rsqrt(jnp.mean(x * x, axis=-1, keepdims=True) + RMS_EPS) * g


def _dot(a, b):
    return jnp.dot(a, b, preferred_element_type=_F32)


def _gelu(x):
    return 0.5 * x * (1.0 + lax.erf(x * (0.5 ** 0.5)))


def _split_heads(t2d, lo):
    rolled = pltpu.roll(t2d, HEAD_DIM, 1)
    zero = jnp.zeros_like(t2d)
    return (jnp.where(lo, t2d, zero), jnp.where(lo, zero, rolled),
            jnp.where(lo, rolled, zero), jnp.where(lo, zero, t2d))


def _softmax_parts(s, sink):
    m = jnp.maximum(jnp.max(s, axis=-1, keepdims=True), sink)
    p = jnp.exp2(s - m)
    denom = jnp.sum(p, axis=-1, keepdims=True) + jnp.exp2(sink - m)
    return p, 1.0 / denom


N_MOD = 6
MOD_COLS = 2048


def _mod_kernel(cs_ref, cp_ref, w_ref, b_ref, o_ref):
    w = w_ref[...].astype(_BF)
    n0 = cs_ref.shape[0]
    for c_ref, r0 in ((cs_ref, 0), (cp_ref, n0)):
        c = c_ref[...]
        a = (c * jax.nn.sigmoid(c)).astype(_BF)
        res = _dot(a, w) + b_ref[...]
        n = c.shape[0]
        o_ref[r0:r0 + n] = res.reshape(n, 1, res.shape[-1])


def _mod_call(c_sample, c_prompt, w_ada, b_ada):
    ns, npr = c_sample.shape[0], c_prompt.shape[0]
    n = ns + npr
    return pl.pallas_call(
        _mod_kernel,
        grid=(N_MOD * D_MODEL // MOD_COLS,),
        in_specs=[pl.BlockSpec((ns, D_MODEL), lambda j: (0, 0)),
                  pl.BlockSpec((npr, D_MODEL), lambda j: (0, 0)),
                  pl.BlockSpec((D_MODEL, MOD_COLS), lambda j: (0, j)),
                  pl.BlockSpec((1, MOD_COLS), lambda j: (0, j))],
        out_specs=pl.BlockSpec((n, 1, MOD_COLS), lambda j: (0, 0, j)),
        out_shape=jax.ShapeDtypeStruct((n, 1, N_MOD * D_MODEL), _F32),
        name="adaln_mod",
    )(c_sample, c_prompt, w_ada, b_ada)


def _mod_norm(x, g, sc, sh):
    return (_rms(x, g) * (1.0 + sc) + sh).astype(_BF)


def _mix_prompt_kernel(x_ref, sh_ref, sc_ref, g_ref, xn_ref, shn_ref, scn_ref, n1g_ref,
                       win_ref, cw_ref, wco_ref, sinks_ref, wao_ref, wmo_ref, *rest,
                       tiles_per_seq, n_cast):
    cast_in, rest = rest[:n_cast], rest[n_cast:]
    (x1_ref, conv_ref, k_ref, v_ref), rest = rest[:4], rest[4:]
    cast_out, (u_ext, k_ext, v_ext, o_scr, bias_scr, h_scr) = rest[:n_cast], rest[n_cast:]
    step = pl.program_id(0)
    i = step % tiles_per_seq
    t = x_ref.shape[0]
    nblk = t // WINDOW

    for src, dst in zip(cast_in, cast_out):
        dst[...] = src[...].astype(_BF)

    @pl.when(step == 0)
    def _():
        h_scr[...] = _mod_norm(x_ref[...], n1g_ref[...], sc_ref[...], sh_ref[...])
        row = lax.broadcasted_iota(jnp.int32, (WINDOW, 2 * WINDOW), 0)
        col = lax.broadcasted_iota(jnp.int32, (WINDOW, 2 * WINDOW), 1)
        dist = WINDOW + row - col
        valid = (dist >= 0) & (dist <= WINDOW)
        distf = dist.astype(_F32)
        for head in range(N_HEADS):
            bias_scr[head] = jnp.where(valid, -(_slope(head) * distf), NEG_INF)

    @pl.when(i == 0)
    def _():
        u_ext[0:CARRY_ROWS, :] = jnp.zeros((CARRY_ROWS, D_MODEL), _F32)
        k_ext[:, 0:WINDOW, :] = jnp.zeros((N_KV_HEADS, WINDOW, LANES), _BF)
        v_ext[:, 0:WINDOW, :] = jnp.zeros((N_KV_HEADS, WINDOW, 2 * LANES), _BF)

    def proj(lo, width):
        return _dot(h_scr[...], win_ref[:, lo:lo + width])

    def next_tile_norm():
        h_scr[...] = _mod_norm(xn_ref[...], n1g_ref[...], scn_ref[...], shn_ref[...])

    side = {}

    def conv_in():
        u = proj(_OFF_C, D_MODEL) * proj(_OFF_X, D_MODEL)
        u_ext[CARRY_ROWS:CARRY_ROWS + t, :] = u
        cw = cw_ref[...]
        side["uc"] = (u_ext[CARRY_ROWS - 2:CARRY_ROWS - 2 + t, :] * cw[0:1, :]
                      + u_ext[CARRY_ROWS - 1:CARRY_ROWS - 1 + t, :] * cw[1:2, :]
                      + u * cw[2:3, :])
        conv_ref[...] = u[t - 2:t, :]
        u_ext[0:CARRY_ROWS, :] = u[t - CARRY_ROWS:t, :]

    def conv_out():
        pb = proj(_OFF_B, D_MODEL)
        side["ya"] = _dot((pb * side.pop("uc")).astype(_BF), wco_ref[...])

    def gate_a():
        side["ga"] = jax.nn.sigmoid(proj(_OFF_GA, D_MODEL))

    def gate_b():
        side["gb"] = jax.nn.sigmoid(proj(_OFF_GB, D_MODEL))
        next_tile_norm()

    side_jobs = [conv_in, conv_out, gate_a, gate_b]

    qf = proj(_OFF_Q, D_MODEL) * Q_SCALE
    q = qf.astype(_BF)
    q_swap = jnp.concatenate(
        [pltpu.roll(qf[:, c * LANES:(c + 1) * LANES], HEAD_DIM, 1) for c in range(N_QCOLS)],
        axis=1).astype(_BF)
    kf = proj(_OFF_K, KV_WIDTH)
    vf = proj(_OFF_V, KV_WIDTH)
    k_ref[...] = kf[t - WINDOW:t, :].T
    v_ref[...] = vf[t - WINDOW:t, :].T
    lo = lax.broadcasted_iota(jnp.int32, (t, LANES), 1) < HEAD_DIM
    one = jnp.ones((t, LANES), _F32)
    for kc in range(KV_WIDTH // LANES):
        ks = _split_heads(kf[:, kc * LANES:(kc + 1) * LANES], lo)
        vs = _split_heads(vf[:, kc * LANES:(kc + 1) * LANES], lo)
        for j in range(2):
            hkv = 2 * kc + j
            k_ext[hkv, WINDOW:WINDOW + t, :] = ks[2 * j].astype(_BF)
            v_ext[hkv, WINDOW:WINDOW + t, :] = jnp.concatenate(
                [jnp.where(lo, vs[2 * j], one), jnp.where(lo, one, vs[2 * j + 1])],
                axis=1).astype(_BF)

    first_cols = lax.broadcasted_iota(jnp.int32, (WINDOW, 2 * WINDOW), 1) < WINDOW
    no_prev = first_cols & (i == 0)
    lo_q = lax.broadcasted_iota(jnp.int32, (WINDOW, LANES), 1) < HEAD_DIM
    group = N_HEADS // N_KV_HEADS
    for blk in range(nblk):
        r0 = blk * WINDOW
        for hkv in range(N_KV_HEADS):
            lhs = []
            for j in range(group):
                c = (group * hkv + j) // HEADS_PER_COL
                src = q if j % HEADS_PER_COL == 0 else q_swap
                lhs.append(src[r0:r0 + WINDOW, c * LANES:(c + 1) * LANES])
            s4 = lax.dot_general(jnp.concatenate(lhs, axis=0), k_ext[hkv, r0:r0 + 2 * WINDOW, :],
                                 _NT, preferred_element_type=_F32)
            ps, sink_terms = [], []
            for j in range(group):
                head = group * hkv + j
                s = s4[j * WINDOW:(j + 1) * WINDOW, :] + bias_scr[head]
                if blk == 0:
                    s = jnp.where(no_prev, NEG_INF, s)
                sink = sinks_ref[head] * LOG2E
                m = jnp.maximum(jnp.max(s, axis=-1, keepdims=True), sink)
                ps.append(jnp.exp2(s - m).astype(_BF))
                sink_terms.append(jnp.exp2(sink - m))
            o4 = _dot(jnp.concatenate(ps, axis=0), v_ext[hkv, r0:r0 + 2 * WINDOW, :])
            for m2 in range(group // HEADS_PER_COL):
                c = (group * hkv) // HEADS_PER_COL + m2
                even = o4[(2 * m2) * WINDOW:(2 * m2 + 1) * WINDOW, :]
                odd = o4[(2 * m2 + 1) * WINDOW:(2 * m2 + 2) * WINDOW, :]
                out = jnp.where(lo_q, even[:, 0:LANES], odd[:, LANES:2 * LANES])
                rsum = jnp.where(lo_q, even[:, LANES:2 * LANES], odd[:, 0:LANES])
                sterm = jnp.where(lo_q, sink_terms[2 * m2], sink_terms[2 * m2 + 1])
                o_scr[r0:r0 + WINDOW, c * LANES:(c + 1) * LANES] = (
                    out * (1.0 / (rsum + sterm))).astype(_BF)
        for _ in range(-(-len(side_jobs) // (nblk - blk))):
            side_jobs.pop(0)()
    k_ext[:, 0:WINDOW, :] = k_ext[:, t:t + WINDOW, :]
    v_ext[:, 0:WINDOW, :] = v_ext[:, t:t + WINDOW, :]
    yb = _dot(o_scr[...], wao_ref[...])

    mixed = (side["ga"] * side["ya"] + side["gb"] * yb).astype(_BF)
    x1_ref[...] = x_ref[...] + g_ref[...] * _dot(mixed, wmo_ref[...])


def _const_spec(shape):
    nd = len(shape)
    return pl.BlockSpec(shape, lambda *_: (0,) * nd, pipeline_mode=pl.Buffered(1))


def _tile_specs(tile, nt, steps, mod_row0):
    def nxt(s):
        return jnp.minimum(s + 1, steps - 1)

    x_spec = pl.BlockSpec((None, tile, D_MODEL), lambda s: (s // nt, s % nt, 0))
    xn_spec = pl.BlockSpec((None, tile, D_MODEL), lambda s: (nxt(s) // nt, nxt(s) % nt, 0))

    def mod_spec(chunk):
        return pl.BlockSpec((None, 1, D_MODEL), lambda s: (mod_row0 + s // nt, 0, chunk))

    def modn_spec(chunk):
        return pl.BlockSpec((None, 1, D_MODEL), lambda s: (mod_row0 + nxt(s) // nt, 0, chunk))

    return x_spec, xn_spec, mod_spec, modn_spec


BF16_SUBLANES = 16


def _cast_stream_spec(shape, steps):
    rows, cols = shape
    for hold in range(1, steps + 1):
        nblocks, rem = divmod(steps, hold)
        if rem == 0 and rows % nblocks == 0 and (rows // nblocks) % BF16_SUBLANES == 0:
            return pl.BlockSpec((rows // nblocks, cols), lambda s: (s // hold, 0))
    raise ValueError(f"no bf16-aligned row split of {shape} over {steps} steps")


def _mix_prompt_call(x, mod3, mod_row0, n1g, win, cw, wco, sinks, wao, wmo, cast_along, tile):
    bsz, seq, _ = x.shape
    nt = seq // tile
    steps = bsz * nt
    x_spec, xn_spec, mod_spec, modn_spec = _tile_specs(tile, nt, steps, mod_row0)
    cast_specs = [_cast_stream_spec(w.shape, steps) for w in cast_along]

    def state_spec(d1, d2):
        return pl.BlockSpec((None, d1, d2), lambda s: (s // nt, 0, 0))

    return pl.pallas_call(
        functools.partial(_mix_prompt_kernel, tiles_per_seq=nt, n_cast=len(cast_along)),
        grid=(steps,),
        in_specs=[x_spec, mod_spec(0), mod_spec(1), mod_spec(2),
                  xn_spec, modn_spec(0), modn_spec(1),
                  _const_spec((1, D_MODEL)),
                  _const_spec((D_MODEL, IN_WIDTH)),
                  _const_spec((3, D_MODEL)),
                  _const_spec((D_MODEL, D_MODEL)),
                  pl.BlockSpec(memory_space=pltpu.SMEM),
                  _const_spec((D_MODEL, D_MODEL)),
                  _const_spec((D_MODEL, D_MODEL))] + cast_specs,
        out_specs=[x_spec,
                   state_spec(2, D_MODEL),
                   state_spec(KV_WIDTH, WINDOW),
                   state_spec(KV_WIDTH, WINDOW)] + cast_specs,
        out_shape=[jax.ShapeDtypeStruct((bsz, seq, D_MODEL), _F32),
                   jax.ShapeDtypeStruct((bsz, 2, D_MODEL), _F32),
                   jax.ShapeDtypeStruct((bsz, KV_WIDTH, WINDOW), _F32),
                   jax.ShapeDtypeStruct((bsz, KV_WIDTH, WINDOW), _F32)]
        + [jax.ShapeDtypeStruct(w.shape, _BF) for w in cast_along],
        scratch_shapes=[pltpu.VMEM((CARRY_ROWS + tile, D_MODEL), _F32),
                        pltpu.VMEM((N_KV_HEADS, WINDOW + tile, LANES), _BF),
                        pltpu.VMEM((N_KV_HEADS, WINDOW + tile, 2 * LANES), _BF),
                        pltpu.VMEM((tile, D_MODEL), _BF),
                        pltpu.VMEM((N_HEADS, WINDOW, 2 * WINDOW), _F32),
                        pltpu.VMEM((tile, D_MODEL), _BF)],
        compiler_params=pltpu.CompilerParams(
            dimension_semantics=("arbitrary",),
            vmem_limit_bytes=VMEM_LIMIT),
        name="mix_prompt",
    )(x, mod3, mod3, mod3, x, mod3, mod3, n1g, win, cw, wco, sinks, wao, wmo, *cast_along)


def _conv_seq(u2d, st, cw, nseq, tlen):
    width = u2d.shape[-1]
    shape3 = (nseq, tlen, width)
    u3 = u2d.reshape(shape3)
    r1 = pltpu.roll(u2d, 1, 0).reshape(shape3)
    r2 = pltpu.roll(u2d, 2, 0).reshape(shape3)
    tpos = lax.broadcasted_iota(jnp.int32, shape3, 1)
    st0 = jnp.broadcast_to(st[:, 0:1, :], shape3)
    st1 = jnp.broadcast_to(st[:, 1:2, :], shape3)
    um1 = jnp.where(tpos == 0, st1, r1)
    um2 = jnp.where(tpos == 0, st0, jnp.where(tpos == 1, st1, r2))
    y = um2 * cw[0:1, :] + um1 * cw[1:2, :] + u3 * cw[2:3, :]
    return y.reshape(nseq * tlen, width), u3[:, tlen - 2:tlen, :]


def _mix_sample_kernel(x_ref, sh_ref, sc_ref, g_ref, n1g_ref, win_ref, cw_ref, wco_ref,
                       sinks_ref, wao_ref, wmo_ref, st_ref, ck_ref, cv_ref,
                       x1_ref, conv_ref, k_ref, v_ref):
    nseq, tlen, _ = x_ref.shape
    rows = nseq * tlen

    assert rows == LANES, "one step's new keys must fill exactly one lane tile"

    x3 = x_ref[...]
    h3 = _rms(x3, n1g_ref[...]) * (1.0 + sc_ref[...]) + sh_ref[...]
    hf = h3.reshape(rows, D_MODEL)
    h_now = [hf.astype(_BF)]

    def proj(lo, width):
        return _dot(h_now[0], win_ref[:, lo:lo + width])

    def order_after(tile_f32):
        bits = pltpu.bitcast(tile_f32[0:8, :], jnp.uint32)
        zero = ((bits >> 16) >> 16).astype(_F32)
        h_now[0] = (hf + jnp.tile(zero, (rows // 8, D_MODEL // LANES))).astype(_BF)

    side = {}

    def job_conv_in():
        u = proj(_OFF_C, D_MODEL) * proj(_OFF_X, D_MODEL)
        side["uc"], new_st = _conv_seq(u, st_ref[...], cw_ref[...], nseq, tlen)
        conv_ref[...] = new_st

    def job_gate_b():
        side["pb"] = proj(_OFF_B, D_MODEL)

    def job_conv_out():
        side["ya"] = _dot((side.pop("pb") * side.pop("uc")).astype(_BF), wco_ref[...])

    def job_ga():
        side["ga"] = jax.nn.sigmoid(proj(_OFF_GA, D_MODEL))

    def job_gb():
        side["gb"] = jax.nn.sigmoid(proj(_OFF_GB, D_MODEL))

    keep = WINDOW - tlen
    qf = proj(_OFF_Q, D_MODEL) * Q_SCALE
    kvt_new = proj(_OFF_K, 2 * KV_WIDTH).T
    keep_old = lax.broadcasted_iota(jnp.int32, (KV_WIDTH, WINDOW), 1) < keep

    def window_update(s):
        shift = (keep - s * tlen) % LANES
        for cache_ref, out_ref, base in ((ck_ref, k_ref, 0), (cv_ref, v_ref, KV_WIDTH)):
            rolled = pltpu.roll(cache_ref[s], keep, 1)
            new = pltpu.roll(kvt_new[base:base + KV_WIDTH, :], shift, 1)
            win = jnp.where(keep_old, rolled, new)
            out_ref[s] = win
        return win

    lo = lax.broadcasted_iota(jnp.int32, (rows, LANES), 1) < HEAD_DIM
    zero = jnp.zeros((rows, LANES), _F32)
    n_tiles = KV_WIDTH // LANES
    heads_per_tile = N_HEADS // n_tiles
    m_rows = heads_per_tile * tlen
    row = lax.broadcasted_iota(jnp.int32, (m_rows, WINDOW), 0)
    col = lax.broadcasted_iota(jnp.int32, (m_rows, WINDOW), 1)
    dist_c = WINDOW + row % tlen - col
    valid_c = dist_c <= WINDOW
    shape3 = (nseq, m_rows, LANES)
    seq3 = lax.broadcasted_iota(jnp.int32, shape3, 0)
    row3 = lax.broadcasted_iota(jnp.int32, shape3, 1)
    col3 = lax.broadcasted_iota(jnp.int32, shape3, 2)
    dist_n = row3 % tlen - col3 % tlen
    valid_n = (col3 // tlen == seq3) & (dist_n >= 0)
    row1 = lax.broadcasted_iota(jnp.int32, (m_rows, 1), 0)

    def attention(kc):
        parts = []
        slope = jnp.zeros((m_rows, 1), _F32)
        sink = jnp.zeros((m_rows, 1), _F32)
        for j in range(heads_per_tile):
            head = heads_per_tile * kc + j
            c, par, half = head // HEADS_PER_COL, j % HEADS_PER_COL, j // (heads_per_tile // 2)
            qc = qf[:, c * LANES:(c + 1) * LANES]
            src = qc if par == half else pltpu.roll(qc, HEAD_DIM, 1)
            part = jnp.where(lo, src, zero) if half == 0 else jnp.where(lo, zero, src)
            parts.append(part.reshape(nseq, tlen, LANES))
            mine = (row1 >= j * tlen) & (row1 < (j + 1) * tlen)
            slope = jnp.where(mine, _slope(head), slope)
            sink = jnp.where(mine, sinks_ref[head] * LOG2E, sink)
        lhs = jnp.concatenate(parts, axis=1).astype(_BF)
        lhs2 = lhs.reshape(nseq * m_rows, LANES)
        rsl = slice(kc * LANES, (kc + 1) * LANES)
        k_new = kvt_new[rsl, :].astype(_BF)
        v_new = kvt_new[KV_WIDTH + kc * LANES:KV_WIDTH + (kc + 1) * LANES, :].astype(_BF)
        s_c = jnp.einsum('smk,skn->smn', lhs, ck_ref[:, rsl, :].astype(_BF),
                         preferred_element_type=_F32)
        s_n = _dot(lhs2, k_new).reshape(shape3)
        sc = jnp.concatenate(
            [s_c + jnp.where(valid_c, -(slope * dist_c.astype(_F32)), NEG_INF),
             s_n + jnp.where(valid_n, -(slope * dist_n.astype(_F32)), NEG_INF)], axis=2)
        p, inv = _softmax_parts(sc, sink)
        p = p.astype(_BF)
        r_c = jnp.einsum('smk,snk->smn', p[:, :, 0:WINDOW], cv_ref[:, rsl, :].astype(_BF),
                         preferred_element_type=_F32)
        r_n = lax.dot_general(p[:, :, WINDOW:].reshape(nseq * m_rows, LANES), v_new, _NT,
                              preferred_element_type=_F32).reshape(shape3)
        r = (r_c + r_n) * inv
        cols = {}
        for m2 in range(heads_per_tile // HEADS_PER_COL):
            c = (heads_per_tile // HEADS_PER_COL) * kc + m2
            half = m2 // (heads_per_tile // (2 * HEADS_PER_COL))
            r0 = r[:, (2 * m2) * tlen:(2 * m2 + 1) * tlen, :].reshape(rows, LANES)
            r1 = r[:, (2 * m2 + 1) * tlen:(2 * m2 + 2) * tlen, :].reshape(rows, LANES)
            if half == 0:
                cols[c] = jnp.where(lo, r0, pltpu.roll(r1, HEAD_DIM, 1))
            else:
                cols[c] = jnp.where(lo, pltpu.roll(r0, HEAD_DIM, 1), r1)
        return cols

    o_cols = {}
    job_conv_in()
    window_jobs = [job_gate_b, job_conv_out, job_ga, job_gb]
    group = -(-nseq // len(window_jobs))
    for s in range(nseq):
        if s == ATTN_AFTER_WINDOWS:
            for kc in range(n_tiles):
                o_cols.update(attention(kc))
        last_win = window_update(s)
        if window_jobs and (s + 1) % group == 0:
            window_jobs.pop(0)()
            order_after(last_win)
    while window_jobs:
        window_jobs.pop(0)()
    o = jnp.concatenate([o_cols[c] for c in range(N_QCOLS)], axis=1).astype(_BF)
    yb = _dot(o, wao_ref[...])

    mixed = (side["ga"] * side["ya"] + side["gb"] * yb).astype(_BF)
    y = _dot(mixed, wmo_ref[...]).reshape(nseq, tlen, D_MODEL)
    x1_ref[...] = x3 + g_ref[...] * y


def _mix_sample_call(x, mod3, n1g, win, cw, wco, sinks, wao, wmo, st, ck, cv, chunk):
    nseq, tlen, _ = x.shape
    assert LANES % tlen == 0 and chunk % (LANES // tlen) == 0

    def mod_spec(col):
        return pl.BlockSpec((chunk, 1, D_MODEL), lambda s: (s, 0, col))

    def seq_spec(d1, d2):
        return pl.BlockSpec((chunk, d1, d2), lambda s: (s, 0, 0))

    return pl.pallas_call(
        _mix_sample_kernel,
        grid=(nseq // chunk,),
        in_specs=[seq_spec(tlen, D_MODEL),
                  mod_spec(0), mod_spec(1), mod_spec(2),
                  _const_spec((1, D_MODEL)),
                  _const_spec((D_MODEL, IN_WIDTH)),
                  _const_spec((3, D_MODEL)),
                  _const_spec((D_MODEL, D_MODEL)),
                  pl.BlockSpec(memory_space=pltpu.SMEM),
                  _const_spec((D_MODEL, D_MODEL)),
                  _const_spec((D_MODEL, D_MODEL)),
                  seq_spec(2, D_MODEL),
                  seq_spec(KV_WIDTH, WINDOW),
                  seq_spec(KV_WIDTH, WINDOW)],
        out_specs=[seq_spec(tlen, D_MODEL),
                   seq_spec(2, D_MODEL),
                   seq_spec(KV_WIDTH, WINDOW),
                   seq_spec(KV_WIDTH, WINDOW)],
        out_shape=[jax.ShapeDtypeStruct((nseq, tlen, D_MODEL), _F32),
                   jax.ShapeDtypeStruct((nseq, 2, D_MODEL), _F32),
                   jax.ShapeDtypeStruct((nseq, KV_WIDTH, WINDOW), _F32),
                   jax.ShapeDtypeStruct((nseq, KV_WIDTH, WINDOW), _F32)],
        compiler_params=pltpu.CompilerParams(
            dimension_semantics=("arbitrary",),
            vmem_limit_bytes=VMEM_LIMIT),
        name="mix_sample",
    )(x, mod3, mod3, mod3, n1g, win, cw, wco, sinks, wao, wmo, st, ck, cv)


def _ffn_tail(x1, ac, val, g, wdown_ref, fg):
    hmid = (_gelu(ac) * val).astype(_BF)
    x2 = x1 + g * _dot(hmid, wdown_ref[...])
    return _rms(x2, fg)


def _ffn_prompt_kernel(x1_ref, sh_ref, sc_ref, g_ref, n2g_ref, wup_ref, fcw_ref, wdown_ref,
                       fg_ref, y_ref, ffn_ref, a_ext, *, parts):
    i = pl.program_id(1)
    t = x1_ref.shape[0]
    rows = t // parts

    @pl.when(i == 0)
    def _():
        a_ext[0:CARRY_ROWS, :] = jnp.zeros((CARRY_ROWS, D_FF), _F32)

    fcw = fcw_ref[...]

    def up(part):
        r0 = part * rows
        e0 = CARRY_ROWS + r0
        h2 = _mod_norm(x1_ref[r0:r0 + rows, :], n2g_ref[...], sc_ref[...], sh_ref[...])
        a = _dot(h2, wup_ref[:, 0:D_FF])
        val = _dot(h2, wup_ref[:, D_FF:2 * D_FF])
        a_ext[e0:e0 + rows, :] = a
        ac = (a_ext[e0 - 2:e0 - 2 + rows, :] * fcw[0:1, :]
              + a_ext[e0 - 1:e0 - 1 + rows, :] * fcw[1:2, :]
              + a * fcw[2:3, :])
        return ac, val

    def down(part, ac, val):
        r0 = part * rows
        y_ref[r0:r0 + rows, :] = _ffn_tail(x1_ref[r0:r0 + rows, :], ac, val, g_ref[...],
                                           wdown_ref, fg_ref[...])

    for part in range(parts):
        down(part, *up(part))
    ffn_ref[...] = a_ext[CARRY_ROWS + t - 2:CARRY_ROWS + t, :]
    a_ext[0:CARRY_ROWS, :] = a_ext[t:t + CARRY_ROWS, :]


def _ffn_prompt_call(x1, mod3, mod_row0, n2g, wup, fcw, wdown, fg, tile, parts):
    bsz, seq, _ = x1.shape

    def mod_spec(chunk):
        return pl.BlockSpec((None, 1, D_MODEL), lambda b, i: (mod_row0 + b, 0, chunk))

    x_spec = pl.BlockSpec((None, tile, D_MODEL), lambda b, i: (b, i, 0))
    return pl.pallas_call(
        functools.partial(_ffn_prompt_kernel, parts=parts),
        grid=(bsz, seq // tile),
        in_specs=[x_spec, mod_spec(3), mod_spec(4), mod_spec(5),
                  _const_spec((1, D_MODEL)),
                  _const_spec((D_MODEL, 2 * D_FF)),
                  _const_spec((3, D_FF)),
                  _const_spec((D_FF, D_MODEL)),
                  _const_spec((1, D_MODEL))],
        out_specs=[x_spec,
                   pl.BlockSpec((None, 2, D_FF), lambda b, i: (b, 0, 0))],
        out_shape=[jax.ShapeDtypeStruct((bsz, seq, D_MODEL), _F32),
                   jax.ShapeDtypeStruct((bsz, 2, D_FF), _F32)],
        scratch_shapes=[pltpu.VMEM((CARRY_ROWS + tile, D_FF), _F32)],
        compiler_params=pltpu.CompilerParams(
            dimension_semantics=("arbitrary", "arbitrary"),
            vmem_limit_bytes=VMEM_LIMIT),
        name="ffn_prompt",
    )(x1, mod3, mod3, mod3, n2g, wup, fcw, wdown, fg)


def _ffn_sample_kernel(x1_ref, sh_ref, sc_ref, g_ref, n2g_ref, wup_ref, fcw_ref, wdown_ref,
                       fg_ref, st_ref, y_ref, ffn_ref):
    nseq, tlen, _ = x1_ref.shape
    rows = nseq * tlen
    x3 = x1_ref[...]
    h3 = _rms(x3, n2g_ref[...]) * (1.0 + sc_ref[...]) + sh_ref[...]
    h2 = h3.reshape(rows, D_MODEL).astype(_BF)
    a = _dot(h2, wup_ref[:, 0:D_FF])
    val = _dot(h2, wup_ref[:, D_FF:2 * D_FF])
    ac, new_st = _conv_seq(a, st_ref[...], fcw_ref[...], nseq, tlen)
    ffn_ref[...] = new_st
    hmid = (_gelu(ac) * val).astype(_BF)
    y = _dot(hmid, wdown_ref[...]).reshape(nseq, tlen, D_MODEL)
    x2 = x3 + g_ref[...] * y
    y_ref[...] = _rms(x2, fg_ref[...])


def _ffn_sample_call(x1, mod3, n2g, wup, fcw, wdown, fg, st, chunk):
    nseq, tlen, _ = x1.shape

    def mod_spec(col):
        return pl.BlockSpec((chunk, 1, D_MODEL), lambda s: (s, 0, col))

    def seq_spec(d1, d2):
        return pl.BlockSpec((chunk, d1, d2), lambda s: (s, 0, 0))

    return pl.pallas_call(
        _ffn_sample_kernel,
        grid=(nseq // chunk,),
        in_specs=[seq_spec(tlen, D_MODEL),
                  mod_spec(3), mod_spec(4), mod_spec(5),
                  _const_spec((1, D_MODEL)),
                  _const_spec((D_MODEL, 2 * D_FF)),
                  _const_spec((3, D_FF)),
                  _const_spec((D_FF, D_MODEL)),
                  _const_spec((1, D_MODEL)),
                  seq_spec(2, D_FF)],
        out_specs=[seq_spec(tlen, D_MODEL), seq_spec(2, D_FF)],
        out_shape=[jax.ShapeDtypeStruct((nseq, tlen, D_MODEL), _F32),
                   jax.ShapeDtypeStruct((nseq, 2, D_FF), _F32)],
        compiler_params=pltpu.CompilerParams(
            dimension_semantics=("arbitrary",),
            vmem_limit_bytes=VMEM_LIMIT),
        name="ffn_sample",
    )(x1, mod3, mod3, mod3, n2g, wup, fcw, wdown, fg, st)


PROMPT_TILE = 512
FFN_TILE = 1024
FFN_PARTS = 2
SAMPLE_CHUNK = 16
ATTN_AFTER_WINDOWS = 4
FFN_SAMPLE_CHUNK = 64


def kernel(x_prompt, x_sample, c_prompt, c_sample, state_conv, cache_k_win, cache_v_win,
           state_ffn_conv, norm1_g, norm2_g, w_ada, b_ada, w_in, conv_w, w_conv_out,
           attn_sinks, w_attn_out, w_mix_out, w_up, ffn_conv_w, w_down, final_g):
    depth = w_in.shape[0]
    assert depth == 1, "the final norm is fused into the (single) layer's FFN call"
    nsamp = x_sample.shape[0]
    bsz = x_prompt.shape[0]
    xp, xs = x_prompt, x_sample
    fg = final_g.reshape(1, D_MODEL)
    outs = [[] for _ in range(8)]
    for layer in range(depth):
        mod3 = _mod_call(c_sample, c_prompt, w_ada[layer], b_ada[layer].reshape(1, -1))
        n1g = norm1_g[layer].reshape(1, D_MODEL)
        n2g = norm2_g[layer].reshape(1, D_MODEL)
        win = w_in[layer].astype(_BF)
        wco = w_conv_out[layer].astype(_BF)
        wao = w_attn_out[layer].astype(_BF)
        wmo = w_mix_out[layer].astype(_BF)
        cw = conv_w[layer]
        fcw = ffn_conv_w[layer]
        sinks = attn_sinks[layer]
        fg_l = fg
        def to_dim_major(win_state):
            return jnp.transpose(win_state, (0, 2, 3, 1)).reshape(-1, KV_WIDTH, WINDOW)

        def to_pos_major(t3):
            return jnp.transpose(t3.reshape(-1, N_KV_HEADS, HEAD_DIM, WINDOW), (0, 3, 1, 2))

        x1p, conv_p, k_p, v_p, wup, wdown = _mix_prompt_call(
            xp, mod3, nsamp, n1g, win, cw, wco, sinks, wao, wmo,
            (w_up[layer], w_down[layer]), PROMPT_TILE)
        x1s, conv_s, k_s, v_s = _mix_sample_call(
            xs, mod3, n1g, win, cw, wco, sinks, wao, wmo, state_conv[layer],
            to_dim_major(cache_k_win[layer]), to_dim_major(cache_v_win[layer]), SAMPLE_CHUNK)
        xp, ffn_p = _ffn_prompt_call(x1p, mod3, nsamp, n2g, wup, fcw, wdown, fg_l,
                                     FFN_TILE, FFN_PARTS)
        xs, ffn_s = _ffn_sample_call(x1s, mod3, n2g, wup, fcw, wdown, fg_l,
                                     state_ffn_conv[layer], FFN_SAMPLE_CHUNK)
        for lst, val in zip(outs, (conv_p, to_pos_major(k_p), to_pos_major(v_p), ffn_p,
                                   conv_s, to_pos_major(k_s), to_pos_major(v_s), ffn_s)):
            lst.append(val)
    return (xp, xs) + tuple(jnp.stack(lst, axis=0) for lst in outs)
```

```python
import functools

import jax
import jax.numpy as jnp
from jax import lax
from jax.experimental import pallas as pl
from jax.experimental.pallas import tpu as pltpu

D_MODEL = 1024
HEAD_DIM = 64
N_HEADS = 16
N_KV_HEADS = 4
WINDOW = 128
D_FF = 2816
KV_WIDTH = N_KV_HEADS * HEAD_DIM
IN_WIDTH = 3 * D_MODEL + D_MODEL + 2 * KV_WIDTH + 2 * D_MODEL
RMS_EPS = 1e-6
NEG_INF = -1e30
LANES = 128
HEADS_PER_COL = LANES // HEAD_DIM
N_QCOLS = N_HEADS * HEAD_DIM // LANES
CARRY_ROWS = 8
VMEM_LIMIT = 60 * 1024 * 1024

_OFF_B, _OFF_C, _OFF_X = 0, D_MODEL, 2 * D_MODEL
_OFF_Q = 3 * D_MODEL
_OFF_K = _OFF_Q + D_MODEL
_OFF_V = _OFF_K + KV_WIDTH
_OFF_GA = _OFF_V + KV_WIDTH
_OFF_GB = _OFF_GA + D_MODEL

_BF = jnp.bfloat16
_F32 = jnp.float32
_NT = (((1,), (1,)), ((), ()))


LOG2E = 1.4426950408889634
Q_SCALE = HEAD_DIM ** -0.5 * LOG2E


def _slope(head):
    return 2.0 ** (-8.0 * (head + 1) / N_HEADS) * LOG2E


def _rms(x, g):
    return x * lax.rsqrt(jnp.mean(x * x, axis=-1, keepdims=True) + RMS_EPS) * g


def _dot(a, b):
    return jnp.dot(a, b, preferred_element_type=_F32)


def _gelu(x):
    return 0.5 * x * (1.0 + lax.erf(x * (0.5 ** 0.5)))


def _split_heads(t2d, lo):
    rolled = pltpu.roll(t2d, HEAD_DIM, 1)
    zero = jnp.zeros_like(t2d)
    return (jnp.where(lo, t2d, zero), jnp.where(lo, zero, rolled),
            jnp.where(lo, rolled, zero), jnp.where(lo, zero, t2d))


def _softmax_parts(s, sink):
    m = jnp.maximum(jnp.max(s, axis=-1, keepdims=True), sink)
    p = jnp.exp2(s - m)
    denom = jnp.sum(p, axis=-1, keepdims=True) + jnp.exp2(sink - m)
    return p, 1.0 / denom


N_MOD = 6
MOD_COLS = 768


def _mod_kernel(cs_ref, cp_ref, w_ref, b_ref, *rest):
    n_cast = (len(rest) - 1) // 2
    cast_in, o_ref, cast_out = rest[:n_cast], rest[n_cast], rest[n_cast + 1:]
    for src, dst in zip(cast_in, cast_out):
        dst[...] = src[...].astype(_BF)
    w = w_ref[...].astype(_BF)
    n0 = cs_ref.shape[0]
    for c_ref, r0 in ((cs_ref, 0), (cp_ref, n0)):
        c = c_ref[...]
        a = (c * jax.nn.sigmoid(c)).astype(_BF)
        res = _dot(a, w) + b_ref[...]
        n = c.shape[0]
        o_ref[r0:r0 + n] = res.reshape(n, 1, res.shape[-1])


def _mod_call(c_sample, c_prompt, w_ada, b_ada, cast_along):
    ns, npr = c_sample.shape[0], c_prompt.shape[0]
    n = ns + npr
    steps = N_MOD * D_MODEL // MOD_COLS
    cast_specs = [_cast_stream_spec(w.shape, steps) for w in cast_along]
    return pl.pallas_call(
        _mod_kernel,
        grid=(steps,),
        in_specs=[pl.BlockSpec((ns, D_MODEL), lambda j: (0, 0)),
                  pl.BlockSpec((npr, D_MODEL), lambda j: (0, 0)),
                  pl.BlockSpec((D_MODEL, MOD_COLS), lambda j: (0, j)),
                  pl.BlockSpec((1, MOD_COLS), lambda j: (0, j))] + cast_specs,
        out_specs=[pl.BlockSpec((n, 1, MOD_COLS), lambda j: (0, 0, j))] + cast_specs,
        out_shape=[jax.ShapeDtypeStruct((n, 1, N_MOD * D_MODEL), _F32)]
        + [jax.ShapeDtypeStruct(w.shape, _BF) for w in cast_along],
        compiler_params=pltpu.CompilerParams(vmem_limit_bytes=VMEM_LIMIT),
        name="adaln_mod",
    )(c_sample, c_prompt, w_ada, b_ada, *cast_along)


def _mod_norm(x, g, sc, sh):
    return (_rms(x, g) * (1.0 + sc) + sh).astype(_BF)


def _mix_prompt_kernel(x_ref, sh_ref, sc_ref, g_ref, xn_ref, shn_ref, scn_ref, n1g_ref,
                       win_ref, cw_ref, wco_ref, sinks_ref, wao_ref, wmo_ref, *rest,
                       tiles_per_seq, n_cast):
    cast_in, rest = rest[:n_cast], rest[n_cast:]
    (x1_ref, conv_ref, k_ref, v_ref), rest = rest[:4], rest[4:]
    cast_out, (u_ext, k_ext, v_ext, o_scr, bias_scr, h_scr) = rest[:n_cast], rest[n_cast:]
    step = pl.program_id(0)
    i = step % tiles_per_seq
    t = x_ref.shape[0]
    nblk = t // WINDOW

    for src, dst in zip(cast_in, cast_out):
        dst[...] = src[...].astype(_BF)

    @pl.when(step == 0)
    def _():
        h_scr[...] = _mod_norm(x_ref[...], n1g_ref[...], sc_ref[...], sh_ref[...])
        row = lax.broadcasted_iota(jnp.int32, (WINDOW, 2 * WINDOW), 0)
        col = lax.broadcasted_iota(jnp.int32, (WINDOW, 2 * WINDOW), 1)
        dist = WINDOW + row - col
        valid = (dist >= 0) & (dist <= WINDOW)
        distf = dist.astype(_F32)
        for head in range(N_HEADS):
            bias_scr[head] = jnp.where(valid, -(_slope(head) * distf), NEG_INF)

    @pl.when(i == 0)
    def _():
        u_ext[0:CARRY_ROWS, :] = jnp.zeros((CARRY_ROWS, D_MODEL), _F32)
        k_ext[:, 0:WINDOW, :] = jnp.zeros((2 * N_KV_HEADS, WINDOW, LANES), _BF)
        v_ext[:, 0:WINDOW, :] = jnp.zeros((2 * N_KV_HEADS, WINDOW, LANES), _BF)

    def proj(lo, width):
        return _dot(h_scr[...], win_ref[:, lo:lo + width])

    def next_tile_norm():
        h_scr[...] = _mod_norm(xn_ref[...], n1g_ref[...], scn_ref[...], shn_ref[...])

    side = {}

    def conv_in():
        u = proj(_OFF_C, D_MODEL) * proj(_OFF_X, D_MODEL)
        u_ext[CARRY_ROWS:CARRY_ROWS + t, :] = u
        cw = cw_ref[...]
        side["uc"] = (u_ext[CARRY_ROWS - 2:CARRY_ROWS - 2 + t, :] * cw[0:1, :]
                      + u_ext[CARRY_ROWS - 1:CARRY_ROWS - 1 + t, :] * cw[1:2, :]
                      + u * cw[2:3, :])
        conv_ref[...] = u[t - 2:t, :]
        u_ext[0:CARRY_ROWS, :] = u[t - CARRY_ROWS:t, :]

    def conv_out():
        pb = proj(_OFF_B, D_MODEL)
        side["ya"] = _dot((pb * side.pop("uc")).astype(_BF), wco_ref[...])

    def gate_a():
        side["ga"] = jax.nn.sigmoid(proj(_OFF_GA, D_MODEL))

    def gate_b():
        side["gb"] = jax.nn.sigmoid(proj(_OFF_GB, D_MODEL))
        next_tile_norm()

    side_jobs = [conv_in, conv_out, gate_a, gate_b]

    q = (proj(_OFF_Q, D_MODEL) * Q_SCALE).astype(_BF)
    kf = proj(_OFF_K, KV_WIDTH)
    vf = proj(_OFF_V, KV_WIDTH)
    k_ref[...] = kf[t - WINDOW:t, :].T
    v_ref[...] = vf[t - WINDOW:t, :].T
    lane_t = lax.broadcasted_iota(jnp.int32, (t, LANES), 1)
    lo = lane_t < HEAD_DIM
    ones_lane = (HEAD_DIM, 0)
    for kc in range(KV_WIDTH // LANES):
        ks = _split_heads(kf[:, kc * LANES:(kc + 1) * LANES], lo)
        vs = _split_heads(vf[:, kc * LANES:(kc + 1) * LANES], lo)
        for j in range(4):
            k_ext[4 * kc + j, WINDOW:WINDOW + t, :] = ks[j].astype(_BF)
            v_one = jnp.where(lane_t == ones_lane[j % HEADS_PER_COL], 1.0, vs[j])
            v_ext[4 * kc + j, WINDOW:WINDOW + t, :] = v_one.astype(_BF)

    first_cols = lax.broadcasted_iota(jnp.int32, (WINDOW, 2 * WINDOW), 1) < WINDOW
    no_prev = first_cols & (i == 0)
    lo_q = lax.broadcasted_iota(jnp.int32, (WINDOW, LANES), 1) < HEAD_DIM
    for blk in range(nblk):
        r0 = blk * WINDOW
        for c in range(N_QCOLS):
            hkv = c // 2
            qc = q[r0:r0 + WINDOW, c * LANES:(c + 1) * LANES]
            o_pair = []
            for par in range(HEADS_PER_COL):
                head = HEADS_PER_COL * c + par
                kk = k_ext[2 * hkv + par, r0:r0 + 2 * WINDOW, :]
                s = lax.dot_general(qc, kk, _NT, preferred_element_type=_F32) + bias_scr[head]
                if blk == 0:
                    s = jnp.where(no_prev, NEG_INF, s)
                sink = sinks_ref[head] * LOG2E
                m = jnp.maximum(jnp.max(s, axis=-1, keepdims=True), sink)
                p = jnp.exp2(s - m).astype(_BF)
                o_h = _dot(p, v_ext[2 * hkv + par, r0:r0 + 2 * WINDOW, :])
                col = ones_lane[par]
                denom = o_h[:, col:col + 1] + jnp.exp2(sink - m)
                o_pair.append(o_h * (1.0 / denom))
            o_c = jnp.where(lo_q, o_pair[0], o_pair[1])
            o_scr[r0:r0 + WINDOW, c * LANES:(c + 1) * LANES] = o_c.astype(_BF)
        for _ in range(-(-len(side_jobs) // (nblk - blk))):
            side_jobs.pop(0)()
    k_ext[:, 0:WINDOW, :] = k_ext[:, t:t + WINDOW, :]
    v_ext[:, 0:WINDOW, :] = v_ext[:, t:t + WINDOW, :]
    yb = _dot(o_scr[...], wao_ref[...])

    mixed = (side["ga"] * side["ya"] + side["gb"] * yb).astype(_BF)
    x1_ref[...] = x_ref[...] + g_ref[...] * _dot(mixed, wmo_ref[...])


def _const_spec(shape):
    nd = len(shape)
    return pl.BlockSpec(shape, lambda *_: (0,) * nd, pipeline_mode=pl.Buffered(1))


def _tile_specs(tile, nt, steps, mod_row0):
    def nxt(s):
        return jnp.minimum(s + 1, steps - 1)

    x_spec = pl.BlockSpec((None, tile, D_MODEL), lambda s: (s // nt, s % nt, 0))
    xn_spec = pl.BlockSpec((None, tile, D_MODEL), lambda s: (nxt(s) // nt, nxt(s) % nt, 0))

    def mod_spec(chunk):
        return pl.BlockSpec((None, 1, D_MODEL), lambda s: (mod_row0 + s // nt, 0, chunk))

    def modn_spec(chunk):
        return pl.BlockSpec((None, 1, D_MODEL), lambda s: (mod_row0 + nxt(s) // nt, 0, chunk))

    return x_spec, xn_spec, mod_spec, modn_spec


BF16_SUBLANES = 16


def _cast_stream_spec(shape, steps):
    rows, cols = shape
    for hold in range(1, steps + 1):
        nblocks, rem = divmod(steps, hold)
        if rem == 0 and rows % nblocks == 0 and (rows // nblocks) % BF16_SUBLANES == 0:
            return pl.BlockSpec((rows // nblocks, cols), lambda s: (s // hold, 0))
    raise ValueError(f"no bf16-aligned row split of {shape} over {steps} steps")


def _mix_prompt_call(x, mod3, mod_row0, n1g, win, cw, wco, sinks, wao, wmo, cast_along, tile):
    bsz, seq, _ = x.shape
    nt = seq // tile
    steps = bsz * nt
    x_spec, xn_spec, mod_spec, modn_spec = _tile_specs(tile, nt, steps, mod_row0)
    cast_specs = [_cast_stream_spec(w.shape, steps) for w in cast_along]

    def state_spec(d1, d2):
        return pl.BlockSpec((None, d1, d2), lambda s: (s // nt, 0, 0))

    return pl.pallas_call(
        functools.partial(_mix_prompt_kernel, tiles_per_seq=nt, n_cast=len(cast_along)),
        grid=(steps,),
        in_specs=[x_spec, mod_spec(0), mod_spec(1), mod_spec(2),
                  xn_spec, modn_spec(0), modn_spec(1),
                  _const_spec((1, D_MODEL)),
                  _const_spec((D_MODEL, IN_WIDTH)),
                  _const_spec((3, D_MODEL)),
                  _const_spec((D_MODEL, D_MODEL)),
                  pl.BlockSpec(memory_space=pltpu.SMEM),
                  _const_spec((D_MODEL, D_MODEL)),
                  _const_spec((D_MODEL, D_MODEL))] + cast_specs,
        out_specs=[x_spec,
                   state_spec(2, D_MODEL),
                   state_spec(KV_WIDTH, WINDOW),
                   state_spec(KV_WIDTH, WINDOW)] + cast_specs,
        out_shape=[jax.ShapeDtypeStruct((bsz, seq, D_MODEL), _F32),
                   jax.ShapeDtypeStruct((bsz, 2, D_MODEL), _F32),
                   jax.ShapeDtypeStruct((bsz, KV_WIDTH, WINDOW), _F32),
                   jax.ShapeDtypeStruct((bsz, KV_WIDTH, WINDOW), _F32)]
        + [jax.ShapeDtypeStruct(w.shape, _BF) for w in cast_along],
        scratch_shapes=[pltpu.VMEM((CARRY_ROWS + tile, D_MODEL), _F32),
                        pltpu.VMEM((2 * N_KV_HEADS, WINDOW + tile, LANES), _BF),
                        pltpu.VMEM((2 * N_KV_HEADS, WINDOW + tile, LANES), _BF),
                        pltpu.VMEM((tile, D_MODEL), _BF),
                        pltpu.VMEM((N_HEADS, WINDOW, 2 * WINDOW), _F32),
                        pltpu.VMEM((tile, D_MODEL), _BF)],
        compiler_params=pltpu.CompilerParams(
            dimension_semantics=("arbitrary",),
            vmem_limit_bytes=VMEM_LIMIT),
        name="mix_prompt",
    )(x, mod3, mod3, mod3, x, mod3, mod3, n1g, win, cw, wco, sinks, wao, wmo, *cast_along)


def _conv_seq(u2d, st, cw, nseq, tlen):
    width = u2d.shape[-1]
    shape3 = (nseq, tlen, width)
    u3 = u2d.reshape(shape3)
    r1 = pltpu.roll(u2d, 1, 0).reshape(shape3)
    r2 = pltpu.roll(u2d, 2, 0).reshape(shape3)
    tpos = lax.broadcasted_iota(jnp.int32, shape3, 1)
    st0 = jnp.broadcast_to(st[:, 0:1, :], shape3)
    st1 = jnp.broadcast_to(st[:, 1:2, :], shape3)
    um1 = jnp.where(tpos == 0, st1, r1)
    um2 = jnp.where(tpos == 0, st0, jnp.where(tpos == 1, st1, r2))
    y = um2 * cw[0:1, :] + um1 * cw[1:2, :] + u3 * cw[2:3, :]
    return y.reshape(nseq * tlen, width), u3[:, tlen - 2:tlen, :]


def _mix_sample_kernel(x_ref, sh_ref, sc_ref, g_ref, n1g_ref, win_ref, cw_ref, wco_ref,
                       sinks_ref, wao_ref, wmo_ref, st_ref, ck_ref, cv_ref,
                       x1_ref, conv_ref, k_ref, v_ref):
    nseq, tlen, _ = x_ref.shape
    rows = nseq * tlen

    assert rows == LANES, "one step's new keys must fill exactly one lane tile"

    x3 = x_ref[...]
    h3 = _rms(x3, n1g_ref[...]) * (1.0 + sc_ref[...]) + sh_ref[...]
    hf = h3.reshape(rows, D_MODEL)
    h_now = [hf.astype(_BF)]

    def proj(lo, width):
        return _dot(h_now[0], win_ref[:, lo:lo + width])

    def order_after(tile_f32):
        bits = pltpu.bitcast(tile_f32[0:8, :], jnp.uint32)
        zero = ((bits >> 16) >> 16).astype(_F32)
        h_now[0] = (hf + jnp.tile(zero, (rows // 8, D_MODEL // LANES))).astype(_BF)

    side = {}

    def job_conv_in():
        u = proj(_OFF_C, D_MODEL) * proj(_OFF_X, D_MODEL)
        side["uc"], new_st = _conv_seq(u, st_ref[...], cw_ref[...], nseq, tlen)
        conv_ref[...] = new_st

    def job_gate_b():
        side["pb"] = proj(_OFF_B, D_MODEL)

    def job_conv_out():
        side["ya"] = _dot((side.pop("pb") * side.pop("uc")).astype(_BF), wco_ref[...])

    def job_ga():
        side["ga"] = jax.nn.sigmoid(proj(_OFF_GA, D_MODEL))

    def job_gb():
        side["gb"] = jax.nn.sigmoid(proj(_OFF_GB, D_MODEL))

    keep = WINDOW - tlen
    qf = proj(_OFF_Q, D_MODEL) * Q_SCALE
    kvt_new = proj(_OFF_K, 2 * KV_WIDTH).T
    keep_old = lax.broadcasted_iota(jnp.int32, (KV_WIDTH, WINDOW), 1) < keep

    def window_update(s):
        shift = (keep - s * tlen) % LANES
        for cache_ref, out_ref, base in ((ck_ref, k_ref, 0), (cv_ref, v_ref, KV_WIDTH)):
            rolled = pltpu.roll(cache_ref[s], keep, 1)
            new = pltpu.roll(kvt_new[base:base + KV_WIDTH, :], shift, 1)
            win = jnp.where(keep_old, rolled, new)
            out_ref[s] = win
        return win

    lo = lax.broadcasted_iota(jnp.int32, (rows, LANES), 1) < HEAD_DIM
    zero = jnp.zeros((rows, LANES), _F32)
    n_tiles = KV_WIDTH // LANES
    heads_per_tile = N_HEADS // n_tiles
    m_rows = heads_per_tile * tlen
    row = lax.broadcasted_iota(jnp.int32, (m_rows, WINDOW), 0)
    col = lax.broadcasted_iota(jnp.int32, (m_rows, WINDOW), 1)
    dist_c = WINDOW + row % tlen - col
    valid_c = dist_c <= WINDOW
    shape3 = (nseq, m_rows, LANES)
    seq3 = lax.broadcasted_iota(jnp.int32, shape3, 0)
    row3 = lax.broadcasted_iota(jnp.int32, shape3, 1)
    col3 = lax.broadcasted_iota(jnp.int32, shape3, 2)
    dist_n = row3 % tlen - col3 % tlen
    valid_n = (col3 // tlen == seq3) & (dist_n >= 0)
    row1 = lax.broadcasted_iota(jnp.int32, (m_rows, 1), 0)

    def attention(kc):
        parts = []
        slope = jnp.zeros((m_rows, 1), _F32)
        sink = jnp.zeros((m_rows, 1), _F32)
        for j in range(heads_per_tile):
            head = heads_per_tile * kc + j
            c, par, half = head // HEADS_PER_COL, j % HEADS_PER_COL, j // (heads_per_tile // 2)
            qc = qf[:, c * LANES:(c + 1) * LANES]
            src = qc if par == half else pltpu.roll(qc, HEAD_DIM, 1)
            part = jnp.where(lo, src, zero) if half == 0 else jnp.where(lo, zero, src)
            parts.append(part.reshape(nseq, tlen, LANES))
            mine = (row1 >= j * tlen) & (row1 < (j + 1) * tlen)
            slope = jnp.where(mine, _slope(head), slope)
            sink = jnp.where(mine, sinks_ref[head] * LOG2E, sink)
        lhs = jnp.concatenate(parts, axis=1).astype(_BF)
        lhs2 = lhs.reshape(nseq * m_rows, LANES)
        rsl = slice(kc * LANES, (kc + 1) * LANES)
        k_new = kvt_new[rsl, :].astype(_BF)
        v_new = kvt_new[KV_WIDTH + kc * LANES:KV_WIDTH + (kc + 1) * LANES, :].astype(_BF)
        s_c = jnp.einsum('smk,skn->smn', lhs, ck_ref[:, rsl, :].astype(_BF),
                         preferred_element_type=_F32)
        s_n = _dot(lhs2, k_new).reshape(shape3)
        sc = jnp.concatenate(
            [s_c + jnp.where(valid_c, -(slope * dist_c.astype(_F32)), NEG_INF),
             s_n + jnp.where(valid_n, -(slope * dist_n.astype(_F32)), NEG_INF)], axis=2)
        p, inv = _softmax_parts(sc, sink)
        p = p.astype(_BF)
        r_c = jnp.einsum('smk,snk->smn', p[:, :, 0:WINDOW], cv_ref[:, rsl, :].astype(_BF),
                         preferred_element_type=_F32)
        r_n = lax.dot_general(p[:, :, WINDOW:].reshape(nseq * m_rows, LANES), v_new, _NT,
                              preferred_element_type=_F32).reshape(shape3)
        r = (r_c + r_n) * inv
        cols = {}
        for m2 in range(heads_per_tile // HEADS_PER_COL):
            c = (heads_per_tile // HEADS_PER_COL) * kc + m2
            half = m2 // (heads_per_tile // (2 * HEADS_PER_COL))
            r0 = r[:, (2 * m2) * tlen:(2 * m2 + 1) * tlen, :].reshape(rows, LANES)
            r1 = r[:, (2 * m2 + 1) * tlen:(2 * m2 + 2) * tlen, :].reshape(rows, LANES)
            if half == 0:
                cols[c] = jnp.where(lo, r0, pltpu.roll(r1, HEAD_DIM, 1))
            else:
                cols[c] = jnp.where(lo, pltpu.roll(r0, HEAD_DIM, 1), r1)
        return cols

    o_cols = {}
    job_conv_in()
    window_jobs = [job_gate_b, job_conv_out, job_ga, job_gb]
    group = -(-nseq // len(window_jobs))
    for s in range(nseq):
        if s == ATTN_AFTER_WINDOWS:
            for kc in range(n_tiles):
                o_cols.update(attention(kc))
        last_win = window_update(s)
        if window_jobs and (s + 1) % group == 0:
            window_jobs.pop(0)()
            order_after(last_win)
    while window_jobs:
        window_jobs.pop(0)()
    o = jnp.concatenate([o_cols[c] for c in range(N_QCOLS)], axis=1).astype(_BF)
    yb = _dot(o, wao_ref[...])

    mixed = (side["ga"] * side["ya"] + side["gb"] * yb).astype(_BF)
    y = _dot(mixed, wmo_ref[...]).reshape(nseq, tlen, D_MODEL)
    x1_ref[...] = x3 + g_ref[...] * y


def _mix_sample_call(x, mod3, n1g, win, cw, wco, sinks, wao, wmo, st, ck, cv, chunk):
    nseq, tlen, _ = x.shape
    assert LANES % tlen == 0 and chunk % (LANES // tlen) == 0

    def mod_spec(col):
        return pl.BlockSpec((chunk, 1, D_MODEL), lambda s: (s, 0, col))

    def seq_spec(d1, d2):
        return pl.BlockSpec((chunk, d1, d2), lambda s: (s, 0, 0))

    return pl.pallas_call(
        _mix_sample_kernel,
        grid=(nseq // chunk,),
        in_specs=[seq_spec(tlen, D_MODEL),
                  mod_spec(0), mod_spec(1), mod_spec(2),
                  _const_spec((1, D_MODEL)),
                  _const_spec((D_MODEL, IN_WIDTH)),
                  _const_spec((3, D_MODEL)),
                  _const_spec((D_MODEL, D_MODEL)),
                  pl.BlockSpec(memory_space=pltpu.SMEM),
                  _const_spec((D_MODEL, D_MODEL)),
                  _const_spec((D_MODEL, D_MODEL)),
                  seq_spec(2, D_MODEL),
                  seq_spec(KV_WIDTH, WINDOW),
                  seq_spec(KV_WIDTH, WINDOW)],
        out_specs=[seq_spec(tlen, D_MODEL),
                   seq_spec(2, D_MODEL),
                   seq_spec(KV_WIDTH, WINDOW),
                   seq_spec(KV_WIDTH, WINDOW)],
        out_shape=[jax.ShapeDtypeStruct((nseq, tlen, D_MODEL), _F32),
                   jax.ShapeDtypeStruct((nseq, 2, D_MODEL), _F32),
                   jax.ShapeDtypeStruct((nseq, KV_WIDTH, WINDOW), _F32),
                   jax.ShapeDtypeStruct((nseq, KV_WIDTH, WINDOW), _F32)],
        compiler_params=pltpu.CompilerParams(
            dimension_semantics=("arbitrary",),
            vmem_limit_bytes=VMEM_LIMIT),
        name="mix_sample",
    )(x, mod3, mod3, mod3, n1g, win, cw, wco, sinks, wao, wmo, st, ck, cv)


def _ffn_tail(x1, ac, val, g, wdown_ref, fg):
    hmid = (_gelu(ac) * val).astype(_BF)
    x2 = x1 + g * _dot(hmid, wdown_ref[...])
    return _rms(x2, fg)


def _ffn_prompt_kernel(x1_ref, sh_ref, sc_ref, g_ref, n2g_ref, wup_ref, fcw_ref, wdown_ref,
                       fg_ref, y_ref, ffn_ref, a_ext, *, parts):
    i = pl.program_id(1)
    t = x1_ref.shape[0]
    rows = t // parts

    @pl.when(i == 0)
    def _():
        a_ext[0:CARRY_ROWS, :] = jnp.zeros((CARRY_ROWS, D_FF), _F32)

    fcw = fcw_ref[...]

    def up(part):
        r0 = part * rows
        e0 = CARRY_ROWS + r0
        h2 = _mod_norm(x1_ref[r0:r0 + rows, :], n2g_ref[...], sc_ref[...], sh_ref[...])
        a = _dot(h2, wup_ref[:, 0:D_FF])
        val = _dot(h2, wup_ref[:, D_FF:2 * D_FF])
        a_ext[e0:e0 + rows, :] = a
        ac = (a_ext[e0 - 2:e0 - 2 + rows, :] * fcw[0:1, :]
              + a_ext[e0 - 1:e0 - 1 + rows, :] * fcw[1:2, :]
              + a * fcw[2:3, :])
        return ac, val

    def down(part, ac, val):
        r0 = part * rows
        y_ref[r0:r0 + rows, :] = _ffn_tail(x1_ref[r0:r0 + rows, :], ac, val, g_ref[...],
                                           wdown_ref, fg_ref[...])

    for part in range(parts):
        down(part, *up(part))
    ffn_ref[...] = a_ext[CARRY_ROWS + t - 2:CARRY_ROWS + t, :]
    a_ext[0:CARRY_ROWS, :] = a_ext[t:t + CARRY_ROWS, :]


def _ffn_prompt_call(x1, mod3, mod_row0, n2g, wup, fcw, wdown, fg, tile, parts):
    bsz, seq, _ = x1.shape

    def mod_spec(chunk):
        return pl.BlockSpec((None, 1, D_MODEL), lambda b, i: (mod_row0 + b, 0, chunk))

    x_spec = pl.BlockSpec((None, tile, D_MODEL), lambda b, i: (b, i, 0))
    return pl.pallas_call(
        functools.partial(_ffn_prompt_kernel, parts=parts),
        grid=(bsz, seq // tile),
        in_specs=[x_spec, mod_spec(3), mod_spec(4), mod_spec(5),
                  _const_spec((1, D_MODEL)),
                  _const_spec((D_MODEL, 2 * D_FF)),
                  _const_spec((3, D_FF)),
                  _const_spec((D_FF, D_MODEL)),
                  _const_spec((1, D_MODEL))],
        out_specs=[x_spec,
                   pl.BlockSpec((None, 2, D_FF), lambda b, i: (b, 0, 0))],
        out_shape=[jax.ShapeDtypeStruct((bsz, seq, D_MODEL), _F32),
                   jax.ShapeDtypeStruct((bsz, 2, D_FF), _F32)],
        scratch_shapes=[pltpu.VMEM((CARRY_ROWS + tile, D_FF), _F32)],
        compiler_params=pltpu.CompilerParams(
            dimension_semantics=("arbitrary", "arbitrary"),
            vmem_limit_bytes=VMEM_LIMIT),
        name="ffn_prompt",
    )(x1, mod3, mod3, mod3, n2g, wup, fcw, wdown, fg)


def _ffn_sample_kernel(x1_ref, sh_ref, sc_ref, g_ref, n2g_ref, wup_ref, fcw_ref, wdown_ref,
                       fg_ref, st_ref, y_ref, ffn_ref):
    nseq, tlen, _ = x1_ref.shape
    rows = nseq * tlen
    x3 = x1_ref[...]
    h3 = _rms(x3, n2g_ref[...]) * (1.0 + sc_ref[...]) + sh_ref[...]
    h2 = h3.reshape(rows, D_MODEL).astype(_BF)
    a = _dot(h2, wup_ref[:, 0:D_FF])
    val = _dot(h2, wup_ref[:, D_FF:2 * D_FF])
    ac, new_st = _conv_seq(a, st_ref[...], fcw_ref[...], nseq, tlen)
    ffn_ref[...] = new_st
    hmid = (_gelu(ac) * val).astype(_BF)
    y = _dot(hmid, wdown_ref[...]).reshape(nseq, tlen, D_MODEL)
    x2 = x3 + g_ref[...] * y
    y_ref[...] = _rms(x2, fg_ref[...])


def _ffn_sample_call(x1, mod3, n2g, wup, fcw, wdown, fg, st, chunk):
    nseq, tlen, _ = x1.shape

    def mod_spec(col):
        return pl.BlockSpec((chunk, 1, D_MODEL), lambda s: (s, 0, col))

    def seq_spec(d1, d2):
        return pl.BlockSpec((chunk, d1, d2), lambda s: (s, 0, 0))

    return pl.pallas_call(
        _ffn_sample_kernel,
        grid=(nseq // chunk,),
        in_specs=[seq_spec(tlen, D_MODEL),
                  mod_spec(3), mod_spec(4), mod_spec(5),
                  _const_spec((1, D_MODEL)),
                  _const_spec((D_MODEL, 2 * D_FF)),
                  _const_spec((3, D_FF)),
                  _const_spec((D_FF, D_MODEL)),
                  _const_spec((1, D_MODEL)),
                  seq_spec(2, D_FF)],
        out_specs=[seq_spec(tlen, D_MODEL), seq_spec(2, D_FF)],
        out_shape=[jax.ShapeDtypeStruct((nseq, tlen, D_MODEL), _F32),
                   jax.ShapeDtypeStruct((nseq, 2, D_FF), _F32)],
        compiler_params=pltpu.CompilerParams(
            dimension_semantics=("arbitrary",),
            vmem_limit_bytes=VMEM_LIMIT),
        name="ffn_sample",
    )(x1, mod3, mod3, mod3, n2g, wup, fcw, wdown, fg, st)


PROMPT_TILE = 512
FFN_TILE = 1024
FFN_PARTS = 2
SAMPLE_CHUNK = 16
ATTN_AFTER_WINDOWS = 4
FFN_SAMPLE_CHUNK = 64


def kernel(x_prompt, x_sample, c_prompt, c_sample, state_conv, cache_k_win, cache_v_win,
           state_ffn_conv, norm1_g, norm2_g, w_ada, b_ada, w_in, conv_w, w_conv_out,
           attn_sinks, w_attn_out, w_mix_out, w_up, ffn_conv_w, w_down, final_g):
    depth = w_in.shape[0]
    assert depth == 1, "the final norm is fused into the (single) layer's FFN call"
    nsamp = x_sample.shape[0]
    bsz = x_prompt.shape[0]
    xp, xs = x_prompt, x_sample
    fg = final_g.reshape(1, D_MODEL)
    outs = [[] for _ in range(8)]
    for layer in range(depth):
        mod3, win, wco, wao, wmo = _mod_call(
            c_sample, c_prompt, w_ada[layer], b_ada[layer].reshape(1, -1),
            (w_in[layer], w_conv_out[layer], w_attn_out[layer], w_mix_out[layer]))
        n1g = norm1_g[layer].reshape(1, D_MODEL)
        n2g = norm2_g[layer].reshape(1, D_MODEL)
        cw = conv_w[layer]
        fcw = ffn_conv_w[layer]
        sinks = attn_sinks[layer]
        fg_l = fg
        def to_dim_major(win_state):
            return jnp.transpose(win_state, (0, 2, 3, 1)).reshape(-1, KV_WIDTH, WINDOW)

        def to_pos_major(t3):
            return jnp.transpose(t3.reshape(-1, N_KV_HEADS, HEAD_DIM, WINDOW), (0, 3, 1, 2))

        x1p, conv_p, k_p, v_p, wup, wdown = _mix_prompt_call(
            xp, mod3, nsamp, n1g, win, cw, wco, sinks, wao, wmo,
            (w_up[layer], w_down[layer]), PROMPT_TILE)
        x1s, conv_s, k_s, v_s = _mix_sample_call(
            xs, mod3, n1g, win, cw, wco, sinks, wao, wmo, state_conv[layer],
            to_dim_major(cache_k_win[layer]), to_dim_major(cache_v_win[layer]), SAMPLE_CHUNK)
        xp, ffn_p = _ffn_prompt_call(x1p, mod3, nsamp, n2g, wup, fcw, wdown, fg_l,
                                     FFN_TILE, FFN_PARTS)
        xs, ffn_s = _ffn_sample_call(x1s, mod3, n2g, wup, fcw, wdown, fg_l,
                                     state_ffn_conv[layer], FFN_SAMPLE_CHUNK)
        for lst, val in zip(outs, (conv_p, to_pos_major(k_p), to_pos_major(v_p), ffn_p,
                                   conv_s, to_pos_major(k_s), to_pos_major(v_s), ffn_s)):
            lst.append(val)
    return (xp, xs) + tuple(jnp.stack(lst, axis=0) for lst in outs)
```

```python
import functools

import jax
import jax.numpy as jnp
from jax import lax
from jax.experimental import pallas as pl
from jax.experimental.pallas import tpu as pltpu

D_MODEL = 1024
HEAD_DIM = 64
N_HEADS = 16
N_KV_HEADS = 4
WINDOW = 128
D_FF = 2816
KV_WIDTH = N_KV_HEADS * HEAD_DIM
IN_WIDTH = 3 * D_MODEL + D_MODEL + 2 * KV_WIDTH + 2 * D_MODEL
RMS_EPS = 1e-6
NEG_INF = -1e30
LANES = 128
HEADS_PER_COL = LANES // HEAD_DIM
N_QCOLS = N_HEADS * HEAD_DIM // LANES
CARRY_ROWS = 8
VMEM_LIMIT = 60 * 1024 * 1024

_OFF_B, _OFF_C, _OFF_X = 0, D_MODEL, 2 * D_MODEL
_OFF_Q = 3 * D_MODEL
_OFF_K = _OFF_Q + D_MODEL
_OFF_V = _OFF_K + KV_WIDTH
_OFF_GA = _OFF_V + KV_WIDTH
_OFF_GB = _OFF_GA + D_MODEL

_BF = jnp.bfloat16
_F32 = jnp.float32
_NT = (((1,), (1,)), ((), ()))


LOG2E = 1.4426950408889634
Q_SCALE = HEAD_DIM ** -0.5 * LOG2E


def _slope(head):
    return 2.0 ** (-8.0 * (head + 1) / N_HEADS) * LOG2E


def _rms(x, g):
    return x * lax.rsqrt(jnp.mean(x * x, axis=-1, keepdims=True) + RMS_EPS) * g


def _dot(a, b):
    return jnp.dot(a, b, preferred_element_type=_F32)


def _gelu(x):
    return 0.5 * x * (1.0 + lax.erf(x * (0.5 ** 0.5)))


def _split_heads(t2d, lo):
    rolled = pltpu.roll(t2d, HEAD_DIM, 1)
    zero = jnp.zeros_like(t2d)
    return (jnp.where(lo, t2d, zero), jnp.where(lo, zero, rolled),
            jnp.where(lo, rolled, zero), jnp.where(lo, zero, t2d))


def _softmax_parts(s, sink):
    m = jnp.maximum(jnp.max(s, axis=-1, keepdims=True), sink)
    p = jnp.exp2(s - m)
    denom = jnp.sum(p, axis=-1, keepdims=True) + jnp.exp2(sink - m)
    return p, 1.0 / denom


N_MOD = 6
MOD_COLS = 1536


def _mod_kernel(cs_ref, cp_ref, w_ref, b_ref, *rest):
    n_cast = (len(rest) - 1) // 2
    cast_in, o_ref, cast_out = rest[:n_cast], rest[n_cast], rest[n_cast + 1:]
    for src, dst in zip(cast_in, cast_out):
        dst[...] = src[...].astype(_BF)
    w = w_ref[...].astype(_BF)
    n0 = cs_ref.shape[0]
    for c_ref, r0 in ((cs_ref, 0), (cp_ref, n0)):
        c = c_ref[...]
        a = (c * jax.nn.sigmoid(c)).astype(_BF)
        res = _dot(a, w) + b_ref[...]
        n = c.shape[0]
        o_ref[r0:r0 + n] = res.reshape(n, 1, res.shape[-1])


def _mod_call(c_sample, c_prompt, w_ada, b_ada, cast_along):
    ns, npr = c_sample.shape[0], c_prompt.shape[0]
    n = ns + npr
    steps = N_MOD * D_MODEL // MOD_COLS
    cast_specs = [_cast_stream_spec(w.shape, steps) for w in cast_along]
    return pl.pallas_call(
        _mod_kernel,
        grid=(steps,),
        in_specs=[pl.BlockSpec((ns, D_MODEL), lambda j: (0, 0)),
                  pl.BlockSpec((npr, D_MODEL), lambda j: (0, 0)),
                  pl.BlockSpec((D_MODEL, MOD_COLS), lambda j: (0, j)),
                  pl.BlockSpec((1, MOD_COLS), lambda j: (0, j))] + cast_specs,
        out_specs=[pl.BlockSpec((n, 1, MOD_COLS), lambda j: (0, 0, j))] + cast_specs,
        out_shape=[jax.ShapeDtypeStruct((n, 1, N_MOD * D_MODEL), _F32)]
        + [jax.ShapeDtypeStruct(w.shape, _BF) for w in cast_along],
        compiler_params=pltpu.CompilerParams(vmem_limit_bytes=VMEM_LIMIT),
        name="adaln_mod",
    )(c_sample, c_prompt, w_ada, b_ada, *cast_along)


def _mod_norm(x, g, sc, sh):
    return (_rms(x, g) * (1.0 + sc) + sh).astype(_BF)


def _mix_prompt_kernel(x_ref, sh_ref, sc_ref, g_ref, xn_ref, shn_ref, scn_ref, n1g_ref,
                       win_ref, cw_ref, wco_ref, sinks_ref, wao_ref, wmo_ref, *rest,
                       tiles_per_seq, n_cast):
    cast_in, rest = rest[:n_cast], rest[n_cast:]
    (x1_ref, conv_ref, k_ref, v_ref), rest = rest[:4], rest[4:]
    cast_out, (u_ext, k_ext, v_ext, o_scr, bias_scr, h_scr) = rest[:n_cast], rest[n_cast:]
    step = pl.program_id(0)
    i = step % tiles_per_seq
    t = x_ref.shape[0]
    nblk = t // WINDOW

    for src, dst in zip(cast_in, cast_out):
        dst[...] = src[...].astype(_BF)

    @pl.when(step == 0)
    def _():
        h_scr[...] = _mod_norm(x_ref[...], n1g_ref[...], sc_ref[...], sh_ref[...])
        row = lax.broadcasted_iota(jnp.int32, (WINDOW, 2 * WINDOW), 0)
        col = lax.broadcasted_iota(jnp.int32, (WINDOW, 2 * WINDOW), 1)
        dist = WINDOW + row - col
        valid = (dist >= 0) & (dist <= WINDOW)
        distf = dist.astype(_F32)
        for head in range(N_HEADS):
            bias_scr[head] = jnp.where(valid, -(_slope(head) * distf), NEG_INF)

    @pl.when(i == 0)
    def _():
        u_ext[0:CARRY_ROWS, :] = jnp.zeros((CARRY_ROWS, D_MODEL), _F32)
        k_ext[:, 0:WINDOW, :] = jnp.zeros((2 * N_KV_HEADS, WINDOW, LANES), _BF)
        v_ext[:, 0:WINDOW, :] = jnp.zeros((2 * N_KV_HEADS, WINDOW, LANES), _BF)

    def proj(lo, width):
        return _dot(h_scr[...], win_ref[:, lo:lo + width])

    def next_tile_norm():
        h_scr[...] = _mod_norm(xn_ref[...], n1g_ref[...], scn_ref[...], shn_ref[...])

    side = {}

    def conv_in():
        u = proj(_OFF_C, D_MODEL) * proj(_OFF_X, D_MODEL)
        u_ext[CARRY_ROWS:CARRY_ROWS + t, :] = u
        cw = cw_ref[...]
        side["uc"] = (u_ext[CARRY_ROWS - 2:CARRY_ROWS - 2 + t, :] * cw[0:1, :]
                      + u_ext[CARRY_ROWS - 1:CARRY_ROWS - 1 + t, :] * cw[1:2, :]
                      + u * cw[2:3, :])
        conv_ref[...] = u[t - 2:t, :]
        u_ext[0:CARRY_ROWS, :] = u[t - CARRY_ROWS:t, :]

    def conv_out():
        pb = proj(_OFF_B, D_MODEL)
        side["ya"] = _dot((pb * side.pop("uc")).astype(_BF), wco_ref[...])

    def gate_a():
        side["ga"] = jax.nn.sigmoid(proj(_OFF_GA, D_MODEL))

    def gate_b():
        side["gb"] = jax.nn.sigmoid(proj(_OFF_GB, D_MODEL))
        next_tile_norm()

    side_jobs = [conv_in, conv_out, gate_a, gate_b]

    q = (proj(_OFF_Q, D_MODEL) * Q_SCALE).astype(_BF)
    kf = proj(_OFF_K, KV_WIDTH)
    vf = proj(_OFF_V, KV_WIDTH)
    k_ref[...] = kf[t - WINDOW:t, :].T
    v_ref[...] = vf[t - WINDOW:t, :].T
    lane_t = lax.broadcasted_iota(jnp.int32, (t, LANES), 1)
    lo = lane_t < HEAD_DIM
    ones_lane = (HEAD_DIM, 0)
    for kc in range(KV_WIDTH // LANES):
        ks = _split_heads(kf[:, kc * LANES:(kc + 1) * LANES], lo)
        vs = _split_heads(vf[:, kc * LANES:(kc + 1) * LANES], lo)
        for j in range(4):
            k_ext[4 * kc + j, WINDOW:WINDOW + t, :] = ks[j].astype(_BF)
            v_one = jnp.where(lane_t == ones_lane[j % HEADS_PER_COL], 1.0, vs[j])
            v_ext[4 * kc + j, WINDOW:WINDOW + t, :] = v_one.astype(_BF)

    first_cols = lax.broadcasted_iota(jnp.int32, (WINDOW, 2 * WINDOW), 1) < WINDOW
    no_prev = first_cols & (i == 0)
    lo_q = lax.broadcasted_iota(jnp.int32, (WINDOW, LANES), 1) < HEAD_DIM
    for blk in range(nblk):
        r0 = blk * WINDOW
        for c in range(N_QCOLS):
            hkv = c // 2
            qc = q[r0:r0 + WINDOW, c * LANES:(c + 1) * LANES]
            o_pair = []
            for par in range(HEADS_PER_COL):
                head = HEADS_PER_COL * c + par
                kk = k_ext[2 * hkv + par, r0:r0 + 2 * WINDOW, :]
                s = lax.dot_general(qc, kk, _NT, preferred_element_type=_F32) + bias_scr[head]
                if blk == 0:
                    s = jnp.where(no_prev, NEG_INF, s)
                sink = sinks_ref[head] * LOG2E
                m = jnp.maximum(jnp.max(s, axis=-1, keepdims=True), sink)
                p = jnp.exp2(s - m).astype(_BF)
                o_h = _dot(p, v_ext[2 * hkv + par, r0:r0 + 2 * WINDOW, :])
                col = ones_lane[par]
                denom = o_h[:, col:col + 1] + jnp.exp2(sink - m)
                o_pair.append(o_h * (1.0 / denom))
            o_c = jnp.where(lo_q, o_pair[0], o_pair[1])
            o_scr[r0:r0 + WINDOW, c * LANES:(c + 1) * LANES] = o_c.astype(_BF)
        for _ in range(-(-len(side_jobs) // (nblk - blk))):
            side_jobs.pop(0)()
    k_ext[:, 0:WINDOW, :] = k_ext[:, t:t + WINDOW, :]
    v_ext[:, 0:WINDOW, :] = v_ext[:, t:t + WINDOW, :]
    yb = _dot(o_scr[...], wao_ref[...])

    mixed = (side["ga"] * side["ya"] + side["gb"] * yb).astype(_BF)
    x1_ref[...] = x_ref[...] + g_ref[...] * _dot(mixed, wmo_ref[...])


def _const_spec(shape):
    nd = len(shape)
    return pl.BlockSpec(shape, lambda *_: (0,) * nd, pipeline_mode=pl.Buffered(1))


def _tile_specs(tile, nt, steps, mod_row0):
    def nxt(s):
        return jnp.minimum(s + 1, steps - 1)

    x_spec = pl.BlockSpec((None, tile, D_MODEL), lambda s: (s // nt, s % nt, 0))
    xn_spec = pl.BlockSpec((None, tile, D_MODEL), lambda s: (nxt(s) // nt, nxt(s) % nt, 0))

    def mod_spec(chunk):
        return pl.BlockSpec((None, 1, D_MODEL), lambda s: (mod_row0 + s // nt, 0, chunk))

    def modn_spec(chunk):
        return pl.BlockSpec((None, 1, D_MODEL), lambda s: (mod_row0 + nxt(s) // nt, 0, chunk))

    return x_spec, xn_spec, mod_spec, modn_spec


BF16_SUBLANES = 16


def _cast_stream_spec(shape, steps):
    rows, cols = shape
    for hold in range(1, steps + 1):
        nblocks, rem = divmod(steps, hold)
        if rem == 0 and rows % nblocks == 0 and (rows // nblocks) % BF16_SUBLANES == 0:
            return pl.BlockSpec((rows // nblocks, cols), lambda s: (s // hold, 0))
    raise ValueError(f"no bf16-aligned row split of {shape} over {steps} steps")


def _mix_prompt_call(x, mod3, mod_row0, n1g, win, cw, wco, sinks, wao, wmo, cast_along, tile):
    bsz, seq, _ = x.shape
    nt = seq // tile
    steps = bsz * nt
    x_spec, xn_spec, mod_spec, modn_spec = _tile_specs(tile, nt, steps, mod_row0)
    cast_specs = [_cast_stream_spec(w.shape, steps) for w in cast_along]

    def state_spec(d1, d2):
        return pl.BlockSpec((None, d1, d2), lambda s: (s // nt, 0, 0))

    return pl.pallas_call(
        functools.partial(_mix_prompt_kernel, tiles_per_seq=nt, n_cast=len(cast_along)),
        grid=(steps,),
        in_specs=[x_spec, mod_spec(0), mod_spec(1), mod_spec(2),
                  xn_spec, modn_spec(0), modn_spec(1),
                  _const_spec((1, D_MODEL)),
                  _const_spec((D_MODEL, IN_WIDTH)),
                  _const_spec((3, D_MODEL)),
                  _const_spec((D_MODEL, D_MODEL)),
                  pl.BlockSpec(memory_space=pltpu.SMEM),
                  _const_spec((D_MODEL, D_MODEL)),
                  _const_spec((D_MODEL, D_MODEL))] + cast_specs,
        out_specs=[x_spec,
                   state_spec(2, D_MODEL),
                   state_spec(KV_WIDTH, WINDOW),
                   state_spec(KV_WIDTH, WINDOW)] + cast_specs,
        out_shape=[jax.ShapeDtypeStruct((bsz, seq, D_MODEL), _F32),
                   jax.ShapeDtypeStruct((bsz, 2, D_MODEL), _F32),
                   jax.ShapeDtypeStruct((bsz, KV_WIDTH, WINDOW), _F32),
                   jax.ShapeDtypeStruct((bsz, KV_WIDTH, WINDOW), _F32)]
        + [jax.ShapeDtypeStruct(w.shape, _BF) for w in cast_along],
        scratch_shapes=[pltpu.VMEM((CARRY_ROWS + tile, D_MODEL), _F32),
                        pltpu.VMEM((2 * N_KV_HEADS, WINDOW + tile, LANES), _BF),
                        pltpu.VMEM((2 * N_KV_HEADS, WINDOW + tile, LANES), _BF),
                        pltpu.VMEM((tile, D_MODEL), _BF),
                        pltpu.VMEM((N_HEADS, WINDOW, 2 * WINDOW), _F32),
                        pltpu.VMEM((tile, D_MODEL), _BF)],
        compiler_params=pltpu.CompilerParams(
            dimension_semantics=("arbitrary",),
            vmem_limit_bytes=VMEM_LIMIT),
        name="mix_prompt",
    )(x, mod3, mod3, mod3, x, mod3, mod3, n1g, win, cw, wco, sinks, wao, wmo, *cast_along)


def _conv_seq(u2d, st, cw, nseq, tlen):
    width = u2d.shape[-1]
    shape3 = (nseq, tlen, width)
    u3 = u2d.reshape(shape3)
    r1 = pltpu.roll(u2d, 1, 0).reshape(shape3)
    r2 = pltpu.roll(u2d, 2, 0).reshape(shape3)
    tpos = lax.broadcasted_iota(jnp.int32, shape3, 1)
    st0 = jnp.broadcast_to(st[:, 0:1, :], shape3)
    st1 = jnp.broadcast_to(st[:, 1:2, :], shape3)
    um1 = jnp.where(tpos == 0, st1, r1)
    um2 = jnp.where(tpos == 0, st0, jnp.where(tpos == 1, st1, r2))
    y = um2 * cw[0:1, :] + um1 * cw[1:2, :] + u3 * cw[2:3, :]
    return y.reshape(nseq * tlen, width), u3[:, tlen - 2:tlen, :]


def _mix_sample_kernel(x_ref, sh_ref, sc_ref, g_ref, n1g_ref, win_ref, cw_ref, wco_ref,
                       sinks_ref, wao_ref, wmo_ref, st_ref, ck_ref, cv_ref,
                       x1_ref, conv_ref, k_ref, v_ref):
    nseq, tlen, _ = x_ref.shape
    rows = nseq * tlen

    assert rows == LANES, "one step's new keys must fill exactly one lane tile"

    x3 = x_ref[...]
    h3 = _rms(x3, n1g_ref[...]) * (1.0 + sc_ref[...]) + sh_ref[...]
    hf = h3.reshape(rows, D_MODEL)
    h_now = [hf.astype(_BF)]

    def proj(lo, width):
        return _dot(h_now[0], win_ref[:, lo:lo + width])

    def order_after(tile_f32):
        bits = pltpu.bitcast(tile_f32[0:8, :], jnp.uint32)
        zero = ((bits >> 16) >> 16).astype(_F32)
        h_now[0] = (hf + jnp.tile(zero, (rows // 8, D_MODEL // LANES))).astype(_BF)

    side = {}

    def job_conv_in():
        u = proj(_OFF_C, D_MODEL) * proj(_OFF_X, D_MODEL)
        side["uc"], new_st = _conv_seq(u, st_ref[...], cw_ref[...], nseq, tlen)
        conv_ref[...] = new_st

    def job_gate_b():
        side["pb"] = proj(_OFF_B, D_MODEL)

    def job_conv_out():
        side["ya"] = _dot((side.pop("pb") * side.pop("uc")).astype(_BF), wco_ref[...])

    def job_ga():
        side["ga"] = jax.nn.sigmoid(proj(_OFF_GA, D_MODEL))

    def job_gb():
        side["gb"] = jax.nn.sigmoid(proj(_OFF_GB, D_MODEL))

    keep = WINDOW - tlen
    qf = proj(_OFF_Q, D_MODEL) * Q_SCALE
    kvt_new = proj(_OFF_K, 2 * KV_WIDTH).T
    keep_old = lax.broadcasted_iota(jnp.int32, (KV_WIDTH, WINDOW), 1) < keep

    def window_update(s):
        shift = (keep - s * tlen) % LANES
        for cache_ref, out_ref, base in ((ck_ref, k_ref, 0), (cv_ref, v_ref, KV_WIDTH)):
            rolled = pltpu.roll(cache_ref[s], keep, 1)
            new = pltpu.roll(kvt_new[base:base + KV_WIDTH, :], shift, 1)
            win = jnp.where(keep_old, rolled, new)
            out_ref[s] = win
        return win

    lo = lax.broadcasted_iota(jnp.int32, (rows, LANES), 1) < HEAD_DIM
    zero = jnp.zeros((rows, LANES), _F32)
    n_tiles = KV_WIDTH // LANES
    heads_per_tile = N_HEADS // n_tiles
    m_rows = heads_per_tile * tlen
    row = lax.broadcasted_iota(jnp.int32, (m_rows, WINDOW), 0)
    col = lax.broadcasted_iota(jnp.int32, (m_rows, WINDOW), 1)
    dist_c = WINDOW + row % tlen - col
    valid_c = dist_c <= WINDOW
    shape3 = (nseq, m_rows, LANES)
    seq3 = lax.broadcasted_iota(jnp.int32, shape3, 0)
    row3 = lax.broadcasted_iota(jnp.int32, shape3, 1)
    col3 = lax.broadcasted_iota(jnp.int32, shape3, 2)
    dist_n = row3 % tlen - col3 % tlen
    valid_n = (col3 // tlen == seq3) & (dist_n >= 0)
    row1 = lax.broadcasted_iota(jnp.int32, (m_rows, 1), 0)

    def attention(kc):
        parts = []
        slope = jnp.zeros((m_rows, 1), _F32)
        sink = jnp.zeros((m_rows, 1), _F32)
        for j in range(heads_per_tile):
            head = heads_per_tile * kc + j
            c, par, half = head // HEADS_PER_COL, j % HEADS_PER_COL, j // (heads_per_tile // 2)
            qc = qf[:, c * LANES:(c + 1) * LANES]
            src = qc if par == half else pltpu.roll(qc, HEAD_DIM, 1)
            part = jnp.where(lo, src, zero) if half == 0 else jnp.where(lo, zero, src)
            parts.append(part.reshape(nseq, tlen, LANES))
            mine = (row1 >= j * tlen) & (row1 < (j + 1) * tlen)
            slope = jnp.where(mine, _slope(head), slope)
            sink = jnp.where(mine, sinks_ref[head] * LOG2E, sink)
        lhs = jnp.concatenate(parts, axis=1).astype(_BF)
        lhs2 = lhs.reshape(nseq * m_rows, LANES)
        rsl = slice(kc * LANES, (kc + 1) * LANES)
        k_new = kvt_new[rsl, :].astype(_BF)
        v_new = kvt_new[KV_WIDTH + kc * LANES:KV_WIDTH + (kc + 1) * LANES, :].astype(_BF)
        s_c = jnp.einsum('smk,skn->smn', lhs, ck_ref[:, rsl, :].astype(_BF),
                         preferred_element_type=_F32)
        s_n = _dot(lhs2, k_new).reshape(shape3)
        sc = jnp.concatenate(
            [s_c + jnp.where(valid_c, -(slope * dist_c.astype(_F32)), NEG_INF),
             s_n + jnp.where(valid_n, -(slope * dist_n.astype(_F32)), NEG_INF)], axis=2)
        p, inv = _softmax_parts(sc, sink)
        p = p.astype(_BF)
        r_c = jnp.einsum('smk,snk->smn', p[:, :, 0:WINDOW], cv_ref[:, rsl, :].astype(_BF),
                         preferred_element_type=_F32)
        r_n = lax.dot_general(p[:, :, WINDOW:].reshape(nseq * m_rows, LANES), v_new, _NT,
                              preferred_element_type=_F32).reshape(shape3)
        r = (r_c + r_n) * inv
        cols = {}
        for m2 in range(heads_per_tile // HEADS_PER_COL):
            c = (heads_per_tile // HEADS_PER_COL) * kc + m2
            half = m2 // (heads_per_tile // (2 * HEADS_PER_COL))
            r0 = r[:, (2 * m2) * tlen:(2 * m2 + 1) * tlen, :].reshape(rows, LANES)
            r1 = r[:, (2 * m2 + 1) * tlen:(2 * m2 + 2) * tlen, :].reshape(rows, LANES)
            if half == 0:
                cols[c] = jnp.where(lo, r0, pltpu.roll(r1, HEAD_DIM, 1))
            else:
                cols[c] = jnp.where(lo, pltpu.roll(r0, HEAD_DIM, 1), r1)
        return cols

    o_cols = {}
    job_conv_in()
    window_jobs = [job_gate_b, job_conv_out, job_ga, job_gb]
    group = -(-nseq // len(window_jobs))
    for s in range(nseq):
        if s == ATTN_AFTER_WINDOWS:
            for kc in range(n_tiles):
                o_cols.update(attention(kc))
        last_win = window_update(s)
        if window_jobs and (s + 1) % group == 0:
            window_jobs.pop(0)()
            order_after(last_win)
    while window_jobs:
        window_jobs.pop(0)()
    o = jnp.concatenate([o_cols[c] for c in range(N_QCOLS)], axis=1).astype(_BF)
    yb = _dot(o, wao_ref[...])

    mixed = (side["ga"] * side["ya"] + side["gb"] * yb).astype(_BF)
    y = _dot(mixed, wmo_ref[...]).reshape(nseq, tlen, D_MODEL)
    x1_ref[...] = x3 + g_ref[...] * y


def _mix_sample_call(x, mod3, n1g, win, cw, wco, sinks, wao, wmo, st, ck, cv, chunk):
    nseq, tlen, _ = x.shape
    assert LANES % tlen == 0 and chunk % (LANES // tlen) == 0

    def mod_spec(col):
        return pl.BlockSpec((chunk, 1, D_MODEL), lambda s: (s, 0, col))

    def seq_spec(d1, d2):
        return pl.BlockSpec((chunk, d1, d2), lambda s: (s, 0, 0))

    return pl.pallas_call(
        _mix_sample_kernel,
        grid=(nseq // chunk,),
        in_specs=[seq_spec(tlen, D_MODEL),
                  mod_spec(0), mod_spec(1), mod_spec(2),
                  _const_spec((1, D_MODEL)),
                  _const_spec((D_MODEL, IN_WIDTH)),
                  _const_spec((3, D_MODEL)),
                  _const_spec((D_MODEL, D_MODEL)),
                  pl.BlockSpec(memory_space=pltpu.SMEM),
                  _const_spec((D_MODEL, D_MODEL)),
                  _const_spec((D_MODEL, D_MODEL)),
                  seq_spec(2, D_MODEL),
                  seq_spec(KV_WIDTH, WINDOW),
                  seq_spec(KV_WIDTH, WINDOW)],
        out_specs=[seq_spec(tlen, D_MODEL),
                   seq_spec(2, D_MODEL),
                   seq_spec(KV_WIDTH, WINDOW),
                   seq_spec(KV_WIDTH, WINDOW)],
        out_shape=[jax.ShapeDtypeStruct((nseq, tlen, D_MODEL), _F32),
                   jax.ShapeDtypeStruct((nseq, 2, D_MODEL), _F32),
                   jax.ShapeDtypeStruct((nseq, KV_WIDTH, WINDOW), _F32),
                   jax.ShapeDtypeStruct((nseq, KV_WIDTH, WINDOW), _F32)],
        compiler_params=pltpu.CompilerParams(
            dimension_semantics=("arbitrary",),
            vmem_limit_bytes=VMEM_LIMIT),
        name="mix_sample",
    )(x, mod3, mod3, mod3, n1g, win, cw, wco, sinks, wao, wmo, st, ck, cv)


def _ffn_tail(x1, ac, val, g, wdown_ref, fg):
    hmid = (_gelu(ac) * val).astype(_BF)
    x2 = x1 + g * _dot(hmid, wdown_ref[...])
    return _rms(x2, fg)


def _ffn_prompt_kernel(x1_ref, sh_ref, sc_ref, g_ref, n2g_ref, wup_ref, fcw_ref, wdown_ref,
                       fg_ref, y_ref, ffn_ref, a_ext, *, parts):
    i = pl.program_id(1)
    t = x1_ref.shape[0]
    rows = t // parts

    @pl.when(i == 0)
    def _():
        a_ext[0:CARRY_ROWS, :] = jnp.zeros((CARRY_ROWS, D_FF), _F32)

    fcw = fcw_ref[...]

    def up(part):
        r0 = part * rows
        e0 = CARRY_ROWS + r0
        h2 = _mod_norm(x1_ref[r0:r0 + rows, :], n2g_ref[...], sc_ref[...], sh_ref[...])
        a = _dot(h2, wup_ref[:, 0:D_FF])
        val = _dot(h2, wup_ref[:, D_FF:2 * D_FF])
        a_ext[e0:e0 + rows, :] = a
        ac = (a_ext[e0 - 2:e0 - 2 + rows, :] * fcw[0:1, :]
              + a_ext[e0 - 1:e0 - 1 + rows, :] * fcw[1:2, :]
              + a * fcw[2:3, :])
        return ac, val

    def down(part, ac, val):
        r0 = part * rows
        y_ref[r0:r0 + rows, :] = _ffn_tail(x1_ref[r0:r0 + rows, :], ac, val, g_ref[...],
                                           wdown_ref, fg_ref[...])

    for part in range(parts):
        down(part, *up(part))
    ffn_ref[...] = a_ext[CARRY_ROWS + t - 2:CARRY_ROWS + t, :]
    a_ext[0:CARRY_ROWS, :] = a_ext[t:t + CARRY_ROWS, :]


def _ffn_prompt_call(x1, mod3, mod_row0, n2g, wup, fcw, wdown, fg, tile, parts):
    bsz, seq, _ = x1.shape

    def mod_spec(chunk):
        return pl.BlockSpec((None, 1, D_MODEL), lambda b, i: (mod_row0 + b, 0, chunk))

    x_spec = pl.BlockSpec((None, tile, D_MODEL), lambda b, i: (b, i, 0))
    return pl.pallas_call(
        functools.partial(_ffn_prompt_kernel, parts=parts),
        grid=(bsz, seq // tile),
        in_specs=[x_spec, mod_spec(3), mod_spec(4), mod_spec(5),
                  _const_spec((1, D_MODEL)),
                  _const_spec((D_MODEL, 2 * D_FF)),
                  _const_spec((3, D_FF)),
                  _const_spec((D_FF, D_MODEL)),
                  _const_spec((1, D_MODEL))],
        out_specs=[x_spec,
                   pl.BlockSpec((None, 2, D_FF), lambda b, i: (b, 0, 0))],
        out_shape=[jax.ShapeDtypeStruct((bsz, seq, D_MODEL), _F32),
                   jax.ShapeDtypeStruct((bsz, 2, D_FF), _F32)],
        scratch_shapes=[pltpu.VMEM((CARRY_ROWS + tile, D_FF), _F32)],
        compiler_params=pltpu.CompilerParams(
            dimension_semantics=("arbitrary", "arbitrary"),
            vmem_limit_bytes=VMEM_LIMIT),
        name="ffn_prompt",
    )(x1, mod3, mod3, mod3, n2g, wup, fcw, wdown, fg)


def _ffn_sample_kernel(x1_ref, sh_ref, sc_ref, g_ref, n2g_ref, wup_ref, fcw_ref, wdown_ref,
                       fg_ref, st_ref, y_ref, ffn_ref):
    nseq, tlen, _ = x1_ref.shape
    rows = nseq * tlen
    x3 = x1_ref[...]
    h3 = _rms(x3, n2g_ref[...]) * (1.0 + sc_ref[...]) + sh_ref[...]
    h2 = h3.reshape(rows, D_MODEL).astype(_BF)
    a = _dot(h2, wup_ref[:, 0:D_FF])
    val = _dot(h2, wup_ref[:, D_FF:2 * D_FF])
    ac, new_st = _conv_seq(a, st_ref[...], fcw_ref[...], nseq, tlen)
    ffn_ref[...] = new_st
    hmid = (_gelu(ac) * val).astype(_BF)
    y = _dot(hmid, wdown_ref[...]).reshape(nseq, tlen, D_MODEL)
    x2 = x3 + g_ref[...] * y
    y_ref[...] = _rms(x2, fg_ref[...])


def _ffn_sample_call(x1, mod3, n2g, wup, fcw, wdown, fg, st, chunk):
    nseq, tlen, _ = x1.shape

    def mod_spec(col):
        return pl.BlockSpec((chunk, 1, D_MODEL), lambda s: (s, 0, col))

    def seq_spec(d1, d2):
        return pl.BlockSpec((chunk, d1, d2), lambda s: (s, 0, 0))

    return pl.pallas_call(
        _ffn_sample_kernel,
        grid=(nseq // chunk,),
        in_specs=[seq_spec(tlen, D_MODEL),
                  mod_spec(3), mod_spec(4), mod_spec(5),
                  _const_spec((1, D_MODEL)),
                  _const_spec((D_MODEL, 2 * D_FF)),
                  _const_spec((3, D_FF)),
                  _const_spec((D_FF, D_MODEL)),
                  _const_spec((1, D_MODEL)),
                  seq_spec(2, D_FF)],
        out_specs=[seq_spec(tlen, D_MODEL), seq_spec(2, D_FF)],
        out_shape=[jax.ShapeDtypeStruct((nseq, tlen, D_MODEL), _F32),
                   jax.ShapeDtypeStruct((nseq, 2, D_FF), _F32)],
        compiler_params=pltpu.CompilerParams(
            dimension_semantics=("arbitrary",),
            vmem_limit_bytes=VMEM_LIMIT),
        name="ffn_sample",
    )(x1, mod3, mod3, mod3, n2g, wup, fcw, wdown, fg, st)


PROMPT_TILE = 512
FFN_TILE = 1024
FFN_PARTS = 2
SAMPLE_CHUNK = 16
ATTN_AFTER_WINDOWS = 4
FFN_SAMPLE_CHUNK = 64


def kernel(x_prompt, x_sample, c_prompt, c_sample, state_conv, cache_k_win, cache_v_win,
           state_ffn_conv, norm1_g, norm2_g, w_ada, b_ada, w_in, conv_w, w_conv_out,
           attn_sinks, w_attn_out, w_mix_out, w_up, ffn_conv_w, w_down, final_g):
    depth = w_in.shape[0]
    assert depth == 1, "the final norm is fused into the (single) layer's FFN call"
    nsamp = x_sample.shape[0]
    bsz = x_prompt.shape[0]
    xp, xs = x_prompt, x_sample
    fg = final_g.reshape(1, D_MODEL)
    outs = [[] for _ in range(8)]
    for layer in range(depth):
        mod3, win, wco, wao, wmo = _mod_call(
            c_sample, c_prompt, w_ada[layer], b_ada[layer].reshape(1, -1),
            (w_in[layer], w_conv_out[layer], w_attn_out[layer], w_mix_out[layer]))
        n1g = norm1_g[layer].reshape(1, D_MODEL)
        n2g = norm2_g[layer].reshape(1, D_MODEL)
        cw = conv_w[layer]
        fcw = ffn_conv_w[layer]
        sinks = attn_sinks[layer]
        fg_l = fg
        def to_dim_major(win_state):
            return jnp.transpose(win_state, (0, 2, 3, 1)).reshape(-1, KV_WIDTH, WINDOW)

        def to_pos_major(t3):
            return jnp.transpose(t3.reshape(-1, N_KV_HEADS, HEAD_DIM, WINDOW), (0, 3, 1, 2))

        x1p, conv_p, k_p, v_p, wup, wdown = _mix_prompt_call(
            xp, mod3, nsamp, n1g, win, cw, wco, sinks, wao, wmo,
            (w_up[layer], w_down[layer]), PROMPT_TILE)
        x1s, conv_s, k_s, v_s = _mix_sample_call(
            xs, mod3, n1g, win, cw, wco, sinks, wao, wmo, state_conv[layer],
            to_dim_major(cache_k_win[layer]), to_dim_major(cache_v_win[layer]), SAMPLE_CHUNK)
        xp, ffn_p = _ffn_prompt_call(x1p, mod3, nsamp, n2g, wup, fcw, wdown, fg_l,
                                     FFN_TILE, FFN_PARTS)
        xs, ffn_s = _ffn_sample_call(x1s, mod3, n2g, wup, fcw, wdown, fg_l,
                                     state_ffn_conv[layer], FFN_SAMPLE_CHUNK)
        for lst, val in zip(outs, (conv_p, to_pos_major(k_p), to_pos_major(v_p), ffn_p,
                                   conv_s, to_pos_major(k_s), to_pos_major(v_s), ffn_s)):
            lst.append(val)
    return (xp, xs) + tuple(jnp.stack(lst, axis=0) for lst in outs)
```

```python
import functools

import jax
import jax.numpy as jnp
from jax import lax
from jax.experimental import pallas as pl
from jax.experimental.pallas import tpu as pltpu

D_MODEL = 1024
HEAD_DIM = 64
N_HEADS = 16
N_KV_HEADS = 4
WINDOW = 128
D_FF = 2816
KV_WIDTH = N_KV_HEADS * HEAD_DIM
IN_WIDTH = 3 * D_MODEL + D_MODEL + 2 * KV_WIDTH + 2 * D_MODEL
RMS_EPS = 1e-6
NEG_INF = -1e30
LANES = 128
HEADS_PER_COL = LANES // HEAD_DIM
N_QCOLS = N_HEADS * HEAD_DIM // LANES
CARRY_ROWS = 8
VMEM_LIMIT = 60 * 1024 * 1024

_OFF_B, _OFF_C, _OFF_X = 0, D_MODEL, 2 * D_MODEL
_OFF_Q = 3 * D_MODEL
_OFF_K = _OFF_Q + D_MODEL
_OFF_V = _OFF_K + KV_WIDTH
_OFF_GA = _OFF_V + KV_WIDTH
_OFF_GB = _OFF_GA + D_MODEL

_BF = jnp.bfloat16
_F32 = jnp.float32
_NT = (((1,), (1,)), ((), ()))


LOG2E = 1.4426950408889634
Q_SCALE = HEAD_DIM ** -0.5 * LOG2E


def _slope(head):
    return 2.0 ** (-8.0 * (head + 1) / N_HEADS) * LOG2E


def _rms(x, g):
    return x * lax.rsqrt(jnp.mean(x * x, axis=-1, keepdims=True) + RMS_EPS) * g


def _dot(a, b):
    return jnp.dot(a, b, preferred_element_type=_F32)


def _gelu(x):
    return 0.5 * x * (1.0 + lax.erf(x * (0.5 ** 0.5)))


def _split_heads(t2d, lo):
    rolled = pltpu.roll(t2d, HEAD_DIM, 1)
    zero = jnp.zeros_like(t2d)
    return (jnp.where(lo, t2d, zero), jnp.where(lo, zero, rolled),
            jnp.where(lo, rolled, zero), jnp.where(lo, zero, t2d))


def _softmax_parts(s, sink):
    m = jnp.maximum(jnp.max(s, axis=-1, keepdims=True), sink)
    p = jnp.exp2(s - m)
    denom = jnp.sum(p, axis=-1, keepdims=True) + jnp.exp2(sink - m)
    return p, 1.0 / denom


N_MOD = 6
MOD_COLS = 384


def _mod_kernel(cs_ref, cp_ref, w_ref, b_ref, *rest):
    n_cast = (len(rest) - 1) // 2
    cast_in, o_ref, cast_out = rest[:n_cast], rest[n_cast], rest[n_cast + 1:]
    for src, dst in zip(cast_in, cast_out):
        dst[...] = src[...].astype(_BF)
    w = w_ref[...].astype(_BF)
    n0 = cs_ref.shape[0]
    for c_ref, r0 in ((cs_ref, 0), (cp_ref, n0)):
        c = c_ref[...]
        a = (c * jax.nn.sigmoid(c)).astype(_BF)
        res = _dot(a, w) + b_ref[...]
        n = c.shape[0]
        o_ref[r0:r0 + n] = res.reshape(n, 1, res.shape[-1])


def _mod_call(c_sample, c_prompt, w_ada, b_ada, cast_along):
    ns, npr = c_sample.shape[0], c_prompt.shape[0]
    n = ns + npr
    steps = N_MOD * D_MODEL // MOD_COLS
    cast_specs = [_cast_stream_spec(w.shape, steps) for w in cast_along]
    return pl.pallas_call(
        _mod_kernel,
        grid=(steps,),
        in_specs=[pl.BlockSpec((ns, D_MODEL), lambda j: (0, 0)),
                  pl.BlockSpec((npr, D_MODEL), lambda j: (0, 0)),
                  pl.BlockSpec((D_MODEL, MOD_COLS), lambda j: (0, j)),
                  pl.BlockSpec((1, MOD_COLS), lambda j: (0, j))] + cast_specs,
        out_specs=[pl.BlockSpec((n, 1, MOD_COLS), lambda j: (0, 0, j))] + cast_specs,
        out_shape=[jax.ShapeDtypeStruct((n, 1, N_MOD * D_MODEL), _F32)]
        + [jax.ShapeDtypeStruct(w.shape, _BF) for w in cast_along],
        compiler_params=pltpu.CompilerParams(vmem_limit_bytes=VMEM_LIMIT),
        name="adaln_mod",
    )(c_sample, c_prompt, w_ada, b_ada, *cast_along)


def _mod_norm(x, g, sc, sh):
    return (_rms(x, g) * (1.0 + sc) + sh).astype(_BF)


def _mix_prompt_kernel(x_ref, sh_ref, sc_ref, g_ref, xn_ref, shn_ref, scn_ref, n1g_ref,
                       win_ref, cw_ref, wco_ref, sinks_ref, wao_ref, wmo_ref, *rest,
                       tiles_per_seq, n_cast):
    cast_in, rest = rest[:n_cast], rest[n_cast:]
    (x1_ref, conv_ref, k_ref, v_ref), rest = rest[:4], rest[4:]
    cast_out, (u_ext, k_ext, v_ext, o_scr, bias_scr, h_scr) = rest[:n_cast], rest[n_cast:]
    step = pl.program_id(0)
    i = step % tiles_per_seq
    t = x_ref.shape[0]
    nblk = t // WINDOW

    for src, dst in zip(cast_in, cast_out):
        dst[...] = src[...].astype(_BF)

    @pl.when(step == 0)
    def _():
        h_scr[...] = _mod_norm(x_ref[...], n1g_ref[...], sc_ref[...], sh_ref[...])
        row = lax.broadcasted_iota(jnp.int32, (WINDOW, 2 * WINDOW), 0)
        col = lax.broadcasted_iota(jnp.int32, (WINDOW, 2 * WINDOW), 1)
        dist = WINDOW + row - col
        valid = (dist >= 0) & (dist <= WINDOW)
        distf = dist.astype(_F32)
        for head in range(N_HEADS):
            bias_scr[head] = jnp.where(valid, -(_slope(head) * distf), NEG_INF)

    @pl.when(i == 0)
    def _():
        u_ext[0:CARRY_ROWS, :] = jnp.zeros((CARRY_ROWS, D_MODEL), _F32)
        k_ext[:, 0:WINDOW, :] = jnp.zeros((2 * N_KV_HEADS, WINDOW, LANES), _BF)
        v_ext[:, 0:WINDOW, :] = jnp.zeros((2 * N_KV_HEADS, WINDOW, LANES), _BF)

    def proj(lo, width):
        return _dot(h_scr[...], win_ref[:, lo:lo + width])

    def next_tile_norm():
        h_scr[...] = _mod_norm(xn_ref[...], n1g_ref[...], scn_ref[...], shn_ref[...])

    side = {}

    def conv_in():
        u = proj(_OFF_C, D_MODEL) * proj(_OFF_X, D_MODEL)
        u_ext[CARRY_ROWS:CARRY_ROWS + t, :] = u
        cw = cw_ref[...]
        side["uc"] = (u_ext[CARRY_ROWS - 2:CARRY_ROWS - 2 + t, :] * cw[0:1, :]
                      + u_ext[CARRY_ROWS - 1:CARRY_ROWS - 1 + t, :] * cw[1:2, :]
                      + u * cw[2:3, :])
        conv_ref[...] = u[t - 2:t, :]
        u_ext[0:CARRY_ROWS, :] = u[t - CARRY_ROWS:t, :]

    def conv_out():
        pb = proj(_OFF_B, D_MODEL)
        side["ya"] = _dot((pb * side.pop("uc")).astype(_BF), wco_ref[...])

    def gate_a():
        side["ga"] = jax.nn.sigmoid(proj(_OFF_GA, D_MODEL))

    def gate_b():
        side["gb"] = jax.nn.sigmoid(proj(_OFF_GB, D_MODEL))
        next_tile_norm()

    side_jobs = [conv_in, conv_out, gate_a, gate_b]

    q = (proj(_OFF_Q, D_MODEL) * Q_SCALE).astype(_BF)
    kf = proj(_OFF_K, KV_WIDTH)
    vf = proj(_OFF_V, KV_WIDTH)
    k_ref[...] = kf[t - WINDOW:t, :].T
    v_ref[...] = vf[t - WINDOW:t, :].T
    lane_t = lax.broadcasted_iota(jnp.int32, (t, LANES), 1)
    lo = lane_t < HEAD_DIM
    ones_lane = (HEAD_DIM, 0)
    for kc in range(KV_WIDTH // LANES):
        ks = _split_heads(kf[:, kc * LANES:(kc + 1) * LANES], lo)
        vs = _split_heads(vf[:, kc * LANES:(kc + 1) * LANES], lo)
        for j in range(4):
            k_ext[4 * kc + j, WINDOW:WINDOW + t, :] = ks[j].astype(_BF)
            v_one = jnp.where(lane_t == ones_lane[j % HEADS_PER_COL], 1.0, vs[j])
            v_ext[4 * kc + j, WINDOW:WINDOW + t, :] = v_one.astype(_BF)

    first_cols = lax.broadcasted_iota(jnp.int32, (WINDOW, 2 * WINDOW), 1) < WINDOW
    no_prev = first_cols & (i == 0)
    lo_q = lax.broadcasted_iota(jnp.int32, (WINDOW, LANES), 1) < HEAD_DIM
    for blk in range(nblk):
        r0 = blk * WINDOW
        for c in range(N_QCOLS):
            hkv = c // 2
            qc = q[r0:r0 + WINDOW, c * LANES:(c + 1) * LANES]
            o_pair = []
            for par in range(HEADS_PER_COL):
                head = HEADS_PER_COL * c + par
                kk = k_ext[2 * hkv + par, r0:r0 + 2 * WINDOW, :]
                s = lax.dot_general(qc, kk, _NT, preferred_element_type=_F32) + bias_scr[head]
                if blk == 0:
                    s = jnp.where(no_prev, NEG_INF, s)
                sink = sinks_ref[head] * LOG2E
                m = jnp.maximum(jnp.max(s, axis=-1, keepdims=True), sink)
                p = jnp.exp2(s - m).astype(_BF)
                o_h = _dot(p, v_ext[2 * hkv + par, r0:r0 + 2 * WINDOW, :])
                col = ones_lane[par]
                denom = o_h[:, col:col + 1] + jnp.exp2(sink - m)
                o_pair.append(o_h * (1.0 / denom))
            o_c = jnp.where(lo_q, o_pair[0], o_pair[1])
            o_scr[r0:r0 + WINDOW, c * LANES:(c + 1) * LANES] = o_c.astype(_BF)
        for _ in range(-(-len(side_jobs) // (nblk - blk))):
            side_jobs.pop(0)()
    k_ext[:, 0:WINDOW, :] = k_ext[:, t:t + WINDOW, :]
    v_ext[:, 0:WINDOW, :] = v_ext[:, t:t + WINDOW, :]
    yb = _dot(o_scr[...], wao_ref[...])

    mixed = (side["ga"] * side["ya"] + side["gb"] * yb).astype(_BF)
    x1_ref[...] = x_ref[...] + g_ref[...] * _dot(mixed, wmo_ref[...])


def _const_spec(shape):
    nd = len(shape)
    return pl.BlockSpec(shape, lambda *_: (0,) * nd, pipeline_mode=pl.Buffered(1))


def _tile_specs(tile, nt, steps, mod_row0):
    def nxt(s):
        return jnp.minimum(s + 1, steps - 1)

    x_spec = pl.BlockSpec((None, tile, D_MODEL), lambda s: (s // nt, s % nt, 0))
    xn_spec = pl.BlockSpec((None, tile, D_MODEL), lambda s: (nxt(s) // nt, nxt(s) % nt, 0))

    def mod_spec(chunk):
        return pl.BlockSpec((None, 1, D_MODEL), lambda s: (mod_row0 + s // nt, 0, chunk))

    def modn_spec(chunk):
        return pl.BlockSpec((None, 1, D_MODEL), lambda s: (mod_row0 + nxt(s) // nt, 0, chunk))

    return x_spec, xn_spec, mod_spec, modn_spec


BF16_SUBLANES = 16


def _cast_stream_spec(shape, steps):
    rows, cols = shape
    for hold in range(1, steps + 1):
        nblocks, rem = divmod(steps, hold)
        if rem == 0 and rows % nblocks == 0 and (rows // nblocks) % BF16_SUBLANES == 0:
            return pl.BlockSpec((rows // nblocks, cols), lambda s: (s // hold, 0))
    raise ValueError(f"no bf16-aligned row split of {shape} over {steps} steps")


def _mix_prompt_call(x, mod3, mod_row0, n1g, win, cw, wco, sinks, wao, wmo, cast_along, tile):
    bsz, seq, _ = x.shape
    nt = seq // tile
    steps = bsz * nt
    x_spec, xn_spec, mod_spec, modn_spec = _tile_specs(tile, nt, steps, mod_row0)
    cast_specs = [_cast_stream_spec(w.shape, steps) for w in cast_along]

    def state_spec(d1, d2):
        return pl.BlockSpec((None, d1, d2), lambda s: (s // nt, 0, 0))

    return pl.pallas_call(
        functools.partial(_mix_prompt_kernel, tiles_per_seq=nt, n_cast=len(cast_along)),
        grid=(steps,),
        in_specs=[x_spec, mod_spec(0), mod_spec(1), mod_spec(2),
                  xn_spec, modn_spec(0), modn_spec(1),
                  _const_spec((1, D_MODEL)),
                  _const_spec((D_MODEL, IN_WIDTH)),
                  _const_spec((3, D_MODEL)),
                  _const_spec((D_MODEL, D_MODEL)),
                  pl.BlockSpec(memory_space=pltpu.SMEM),
                  _const_spec((D_MODEL, D_MODEL)),
                  _const_spec((D_MODEL, D_MODEL))] + cast_specs,
        out_specs=[x_spec,
                   state_spec(2, D_MODEL),
                   state_spec(KV_WIDTH, WINDOW),
                   state_spec(KV_WIDTH, WINDOW)] + cast_specs,
        out_shape=[jax.ShapeDtypeStruct((bsz, seq, D_MODEL), _F32),
                   jax.ShapeDtypeStruct((bsz, 2, D_MODEL), _F32),
                   jax.ShapeDtypeStruct((bsz, KV_WIDTH, WINDOW), _F32),
                   jax.ShapeDtypeStruct((bsz, KV_WIDTH, WINDOW), _F32)]
        + [jax.ShapeDtypeStruct(w.shape, _BF) for w in cast_along],
        scratch_shapes=[pltpu.VMEM((CARRY_ROWS + tile, D_MODEL), _F32),
                        pltpu.VMEM((2 * N_KV_HEADS, WINDOW + tile, LANES), _BF),
                        pltpu.VMEM((2 * N_KV_HEADS, WINDOW + tile, LANES), _BF),
                        pltpu.VMEM((tile, D_MODEL), _BF),
                        pltpu.VMEM((N_HEADS, WINDOW, 2 * WINDOW), _F32),
                        pltpu.VMEM((tile, D_MODEL), _BF)],
        compiler_params=pltpu.CompilerParams(
            dimension_semantics=("arbitrary",),
            vmem_limit_bytes=VMEM_LIMIT),
        name="mix_prompt",
    )(x, mod3, mod3, mod3, x, mod3, mod3, n1g, win, cw, wco, sinks, wao, wmo, *cast_along)


def _conv_seq(u2d, st, cw, nseq, tlen):
    width = u2d.shape[-1]
    shape3 = (nseq, tlen, width)
    u3 = u2d.reshape(shape3)
    r1 = pltpu.roll(u2d, 1, 0).reshape(shape3)
    r2 = pltpu.roll(u2d, 2, 0).reshape(shape3)
    tpos = lax.broadcasted_iota(jnp.int32, shape3, 1)
    st0 = jnp.broadcast_to(st[:, 0:1, :], shape3)
    st1 = jnp.broadcast_to(st[:, 1:2, :], shape3)
    um1 = jnp.where(tpos == 0, st1, r1)
    um2 = jnp.where(tpos == 0, st0, jnp.where(tpos == 1, st1, r2))
    y = um2 * cw[0:1, :] + um1 * cw[1:2, :] + u3 * cw[2:3, :]
    return y.reshape(nseq * tlen, width), u3[:, tlen - 2:tlen, :]


def _mix_sample_kernel(x_ref, sh_ref, sc_ref, g_ref, n1g_ref, win_ref, cw_ref, wco_ref,
                       sinks_ref, wao_ref, wmo_ref, st_ref, ck_ref, cv_ref,
                       x1_ref, conv_ref, k_ref, v_ref):
    nseq, tlen, _ = x_ref.shape
    rows = nseq * tlen

    assert rows == LANES, "one step's new keys must fill exactly one lane tile"

    x3 = x_ref[...]
    h3 = _rms(x3, n1g_ref[...]) * (1.0 + sc_ref[...]) + sh_ref[...]
    hf = h3.reshape(rows, D_MODEL)
    h_now = [hf.astype(_BF)]

    def proj(lo, width):
        return _dot(h_now[0], win_ref[:, lo:lo + width])

    def order_after(tile_f32):
        bits = pltpu.bitcast(tile_f32[0:8, :], jnp.uint32)
        zero = ((bits >> 16) >> 16).astype(_F32)
        h_now[0] = (hf + jnp.tile(zero, (rows // 8, D_MODEL // LANES))).astype(_BF)

    side = {}

    def job_conv_in():
        u = proj(_OFF_C, D_MODEL) * proj(_OFF_X, D_MODEL)
        side["uc"], new_st = _conv_seq(u, st_ref[...], cw_ref[...], nseq, tlen)
        conv_ref[...] = new_st

    def job_gate_b():
        side["pb"] = proj(_OFF_B, D_MODEL)

    def job_conv_out():
        side["ya"] = _dot((side.pop("pb") * side.pop("uc")).astype(_BF), wco_ref[...])

    def job_ga():
        side["ga"] = jax.nn.sigmoid(proj(_OFF_GA, D_MODEL))

    def job_gb():
        side["gb"] = jax.nn.sigmoid(proj(_OFF_GB, D_MODEL))

    keep = WINDOW - tlen
    qf = proj(_OFF_Q, D_MODEL) * Q_SCALE
    kvt_new = proj(_OFF_K, 2 * KV_WIDTH).T
    keep_old = lax.broadcasted_iota(jnp.int32, (KV_WIDTH, WINDOW), 1) < keep

    def window_update(s):
        shift = (keep - s * tlen) % LANES
        for cache_ref, out_ref, base in ((ck_ref, k_ref, 0), (cv_ref, v_ref, KV_WIDTH)):
            rolled = pltpu.roll(cache_ref[s], keep, 1)
            new = pltpu.roll(kvt_new[base:base + KV_WIDTH, :], shift, 1)
            win = jnp.where(keep_old, rolled, new)
            out_ref[s] = win
        return win

    lo = lax.broadcasted_iota(jnp.int32, (rows, LANES), 1) < HEAD_DIM
    zero = jnp.zeros((rows, LANES), _F32)
    n_tiles = KV_WIDTH // LANES
    heads_per_tile = N_HEADS // n_tiles
    m_rows = heads_per_tile * tlen
    row = lax.broadcasted_iota(jnp.int32, (m_rows, WINDOW), 0)
    col = lax.broadcasted_iota(jnp.int32, (m_rows, WINDOW), 1)
    dist_c = WINDOW + row % tlen - col
    valid_c = dist_c <= WINDOW
    shape3 = (nseq, m_rows, LANES)
    seq3 = lax.broadcasted_iota(jnp.int32, shape3, 0)
    row3 = lax.broadcasted_iota(jnp.int32, shape3, 1)
    col3 = lax.broadcasted_iota(jnp.int32, shape3, 2)
    dist_n = row3 % tlen - col3 % tlen
    valid_n = (col3 // tlen == seq3) & (dist_n >= 0)
    row1 = lax.broadcasted_iota(jnp.int32, (m_rows, 1), 0)

    def attention(kc):
        parts = []
        slope = jnp.zeros((m_rows, 1), _F32)
        sink = jnp.zeros((m_rows, 1), _F32)
        for j in range(heads_per_tile):
            head = heads_per_tile * kc + j
            c, par, half = head // HEADS_PER_COL, j % HEADS_PER_COL, j // (heads_per_tile // 2)
            qc = qf[:, c * LANES:(c + 1) * LANES]
            src = qc if par == half else pltpu.roll(qc, HEAD_DIM, 1)
            part = jnp.where(lo, src, zero) if half == 0 else jnp.where(lo, zero, src)
            parts.append(part.reshape(nseq, tlen, LANES))
            mine = (row1 >= j * tlen) & (row1 < (j + 1) * tlen)
            slope = jnp.where(mine, _slope(head), slope)
            sink = jnp.where(mine, sinks_ref[head] * LOG2E, sink)
        lhs = jnp.concatenate(parts, axis=1).astype(_BF)
        lhs2 = lhs.reshape(nseq * m_rows, LANES)
        rsl = slice(kc * LANES, (kc + 1) * LANES)
        k_new = kvt_new[rsl, :].astype(_BF)
        v_new = kvt_new[KV_WIDTH + kc * LANES:KV_WIDTH + (kc + 1) * LANES, :].astype(_BF)
        s_c = jnp.einsum('smk,skn->smn', lhs, ck_ref[:, rsl, :].astype(_BF),
                         preferred_element_type=_F32)
        s_n = _dot(lhs2, k_new).reshape(shape3)
        sc = jnp.concatenate(
            [s_c + jnp.where(valid_c, -(slope * dist_c.astype(_F32)), NEG_INF),
             s_n + jnp.where(valid_n, -(slope * dist_n.astype(_F32)), NEG_INF)], axis=2)
        p, inv = _softmax_parts(sc, sink)
        p = p.astype(_BF)
        r_c = jnp.einsum('smk,snk->smn', p[:, :, 0:WINDOW], cv_ref[:, rsl, :].astype(_BF),
                         preferred_element_type=_F32)
        r_n = lax.dot_general(p[:, :, WINDOW:].reshape(nseq * m_rows, LANES), v_new, _NT,
                              preferred_element_type=_F32).reshape(shape3)
        r = (r_c + r_n) * inv
        cols = {}
        for m2 in range(heads_per_tile // HEADS_PER_COL):
            c = (heads_per_tile // HEADS_PER_COL) * kc + m2
            half = m2 // (heads_per_tile // (2 * HEADS_PER_COL))
            r0 = r[:, (2 * m2) * tlen:(2 * m2 + 1) * tlen, :].reshape(rows, LANES)
            r1 = r[:, (2 * m2 + 1) * tlen:(2 * m2 + 2) * tlen, :].reshape(rows, LANES)
            if half == 0:
                cols[c] = jnp.where(lo, r0, pltpu.roll(r1, HEAD_DIM, 1))
            else:
                cols[c] = jnp.where(lo, pltpu.roll(r0, HEAD_DIM, 1), r1)
        return cols

    o_cols = {}
    job_conv_in()
    window_jobs = [job_gate_b, job_conv_out, job_ga, job_gb]
    group = -(-nseq // len(window_jobs))
    for s in range(nseq):
        if s == ATTN_AFTER_WINDOWS:
            for kc in range(n_tiles):
                o_cols.update(attention(kc))
        last_win = window_update(s)
        if window_jobs and (s + 1) % group == 0:
            window_jobs.pop(0)()
            order_after(last_win)
    while window_jobs:
        window_jobs.pop(0)()
    o = jnp.concatenate([o_cols[c] for c in range(N_QCOLS)], axis=1).astype(_BF)
    yb = _dot(o, wao_ref[...])

    mixed = (side["ga"] * side["ya"] + side["gb"] * yb).astype(_BF)
    y = _dot(mixed, wmo_ref[...]).reshape(nseq, tlen, D_MODEL)
    x1_ref[...] = x3 + g_ref[...] * y


def _mix_sample_call(x, mod3, n1g, win, cw, wco, sinks, wao, wmo, st, ck, cv, chunk):
    nseq, tlen, _ = x.shape
    assert LANES % tlen == 0 and chunk % (LANES // tlen) == 0

    def mod_spec(col):
        return pl.BlockSpec((chunk, 1, D_MODEL), lambda s: (s, 0, col))

    def seq_spec(d1, d2):
        return pl.BlockSpec((chunk, d1, d2), lambda s: (s, 0, 0))

    return pl.pallas_call(
        _mix_sample_kernel,
        grid=(nseq // chunk,),
        in_specs=[seq_spec(tlen, D_MODEL),
                  mod_spec(0), mod_spec(1), mod_spec(2),
                  _const_spec((1, D_MODEL)),
                  _const_spec((D_MODEL, IN_WIDTH)),
                  _const_spec((3, D_MODEL)),
                  _const_spec((D_MODEL, D_MODEL)),
                  pl.BlockSpec(memory_space=pltpu.SMEM),
                  _const_spec((D_MODEL, D_MODEL)),
                  _const_spec((D_MODEL, D_MODEL)),
                  seq_spec(2, D_MODEL),
                  seq_spec(KV_WIDTH, WINDOW),
                  seq_spec(KV_WIDTH, WINDOW)],
        out_specs=[seq_spec(tlen, D_MODEL),
                   seq_spec(2, D_MODEL),
                   seq_spec(KV_WIDTH, WINDOW),
                   seq_spec(KV_WIDTH, WINDOW)],
        out_shape=[jax.ShapeDtypeStruct((nseq, tlen, D_MODEL), _F32),
                   jax.ShapeDtypeStruct((nseq, 2, D_MODEL), _F32),
                   jax.ShapeDtypeStruct((nseq, KV_WIDTH, WINDOW), _F32),
                   jax.ShapeDtypeStruct((nseq, KV_WIDTH, WINDOW), _F32)],
        compiler_params=pltpu.CompilerParams(
            dimension_semantics=("arbitrary",),
            vmem_limit_bytes=VMEM_LIMIT),
        name="mix_sample",
    )(x, mod3, mod3, mod3, n1g, win, cw, wco, sinks, wao, wmo, st, ck, cv)


def _ffn_tail(x1, ac, val, g, wdown_ref, fg):
    hmid = (_gelu(ac) * val).astype(_BF)
    x2 = x1 + g * _dot(hmid, wdown_ref[...])
    return _rms(x2, fg)


def _ffn_prompt_kernel(x1_ref, sh_ref, sc_ref, g_ref, n2g_ref, wup_ref, fcw_ref, wdown_ref,
                       fg_ref, y_ref, ffn_ref, a_ext, *, parts):
    i = pl.program_id(1)
    t = x1_ref.shape[0]
    rows = t // parts

    @pl.when(i == 0)
    def _():
        a_ext[0:CARRY_ROWS, :] = jnp.zeros((CARRY_ROWS, D_FF), _F32)

    fcw = fcw_ref[...]

    def up(part):
        r0 = part * rows
        e0 = CARRY_ROWS + r0
        h2 = _mod_norm(x1_ref[r0:r0 + rows, :], n2g_ref[...], sc_ref[...], sh_ref[...])
        a = _dot(h2, wup_ref[:, 0:D_FF])
        val = _dot(h2, wup_ref[:, D_FF:2 * D_FF])
        a_ext[e0:e0 + rows, :] = a
        ac = (a_ext[e0 - 2:e0 - 2 + rows, :] * fcw[0:1, :]
              + a_ext[e0 - 1:e0 - 1 + rows, :] * fcw[1:2, :]
              + a * fcw[2:3, :])
        return ac, val

    def down(part, ac, val):
        r0 = part * rows
        y_ref[r0:r0 + rows, :] = _ffn_tail(x1_ref[r0:r0 + rows, :], ac, val, g_ref[...],
                                           wdown_ref, fg_ref[...])

    for part in range(parts):
        down(part, *up(part))
    ffn_ref[...] = a_ext[CARRY_ROWS + t - 2:CARRY_ROWS + t, :]
    a_ext[0:CARRY_ROWS, :] = a_ext[t:t + CARRY_ROWS, :]


def _ffn_prompt_call(x1, mod3, mod_row0, n2g, wup, fcw, wdown, fg, tile, parts):
    bsz, seq, _ = x1.shape

    def mod_spec(chunk):
        return pl.BlockSpec((None, 1, D_MODEL), lambda b, i: (mod_row0 + b, 0, chunk))

    x_spec = pl.BlockSpec((None, tile, D_MODEL), lambda b, i: (b, i, 0))
    return pl.pallas_call(
        functools.partial(_ffn_prompt_kernel, parts=parts),
        grid=(bsz, seq // tile),
        in_specs=[x_spec, mod_spec(3), mod_spec(4), mod_spec(5),
                  _const_spec((1, D_MODEL)),
                  _const_spec((D_MODEL, 2 * D_FF)),
                  _const_spec((3, D_FF)),
                  _const_spec((D_FF, D_MODEL)),
                  _const_spec((1, D_MODEL))],
        out_specs=[x_spec,
                   pl.BlockSpec((None, 2, D_FF), lambda b, i: (b, 0, 0))],
        out_shape=[jax.ShapeDtypeStruct((bsz, seq, D_MODEL), _F32),
                   jax.ShapeDtypeStruct((bsz, 2, D_FF), _F32)],
        scratch_shapes=[pltpu.VMEM((CARRY_ROWS + tile, D_FF), _F32)],
        compiler_params=pltpu.CompilerParams(
            dimension_semantics=("arbitrary", "arbitrary"),
            vmem_limit_bytes=VMEM_LIMIT),
        name="ffn_prompt",
    )(x1, mod3, mod3, mod3, n2g, wup, fcw, wdown, fg)


def _ffn_sample_kernel(x1_ref, sh_ref, sc_ref, g_ref, n2g_ref, wup_ref, fcw_ref, wdown_ref,
                       fg_ref, st_ref, y_ref, ffn_ref):
    nseq, tlen, _ = x1_ref.shape
    rows = nseq * tlen
    x3 = x1_ref[...]
    h3 = _rms(x3, n2g_ref[...]) * (1.0 + sc_ref[...]) + sh_ref[...]
    h2 = h3.reshape(rows, D_MODEL).astype(_BF)
    a = _dot(h2, wup_ref[:, 0:D_FF])
    val = _dot(h2, wup_ref[:, D_FF:2 * D_FF])
    ac, new_st = _conv_seq(a, st_ref[...], fcw_ref[...], nseq, tlen)
    ffn_ref[...] = new_st
    hmid = (_gelu(ac) * val).astype(_BF)
    y = _dot(hmid, wdown_ref[...]).reshape(nseq, tlen, D_MODEL)
    x2 = x3 + g_ref[...] * y
    y_ref[...] = _rms(x2, fg_ref[...])


def _ffn_sample_call(x1, mod3, n2g, wup, fcw, wdown, fg, st, chunk):
    nseq, tlen, _ = x1.shape

    def mod_spec(col):
        return pl.BlockSpec((chunk, 1, D_MODEL), lambda s: (s, 0, col))

    def seq_spec(d1, d2):
        return pl.BlockSpec((chunk, d1, d2), lambda s: (s, 0, 0))

    return pl.pallas_call(
        _ffn_sample_kernel,
        grid=(nseq // chunk,),
        in_specs=[seq_spec(tlen, D_MODEL),
                  mod_spec(3), mod_spec(4), mod_spec(5),
                  _const_spec((1, D_MODEL)),
                  _const_spec((D_MODEL, 2 * D_FF)),
                  _const_spec((3, D_FF)),
                  _const_spec((D_FF, D_MODEL)),
                  _const_spec((1, D_MODEL)),
                  seq_spec(2, D_FF)],
        out_specs=[seq_spec(tlen, D_MODEL), seq_spec(2, D_FF)],
        out_shape=[jax.ShapeDtypeStruct((nseq, tlen, D_MODEL), _F32),
                   jax.ShapeDtypeStruct((nseq, 2, D_FF), _F32)],
        compiler_params=pltpu.CompilerParams(
            dimension_semantics=("arbitrary",),
            vmem_limit_bytes=VMEM_LIMIT),
        name="ffn_sample",
    )(x1, mod3, mod3, mod3, n2g, wup, fcw, wdown, fg, st)


PROMPT_TILE = 512
FFN_TILE = 1024
FFN_PARTS = 2
SAMPLE_CHUNK = 16
ATTN_AFTER_WINDOWS = 4
FFN_SAMPLE_CHUNK = 64


def kernel(x_prompt, x_sample, c_prompt, c_sample, state_conv, cache_k_win, cache_v_win,
           state_ffn_conv, norm1_g, norm2_g, w_ada, b_ada, w_in, conv_w, w_conv_out,
           attn_sinks, w_attn_out, w_mix_out, w_up, ffn_conv_w, w_down, final_g):
    depth = w_in.shape[0]
    assert depth == 1, "the final norm is fused into the (single) layer's FFN call"
    nsamp = x_sample.shape[0]
    bsz = x_prompt.shape[0]
    xp, xs = x_prompt, x_sample
    fg = final_g.reshape(1, D_MODEL)
    outs = [[] for _ in range(8)]
    for layer in range(depth):
        mod3, win, wco, wao, wmo = _mod_call(
            c_sample, c_prompt, w_ada[layer], b_ada[layer].reshape(1, -1),
            (w_in[layer], w_conv_out[layer], w_attn_out[layer], w_mix_out[layer]))
        n1g = norm1_g[layer].reshape(1, D_MODEL)
        n2g = norm2_g[layer].reshape(1, D_MODEL)
        cw = conv_w[layer]
        fcw = ffn_conv_w[layer]
        sinks = attn_sinks[layer]
        fg_l = fg
        def to_dim_major(win_state):
            return jnp.transpose(win_state, (0, 2, 3, 1)).reshape(-1, KV_WIDTH, WINDOW)

        def to_pos_major(t3):
            return jnp.transpose(t3.reshape(-1, N_KV_HEADS, HEAD_DIM, WINDOW), (0, 3, 1, 2))

        x1p, conv_p, k_p, v_p, wup, wdown = _mix_prompt_call(
            xp, mod3, nsamp, n1g, win, cw, wco, sinks, wao, wmo,
            (w_up[layer], w_down[layer]), PROMPT_TILE)
        x1s, conv_s, k_s, v_s = _mix_sample_call(
            xs, mod3, n1g, win, cw, wco, sinks, wao, wmo, state_conv[layer],
            to_dim_major(cache_k_win[layer]), to_dim_major(cache_v_win[layer]), SAMPLE_CHUNK)
        xp, ffn_p = _ffn_prompt_call(x1p, mod3, nsamp, n2g, wup, fcw, wdown, fg_l,
                                     FFN_TILE, FFN_PARTS)
        xs, ffn_s = _ffn_sample_call(x1s, mod3, n2g, wup, fcw, wdown, fg_l,
                                     state_ffn_conv[layer], FFN_SAMPLE_CHUNK)
        for lst, val in zip(outs, (conv_p, to_pos_major(k_p), to_pos_major(v_p), ffn_p,
                                   conv_s, to_pos_major(k_s), to_pos_major(v_s), ffn_s)):
            lst.append(val)
    return (xp, xs) + tuple(jnp.stack(lst, axis=0) for lst in outs)
```

```python
import functools

import jax
import jax.numpy as jnp
from jax import lax
from jax.experimental import pallas as pl
from jax.experimental.pallas import tpu as pltpu

D_MODEL = 1024
HEAD_DIM = 64
N_HEADS = 16
N_KV_HEADS = 4
WINDOW = 128
D_FF = 2816
KV_WIDTH = N_KV_HEADS * HEAD_DIM
IN_WIDTH = 3 * D_MODEL + D_MODEL + 2 * KV_WIDTH + 2 * D_MODEL
RMS_EPS = 1e-6
NEG_INF = -1e30
LANES = 128
HEADS_PER_COL = LANES // HEAD_DIM
N_QCOLS = N_HEADS * HEAD_DIM // LANES
CARRY_ROWS = 8
VMEM_LIMIT = 60 * 1024 * 1024

_OFF_B, _OFF_C, _OFF_X = 0, D_MODEL, 2 * D_MODEL
_OFF_Q = 3 * D_MODEL
_OFF_K = _OFF_Q + D_MODEL
_OFF_V = _OFF_K + KV_WIDTH
_OFF_GA = _OFF_V + KV_WIDTH
_OFF_GB = _OFF_GA + D_MODEL

_BF = jnp.bfloat16
_F32 = jnp.float32
_NT = (((1,), (1,)), ((), ()))


LOG2E = 1.4426950408889634
Q_SCALE = HEAD_DIM ** -0.5 * LOG2E


def _slope(head):
    return 2.0 ** (-8.0 * (head + 1) / N_HEADS) * LOG2E


def _rms(x, g):
    return x * lax.rsqrt(jnp.mean(x * x, axis=-1, keepdims=True) + RMS_EPS) * g


def _dot(a, b):
    return jnp.dot(a, b, preferred_element_type=_F32)


def _gelu(x):
    return 0.5 * x * (1.0 + lax.erf(x * (0.5 ** 0.5)))


def _split_heads(t2d, lo):
    rolled = pltpu.roll(t2d, HEAD_DIM, 1)
    zero = jnp.zeros_like(t2d)
    return (jnp.where(lo, t2d, zero), jnp.where(lo, zero, rolled),
            jnp.where(lo, rolled, zero), jnp.where(lo, zero, t2d))


def _softmax_parts(s, sink):
    m = jnp.maximum(jnp.max(s, axis=-1, keepdims=True), sink)
    p = jnp.exp2(s - m)
    denom = jnp.sum(p, axis=-1, keepdims=True) + jnp.exp2(sink - m)
    return p, 1.0 / denom


N_MOD = 6
MOD_COLS = 768


def _mod_kernel(cs_ref, cp_ref, w_ref, b_ref, *rest):
    n_cast = (len(rest) - 1) // 2
    cast_in, o_ref, cast_out = rest[:n_cast], rest[n_cast], rest[n_cast + 1:]
    for src, dst in zip(cast_in, cast_out):
        dst[...] = src[...].astype(_BF)
    w = w_ref[...].astype(_BF)
    n0 = cs_ref.shape[0]
    for c_ref, r0 in ((cs_ref, 0), (cp_ref, n0)):
        c = c_ref[...]
        a = (c * jax.nn.sigmoid(c)).astype(_BF)
        res = _dot(a, w) + b_ref[...]
        n = c.shape[0]
        o_ref[r0:r0 + n] = res.reshape(n, 1, res.shape[-1])


def _mod_call(c_sample, c_prompt, w_ada, b_ada, cast_along):
    ns, npr = c_sample.shape[0], c_prompt.shape[0]
    n = ns + npr
    steps = N_MOD * D_MODEL // MOD_COLS
    cast_specs = [_cast_stream_spec(w.shape, steps) for w in cast_along]
    return pl.pallas_call(
        _mod_kernel,
        grid=(steps,),
        in_specs=[pl.BlockSpec((ns, D_MODEL), lambda j: (0, 0)),
                  pl.BlockSpec((npr, D_MODEL), lambda j: (0, 0)),
                  pl.BlockSpec((D_MODEL, MOD_COLS), lambda j: (0, j)),
                  pl.BlockSpec((1, MOD_COLS), lambda j: (0, j))] + cast_specs,
        out_specs=[pl.BlockSpec((n, 1, MOD_COLS), lambda j: (0, 0, j))] + cast_specs,
        out_shape=[jax.ShapeDtypeStruct((n, 1, N_MOD * D_MODEL), _F32)]
        + [jax.ShapeDtypeStruct(w.shape, _BF) for w in cast_along],
        compiler_params=pltpu.CompilerParams(vmem_limit_bytes=VMEM_LIMIT),
        name="adaln_mod",
    )(c_sample, c_prompt, w_ada, b_ada, *cast_along)


def _mod_norm(x, g, sc, sh):
    return (_rms(x, g) * (1.0 + sc) + sh).astype(_BF)


def _mix_prompt_kernel(x_ref, sh_ref, sc_ref, g_ref, xn_ref, shn_ref, scn_ref, n1g_ref,
                       win_ref, cw_ref, wco_ref, sinks_ref, wao_ref, wmo_ref, *rest,
                       tiles_per_seq, n_cast):
    cast_in, rest = rest[:n_cast], rest[n_cast:]
    (x1_ref, conv_ref, k_ref, v_ref), rest = rest[:4], rest[4:]
    cast_out, (u_ext, k_ext, v_ext, o_scr, bias_scr, h_scr) = rest[:n_cast], rest[n_cast:]
    step = pl.program_id(0)
    i = step % tiles_per_seq
    t = x_ref.shape[0]
    nblk = t // WINDOW

    for src, dst in zip(cast_in, cast_out):
        dst[...] = src[...].astype(_BF)

    @pl.when(step == 0)
    def _():
        h_scr[...] = _mod_norm(x_ref[...], n1g_ref[...], sc_ref[...], sh_ref[...])
        row = lax.broadcasted_iota(jnp.int32, (WINDOW, 2 * WINDOW), 0)
        col = lax.broadcasted_iota(jnp.int32, (WINDOW, 2 * WINDOW), 1)
        dist = WINDOW + row - col
        valid = (dist >= 0) & (dist <= WINDOW)
        distf = dist.astype(_F32)
        for head in range(N_HEADS):
            bias_scr[head] = jnp.where(valid, -(_slope(head) * distf), NEG_INF)

    @pl.when(i == 0)
    def _():
        u_ext[0:CARRY_ROWS, :] = jnp.zeros((CARRY_ROWS, D_MODEL), _F32)
        k_ext[:, 0:WINDOW, :] = jnp.zeros((2 * N_KV_HEADS, WINDOW, LANES), _BF)
        v_ext[:, 0:WINDOW, :] = jnp.zeros((2 * N_KV_HEADS, WINDOW, LANES), _BF)

    def proj(lo, width):
        return _dot(h_scr[...], win_ref[:, lo:lo + width])

    def next_tile_norm():
        h_scr[...] = _mod_norm(xn_ref[...], n1g_ref[...], scn_ref[...], shn_ref[...])

    side = {}

    def conv_in():
        u = proj(_OFF_C, D_MODEL) * proj(_OFF_X, D_MODEL)
        u_ext[CARRY_ROWS:CARRY_ROWS + t, :] = u
        cw = cw_ref[...]
        side["uc"] = (u_ext[CARRY_ROWS - 2:CARRY_ROWS - 2 + t, :] * cw[0:1, :]
                      + u_ext[CARRY_ROWS - 1:CARRY_ROWS - 1 + t, :] * cw[1:2, :]
                      + u * cw[2:3, :])
        conv_ref[...] = u[t - 2:t, :]
        u_ext[0:CARRY_ROWS, :] = u[t - CARRY_ROWS:t, :]

    def conv_out():
        pb = proj(_OFF_B, D_MODEL)
        side["ya"] = _dot((pb * side.pop("uc")).astype(_BF), wco_ref[...])

    def gate_a():
        side["ga"] = jax.nn.sigmoid(proj(_OFF_GA, D_MODEL))

    def gate_b():
        side["gb"] = jax.nn.sigmoid(proj(_OFF_GB, D_MODEL))
        next_tile_norm()

    side_jobs = [conv_in, conv_out, gate_a, gate_b]

    q = (proj(_OFF_Q, D_MODEL) * Q_SCALE).astype(_BF)
    kf = proj(_OFF_K, KV_WIDTH)
    vf = proj(_OFF_V, KV_WIDTH)
    k_ref[...] = kf[t - WINDOW:t, :].T
    v_ref[...] = vf[t - WINDOW:t, :].T
    lane_t = lax.broadcasted_iota(jnp.int32, (t, LANES), 1)
    lo = lane_t < HEAD_DIM
    ones_lane = (HEAD_DIM, 0)
    for kc in range(KV_WIDTH // LANES):
        ks = _split_heads(kf[:, kc * LANES:(kc + 1) * LANES], lo)
        vs = _split_heads(vf[:, kc * LANES:(kc + 1) * LANES], lo)
        for j in range(4):
            k_ext[4 * kc + j, WINDOW:WINDOW + t, :] = ks[j].astype(_BF)
            v_one = jnp.where(lane_t == ones_lane[j % HEADS_PER_COL], 1.0, vs[j])
            v_ext[4 * kc + j, WINDOW:WINDOW + t, :] = v_one.astype(_BF)

    first_cols = lax.broadcasted_iota(jnp.int32, (WINDOW, 2 * WINDOW), 1) < WINDOW
    no_prev = first_cols & (i == 0)
    lo_q = lax.broadcasted_iota(jnp.int32, (WINDOW, LANES), 1) < HEAD_DIM
    for blk in range(nblk):
        r0 = blk * WINDOW
        for c in range(N_QCOLS):
            hkv = c // 2
            qc = q[r0:r0 + WINDOW, c * LANES:(c + 1) * LANES]
            o_pair = []
            for par in range(HEADS_PER_COL):
                head = HEADS_PER_COL * c + par
                kk = k_ext[2 * hkv + par, r0:r0 + 2 * WINDOW, :]
                s = lax.dot_general(qc, kk, _NT, preferred_element_type=_F32) + bias_scr[head]
                if blk == 0:
                    s = jnp.where(no_prev, NEG_INF, s)
                sink = sinks_ref[head] * LOG2E
                m = jnp.maximum(jnp.max(s, axis=-1, keepdims=True), sink)
                p = jnp.exp2(s - m).astype(_BF)
                o_h = _dot(p, v_ext[2 * hkv + par, r0:r0 + 2 * WINDOW, :])
                col = ones_lane[par]
                denom = o_h[:, col:col + 1] + jnp.exp2(sink - m)
                o_pair.append(o_h * (1.0 / denom))
            o_c = jnp.where(lo_q, o_pair[0], o_pair[1])
            o_scr[r0:r0 + WINDOW, c * LANES:(c + 1) * LANES] = o_c.astype(_BF)
        for _ in range(-(-len(side_jobs) // (nblk - blk))):
            side_jobs.pop(0)()
    k_ext[:, 0:WINDOW, :] = k_ext[:, t:t + WINDOW, :]
    v_ext[:, 0:WINDOW, :] = v_ext[:, t:t + WINDOW, :]
    yb = _dot(o_scr[...], wao_ref[...])

    mixed = (side["ga"] * side["ya"] + side["gb"] * yb).astype(_BF)
    x1_ref[...] = x_ref[...] + g_ref[...] * _dot(mixed, wmo_ref[...])


def _const_spec(shape):
    nd = len(shape)
    return pl.BlockSpec(shape, lambda *_: (0,) * nd, pipeline_mode=pl.Buffered(1))


def _tile_specs(tile, nt, steps, mod_row0):
    def nxt(s):
        return jnp.minimum(s + 1, steps - 1)

    x_spec = pl.BlockSpec((None, tile, D_MODEL), lambda s: (s // nt, s % nt, 0))
    xn_spec = pl.BlockSpec((None, tile, D_MODEL), lambda s: (nxt(s) // nt, nxt(s) % nt, 0))

    def mod_spec(chunk):
        return pl.BlockSpec((None, 1, D_MODEL), lambda s: (mod_row0 + s // nt, 0, chunk))

    def modn_spec(chunk):
        return pl.BlockSpec((None, 1, D_MODEL), lambda s: (mod_row0 + nxt(s) // nt, 0, chunk))

    return x_spec, xn_spec, mod_spec, modn_spec


BF16_SUBLANES = 16


def _cast_stream_spec(shape, steps):
    rows, cols = shape
    for hold in range(1, steps + 1):
        nblocks, rem = divmod(steps, hold)
        if rem == 0 and rows % nblocks == 0 and (rows // nblocks) % BF16_SUBLANES == 0:
            return pl.BlockSpec((rows // nblocks, cols), lambda s: (s // hold, 0))
    raise ValueError(f"no bf16-aligned row split of {shape} over {steps} steps")


def _mix_prompt_call(x, mod3, mod_row0, n1g, win, cw, wco, sinks, wao, wmo, cast_along, tile):
    bsz, seq, _ = x.shape
    nt = seq // tile
    steps = bsz * nt
    x_spec, xn_spec, mod_spec, modn_spec = _tile_specs(tile, nt, steps, mod_row0)
    cast_specs = [_cast_stream_spec(w.shape, steps) for w in cast_along]

    def state_spec(d1, d2):
        return pl.BlockSpec((None, d1, d2), lambda s: (s // nt, 0, 0))

    return pl.pallas_call(
        functools.partial(_mix_prompt_kernel, tiles_per_seq=nt, n_cast=len(cast_along)),
        grid=(steps,),
        in_specs=[x_spec, mod_spec(0), mod_spec(1), mod_spec(2),
                  xn_spec, modn_spec(0), modn_spec(1),
                  _const_spec((1, D_MODEL)),
                  _const_spec((D_MODEL, IN_WIDTH)),
                  _const_spec((3, D_MODEL)),
                  _const_spec((D_MODEL, D_MODEL)),
                  pl.BlockSpec(memory_space=pltpu.SMEM),
                  _const_spec((D_MODEL, D_MODEL)),
                  _const_spec((D_MODEL, D_MODEL))] + cast_specs,
        out_specs=[x_spec,
                   state_spec(2, D_MODEL),
                   state_spec(KV_WIDTH, WINDOW),
                   state_spec(KV_WIDTH, WINDOW)] + cast_specs,
        out_shape=[jax.ShapeDtypeStruct((bsz, seq, D_MODEL), _F32),
                   jax.ShapeDtypeStruct((bsz, 2, D_MODEL), _F32),
                   jax.ShapeDtypeStruct((bsz, KV_WIDTH, WINDOW), _F32),
                   jax.ShapeDtypeStruct((bsz, KV_WIDTH, WINDOW), _F32)]
        + [jax.ShapeDtypeStruct(w.shape, _BF) for w in cast_along],
        scratch_shapes=[pltpu.VMEM((CARRY_ROWS + tile, D_MODEL), _F32),
                        pltpu.VMEM((2 * N_KV_HEADS, WINDOW + tile, LANES), _BF),
                        pltpu.VMEM((2 * N_KV_HEADS, WINDOW + tile, LANES), _BF),
                        pltpu.VMEM((tile, D_MODEL), _BF),
                        pltpu.VMEM((N_HEADS, WINDOW, 2 * WINDOW), _F32),
                        pltpu.VMEM((tile, D_MODEL), _BF)],
        compiler_params=pltpu.CompilerParams(
            dimension_semantics=("arbitrary",),
            vmem_limit_bytes=VMEM_LIMIT),
        name="mix_prompt",
    )(x, mod3, mod3, mod3, x, mod3, mod3, n1g, win, cw, wco, sinks, wao, wmo, *cast_along)


def _conv_seq(u2d, st, cw, nseq, tlen):
    width = u2d.shape[-1]
    shape3 = (nseq, tlen, width)
    u3 = u2d.reshape(shape3)
    r1 = pltpu.roll(u2d, 1, 0).reshape(shape3)
    r2 = pltpu.roll(u2d, 2, 0).reshape(shape3)
    tpos = lax.broadcasted_iota(jnp.int32, shape3, 1)
    st0 = jnp.broadcast_to(st[:, 0:1, :], shape3)
    st1 = jnp.broadcast_to(st[:, 1:2, :], shape3)
    um1 = jnp.where(tpos == 0, st1, r1)
    um2 = jnp.where(tpos == 0, st0, jnp.where(tpos == 1, st1, r2))
    y = um2 * cw[0:1, :] + um1 * cw[1:2, :] + u3 * cw[2:3, :]
    return y.reshape(nseq * tlen, width), u3[:, tlen - 2:tlen, :]


def _mix_sample_kernel(x_ref, sh_ref, sc_ref, g_ref, n1g_ref, win_ref, cw_ref, wco_ref,
                       sinks_ref, wao_ref, wmo_ref, st_ref, ck_ref, cv_ref,
                       x1_ref, conv_ref, k_ref, v_ref):
    nseq, tlen, _ = x_ref.shape
    rows = nseq * tlen

    assert rows == LANES, "one step's new keys must fill exactly one lane tile"

    x3 = x_ref[...]
    h3 = _rms(x3, n1g_ref[...]) * (1.0 + sc_ref[...]) + sh_ref[...]
    hf = h3.reshape(rows, D_MODEL)
    h_now = [hf.astype(_BF)]

    def proj(lo, width):
        return _dot(h_now[0], win_ref[:, lo:lo + width])

    def order_after(tile_f32):
        bits = pltpu.bitcast(tile_f32[0:8, :], jnp.uint32)
        zero = ((bits >> 16) >> 16).astype(_F32)
        h_now[0] = (hf + jnp.tile(zero, (rows // 8, D_MODEL // LANES))).astype(_BF)

    side = {}

    def job_conv_in():
        u = proj(_OFF_C, D_MODEL) * proj(_OFF_X, D_MODEL)
        side["uc"], new_st = _conv_seq(u, st_ref[...], cw_ref[...], nseq, tlen)
        conv_ref[...] = new_st

    def job_gate_b():
        side["pb"] = proj(_OFF_B, D_MODEL)

    def job_conv_out():
        side["ya"] = _dot((side.pop("pb") * side.pop("uc")).astype(_BF), wco_ref[...])

    def job_ga():
        side["ga"] = jax.nn.sigmoid(proj(_OFF_GA, D_MODEL))

    def job_gb():
        side["gb"] = jax.nn.sigmoid(proj(_OFF_GB, D_MODEL))

    keep = WINDOW - tlen
    qf = proj(_OFF_Q, D_MODEL) * Q_SCALE
    kvt_new = proj(_OFF_K, 2 * KV_WIDTH).T
    keep_old = lax.broadcasted_iota(jnp.int32, (KV_WIDTH, WINDOW), 1) < keep

    def window_update(s):
        shift = (keep - s * tlen) % LANES
        for cache_ref, out_ref, base in ((ck_ref, k_ref, 0), (cv_ref, v_ref, KV_WIDTH)):
            rolled = pltpu.roll(cache_ref[s], keep, 1)
            new = pltpu.roll(kvt_new[base:base + KV_WIDTH, :], shift, 1)
            win = jnp.where(keep_old, rolled, new)
            out_ref[s] = win
        return win

    lo = lax.broadcasted_iota(jnp.int32, (rows, LANES), 1) < HEAD_DIM
    zero = jnp.zeros((rows, LANES), _F32)
    n_tiles = KV_WIDTH // LANES
    heads_per_tile = N_HEADS // n_tiles
    m_rows = heads_per_tile * tlen
    row = lax.broadcasted_iota(jnp.int32, (m_rows, WINDOW), 0)
    col = lax.broadcasted_iota(jnp.int32, (m_rows, WINDOW), 1)
    dist_c = WINDOW + row % tlen - col
    valid_c = dist_c <= WINDOW
    shape3 = (nseq, m_rows, LANES)
    seq3 = lax.broadcasted_iota(jnp.int32, shape3, 0)
    row3 = lax.broadcasted_iota(jnp.int32, shape3, 1)
    col3 = lax.broadcasted_iota(jnp.int32, shape3, 2)
    dist_n = row3 % tlen - col3 % tlen
    valid_n = (col3 // tlen == seq3) & (dist_n >= 0)
    row1 = lax.broadcasted_iota(jnp.int32, (m_rows, 1), 0)

    def attention(kc):
        parts = []
        slope = jnp.zeros((m_rows, 1), _F32)
        sink = jnp.zeros((m_rows, 1), _F32)
        for j in range(heads_per_tile):
            head = heads_per_tile * kc + j
            c, par, half = head // HEADS_PER_COL, j % HEADS_PER_COL, j // (heads_per_tile // 2)
            qc = qf[:, c * LANES:(c + 1) * LANES]
            src = qc if par == half else pltpu.roll(qc, HEAD_DIM, 1)
            part = jnp.where(lo, src, zero) if half == 0 else jnp.where(lo, zero, src)
            parts.append(part.reshape(nseq, tlen, LANES))
            mine = (row1 >= j * tlen) & (row1 < (j + 1) * tlen)
            slope = jnp.where(mine, _slope(head), slope)
            sink = jnp.where(mine, sinks_ref[head] * LOG2E, sink)
        lhs = jnp.concatenate(parts, axis=1).astype(_BF)
        lhs2 = lhs.reshape(nseq * m_rows, LANES)
        rsl = slice(kc * LANES, (kc + 1) * LANES)
        k_new = kvt_new[rsl, :].astype(_BF)
        v_new = kvt_new[KV_WIDTH + kc * LANES:KV_WIDTH + (kc + 1) * LANES, :].astype(_BF)
        s_c = jnp.einsum('smk,skn->smn', lhs, ck_ref[:, rsl, :].astype(_BF),
                         preferred_element_type=_F32)
        s_n = _dot(lhs2, k_new).reshape(shape3)
        sc = jnp.concatenate(
            [s_c + jnp.where(valid_c, -(slope * dist_c.astype(_F32)), NEG_INF),
             s_n + jnp.where(valid_n, -(slope * dist_n.astype(_F32)), NEG_INF)], axis=2)
        p, inv = _softmax_parts(sc, sink)
        p = p.astype(_BF)
        r_c = jnp.einsum('smk,snk->smn', p[:, :, 0:WINDOW], cv_ref[:, rsl, :].astype(_BF),
                         preferred_element_type=_F32)
        r_n = lax.dot_general(p[:, :, WINDOW:].reshape(nseq * m_rows, LANES), v_new, _NT,
                              preferred_element_type=_F32).reshape(shape3)
        r = (r_c + r_n) * inv
        cols = {}
        for m2 in range(heads_per_tile // HEADS_PER_COL):
            c = (heads_per_tile // HEADS_PER_COL) * kc + m2
            half = m2 // (heads_per_tile // (2 * HEADS_PER_COL))
            r0 = r[:, (2 * m2) * tlen:(2 * m2 + 1) * tlen, :].reshape(rows, LANES)
            r1 = r[:, (2 * m2 + 1) * tlen:(2 * m2 + 2) * tlen, :].reshape(rows, LANES)
            if half == 0:
                cols[c] = jnp.where(lo, r0, pltpu.roll(r1, HEAD_DIM, 1))
            else:
                cols[c] = jnp.where(lo, pltpu.roll(r0, HEAD_DIM, 1), r1)
        return cols

    o_cols = {}
    job_conv_in()
    window_jobs = [job_gate_b, job_conv_out, job_ga, job_gb]
    group = -(-nseq // len(window_jobs))
    for s in range(nseq):
        if s == ATTN_AFTER_WINDOWS:
            for kc in range(n_tiles):
                o_cols.update(attention(kc))
        last_win = window_update(s)
        if window_jobs and (s + 1) % group == 0:
            window_jobs.pop(0)()
            order_after(last_win)
    while window_jobs:
        window_jobs.pop(0)()
    o = jnp.concatenate([o_cols[c] for c in range(N_QCOLS)], axis=1).astype(_BF)
    yb = _dot(o, wao_ref[...])

    mixed = (side["ga"] * side["ya"] + side["gb"] * yb).astype(_BF)
    y = _dot(mixed, wmo_ref[...]).reshape(nseq, tlen, D_MODEL)
    x1_ref[...] = x3 + g_ref[...] * y


def _mix_sample_call(x, mod3, n1g, win, cw, wco, sinks, wao, wmo, st, ck, cv, chunk):
    nseq, tlen, _ = x.shape
    assert LANES % tlen == 0 and chunk % (LANES // tlen) == 0

    def mod_spec(col):
        return pl.BlockSpec((chunk, 1, D_MODEL), lambda s: (s, 0, col))

    def seq_spec(d1, d2):
        return pl.BlockSpec((chunk, d1, d2), lambda s: (s, 0, 0))

    return pl.pallas_call(
        _mix_sample_kernel,
        grid=(nseq // chunk,),
        in_specs=[seq_spec(tlen, D_MODEL),
                  mod_spec(0), mod_spec(1), mod_spec(2),
                  _const_spec((1, D_MODEL)),
                  _const_spec((D_MODEL, IN_WIDTH)),
                  _const_spec((3, D_MODEL)),
                  _const_spec((D_MODEL, D_MODEL)),
                  pl.BlockSpec(memory_space=pltpu.SMEM),
                  _const_spec((D_MODEL, D_MODEL)),
                  _const_spec((D_MODEL, D_MODEL)),
                  seq_spec(2, D_MODEL),
                  seq_spec(KV_WIDTH, WINDOW),
                  seq_spec(KV_WIDTH, WINDOW)],
        out_specs=[seq_spec(tlen, D_MODEL),
                   seq_spec(2, D_MODEL),
                   seq_spec(KV_WIDTH, WINDOW),
                   seq_spec(KV_WIDTH, WINDOW)],
        out_shape=[jax.ShapeDtypeStruct((nseq, tlen, D_MODEL), _F32),
                   jax.ShapeDtypeStruct((nseq, 2, D_MODEL), _F32),
                   jax.ShapeDtypeStruct((nseq, KV_WIDTH, WINDOW), _F32),
                   jax.ShapeDtypeStruct((nseq, KV_WIDTH, WINDOW), _F32)],
        compiler_params=pltpu.CompilerParams(
            dimension_semantics=("arbitrary",),
            vmem_limit_bytes=VMEM_LIMIT),
        name="mix_sample",
    )(x, mod3, mod3, mod3, n1g, win, cw, wco, sinks, wao, wmo, st, ck, cv)


def _ffn_tail(x1, ac, val, g, wdown_ref, fg):
    hmid = (_gelu(ac) * val).astype(_BF)
    x2 = x1 + g * _dot(hmid, wdown_ref[...])
    return _rms(x2, fg)


def _ffn_prompt_kernel(x1_ref, sh_ref, sc_ref, g_ref, n2g_ref, wup_ref, fcw_ref, wdown_ref,
                       fg_ref, y_ref, ffn_ref, a_ext, *, parts):
    i = pl.program_id(1)
    t = x1_ref.shape[0]
    rows = t // parts

    @pl.when(i == 0)
    def _():
        a_ext[0:CARRY_ROWS, :] = jnp.zeros((CARRY_ROWS, D_FF), _F32)

    fcw = fcw_ref[...]

    def up(part):
        r0 = part * rows
        e0 = CARRY_ROWS + r0
        h2 = _mod_norm(x1_ref[r0:r0 + rows, :], n2g_ref[...], sc_ref[...], sh_ref[...])
        a = _dot(h2, wup_ref[:, 0:D_FF])
        val = _dot(h2, wup_ref[:, D_FF:2 * D_FF])
        a_ext[e0:e0 + rows, :] = a
        ac = (a_ext[e0 - 2:e0 - 2 + rows, :] * fcw[0:1, :]
              + a_ext[e0 - 1:e0 - 1 + rows, :] * fcw[1:2, :]
              + a * fcw[2:3, :])
        return ac, val

    def down(part, ac, val):
        r0 = part * rows
        y_ref[r0:r0 + rows, :] = _ffn_tail(x1_ref[r0:r0 + rows, :], ac, val, g_ref[...],
                                           wdown_ref, fg_ref[...])

    for part in range(parts):
        down(part, *up(part))
    ffn_ref[...] = a_ext[CARRY_ROWS + t - 2:CARRY_ROWS + t, :]
    a_ext[0:CARRY_ROWS, :] = a_ext[t:t + CARRY_ROWS, :]


def _ffn_prompt_call(x1, mod3, mod_row0, n2g, wup, fcw, wdown, fg, tile, parts):
    bsz, seq, _ = x1.shape

    def mod_spec(chunk):
        return pl.BlockSpec((None, 1, D_MODEL), lambda b, i: (mod_row0 + b, 0, chunk))

    x_spec = pl.BlockSpec((None, tile, D_MODEL), lambda b, i: (b, i, 0))
    return pl.pallas_call(
        functools.partial(_ffn_prompt_kernel, parts=parts),
        grid=(bsz, seq // tile),
        in_specs=[x_spec, mod_spec(3), mod_spec(4), mod_spec(5),
                  _const_spec((1, D_MODEL)),
                  _const_spec((D_MODEL, 2 * D_FF)),
                  _const_spec((3, D_FF)),
                  _const_spec((D_FF, D_MODEL)),
                  _const_spec((1, D_MODEL))],
        out_specs=[x_spec,
                   pl.BlockSpec((None, 2, D_FF), lambda b, i: (b, 0, 0))],
        out_shape=[jax.ShapeDtypeStruct((bsz, seq, D_MODEL), _F32),
                   jax.ShapeDtypeStruct((bsz, 2, D_FF), _F32)],
        scratch_shapes=[pltpu.VMEM((CARRY_ROWS + tile, D_FF), _F32)],
        compiler_params=pltpu.CompilerParams(
            dimension_semantics=("arbitrary", "arbitrary"),
            vmem_limit_bytes=VMEM_LIMIT),
        name="ffn_prompt",
    )(x1, mod3, mod3, mod3, n2g, wup, fcw, wdown, fg)


def _ffn_sample_kernel(x1_ref, sh_ref, sc_ref, g_ref, n2g_ref, wup_ref, fcw_ref, wdown_ref,
                       fg_ref, st_ref, y_ref, ffn_ref):
    nseq, tlen, _ = x1_ref.shape
    rows = nseq * tlen
    x3 = x1_ref[...]
    h3 = _rms(x3, n2g_ref[...]) * (1.0 + sc_ref[...]) + sh_ref[...]
    h2 = h3.reshape(rows, D_MODEL).astype(_BF)
    a = _dot(h2, wup_ref[:, 0:D_FF])
    val = _dot(h2, wup_ref[:, D_FF:2 * D_FF])
    ac, new_st = _conv_seq(a, st_ref[...], fcw_ref[...], nseq, tlen)
    ffn_ref[...] = new_st
    hmid = (_gelu(ac) * val).astype(_BF)
    y = _dot(hmid, wdown_ref[...]).reshape(nseq, tlen, D_MODEL)
    x2 = x3 + g_ref[...] * y
    y_ref[...] = _rms(x2, fg_ref[...])


def _ffn_sample_call(x1, mod3, n2g, wup, fcw, wdown, fg, st, chunk):
    nseq, tlen, _ = x1.shape

    def mod_spec(col):
        return pl.BlockSpec((chunk, 1, D_MODEL), lambda s: (s, 0, col))

    def seq_spec(d1, d2):
        return pl.BlockSpec((chunk, d1, d2), lambda s: (s, 0, 0))

    return pl.pallas_call(
        _ffn_sample_kernel,
        grid=(nseq // chunk,),
        in_specs=[seq_spec(tlen, D_MODEL),
                  mod_spec(3), mod_spec(4), mod_spec(5),
                  _const_spec((1, D_MODEL)),
                  _const_spec((D_MODEL, 2 * D_FF)),
                  _const_spec((3, D_FF)),
                  _const_spec((D_FF, D_MODEL)),
                  _const_spec((1, D_MODEL)),
                  seq_spec(2, D_FF)],
        out_specs=[seq_spec(tlen, D_MODEL), seq_spec(2, D_FF)],
        out_shape=[jax.ShapeDtypeStruct((nseq, tlen, D_MODEL), _F32),
                   jax.ShapeDtypeStruct((nseq, 2, D_FF), _F32)],
        compiler_params=pltpu.CompilerParams(
            dimension_semantics=("arbitrary",),
            vmem_limit_bytes=VMEM_LIMIT),
        name="ffn_sample",
    )(x1, mod3, mod3, mod3, n2g, wup, fcw, wdown, fg, st)


PROMPT_TILE = 512
FFN_TILE = 1024
FFN_PARTS = 2
SAMPLE_CHUNK = 16
ATTN_AFTER_WINDOWS = 4
FFN_SAMPLE_CHUNK = 64


def kernel(x_prompt, x_sample, c_prompt, c_sample, state_conv, cache_k_win, cache_v_win,
           state_ffn_conv, norm1_g, norm2_g, w_ada, b_ada, w_in, conv_w, w_conv_out,
           attn_sinks, w_attn_out, w_mix_out, w_up, ffn_conv_w, w_down, final_g):
    depth = w_in.shape[0]
    assert depth == 1, "the final norm is fused into the (single) layer's FFN call"
    nsamp = x_sample.shape[0]
    bsz = x_prompt.shape[0]
    xp, xs = x_prompt, x_sample
    fg = final_g.reshape(1, D_MODEL)
    outs = [[] for _ in range(8)]
    for layer in range(depth):
        mod3, win, wco, wao, wmo = _mod_call(
            c_sample, c_prompt, w_ada[layer], b_ada[layer].reshape(1, -1),
            (w_in[layer], w_conv_out[layer], w_attn_out[layer], w_mix_out[layer]))
        n1g = norm1_g[layer].reshape(1, D_MODEL)
        n2g = norm2_g[layer].reshape(1, D_MODEL)
        cw = conv_w[layer]
        fcw = ffn_conv_w[layer]
        sinks = attn_sinks[layer]
        fg_l = fg
        def to_dim_major(win_state):
            return jnp.transpose(win_state, (0, 2, 3, 1)).reshape(-1, KV_WIDTH, WINDOW)

        def to_pos_major(t3):
            return jnp.transpose(t3.reshape(-1, N_KV_HEADS, HEAD_DIM, WINDOW), (0, 3, 1, 2))

        x1p, conv_p, k_p, v_p, wup, wdown = _mix_prompt_call(
            xp, mod3, nsamp, n1g, win, cw, wco, sinks, wao, wmo,
            (w_up[layer], w_down[layer]), PROMPT_TILE)
        x1s, conv_s, k_s, v_s = _mix_sample_call(
            xs, mod3, n1g, win, cw, wco, sinks, wao, wmo, state_conv[layer],
            to_dim_major(cache_k_win[layer]), to_dim_major(cache_v_win[layer]), SAMPLE_CHUNK)
        xp, ffn_p = _ffn_prompt_call(x1p, mod3, nsamp, n2g, wup, fcw, wdown, fg_l,
                                     FFN_TILE, FFN_PARTS)
        xs, ffn_s = _ffn_sample_call(x1s, mod3, n2g, wup, fcw, wdown, fg_l,
                                     state_ffn_conv[layer], FFN_SAMPLE_CHUNK)
        for lst, val in zip(outs, (conv_p, to_pos_major(k_p), to_pos_major(v_p), ffn_p,
                                   conv_s, to_pos_major(k_s), to_pos_major(v_s), ffn_s)):
            lst.append(val)
    return (xp, xs) + tuple(jnp.stack(lst, axis=0) for lst in outs)
```

```python
import functools

import jax
import jax.numpy as jnp
from jax import lax
from jax.experimental import pallas as pl
from jax.experimental.pallas import tpu as pltpu

D_MODEL = 1024
HEAD_DIM = 64
N_HEADS = 16
N_KV_HEADS = 4
WINDOW = 128
D_FF = 2816
KV_WIDTH = N_KV_HEADS * HEAD_DIM
IN_WIDTH = 3 * D_MODEL + D_MODEL + 2 * KV_WIDTH + 2 * D_MODEL
RMS_EPS = 1e-6
NEG_INF = -1e30
LANES = 128
HEADS_PER_COL = LANES // HEAD_DIM
N_QCOLS = N_HEADS * HEAD_DIM // LANES
CARRY_ROWS = 8
VMEM_LIMIT = 60 * 1024 * 1024

_OFF_B, _OFF_C, _OFF_X = 0, D_MODEL, 2 * D_MODEL
_OFF_Q = 3 * D_MODEL
_OFF_K = _OFF_Q + D_MODEL
_OFF_V = _OFF_K + KV_WIDTH
_OFF_GA = _OFF_V + KV_WIDTH
_OFF_GB = _OFF_GA + D_MODEL

_BF = jnp.bfloat16
_F32 = jnp.float32
_NT = (((1,), (1,)), ((), ()))


LOG2E = 1.4426950408889634
Q_SCALE = HEAD_DIM ** -0.5 * LOG2E


def _slope(head):
    return 2.0 ** (-8.0 * (head + 1) / N_HEADS) * LOG2E


def _rms(x, g):
    return x * lax.rsqrt(jnp.mean(x * x, axis=-1, keepdims=True) + RMS_EPS) * g


def _dot(a, b):
    return jnp.dot(a, b, preferred_element_type=_F32)


def _gelu(x):
    return 0.5 * x * (1.0 + lax.erf(x * (0.5 ** 0.5)))


def _split_heads(t2d, lo):
    rolled = pltpu.roll(t2d, HEAD_DIM, 1)
    zero = jnp.zeros_like(t2d)
    return (jnp.where(lo, t2d, zero), jnp.where(lo, zero, rolled),
            jnp.where(lo, rolled, zero), jnp.where(lo, zero, t2d))


def _softmax_parts(s, sink):
    m = jnp.maximum(jnp.max(s, axis=-1, keepdims=True), sink)
    p = jnp.exp2(s - m)
    denom = jnp.sum(p, axis=-1, keepdims=True) + jnp.exp2(sink - m)
    return p, 1.0 / denom


N_MOD = 6
MOD_COLS = 768


def _mod_kernel(cs_ref, cp_ref, w_ref, b_ref, *rest):
    n_cast = (len(rest) - 1) // 2
    cast_in, o_ref, cast_out = rest[:n_cast], rest[n_cast], rest[n_cast + 1:]
    for src, dst in zip(cast_in, cast_out):
        dst[...] = src[...].astype(_BF)
    w = w_ref[...].astype(_BF)
    n0 = cs_ref.shape[0]
    for c_ref, r0 in ((cs_ref, 0), (cp_ref, n0)):
        c = c_ref[...]
        a = (c * jax.nn.sigmoid(c)).astype(_BF)
        res = _dot(a, w) + b_ref[...]
        n = c.shape[0]
        o_ref[r0:r0 + n] = res.reshape(n, 1, res.shape[-1])


def _mod_call(c_sample, c_prompt, w_ada, b_ada, cast_along):
    ns, npr = c_sample.shape[0], c_prompt.shape[0]
    n = ns + npr
    steps = N_MOD * D_MODEL // MOD_COLS
    cast_specs = [_cast_stream_spec(w.shape, steps) for w in cast_along]
    return pl.pallas_call(
        _mod_kernel,
        grid=(steps,),
        in_specs=[pl.BlockSpec((ns, D_MODEL), lambda j: (0, 0)),
                  pl.BlockSpec((npr, D_MODEL), lambda j: (0, 0)),
                  pl.BlockSpec((D_MODEL, MOD_COLS), lambda j: (0, j)),
                  pl.BlockSpec((1, MOD_COLS), lambda j: (0, j))] + cast_specs,
        out_specs=[pl.BlockSpec((n, 1, MOD_COLS), lambda j: (0, 0, j))] + cast_specs,
        out_shape=[jax.ShapeDtypeStruct((n, 1, N_MOD * D_MODEL), _F32)]
        + [jax.ShapeDtypeStruct(w.shape, _BF) for w in cast_along],
        compiler_params=pltpu.CompilerParams(vmem_limit_bytes=VMEM_LIMIT),
        name="adaln_mod",
    )(c_sample, c_prompt, w_ada, b_ada, *cast_along)


def _mod_norm(x, g, sc, sh):
    return (_rms(x, g) * (1.0 + sc) + sh).astype(_BF)


def _mix_prompt_kernel(x_ref, sh_ref, sc_ref, g_ref, xn_ref, shn_ref, scn_ref, n1g_ref,
                       win_ref, cw_ref, wco_ref, sinks_ref, wao_ref, wmo_ref, *rest,
                       tiles_per_seq, n_cast):
    cast_in, rest = rest[:n_cast], rest[n_cast:]
    (x1_ref, conv_ref, k_ref, v_ref), rest = rest[:4], rest[4:]
    cast_out, (u_ext, k_ext, v_ext, o_scr, bias_scr, h_scr) = rest[:n_cast], rest[n_cast:]
    step = pl.program_id(0)
    i = step % tiles_per_seq
    t = x_ref.shape[0]
    nblk = t // WINDOW

    for src, dst in zip(cast_in, cast_out):
        dst[...] = src[...].astype(_BF)

    @pl.when(step == 0)
    def _():
        h_scr[...] = _mod_norm(x_ref[...], n1g_ref[...], sc_ref[...], sh_ref[...])
        row = lax.broadcasted_iota(jnp.int32, (WINDOW, 2 * WINDOW), 0)
        col = lax.broadcasted_iota(jnp.int32, (WINDOW, 2 * WINDOW), 1)
        dist = WINDOW + row - col
        valid = (dist >= 0) & (dist <= WINDOW)
        distf = dist.astype(_F32)
        for head in range(N_HEADS):
            bias_scr[head] = jnp.where(valid, -(_slope(head) * distf), NEG_INF)

    @pl.when(i == 0)
    def _():
        u_ext[0:CARRY_ROWS, :] = jnp.zeros((CARRY_ROWS, D_MODEL), _F32)
        k_ext[:, 0:WINDOW, :] = jnp.zeros((2 * N_KV_HEADS, WINDOW, LANES), _BF)
        v_ext[:, 0:WINDOW, :] = jnp.zeros((2 * N_KV_HEADS, WINDOW, LANES), _BF)

    def proj(lo, width):
        return _dot(h_scr[...], win_ref[:, lo:lo + width])

    def next_tile_norm():
        h_scr[...] = _mod_norm(xn_ref[...], n1g_ref[...], scn_ref[...], shn_ref[...])

    side = {}

    def conv_in():
        u = proj(_OFF_C, D_MODEL) * proj(_OFF_X, D_MODEL)
        u_ext[CARRY_ROWS:CARRY_ROWS + t, :] = u
        cw = cw_ref[...]
        side["uc"] = (u_ext[CARRY_ROWS - 2:CARRY_ROWS - 2 + t, :] * cw[0:1, :]
                      + u_ext[CARRY_ROWS - 1:CARRY_ROWS - 1 + t, :] * cw[1:2, :]
                      + u * cw[2:3, :])
        conv_ref[...] = u[t - 2:t, :]
        u_ext[0:CARRY_ROWS, :] = u[t - CARRY_ROWS:t, :]

    def conv_out():
        pb = proj(_OFF_B, D_MODEL)
        side["ya"] = _dot((pb * side.pop("uc")).astype(_BF), wco_ref[...])

    def gate_a():
        side["ga"] = jax.nn.sigmoid(proj(_OFF_GA, D_MODEL))

    def gate_b():
        side["gb"] = jax.nn.sigmoid(proj(_OFF_GB, D_MODEL))
        next_tile_norm()

    side_jobs = [conv_in, conv_out, gate_a, gate_b]

    q = (proj(_OFF_Q, D_MODEL) * Q_SCALE).astype(_BF)
    kf = proj(_OFF_K, KV_WIDTH)
    vf = proj(_OFF_V, KV_WIDTH)
    k_ref[...] = kf[t - WINDOW:t, :].T
    v_ref[...] = vf[t - WINDOW:t, :].T
    lane_t = lax.broadcasted_iota(jnp.int32, (t, LANES), 1)
    lo = lane_t < HEAD_DIM
    ones_lane = (HEAD_DIM, 0)
    for kc in range(KV_WIDTH // LANES):
        ks = _split_heads(kf[:, kc * LANES:(kc + 1) * LANES], lo)
        vs = _split_heads(vf[:, kc * LANES:(kc + 1) * LANES], lo)
        for j in range(4):
            k_ext[4 * kc + j, WINDOW:WINDOW + t, :] = ks[j].astype(_BF)
            v_one = jnp.where(lane_t == ones_lane[j % HEADS_PER_COL], 1.0, vs[j])
            v_ext[4 * kc + j, WINDOW:WINDOW + t, :] = v_one.astype(_BF)

    first_cols = lax.broadcasted_iota(jnp.int32, (WINDOW, 2 * WINDOW), 1) < WINDOW
    no_prev = first_cols & (i == 0)
    lo_q = lax.broadcasted_iota(jnp.int32, (WINDOW, LANES), 1) < HEAD_DIM
    for blk in range(nblk):
        r0 = blk * WINDOW
        for c in range(N_QCOLS):
            hkv = c // 2
            qc = q[r0:r0 + WINDOW, c * LANES:(c + 1) * LANES]
            o_pair = []
            for par in range(HEADS_PER_COL):
                head = HEADS_PER_COL * c + par
                kk = k_ext[2 * hkv + par, r0:r0 + 2 * WINDOW, :]
                s = lax.dot_general(qc, kk, _NT, preferred_element_type=_F32) + bias_scr[head]
                if blk == 0:
                    s = jnp.where(no_prev, NEG_INF, s)
                sink = sinks_ref[head] * LOG2E
                m = jnp.max(s, axis=-1, keepdims=True)
                p = jnp.exp2(s - m).astype(_BF)
                o_h = _dot(p, v_ext[2 * hkv + par, r0:r0 + 2 * WINDOW, :])
                col = ones_lane[par]
                denom = o_h[:, col:col + 1] + jnp.exp2(sink - m)
                o_pair.append(o_h * (1.0 / denom))
            o_c = jnp.where(lo_q, o_pair[0], o_pair[1])
            o_scr[r0:r0 + WINDOW, c * LANES:(c + 1) * LANES] = o_c.astype(_BF)
        for _ in range(-(-len(side_jobs) // (nblk - blk))):
            side_jobs.pop(0)()
    k_ext[:, 0:WINDOW, :] = k_ext[:, t:t + WINDOW, :]
    v_ext[:, 0:WINDOW, :] = v_ext[:, t:t + WINDOW, :]
    yb = _dot(o_scr[...], wao_ref[...])

    mixed = (side["ga"] * side["ya"] + side["gb"] * yb).astype(_BF)
    x1_ref[...] = x_ref[...] + g_ref[...] * _dot(mixed, wmo_ref[...])


def _const_spec(shape):
    nd = len(shape)
    return pl.BlockSpec(shape, lambda *_: (0,) * nd, pipeline_mode=pl.Buffered(1))


def _tile_specs(tile, nt, steps, mod_row0):
    def nxt(s):
        return jnp.minimum(s + 1, steps - 1)

    x_spec = pl.BlockSpec((None, tile, D_MODEL), lambda s: (s // nt, s % nt, 0))
    xn_spec = pl.BlockSpec((None, tile, D_MODEL), lambda s: (nxt(s) // nt, nxt(s) % nt, 0))

    def mod_spec(chunk):
        return pl.BlockSpec((None, 1, D_MODEL), lambda s: (mod_row0 + s // nt, 0, chunk))

    def modn_spec(chunk):
        return pl.BlockSpec((None, 1, D_MODEL), lambda s: (mod_row0 + nxt(s) // nt, 0, chunk))

    return x_spec, xn_spec, mod_spec, modn_spec


BF16_SUBLANES = 16


def _cast_stream_spec(shape, steps):
    rows, cols = shape
    for hold in range(1, steps + 1):
        nblocks, rem = divmod(steps, hold)
        if rem == 0 and rows % nblocks == 0 and (rows // nblocks) % BF16_SUBLANES == 0:
            return pl.BlockSpec((rows // nblocks, cols), lambda s: (s // hold, 0))
    raise ValueError(f"no bf16-aligned row split of {shape} over {steps} steps")


def _mix_prompt_call(x, mod3, mod_row0, n1g, win, cw, wco, sinks, wao, wmo, cast_along, tile):
    bsz, seq, _ = x.shape
    nt = seq // tile
    steps = bsz * nt
    x_spec, xn_spec, mod_spec, modn_spec = _tile_specs(tile, nt, steps, mod_row0)
    cast_specs = [_cast_stream_spec(w.shape, steps) for w in cast_along]

    def state_spec(d1, d2):
        return pl.BlockSpec((None, d1, d2), lambda s: (s // nt, 0, 0))

    return pl.pallas_call(
        functools.partial(_mix_prompt_kernel, tiles_per_seq=nt, n_cast=len(cast_along)),
        grid=(steps,),
        in_specs=[x_spec, mod_spec(0), mod_spec(1), mod_spec(2),
                  xn_spec, modn_spec(0), modn_spec(1),
                  _const_spec((1, D_MODEL)),
                  _const_spec((D_MODEL, IN_WIDTH)),
                  _const_spec((3, D_MODEL)),
                  _const_spec((D_MODEL, D_MODEL)),
                  pl.BlockSpec(memory_space=pltpu.SMEM),
                  _const_spec((D_MODEL, D_MODEL)),
                  _const_spec((D_MODEL, D_MODEL))] + cast_specs,
        out_specs=[x_spec,
                   state_spec(2, D_MODEL),
                   state_spec(KV_WIDTH, WINDOW),
                   state_spec(KV_WIDTH, WINDOW)] + cast_specs,
        out_shape=[jax.ShapeDtypeStruct((bsz, seq, D_MODEL), _F32),
                   jax.ShapeDtypeStruct((bsz, 2, D_MODEL), _F32),
                   jax.ShapeDtypeStruct((bsz, KV_WIDTH, WINDOW), _F32),
                   jax.ShapeDtypeStruct((bsz, KV_WIDTH, WINDOW), _F32)]
        + [jax.ShapeDtypeStruct(w.shape, _BF) for w in cast_along],
        scratch_shapes=[pltpu.VMEM((CARRY_ROWS + tile, D_MODEL), _F32),
                        pltpu.VMEM((2 * N_KV_HEADS, WINDOW + tile, LANES), _BF),
                        pltpu.VMEM((2 * N_KV_HEADS, WINDOW + tile, LANES), _BF),
                        pltpu.VMEM((tile, D_MODEL), _BF),
                        pltpu.VMEM((N_HEADS, WINDOW, 2 * WINDOW), _F32),
                        pltpu.VMEM((tile, D_MODEL), _BF)],
        compiler_params=pltpu.CompilerParams(
            dimension_semantics=("arbitrary",),
            vmem_limit_bytes=VMEM_LIMIT),
        name="mix_prompt",
    )(x, mod3, mod3, mod3, x, mod3, mod3, n1g, win, cw, wco, sinks, wao, wmo, *cast_along)


def _conv_seq(u2d, st, cw, nseq, tlen):
    width = u2d.shape[-1]
    shape3 = (nseq, tlen, width)
    u3 = u2d.reshape(shape3)
    r1 = pltpu.roll(u2d, 1, 0).reshape(shape3)
    r2 = pltpu.roll(u2d, 2, 0).reshape(shape3)
    tpos = lax.broadcasted_iota(jnp.int32, shape3, 1)
    st0 = jnp.broadcast_to(st[:, 0:1, :], shape3)
    st1 = jnp.broadcast_to(st[:, 1:2, :], shape3)
    um1 = jnp.where(tpos == 0, st1, r1)
    um2 = jnp.where(tpos == 0, st0, jnp.where(tpos == 1, st1, r2))
    y = um2 * cw[0:1, :] + um1 * cw[1:2, :] + u3 * cw[2:3, :]
    return y.reshape(nseq * tlen, width), u3[:, tlen - 2:tlen, :]


def _mix_sample_kernel(x_ref, sh_ref, sc_ref, g_ref, n1g_ref, win_ref, cw_ref, wco_ref,
                       sinks_ref, wao_ref, wmo_ref, st_ref, ck_ref, cv_ref,
                       x1_ref, conv_ref, k_ref, v_ref):
    nseq, tlen, _ = x_ref.shape
    rows = nseq * tlen

    assert rows == LANES, "one step's new keys must fill exactly one lane tile"

    x3 = x_ref[...]
    h3 = _rms(x3, n1g_ref[...]) * (1.0 + sc_ref[...]) + sh_ref[...]
    hf = h3.reshape(rows, D_MODEL)
    h_now = [hf.astype(_BF)]

    def proj(lo, width):
        return _dot(h_now[0], win_ref[:, lo:lo + width])

    def order_after(tile_f32):
        bits = pltpu.bitcast(tile_f32[0:8, :], jnp.uint32)
        zero = ((bits >> 16) >> 16).astype(_F32)
        h_now[0] = (hf + jnp.tile(zero, (rows // 8, D_MODEL // LANES))).astype(_BF)

    side = {}

    def job_conv_in():
        u = proj(_OFF_C, D_MODEL) * proj(_OFF_X, D_MODEL)
        side["uc"], new_st = _conv_seq(u, st_ref[...], cw_ref[...], nseq, tlen)
        conv_ref[...] = new_st

    def job_gate_b():
        side["pb"] = proj(_OFF_B, D_MODEL)

    def job_conv_out():
        side["ya"] = _dot((side.pop("pb") * side.pop("uc")).astype(_BF), wco_ref[...])

    def job_ga():
        side["ga"] = jax.nn.sigmoid(proj(_OFF_GA, D_MODEL))

    def job_gb():
        side["gb"] = jax.nn.sigmoid(proj(_OFF_GB, D_MODEL))

    keep = WINDOW - tlen
    qf = proj(_OFF_Q, D_MODEL) * Q_SCALE
    kvt_new = proj(_OFF_K, 2 * KV_WIDTH).T
    keep_old = lax.broadcasted_iota(jnp.int32, (KV_WIDTH, WINDOW), 1) < keep

    def window_update(s):
        shift = (keep - s * tlen) % LANES
        for cache_ref, out_ref, base in ((ck_ref, k_ref, 0), (cv_ref, v_ref, KV_WIDTH)):
            rolled = pltpu.roll(cache_ref[s], keep, 1)
            new = pltpu.roll(kvt_new[base:base + KV_WIDTH, :], shift, 1)
            win = jnp.where(keep_old, rolled, new)
            out_ref[s] = win
        return win

    lo = lax.broadcasted_iota(jnp.int32, (rows, LANES), 1) < HEAD_DIM
    zero = jnp.zeros((rows, LANES), _F32)
    n_tiles = KV_WIDTH // LANES
    heads_per_tile = N_HEADS // n_tiles
    m_rows = heads_per_tile * tlen
    row = lax.broadcasted_iota(jnp.int32, (m_rows, WINDOW), 0)
    col = lax.broadcasted_iota(jnp.int32, (m_rows, WINDOW), 1)
    dist_c = WINDOW + row % tlen - col
    valid_c = dist_c <= WINDOW
    shape3 = (nseq, m_rows, LANES)
    seq3 = lax.broadcasted_iota(jnp.int32, shape3, 0)
    row3 = lax.broadcasted_iota(jnp.int32, shape3, 1)
    col3 = lax.broadcasted_iota(jnp.int32, shape3, 2)
    dist_n = row3 % tlen - col3 % tlen
    valid_n = (col3 // tlen == seq3) & (dist_n >= 0)
    row1 = lax.broadcasted_iota(jnp.int32, (m_rows, 1), 0)

    def attention(kc):
        parts = []
        slope = jnp.zeros((m_rows, 1), _F32)
        sink = jnp.zeros((m_rows, 1), _F32)
        for j in range(heads_per_tile):
            head = heads_per_tile * kc + j
            c, par, half = head // HEADS_PER_COL, j % HEADS_PER_COL, j // (heads_per_tile // 2)
            qc = qf[:, c * LANES:(c + 1) * LANES]
            src = qc if par == half else pltpu.roll(qc, HEAD_DIM, 1)
            part = jnp.where(lo, src, zero) if half == 0 else jnp.where(lo, zero, src)
            parts.append(part.reshape(nseq, tlen, LANES))
            mine = (row1 >= j * tlen) & (row1 < (j + 1) * tlen)
            slope = jnp.where(mine, _slope(head), slope)
            sink = jnp.where(mine, sinks_ref[head] * LOG2E, sink)
        lhs = jnp.concatenate(parts, axis=1).astype(_BF)
        lhs2 = lhs.reshape(nseq * m_rows, LANES)
        rsl = slice(kc * LANES, (kc + 1) * LANES)
        k_new = kvt_new[rsl, :].astype(_BF)
        v_new = kvt_new[KV_WIDTH + kc * LANES:KV_WIDTH + (kc + 1) * LANES, :].astype(_BF)
        s_c = jnp.einsum('smk,skn->smn', lhs, ck_ref[:, rsl, :].astype(_BF),
                         preferred_element_type=_F32)
        s_n = _dot(lhs2, k_new).reshape(shape3)
        sc = jnp.concatenate(
            [s_c + jnp.where(valid_c, -(slope * dist_c.astype(_F32)), NEG_INF),
             s_n + jnp.where(valid_n, -(slope * dist_n.astype(_F32)), NEG_INF)], axis=2)
        p, inv = _softmax_parts(sc, sink)
        p = p.astype(_BF)
        r_c = jnp.einsum('smk,snk->smn', p[:, :, 0:WINDOW], cv_ref[:, rsl, :].astype(_BF),
                         preferred_element_type=_F32)
        r_n = lax.dot_general(p[:, :, WINDOW:].reshape(nseq * m_rows, LANES), v_new, _NT,
                              preferred_element_type=_F32).reshape(shape3)
        r = (r_c + r_n) * inv
        cols = {}
        for m2 in range(heads_per_tile // HEADS_PER_COL):
            c = (heads_per_tile // HEADS_PER_COL) * kc + m2
            half = m2 // (heads_per_tile // (2 * HEADS_PER_COL))
            r0 = r[:, (2 * m2) * tlen:(2 * m2 + 1) * tlen, :].reshape(rows, LANES)
            r1 = r[:, (2 * m2 + 1) * tlen:(2 * m2 + 2) * tlen, :].reshape(rows, LANES)
            if half == 0:
                cols[c] = jnp.where(lo, r0, pltpu.roll(r1, HEAD_DIM, 1))
            else:
                cols[c] = jnp.where(lo, pltpu.roll(r0, HEAD_DIM, 1), r1)
        return cols

    o_cols = {}
    job_conv_in()
    window_jobs = [job_gate_b, job_conv_out, job_ga, job_gb]
    group = -(-nseq // len(window_jobs))
    for s in range(nseq):
        if s == ATTN_AFTER_WINDOWS:
            for kc in range(n_tiles):
                o_cols.update(attention(kc))
        last_win = window_update(s)
        if window_jobs and (s + 1) % group == 0:
            window_jobs.pop(0)()
            order_after(last_win)
    while window_jobs:
        window_jobs.pop(0)()
    o = jnp.concatenate([o_cols[c] for c in range(N_QCOLS)], axis=1).astype(_BF)
    yb = _dot(o, wao_ref[...])

    mixed = (side["ga"] * side["ya"] + side["gb"] * yb).astype(_BF)
    y = _dot(mixed, wmo_ref[...]).reshape(nseq, tlen, D_MODEL)
    x1_ref[...] = x3 + g_ref[...] * y


def _mix_sample_call(x, mod3, n1g, win, cw, wco, sinks, wao, wmo, st, ck, cv, chunk):
    nseq, tlen, _ = x.shape
    assert LANES % tlen == 0 and chunk % (LANES // tlen) == 0

    def mod_spec(col):
        return pl.BlockSpec((chunk, 1, D_MODEL), lambda s: (s, 0, col))

    def seq_spec(d1, d2):
        return pl.BlockSpec((chunk, d1, d2), lambda s: (s, 0, 0))

    return pl.pallas_call(
        _mix_sample_kernel,
        grid=(nseq // chunk,),
        in_specs=[seq_spec(tlen, D_MODEL),
                  mod_spec(0), mod_spec(1), mod_spec(2),
                  _const_spec((1, D_MODEL)),
                  _const_spec((D_MODEL, IN_WIDTH)),
                  _const_spec((3, D_MODEL)),
                  _const_spec((D_MODEL, D_MODEL)),
                  pl.BlockSpec(memory_space=pltpu.SMEM),
                  _const_spec((D_MODEL, D_MODEL)),
                  _const_spec((D_MODEL, D_MODEL)),
                  seq_spec(2, D_MODEL),
                  seq_spec(KV_WIDTH, WINDOW),
                  seq_spec(KV_WIDTH, WINDOW)],
        out_specs=[seq_spec(tlen, D_MODEL),
                   seq_spec(2, D_MODEL),
                   seq_spec(KV_WIDTH, WINDOW),
                   seq_spec(KV_WIDTH, WINDOW)],
        out_shape=[jax.ShapeDtypeStruct((nseq, tlen, D_MODEL), _F32),
                   jax.ShapeDtypeStruct((nseq, 2, D_MODEL), _F32),
                   jax.ShapeDtypeStruct((nseq, KV_WIDTH, WINDOW), _F32),
                   jax.ShapeDtypeStruct((nseq, KV_WIDTH, WINDOW), _F32)],
        compiler_params=pltpu.CompilerParams(
            dimension_semantics=("arbitrary",),
            vmem_limit_bytes=VMEM_LIMIT),
        name="mix_sample",
    )(x, mod3, mod3, mod3, n1g, win, cw, wco, sinks, wao, wmo, st, ck, cv)


def _ffn_tail(x1, ac, val, g, wdown_ref, fg):
    hmid = (_gelu(ac) * val).astype(_BF)
    x2 = x1 + g * _dot(hmid, wdown_ref[...])
    return _rms(x2, fg)


def _ffn_prompt_kernel(x1_ref, sh_ref, sc_ref, g_ref, n2g_ref, wup_ref, fcw_ref, wdown_ref,
                       fg_ref, y_ref, ffn_ref, a_ext, *, parts):
    i = pl.program_id(1)
    t = x1_ref.shape[0]
    rows = t // parts

    @pl.when(i == 0)
    def _():
        a_ext[0:CARRY_ROWS, :] = jnp.zeros((CARRY_ROWS, D_FF), _F32)

    fcw = fcw_ref[...]

    def up(part):
        r0 = part * rows
        e0 = CARRY_ROWS + r0
        h2 = _mod_norm(x1_ref[r0:r0 + rows, :], n2g_ref[...], sc_ref[...], sh_ref[...])
        a = _dot(h2, wup_ref[:, 0:D_FF])
        val = _dot(h2, wup_ref[:, D_FF:2 * D_FF])
        a_ext[e0:e0 + rows, :] = a
        ac = (a_ext[e0 - 2:e0 - 2 + rows, :] * fcw[0:1, :]
              + a_ext[e0 - 1:e0 - 1 + rows, :] * fcw[1:2, :]
              + a * fcw[2:3, :])
        return ac, val

    def down(part, ac, val):
        r0 = part * rows
        y_ref[r0:r0 + rows, :] = _ffn_tail(x1_ref[r0:r0 + rows, :], ac, val, g_ref[...],
                                           wdown_ref, fg_ref[...])

    for part in range(parts):
        down(part, *up(part))
    ffn_ref[...] = a_ext[CARRY_ROWS + t - 2:CARRY_ROWS + t, :]
    a_ext[0:CARRY_ROWS, :] = a_ext[t:t + CARRY_ROWS, :]


def _ffn_prompt_call(x1, mod3, mod_row0, n2g, wup, fcw, wdown, fg, tile, parts):
    bsz, seq, _ = x1.shape

    def mod_spec(chunk):
        return pl.BlockSpec((None, 1, D_MODEL), lambda b, i: (mod_row0 + b, 0, chunk))

    x_spec = pl.BlockSpec((None, tile, D_MODEL), lambda b, i: (b, i, 0))
    return pl.pallas_call(
        functools.partial(_ffn_prompt_kernel, parts=parts),
        grid=(bsz, seq // tile),
        in_specs=[x_spec, mod_spec(3), mod_spec(4), mod_spec(5),
                  _const_spec((1, D_MODEL)),
                  _const_spec((D_MODEL, 2 * D_FF)),
                  _const_spec((3, D_FF)),
                  _const_spec((D_FF, D_MODEL)),
                  _const_spec((1, D_MODEL))],
        out_specs=[x_spec,
                   pl.BlockSpec((None, 2, D_FF), lambda b, i: (b, 0, 0))],
        out_shape=[jax.ShapeDtypeStruct((bsz, seq, D_MODEL), _F32),
                   jax.ShapeDtypeStruct((bsz, 2, D_FF), _F32)],
        scratch_shapes=[pltpu.VMEM((CARRY_ROWS + tile, D_FF), _F32)],
        compiler_params=pltpu.CompilerParams(
            dimension_semantics=("arbitrary", "arbitrary"),
            vmem_limit_bytes=VMEM_LIMIT),
        name="ffn_prompt",
    )(x1, mod3, mod3, mod3, n2g, wup, fcw, wdown, fg)


def _ffn_sample_kernel(x1_ref, sh_ref, sc_ref, g_ref, n2g_ref, wup_ref, fcw_ref, wdown_ref,
                       fg_ref, st_ref, y_ref, ffn_ref):
    nseq, tlen, _ = x1_ref.shape
    rows = nseq * tlen
    x3 = x1_ref[...]
    h3 = _rms(x3, n2g_ref[...]) * (1.0 + sc_ref[...]) + sh_ref[...]
    h2 = h3.reshape(rows, D_MODEL).astype(_BF)
    a = _dot(h2, wup_ref[:, 0:D_FF])
    val = _dot(h2, wup_ref[:, D_FF:2 * D_FF])
    ac, new_st = _conv_seq(a, st_ref[...], fcw_ref[...], nseq, tlen)
    ffn_ref[...] = new_st
    hmid = (_gelu(ac) * val).astype(_BF)
    y = _dot(hmid, wdown_ref[...]).reshape(nseq, tlen, D_MODEL)
    x2 = x3 + g_ref[...] * y
    y_ref[...] = _rms(x2, fg_ref[...])


def _ffn_sample_call(x1, mod3, n2g, wup, fcw, wdown, fg, st, chunk):
    nseq, tlen, _ = x1.shape

    def mod_spec(col):
        return pl.BlockSpec((chunk, 1, D_MODEL), lambda s: (s, 0, col))

    def seq_spec(d1, d2):
        return pl.BlockSpec((chunk, d1, d2), lambda s: (s, 0, 0))

    return pl.pallas_call(
        _ffn_sample_kernel,
        grid=(nseq // chunk,),
        in_specs=[seq_spec(tlen, D_MODEL),
                  mod_spec(3), mod_spec(4), mod_spec(5),
                  _const_spec((1, D_MODEL)),
                  _const_spec((D_MODEL, 2 * D_FF)),
                  _const_spec((3, D_FF)),
                  _const_spec((D_FF, D_MODEL)),
                  _const_spec((1, D_MODEL)),
                  seq_spec(2, D_FF)],
        out_specs=[seq_spec(tlen, D_MODEL), seq_spec(2, D_FF)],
        out_shape=[jax.ShapeDtypeStruct((nseq, tlen, D_MODEL), _F32),
                   jax.ShapeDtypeStruct((nseq, 2, D_FF), _F32)],
        compiler_params=pltpu.CompilerParams(
            dimension_semantics=("arbitrary",),
            vmem_limit_bytes=VMEM_LIMIT),
        name="ffn_sample",
    )(x1, mod3, mod3, mod3, n2g, wup, fcw, wdown, fg, st)


PROMPT_TILE = 512
FFN_TILE = 1024
FFN_PARTS = 2
SAMPLE_CHUNK = 16
ATTN_AFTER_WINDOWS = 4
FFN_SAMPLE_CHUNK = 64


def kernel(x_prompt, x_sample, c_prompt, c_sample, state_conv, cache_k_win, cache_v_win,
           state_ffn_conv, norm1_g, norm2_g, w_ada, b_ada, w_in, conv_w, w_conv_out,
           attn_sinks, w_attn_out, w_mix_out, w_up, ffn_conv_w, w_down, final_g):
    depth = w_in.shape[0]
    assert depth == 1, "the final norm is fused into the (single) layer's FFN call"
    nsamp = x_sample.shape[0]
    bsz = x_prompt.shape[0]
    xp, xs = x_prompt, x_sample
    fg = final_g.reshape(1, D_MODEL)
    outs = [[] for _ in range(8)]
    for layer in range(depth):
        mod3, win, wco, wao, wmo = _mod_call(
            c_sample, c_prompt, w_ada[layer], b_ada[layer].reshape(1, -1),
            (w_in[layer], w_conv_out[layer], w_attn_out[layer], w_mix_out[layer]))
        n1g = norm1_g[layer].reshape(1, D_MODEL)
        n2g = norm2_g[layer].reshape(1, D_MODEL)
        cw = conv_w[layer]
        fcw = ffn_conv_w[layer]
        sinks = attn_sinks[layer]
        fg_l = fg
        def to_dim_major(win_state):
            return jnp.transpose(win_state, (0, 2, 3, 1)).reshape(-1, KV_WIDTH, WINDOW)

        def to_pos_major(t3):
            return jnp.transpose(t3.reshape(-1, N_KV_HEADS, HEAD_DIM, WINDOW), (0, 3, 1, 2))

        x1p, conv_p, k_p, v_p, wup, wdown = _mix_prompt_call(
            xp, mod3, nsamp, n1g, win, cw, wco, sinks, wao, wmo,
            (w_up[layer], w_down[layer]), PROMPT_TILE)
        x1s, conv_s, k_s, v_s = _mix_sample_call(
            xs, mod3, n1g, win, cw, wco, sinks, wao, wmo, state_conv[layer],
            to_dim_major(cache_k_win[layer]), to_dim_major(cache_v_win[layer]), SAMPLE_CHUNK)
        xp, ffn_p = _ffn_prompt_call(x1p, mod3, nsamp, n2g, wup, fcw, wdown, fg_l,
                                     FFN_TILE, FFN_PARTS)
        xs, ffn_s = _ffn_sample_call(x1s, mod3, n2g, wup, fcw, wdown, fg_l,
                                     state_ffn_conv[layer], FFN_SAMPLE_CHUNK)
        for lst, val in zip(outs, (conv_p, to_pos_major(k_p), to_pos_major(v_p), ffn_p,
                                   conv_s, to_pos_major(k_s), to_pos_major(v_s), ffn_s)):
            lst.append(val)
    return (xp, xs) + tuple(jnp.stack(lst, axis=0) for lst in outs)
```

```python
import functools

import jax
import jax.numpy as jnp
from jax import lax
from jax.experimental import pallas as pl
from jax.experimental.pallas import tpu as pltpu

D_MODEL = 1024
HEAD_DIM = 64
N_HEADS = 16
N_KV_HEADS = 4
WINDOW = 128
D_FF = 2816
KV_WIDTH = N_KV_HEADS * HEAD_DIM
IN_WIDTH = 3 * D_MODEL + D_MODEL + 2 * KV_WIDTH + 2 * D_MODEL
RMS_EPS = 1e-6
NEG_INF = -1e30
LANES = 128
HEADS_PER_COL = LANES // HEAD_DIM
N_QCOLS = N_HEADS * HEAD_DIM // LANES
CARRY_ROWS = 8
VMEM_LIMIT = 60 * 1024 * 1024

_OFF_B, _OFF_C, _OFF_X = 0, D_MODEL, 2 * D_MODEL
_OFF_Q = 3 * D_MODEL
_OFF_K = _OFF_Q + D_MODEL
_OFF_V = _OFF_K + KV_WIDTH
_OFF_GA = _OFF_V + KV_WIDTH
_OFF_GB = _OFF_GA + D_MODEL

_BF = jnp.bfloat16
_F32 = jnp.float32
_NT = (((1,), (1,)), ((), ()))


LOG2E = 1.4426950408889634
Q_SCALE = HEAD_DIM ** -0.5 * LOG2E


def _slope(head):
    return 2.0 ** (-8.0 * (head + 1) / N_HEADS) * LOG2E


def _rms(x, g):
    return x * lax.rsqrt(jnp.mean(x * x, axis=-1, keepdims=True) + RMS_EPS) * g


def _dot(a, b):
    return jnp.dot(a, b, preferred_element_type=_F32)


def _gelu(x):
    return 0.5 * x * (1.0 + lax.erf(x * (0.5 ** 0.5)))


def _split_heads(t2d, lo):
    rolled = pltpu.roll(t2d, HEAD_DIM, 1)
    zero = jnp.zeros_like(t2d)
    return (jnp.where(lo, t2d, zero), jnp.where(lo, zero, rolled),
            jnp.where(lo, rolled, zero), jnp.where(lo, zero, t2d))


def _softmax_parts(s, sink):
    m = jnp.maximum(jnp.max(s, axis=-1, keepdims=True), sink)
    p = jnp.exp2(s - m)
    denom = jnp.sum(p, axis=-1, keepdims=True) + jnp.exp2(sink - m)
    return p, 1.0 / denom


N_MOD = 6
MOD_COLS = 768


def _mod_kernel(cs_ref, cp_ref, w_ref, b_ref, *rest):
    n_cast = (len(rest) - 1) // 2
    cast_in, o_ref, cast_out = rest[:n_cast], rest[n_cast], rest[n_cast + 1:]
    for src, dst in zip(cast_in, cast_out):
        dst[...] = src[...].astype(_BF)
    w = w_ref[...].astype(_BF)
    n0 = cs_ref.shape[0]
    for c_ref, r0 in ((cs_ref, 0), (cp_ref, n0)):
        c = c_ref[...]
        a = (c * jax.nn.sigmoid(c)).astype(_BF)
        res = _dot(a, w) + b_ref[...]
        n = c.shape[0]
        o_ref[r0:r0 + n] = res.reshape(n, 1, res.shape[-1])


def _mod_call(c_sample, c_prompt, w_ada, b_ada, cast_along):
    ns, npr = c_sample.shape[0], c_prompt.shape[0]
    n = ns + npr
    steps = N_MOD * D_MODEL // MOD_COLS
    cast_specs = [_cast_stream_spec(w.shape, steps) for w in cast_along]
    return pl.pallas_call(
        _mod_kernel,
        grid=(steps,),
        in_specs=[pl.BlockSpec((ns, D_MODEL), lambda j: (0, 0)),
                  pl.BlockSpec((npr, D_MODEL), lambda j: (0, 0)),
                  pl.BlockSpec((D_MODEL, MOD_COLS), lambda j: (0, j)),
                  pl.BlockSpec((1, MOD_COLS), lambda j: (0, j))] + cast_specs,
        out_specs=[pl.BlockSpec((n, 1, MOD_COLS), lambda j: (0, 0, j))] + cast_specs,
        out_shape=[jax.ShapeDtypeStruct((n, 1, N_MOD * D_MODEL), _F32)]
        + [jax.ShapeDtypeStruct(w.shape, _BF) for w in cast_along],
        compiler_params=pltpu.CompilerParams(vmem_limit_bytes=VMEM_LIMIT),
        name="adaln_mod",
    )(c_sample, c_prompt, w_ada, b_ada, *cast_along)


def _mod_norm(x, g, sc, sh):
    return (_rms(x, g) * (1.0 + sc) + sh).astype(_BF)


def _mix_prompt_kernel(x_ref, sh_ref, sc_ref, g_ref, xn_ref, shn_ref, scn_ref, n1g_ref,
                       win_ref, cw_ref, wco_ref, sinks_ref, wao_ref, wmo_ref, *rest,
                       tiles_per_seq, n_cast):
    cast_in, rest = rest[:n_cast], rest[n_cast:]
    (x1_ref, conv_ref, k_ref, v_ref), rest = rest[:4], rest[4:]
    cast_out, (u_ext, k_ext, v_ext, o_scr, bias_scr, h_scr) = rest[:n_cast], rest[n_cast:]
    step = pl.program_id(0)
    i = step % tiles_per_seq
    t = x_ref.shape[0]
    nblk = t // WINDOW

    for src, dst in zip(cast_in, cast_out):
        dst[...] = src[...].astype(_BF)

    @pl.when(step == 0)
    def _():
        h_scr[...] = _mod_norm(x_ref[...], n1g_ref[...], sc_ref[...], sh_ref[...])
        row = lax.broadcasted_iota(jnp.int32, (WINDOW, 2 * WINDOW), 0)
        col = lax.broadcasted_iota(jnp.int32, (WINDOW, 2 * WINDOW), 1)
        dist = WINDOW + row - col
        valid = (dist >= 0) & (dist <= WINDOW)
        distf = dist.astype(_F32)
        for head in range(N_HEADS):
            bias_scr[head] = jnp.where(valid, -(_slope(head) * distf), NEG_INF)

    @pl.when(i == 0)
    def _():
        u_ext[0:CARRY_ROWS, :] = jnp.zeros((CARRY_ROWS, D_MODEL), _F32)
        k_ext[:, 0:WINDOW, :] = jnp.zeros((N_KV_HEADS, WINDOW, LANES), _BF)
        v_ext[:, 0:WINDOW, :] = jnp.zeros((N_KV_HEADS, WINDOW, 2 * LANES), _BF)

    def proj(lo, width):
        return _dot(h_scr[...], win_ref[:, lo:lo + width])

    def next_tile_norm():
        h_scr[...] = _mod_norm(xn_ref[...], n1g_ref[...], scn_ref[...], shn_ref[...])

    side = {}

    def conv_in():
        u = proj(_OFF_C, D_MODEL) * proj(_OFF_X, D_MODEL)
        u_ext[CARRY_ROWS:CARRY_ROWS + t, :] = u
        cw = cw_ref[...]
        side["uc"] = (u_ext[CARRY_ROWS - 2:CARRY_ROWS - 2 + t, :] * cw[0:1, :]
                      + u_ext[CARRY_ROWS - 1:CARRY_ROWS - 1 + t, :] * cw[1:2, :]
                      + u * cw[2:3, :])
        conv_ref[...] = u[t - 2:t, :]
        u_ext[0:CARRY_ROWS, :] = u[t - CARRY_ROWS:t, :]

    def conv_out():
        pb = proj(_OFF_B, D_MODEL)
        side["ya"] = _dot((pb * side.pop("uc")).astype(_BF), wco_ref[...])

    def gate_a():
        side["ga"] = jax.nn.sigmoid(proj(_OFF_GA, D_MODEL))

    def gate_b():
        side["gb"] = jax.nn.sigmoid(proj(_OFF_GB, D_MODEL))
        next_tile_norm()

    side_jobs = [conv_in, conv_out, gate_a, gate_b]

    qf = proj(_OFF_Q, D_MODEL) * Q_SCALE
    q = qf.astype(_BF)
    q_swap = jnp.concatenate(
        [pltpu.roll(qf[:, c * LANES:(c + 1) * LANES], HEAD_DIM, 1) for c in range(N_QCOLS)],
        axis=1).astype(_BF)
    kf = proj(_OFF_K, KV_WIDTH)
    vf = proj(_OFF_V, KV_WIDTH)
    k_ref[...] = kf[t - WINDOW:t, :].T
    v_ref[...] = vf[t - WINDOW:t, :].T
    lo = lax.broadcasted_iota(jnp.int32, (t, LANES), 1) < HEAD_DIM
    one = jnp.ones((t, LANES), _F32)
    for kc in range(KV_WIDTH // LANES):
        ks = _split_heads(kf[:, kc * LANES:(kc + 1) * LANES], lo)
        vs = _split_heads(vf[:, kc * LANES:(kc + 1) * LANES], lo)
        for j in range(2):
            hkv = 2 * kc + j
            k_ext[hkv, WINDOW:WINDOW + t, :] = ks[2 * j].astype(_BF)
            v_ext[hkv, WINDOW:WINDOW + t, :] = jnp.concatenate(
                [jnp.where(lo, vs[2 * j], one), jnp.where(lo, one, vs[2 * j + 1])],
                axis=1).astype(_BF)

    first_cols = lax.broadcasted_iota(jnp.int32, (WINDOW, 2 * WINDOW), 1) < WINDOW
    no_prev = first_cols & (i == 0)
    lo_q = lax.broadcasted_iota(jnp.int32, (WINDOW, LANES), 1) < HEAD_DIM
    group = N_HEADS // N_KV_HEADS
    for blk in range(nblk):
        r0 = blk * WINDOW
        for hkv in range(N_KV_HEADS):
            lhs = []
            for j in range(group):
                c = (group * hkv + j) // HEADS_PER_COL
                src = q if j % HEADS_PER_COL == 0 else q_swap
                lhs.append(src[r0:r0 + WINDOW, c * LANES:(c + 1) * LANES])
            s4 = lax.dot_general(jnp.concatenate(lhs, axis=0), k_ext[hkv, r0:r0 + 2 * WINDOW, :],
                                 _NT, preferred_element_type=_F32)
            ps, sink_terms = [], []
            for j in range(group):
                head = group * hkv + j
                s = s4[j * WINDOW:(j + 1) * WINDOW, :] + bias_scr[head]
                if blk == 0:
                    s = jnp.where(no_prev, NEG_INF, s)
                m = jnp.max(s, axis=-1, keepdims=True)
                ps.append(jnp.exp2(s - m).astype(_BF))
                sink_terms.append(jnp.exp2(sinks_ref[head] * LOG2E - m))
            o4 = _dot(jnp.concatenate(ps, axis=0), v_ext[hkv, r0:r0 + 2 * WINDOW, :])
            for m2 in range(group // HEADS_PER_COL):
                c = (group * hkv) // HEADS_PER_COL + m2
                even = o4[(2 * m2) * WINDOW:(2 * m2 + 1) * WINDOW, :]
                odd = o4[(2 * m2 + 1) * WINDOW:(2 * m2 + 2) * WINDOW, :]
                out = jnp.where(lo_q, even[:, 0:LANES], odd[:, LANES:2 * LANES])
                rsum = jnp.where(lo_q, even[:, LANES:2 * LANES], odd[:, 0:LANES])
                sterm = jnp.where(lo_q, sink_terms[2 * m2], sink_terms[2 * m2 + 1])
                o_scr[r0:r0 + WINDOW, c * LANES:(c + 1) * LANES] = (
                    out * (1.0 / (rsum + sterm))).astype(_BF)
        for _ in range(-(-len(side_jobs) // (nblk - blk))):
            side_jobs.pop(0)()
    k_ext[:, 0:WINDOW, :] = k_ext[:, t:t + WINDOW, :]
    v_ext[:, 0:WINDOW, :] = v_ext[:, t:t + WINDOW, :]
    yb = _dot(o_scr[...], wao_ref[...])

    mixed = (side["ga"] * side["ya"] + side["gb"] * yb).astype(_BF)
    x1_ref[...] = x_ref[...] + g_ref[...] * _dot(mixed, wmo_ref[...])


def _const_spec(shape):
    nd = len(shape)
    return pl.BlockSpec(shape, lambda *_: (0,) * nd, pipeline_mode=pl.Buffered(1))


def _tile_specs(tile, nt, steps, mod_row0):
    def nxt(s):
        return jnp.minimum(s + 1, steps - 1)

    x_spec = pl.BlockSpec((None, tile, D_MODEL), lambda s: (s // nt, s % nt, 0))
    xn_spec = pl.BlockSpec((None, tile, D_MODEL), lambda s: (nxt(s) // nt, nxt(s) % nt, 0))

    def mod_spec(chunk):
        return pl.BlockSpec((None, 1, D_MODEL), lambda s: (mod_row0 + s // nt, 0, chunk))

    def modn_spec(chunk):
        return pl.BlockSpec((None, 1, D_MODEL), lambda s: (mod_row0 + nxt(s) // nt, 0, chunk))

    return x_spec, xn_spec, mod_spec, modn_spec


BF16_SUBLANES = 16


def _cast_stream_spec(shape, steps):
    rows, cols = shape
    for hold in range(1, steps + 1):
        nblocks, rem = divmod(steps, hold)
        if rem == 0 and rows % nblocks == 0 and (rows // nblocks) % BF16_SUBLANES == 0:
            return pl.BlockSpec((rows // nblocks, cols), lambda s: (s // hold, 0))
    raise ValueError(f"no bf16-aligned row split of {shape} over {steps} steps")


def _mix_prompt_call(x, mod3, mod_row0, n1g, win, cw, wco, sinks, wao, wmo, cast_along, tile):
    bsz, seq, _ = x.shape
    nt = seq // tile
    steps = bsz * nt
    x_spec, xn_spec, mod_spec, modn_spec = _tile_specs(tile, nt, steps, mod_row0)
    cast_specs = [_cast_stream_spec(w.shape, steps) for w in cast_along]

    def state_spec(d1, d2):
        return pl.BlockSpec((None, d1, d2), lambda s: (s // nt, 0, 0))

    return pl.pallas_call(
        functools.partial(_mix_prompt_kernel, tiles_per_seq=nt, n_cast=len(cast_along)),
        grid=(steps,),
        in_specs=[x_spec, mod_spec(0), mod_spec(1), mod_spec(2),
                  xn_spec, modn_spec(0), modn_spec(1),
                  _const_spec((1, D_MODEL)),
                  _const_spec((D_MODEL, IN_WIDTH)),
                  _const_spec((3, D_MODEL)),
                  _const_spec((D_MODEL, D_MODEL)),
                  pl.BlockSpec(memory_space=pltpu.SMEM),
                  _const_spec((D_MODEL, D_MODEL)),
                  _const_spec((D_MODEL, D_MODEL))] + cast_specs,
        out_specs=[x_spec,
                   state_spec(2, D_MODEL),
                   state_spec(KV_WIDTH, WINDOW),
                   state_spec(KV_WIDTH, WINDOW)] + cast_specs,
        out_shape=[jax.ShapeDtypeStruct((bsz, seq, D_MODEL), _F32),
                   jax.ShapeDtypeStruct((bsz, 2, D_MODEL), _F32),
                   jax.ShapeDtypeStruct((bsz, KV_WIDTH, WINDOW), _F32),
                   jax.ShapeDtypeStruct((bsz, KV_WIDTH, WINDOW), _F32)]
        + [jax.ShapeDtypeStruct(w.shape, _BF) for w in cast_along],
        scratch_shapes=[pltpu.VMEM((CARRY_ROWS + tile, D_MODEL), _F32),
                        pltpu.VMEM((N_KV_HEADS, WINDOW + tile, LANES), _BF),
                        pltpu.VMEM((N_KV_HEADS, WINDOW + tile, 2 * LANES), _BF),
                        pltpu.VMEM((tile, D_MODEL), _BF),
                        pltpu.VMEM((N_HEADS, WINDOW, 2 * WINDOW), _F32),
                        pltpu.VMEM((tile, D_MODEL), _BF)],
        compiler_params=pltpu.CompilerParams(
            dimension_semantics=("arbitrary",),
            vmem_limit_bytes=VMEM_LIMIT),
        name="mix_prompt",
    )(x, mod3, mod3, mod3, x, mod3, mod3, n1g, win, cw, wco, sinks, wao, wmo, *cast_along)


def _conv_seq(u2d, st, cw, nseq, tlen):
    width = u2d.shape[-1]
    shape3 = (nseq, tlen, width)
    u3 = u2d.reshape(shape3)
    r1 = pltpu.roll(u2d, 1, 0).reshape(shape3)
    r2 = pltpu.roll(u2d, 2, 0).reshape(shape3)
    tpos = lax.broadcasted_iota(jnp.int32, shape3, 1)
    st0 = jnp.broadcast_to(st[:, 0:1, :], shape3)
    st1 = jnp.broadcast_to(st[:, 1:2, :], shape3)
    um1 = jnp.where(tpos == 0, st1, r1)
    um2 = jnp.where(tpos == 0, st0, jnp.where(tpos == 1, st1, r2))
    y = um2 * cw[0:1, :] + um1 * cw[1:2, :] + u3 * cw[2:3, :]
    return y.reshape(nseq * tlen, width), u3[:, tlen - 2:tlen, :]


def _mix_sample_kernel(x_ref, sh_ref, sc_ref, g_ref, n1g_ref, win_ref, cw_ref, wco_ref,
                       sinks_ref, wao_ref, wmo_ref, st_ref, ck_ref, cv_ref,
                       x1_ref, conv_ref, k_ref, v_ref):
    nseq, tlen, _ = x_ref.shape
    rows = nseq * tlen

    assert rows == LANES, "one step's new keys must fill exactly one lane tile"

    x3 = x_ref[...]
    h3 = _rms(x3, n1g_ref[...]) * (1.0 + sc_ref[...]) + sh_ref[...]
    hf = h3.reshape(rows, D_MODEL)
    h_now = [hf.astype(_BF)]

    def proj(lo, width):
        return _dot(h_now[0], win_ref[:, lo:lo + width])

    def order_after(tile_f32):
        bits = pltpu.bitcast(tile_f32[0:8, :], jnp.uint32)
        zero = ((bits >> 16) >> 16).astype(_F32)
        h_now[0] = (hf + jnp.tile(zero, (rows // 8, D_MODEL // LANES))).astype(_BF)

    side = {}

    def job_conv_in():
        u = proj(_OFF_C, D_MODEL) * proj(_OFF_X, D_MODEL)
        side["uc"], new_st = _conv_seq(u, st_ref[...], cw_ref[...], nseq, tlen)
        conv_ref[...] = new_st

    def job_gate_b():
        side["pb"] = proj(_OFF_B, D_MODEL)

    def job_conv_out():
        side["ya"] = _dot((side.pop("pb") * side.pop("uc")).astype(_BF), wco_ref[...])

    def job_ga():
        side["ga"] = jax.nn.sigmoid(proj(_OFF_GA, D_MODEL))

    def job_gb():
        side["gb"] = jax.nn.sigmoid(proj(_OFF_GB, D_MODEL))

    keep = WINDOW - tlen
    qf = proj(_OFF_Q, D_MODEL) * Q_SCALE
    kvt_new = proj(_OFF_K, 2 * KV_WIDTH).T
    keep_old = lax.broadcasted_iota(jnp.int32, (KV_WIDTH, WINDOW), 1) < keep

    def window_update(s):
        shift = (keep - s * tlen) % LANES
        for cache_ref, out_ref, base in ((ck_ref, k_ref, 0), (cv_ref, v_ref, KV_WIDTH)):
            rolled = pltpu.roll(cache_ref[s], keep, 1)
            new = pltpu.roll(kvt_new[base:base + KV_WIDTH, :], shift, 1)
            win = jnp.where(keep_old, rolled, new)
            out_ref[s] = win
        return win

    lo = lax.broadcasted_iota(jnp.int32, (rows, LANES), 1) < HEAD_DIM
    zero = jnp.zeros((rows, LANES), _F32)
    n_tiles = KV_WIDTH // LANES
    heads_per_tile = N_HEADS // n_tiles
    m_rows = heads_per_tile * tlen
    row = lax.broadcasted_iota(jnp.int32, (m_rows, WINDOW), 0)
    col = lax.broadcasted_iota(jnp.int32, (m_rows, WINDOW), 1)
    dist_c = WINDOW + row % tlen - col
    valid_c = dist_c <= WINDOW
    shape3 = (nseq, m_rows, LANES)
    seq3 = lax.broadcasted_iota(jnp.int32, shape3, 0)
    row3 = lax.broadcasted_iota(jnp.int32, shape3, 1)
    col3 = lax.broadcasted_iota(jnp.int32, shape3, 2)
    dist_n = row3 % tlen - col3 % tlen
    valid_n = (col3 // tlen == seq3) & (dist_n >= 0)
    row1 = lax.broadcasted_iota(jnp.int32, (m_rows, 1), 0)

    def attention(kc):
        parts = []
        slope = jnp.zeros((m_rows, 1), _F32)
        sink = jnp.zeros((m_rows, 1), _F32)
        for j in range(heads_per_tile):
            head = heads_per_tile * kc + j
            c, par, half = head // HEADS_PER_COL, j % HEADS_PER_COL, j // (heads_per_tile // 2)
            qc = qf[:, c * LANES:(c + 1) * LANES]
            src = qc if par == half else pltpu.roll(qc, HEAD_DIM, 1)
            part = jnp.where(lo, src, zero) if half == 0 else jnp.where(lo, zero, src)
            parts.append(part.reshape(nseq, tlen, LANES))
            mine = (row1 >= j * tlen) & (row1 < (j + 1) * tlen)
            slope = jnp.where(mine, _slope(head), slope)
            sink = jnp.where(mine, sinks_ref[head] * LOG2E, sink)
        lhs = jnp.concatenate(parts, axis=1).astype(_BF)
        lhs2 = lhs.reshape(nseq * m_rows, LANES)
        rsl = slice(kc * LANES, (kc + 1) * LANES)
        k_new = kvt_new[rsl, :].astype(_BF)
        v_new = kvt_new[KV_WIDTH + kc * LANES:KV_WIDTH + (kc + 1) * LANES, :].astype(_BF)
        s_c = jnp.einsum('smk,skn->smn', lhs, ck_ref[:, rsl, :].astype(_BF),
                         preferred_element_type=_F32)
        s_n = _dot(lhs2, k_new).reshape(shape3)
        sc = jnp.concatenate(
            [s_c + jnp.where(valid_c, -(slope * dist_c.astype(_F32)), NEG_INF),
             s_n + jnp.where(valid_n, -(slope * dist_n.astype(_F32)), NEG_INF)], axis=2)
        p, inv = _softmax_parts(sc, sink)
        p = p.astype(_BF)
        r_c = jnp.einsum('smk,snk->smn', p[:, :, 0:WINDOW], cv_ref[:, rsl, :].astype(_BF),
                         preferred_element_type=_F32)
        r_n = lax.dot_general(p[:, :, WINDOW:].reshape(nseq * m_rows, LANES), v_new, _NT,
                              preferred_element_type=_F32).reshape(shape3)
        r = (r_c + r_n) * inv
        cols = {}
        for m2 in range(heads_per_tile // HEADS_PER_COL):
            c = (heads_per_tile // HEADS_PER_COL) * kc + m2
            half = m2 // (heads_per_tile // (2 * HEADS_PER_COL))
            r0 = r[:, (2 * m2) * tlen:(2 * m2 + 1) * tlen, :].reshape(rows, LANES)
            r1 = r[:, (2 * m2 + 1) * tlen:(2 * m2 + 2) * tlen, :].reshape(rows, LANES)
            if half == 0:
                cols[c] = jnp.where(lo, r0, pltpu.roll(r1, HEAD_DIM, 1))
            else:
                cols[c] = jnp.where(lo, pltpu.roll(r0, HEAD_DIM, 1), r1)
        return cols

    o_cols = {}
    job_conv_in()
    window_jobs = [job_gate_b, job_conv_out, job_ga, job_gb]
    group = -(-nseq // len(window_jobs))
    for s in range(nseq):
        if s == ATTN_AFTER_WINDOWS:
            for kc in range(n_tiles):
                o_cols.update(attention(kc))
        last_win = window_update(s)
        if window_jobs and (s + 1) % group == 0:
            window_jobs.pop(0)()
            order_after(last_win)
    while window_jobs:
        window_jobs.pop(0)()
    o = jnp.concatenate([o_cols[c] for c in range(N_QCOLS)], axis=1).astype(_BF)
    yb = _dot(o, wao_ref[...])

    mixed = (side["ga"] * side["ya"] + side["gb"] * yb).astype(_BF)
    y = _dot(mixed, wmo_ref[...]).reshape(nseq, tlen, D_MODEL)
    x1_ref[...] = x3 + g_ref[...] * y


def _mix_sample_call(x, mod3, n1g, win, cw, wco, sinks, wao, wmo, st, ck, cv, chunk):
    nseq, tlen, _ = x.shape
    assert LANES % tlen == 0 and chunk % (LANES // tlen) == 0

    def mod_spec(col):
        return pl.BlockSpec((chunk, 1, D_MODEL), lambda s: (s, 0, col))

    def seq_spec(d1, d2):
        return pl.BlockSpec((chunk, d1, d2), lambda s: (s, 0, 0))

    return pl.pallas_call(
        _mix_sample_kernel,
        grid=(nseq // chunk,),
        in_specs=[seq_spec(tlen, D_MODEL),
                  mod_spec(0), mod_spec(1), mod_spec(2),
                  _const_spec((1, D_MODEL)),
                  _const_spec((D_MODEL, IN_WIDTH)),
                  _const_spec((3, D_MODEL)),
                  _const_spec((D_MODEL, D_MODEL)),
                  pl.BlockSpec(memory_space=pltpu.SMEM),
                  _const_spec((D_MODEL, D_MODEL)),
                  _const_spec((D_MODEL, D_MODEL)),
                  seq_spec(2, D_MODEL),
                  seq_spec(KV_WIDTH, WINDOW),
                  seq_spec(KV_WIDTH, WINDOW)],
        out_specs=[seq_spec(tlen, D_MODEL),
                   seq_spec(2, D_MODEL),
                   seq_spec(KV_WIDTH, WINDOW),
                   seq_spec(KV_WIDTH, WINDOW)],
        out_shape=[jax.ShapeDtypeStruct((nseq, tlen, D_MODEL), _F32),
                   jax.ShapeDtypeStruct((nseq, 2, D_MODEL), _F32),
                   jax.ShapeDtypeStruct((nseq, KV_WIDTH, WINDOW), _F32),
                   jax.ShapeDtypeStruct((nseq, KV_WIDTH, WINDOW), _F32)],
        compiler_params=pltpu.CompilerParams(
            dimension_semantics=("arbitrary",),
            vmem_limit_bytes=VMEM_LIMIT),
        name="mix_sample",
    )(x, mod3, mod3, mod3, n1g, win, cw, wco, sinks, wao, wmo, st, ck, cv)


def _ffn_tail(x1, ac, val, g, wdown_ref, fg):
    hmid = (_gelu(ac) * val).astype(_BF)
    x2 = x1 + g * _dot(hmid, wdown_ref[...])
    return _rms(x2, fg)


def _ffn_prompt_kernel(x1_ref, sh_ref, sc_ref, g_ref, n2g_ref, wup_ref, fcw_ref, wdown_ref,
                       fg_ref, y_ref, ffn_ref, a_ext, *, parts):
    i = pl.program_id(1)
    t = x1_ref.shape[0]
    rows = t // parts

    @pl.when(i == 0)
    def _():
        a_ext[0:CARRY_ROWS, :] = jnp.zeros((CARRY_ROWS, D_FF), _F32)

    fcw = fcw_ref[...]

    def up(part):
        r0 = part * rows
        e0 = CARRY_ROWS + r0
        h2 = _mod_norm(x1_ref[r0:r0 + rows, :], n2g_ref[...], sc_ref[...], sh_ref[...])
        a = _dot(h2, wup_ref[:, 0:D_FF])
        val = _dot(h2, wup_ref[:, D_FF:2 * D_FF])
        a_ext[e0:e0 + rows, :] = a
        ac = (a_ext[e0 - 2:e0 - 2 + rows, :] * fcw[0:1, :]
              + a_ext[e0 - 1:e0 - 1 + rows, :] * fcw[1:2, :]
              + a * fcw[2:3, :])
        return ac, val

    def down(part, ac, val):
        r0 = part * rows
        y_ref[r0:r0 + rows, :] = _ffn_tail(x1_ref[r0:r0 + rows, :], ac, val, g_ref[...],
                                           wdown_ref, fg_ref[...])

    for part in range(parts):
        down(part, *up(part))
    ffn_ref[...] = a_ext[CARRY_ROWS + t - 2:CARRY_ROWS + t, :]
    a_ext[0:CARRY_ROWS, :] = a_ext[t:t + CARRY_ROWS, :]


def _ffn_prompt_call(x1, mod3, mod_row0, n2g, wup, fcw, wdown, fg, tile, parts):
    bsz, seq, _ = x1.shape

    def mod_spec(chunk):
        return pl.BlockSpec((None, 1, D_MODEL), lambda b, i: (mod_row0 + b, 0, chunk))

    x_spec = pl.BlockSpec((None, tile, D_MODEL), lambda b, i: (b, i, 0))
    return pl.pallas_call(
        functools.partial(_ffn_prompt_kernel, parts=parts),
        grid=(bsz, seq // tile),
        in_specs=[x_spec, mod_spec(3), mod_spec(4), mod_spec(5),
                  _const_spec((1, D_MODEL)),
                  _const_spec((D_MODEL, 2 * D_FF)),
                  _const_spec((3, D_FF)),
                  _const_spec((D_FF, D_MODEL)),
                  _const_spec((1, D_MODEL))],
        out_specs=[x_spec,
                   pl.BlockSpec((None, 2, D_FF), lambda b, i: (b, 0, 0))],
        out_shape=[jax.ShapeDtypeStruct((bsz, seq, D_MODEL), _F32),
                   jax.ShapeDtypeStruct((bsz, 2, D_FF), _F32)],
        scratch_shapes=[pltpu.VMEM((CARRY_ROWS + tile, D_FF), _F32)],
        compiler_params=pltpu.CompilerParams(
            dimension_semantics=("arbitrary", "arbitrary"),
            vmem_limit_bytes=VMEM_LIMIT),
        name="ffn_prompt",
    )(x1, mod3, mod3, mod3, n2g, wup, fcw, wdown, fg)


def _ffn_sample_kernel(x1_ref, sh_ref, sc_ref, g_ref, n2g_ref, wup_ref, fcw_ref, wdown_ref,
                       fg_ref, st_ref, y_ref, ffn_ref):
    nseq, tlen, _ = x1_ref.shape
    rows = nseq * tlen
    x3 = x1_ref[...]
    h3 = _rms(x3, n2g_ref[...]) * (1.0 + sc_ref[...]) + sh_ref[...]
    h2 = h3.reshape(rows, D_MODEL).astype(_BF)
    a = _dot(h2, wup_ref[:, 0:D_FF])
    val = _dot(h2, wup_ref[:, D_FF:2 * D_FF])
    ac, new_st = _conv_seq(a, st_ref[...], fcw_ref[...], nseq, tlen)
    ffn_ref[...] = new_st
    hmid = (_gelu(ac) * val).astype(_BF)
    y = _dot(hmid, wdown_ref[...]).reshape(nseq, tlen, D_MODEL)
    x2 = x3 + g_ref[...] * y
    y_ref[...] = _rms(x2, fg_ref[...])


def _ffn_sample_call(x1, mod3, n2g, wup, fcw, wdown, fg, st, chunk):
    nseq, tlen, _ = x1.shape

    def mod_spec(col):
        return pl.BlockSpec((chunk, 1, D_MODEL), lambda s: (s, 0, col))

    def seq_spec(d1, d2):
        return pl.BlockSpec((chunk, d1, d2), lambda s: (s, 0, 0))

    return pl.pallas_call(
        _ffn_sample_kernel,
        grid=(nseq // chunk,),
        in_specs=[seq_spec(tlen, D_MODEL),
                  mod_spec(3), mod_spec(4), mod_spec(5),
                  _const_spec((1, D_MODEL)),
                  _const_spec((D_MODEL, 2 * D_FF)),
                  _const_spec((3, D_FF)),
                  _const_spec((D_FF, D_MODEL)),
                  _const_spec((1, D_MODEL)),
                  seq_spec(2, D_FF)],
        out_specs=[seq_spec(tlen, D_MODEL), seq_spec(2, D_FF)],
        out_shape=[jax.ShapeDtypeStruct((nseq, tlen, D_MODEL), _F32),
                   jax.ShapeDtypeStruct((nseq, 2, D_FF), _F32)],
        compiler_params=pltpu.CompilerParams(
            dimension_semantics=("arbitrary",),
            vmem_limit_bytes=VMEM_LIMIT),
        name="ffn_sample",
    )(x1, mod3, mod3, mod3, n2g, wup, fcw, wdown, fg, st)


PROMPT_TILE = 512
FFN_TILE = 1024
FFN_PARTS = 2
SAMPLE_CHUNK = 16
ATTN_AFTER_WINDOWS = 4
FFN_SAMPLE_CHUNK = 64


def kernel(x_prompt, x_sample, c_prompt, c_sample, state_conv, cache_k_win, cache_v_win,
           state_ffn_conv, norm1_g, norm2_g, w_ada, b_ada, w_in, conv_w, w_conv_out,
           attn_sinks, w_attn_out, w_mix_out, w_up, ffn_conv_w, w_down, final_g):
    depth = w_in.shape[0]
    assert depth == 1, "the final norm is fused into the (single) layer's FFN call"
    nsamp = x_sample.shape[0]
    bsz = x_prompt.shape[0]
    xp, xs = x_prompt, x_sample
    fg = final_g.reshape(1, D_MODEL)
    outs = [[] for _ in range(8)]
    for layer in range(depth):
        mod3, win, wco, wao, wmo = _mod_call(
            c_sample, c_prompt, w_ada[layer], b_ada[layer].reshape(1, -1),
            (w_in[layer], w_conv_out[layer], w_attn_out[layer], w_mix_out[layer]))
        n1g = norm1_g[layer].reshape(1, D_MODEL)
        n2g = norm2_g[layer].reshape(1, D_MODEL)
        cw = conv_w[layer]
        fcw = ffn_conv_w[layer]
        sinks = attn_sinks[layer]
        fg_l = fg
        def to_dim_major(win_state):
            return jnp.transpose(win_state, (0, 2, 3, 1)).reshape(-1, KV_WIDTH, WINDOW)

        def to_pos_major(t3):
            return jnp.transpose(t3.reshape(-1, N_KV_HEADS, HEAD_DIM, WINDOW), (0, 3, 1, 2))

        x1p, conv_p, k_p, v_p, wup, wdown = _mix_prompt_call(
            xp, mod3, nsamp, n1g, win, cw, wco, sinks, wao, wmo,
            (w_up[layer], w_down[layer]), PROMPT_TILE)
        x1s, conv_s, k_s, v_s = _mix_sample_call(
            xs, mod3, n1g, win, cw, wco, sinks, wao, wmo, state_conv[layer],
            to_dim_major(cache_k_win[layer]), to_dim_major(cache_v_win[layer]), SAMPLE_CHUNK)
        xp, ffn_p = _ffn_prompt_call(x1p, mod3, nsamp, n2g, wup, fcw, wdown, fg_l,
                                     FFN_TILE, FFN_PARTS)
        xs, ffn_s = _ffn_sample_call(x1s, mod3, n2g, wup, fcw, wdown, fg_l,
                                     state_ffn_conv[layer], FFN_SAMPLE_CHUNK)
        for lst, val in zip(outs, (conv_p, to_pos_major(k_p), to_pos_major(v_p), ffn_p,
                                   conv_s, to_pos_major(k_s), to_pos_major(v_s), ffn_s)):
            lst.append(val)
    return (xp, xs) + tuple(jnp.stack(lst, axis=0) for lst in outs)
```

```python
import functools

import jax
import jax.numpy as jnp
from jax import lax
from jax.experimental import pallas as pl
from jax.experimental.pallas import tpu as pltpu

D_MODEL = 1024
HEAD_DIM = 64
N_HEADS = 16
N_KV_HEADS = 4
WINDOW = 128
D_FF = 2816
KV_WIDTH = N_KV_HEADS * HEAD_DIM
IN_WIDTH = 3 * D_MODEL + D_MODEL + 2 * KV_WIDTH + 2 * D_MODEL
RMS_EPS = 1e-6
NEG_INF = -1e30
LANES = 128
HEADS_PER_COL = LANES // HEAD_DIM
N_QCOLS = N_HEADS * HEAD_DIM // LANES
CARRY_ROWS = 8
VMEM_LIMIT = 60 * 1024 * 1024

_OFF_B, _OFF_C, _OFF_X = 0, D_MODEL, 2 * D_MODEL
_OFF_Q = 3 * D_MODEL
_OFF_K = _OFF_Q + D_MODEL
_OFF_V = _OFF_K + KV_WIDTH
_OFF_GA = _OFF_V + KV_WIDTH
_OFF_GB = _OFF_GA + D_MODEL

_BF = jnp.bfloat16
_F32 = jnp.float32
_NT = (((1,), (1,)), ((), ()))


LOG2E = 1.4426950408889634
Q_SCALE = HEAD_DIM ** -0.5 * LOG2E


def _slope(head):
    return 2.0 ** (-8.0 * (head + 1) / N_HEADS) * LOG2E


def _rms(x, g):
    return x * lax.rsqrt(jnp.mean(x * x, axis=-1, keepdims=True) + RMS_EPS) * g


def _dot(a, b):
    return jnp.dot(a, b, preferred_element_type=_F32)


def _gelu(x):
    return 0.5 * x * (1.0 + lax.erf(x * (0.5 ** 0.5)))


def _split_heads(t2d, lo):
    rolled = pltpu.roll(t2d, HEAD_DIM, 1)
    zero = jnp.zeros_like(t2d)
    return (jnp.where(lo, t2d, zero), jnp.where(lo, zero, rolled),
            jnp.where(lo, rolled, zero), jnp.where(lo, zero, t2d))


def _softmax_parts(s, sink):
    m = jnp.maximum(jnp.max(s, axis=-1, keepdims=True), sink)
    p = jnp.exp2(s - m)
    denom = jnp.sum(p, axis=-1, keepdims=True) + jnp.exp2(sink - m)
    return p, 1.0 / denom


N_MOD = 6
MOD_COLS = 768


def _mod_kernel(cs_ref, cp_ref, w_ref, b_ref, *rest):
    n_cast = (len(rest) - 1) // 2
    cast_in, o_ref, cast_out = rest[:n_cast], rest[n_cast], rest[n_cast + 1:]
    for src, dst in zip(cast_in, cast_out):
        dst[...] = src[...].astype(_BF)
    w = w_ref[...].astype(_BF)
    n0 = cs_ref.shape[0]
    for c_ref, r0 in ((cs_ref, 0), (cp_ref, n0)):
        c = c_ref[...]
        a = (c * jax.nn.sigmoid(c)).astype(_BF)
        res = _dot(a, w) + b_ref[...]
        n = c.shape[0]
        o_ref[r0:r0 + n] = res.reshape(n, 1, res.shape[-1])


def _mod_call(c_sample, c_prompt, w_ada, b_ada, cast_along):
    ns, npr = c_sample.shape[0], c_prompt.shape[0]
    n = ns + npr
    steps = N_MOD * D_MODEL // MOD_COLS
    cast_specs = [_cast_stream_spec(w.shape, steps) for w in cast_along]
    return pl.pallas_call(
        _mod_kernel,
        grid=(steps,),
        in_specs=[pl.BlockSpec((ns, D_MODEL), lambda j: (0, 0)),
                  pl.BlockSpec((npr, D_MODEL), lambda j: (0, 0)),
                  pl.BlockSpec((D_MODEL, MOD_COLS), lambda j: (0, j)),
                  pl.BlockSpec((1, MOD_COLS), lambda j: (0, j))] + cast_specs,
        out_specs=[pl.BlockSpec((n, 1, MOD_COLS), lambda j: (0, 0, j))] + cast_specs,
        out_shape=[jax.ShapeDtypeStruct((n, 1, N_MOD * D_MODEL), _F32)]
        + [jax.ShapeDtypeStruct(w.shape, _BF) for w in cast_along],
        compiler_params=pltpu.CompilerParams(vmem_limit_bytes=VMEM_LIMIT),
        name="adaln_mod",
    )(c_sample, c_prompt, w_ada, b_ada, *cast_along)


def _mod_norm(x, g, sc, sh):
    return (_rms(x, g) * (1.0 + sc) + sh).astype(_BF)


def _mix_prompt_kernel(x_ref, sh_ref, sc_ref, g_ref, xn_ref, shn_ref, scn_ref, n1g_ref,
                       win_ref, cw_ref, wco_ref, sinks_ref, wao_ref, wmo_ref, *rest,
                       tiles_per_seq, n_cast):
    cast_in, rest = rest[:n_cast], rest[n_cast:]
    (x1_ref, conv_ref, k_ref, v_ref), rest = rest[:4], rest[4:]
    cast_out, (u_ext, k_ext, v_ext, o_scr, bias_scr, h_scr, h_next) = rest[:n_cast], rest[n_cast:]
    step = pl.program_id(0)
    i = step % tiles_per_seq
    t = x_ref.shape[0]
    nblk = t // WINDOW

    for src, dst in zip(cast_in, cast_out):
        dst[...] = src[...].astype(_BF)

    @pl.when(step == 0)
    def _():
        h_scr[...] = _mod_norm(x_ref[...], n1g_ref[...], sc_ref[...], sh_ref[...])
        row = lax.broadcasted_iota(jnp.int32, (WINDOW, 2 * WINDOW), 0)
        col = lax.broadcasted_iota(jnp.int32, (WINDOW, 2 * WINDOW), 1)
        dist = WINDOW + row - col
        valid = (dist >= 0) & (dist <= WINDOW)
        distf = dist.astype(_F32)
        for head in range(N_HEADS):
            bias_scr[head] = jnp.where(valid, -(_slope(head) * distf), NEG_INF)

    @pl.when(i == 0)
    def _():
        u_ext[0:CARRY_ROWS, :] = jnp.zeros((CARRY_ROWS, D_MODEL), _F32)
        k_ext[:, 0:WINDOW, :] = jnp.zeros((N_KV_HEADS, WINDOW, LANES), _BF)
        v_ext[:, 0:WINDOW, :] = jnp.zeros((N_KV_HEADS, WINDOW, 2 * LANES), _BF)

    def proj(lo, width):
        return _dot(h_scr[...], win_ref[:, lo:lo + width])

    def next_tile_norm():
        h_next[...] = _mod_norm(xn_ref[...], n1g_ref[...], scn_ref[...], shn_ref[...])

    side = {}

    def conv_in():
        u = proj(_OFF_C, D_MODEL) * proj(_OFF_X, D_MODEL)
        u_ext[CARRY_ROWS:CARRY_ROWS + t, :] = u
        cw = cw_ref[...]
        side["uc"] = (u_ext[CARRY_ROWS - 2:CARRY_ROWS - 2 + t, :] * cw[0:1, :]
                      + u_ext[CARRY_ROWS - 1:CARRY_ROWS - 1 + t, :] * cw[1:2, :]
                      + u * cw[2:3, :])
        conv_ref[...] = u[t - 2:t, :]
        u_ext[0:CARRY_ROWS, :] = u[t - CARRY_ROWS:t, :]

    def conv_out():
        pb = proj(_OFF_B, D_MODEL)
        side["ya"] = _dot((pb * side.pop("uc")).astype(_BF), wco_ref[...])

    def gate_a():
        side["ga"] = jax.nn.sigmoid(proj(_OFF_GA, D_MODEL))

    def gate_b():
        side["gb"] = jax.nn.sigmoid(proj(_OFF_GB, D_MODEL))

    next_tile_norm()
    side_jobs = [conv_in, conv_out, gate_a, gate_b]

    qf = proj(_OFF_Q, D_MODEL) * Q_SCALE
    q = qf.astype(_BF)
    q_swap = jnp.concatenate(
        [pltpu.roll(qf[:, c * LANES:(c + 1) * LANES], HEAD_DIM, 1) for c in range(N_QCOLS)],
        axis=1).astype(_BF)
    kf = proj(_OFF_K, KV_WIDTH)
    vf = proj(_OFF_V, KV_WIDTH)
    k_ref[...] = kf[t - WINDOW:t, :].T
    v_ref[...] = vf[t - WINDOW:t, :].T
    lo = lax.broadcasted_iota(jnp.int32, (t, LANES), 1) < HEAD_DIM
    one = jnp.ones((t, LANES), _F32)
    for kc in range(KV_WIDTH // LANES):
        ks = _split_heads(kf[:, kc * LANES:(kc + 1) * LANES], lo)
        vs = _split_heads(vf[:, kc * LANES:(kc + 1) * LANES], lo)
        for j in range(2):
            hkv = 2 * kc + j
            k_ext[hkv, WINDOW:WINDOW + t, :] = ks[2 * j].astype(_BF)
            v_ext[hkv, WINDOW:WINDOW + t, :] = jnp.concatenate(
                [jnp.where(lo, vs[2 * j], one), jnp.where(lo, one, vs[2 * j + 1])],
                axis=1).astype(_BF)

    first_cols = lax.broadcasted_iota(jnp.int32, (WINDOW, 2 * WINDOW), 1) < WINDOW
    no_prev = first_cols & (i == 0)
    lo_q = lax.broadcasted_iota(jnp.int32, (WINDOW, LANES), 1) < HEAD_DIM
    group = N_HEADS // N_KV_HEADS
    for blk in range(nblk):
        r0 = blk * WINDOW
        for hkv in range(N_KV_HEADS):
            lhs = []
            for j in range(group):
                c = (group * hkv + j) // HEADS_PER_COL
                src = q if j % HEADS_PER_COL == 0 else q_swap
                lhs.append(src[r0:r0 + WINDOW, c * LANES:(c + 1) * LANES])
            s4 = lax.dot_general(jnp.concatenate(lhs, axis=0), k_ext[hkv, r0:r0 + 2 * WINDOW, :],
                                 _NT, preferred_element_type=_F32)
            ps, sink_terms = [], []
            for j in range(group):
                head = group * hkv + j
                s = s4[j * WINDOW:(j + 1) * WINDOW, :] + bias_scr[head]
                if blk == 0:
                    s = jnp.where(no_prev, NEG_INF, s)
                m = jnp.max(s, axis=-1, keepdims=True)
                ps.append(jnp.exp2(s - m).astype(_BF))
                sink_terms.append(jnp.exp2(sinks_ref[head] * LOG2E - m))
            o4 = _dot(jnp.concatenate(ps, axis=0), v_ext[hkv, r0:r0 + 2 * WINDOW, :])
            for m2 in range(group // HEADS_PER_COL):
                c = (group * hkv) // HEADS_PER_COL + m2
                even = o4[(2 * m2) * WINDOW:(2 * m2 + 1) * WINDOW, :]
                odd = o4[(2 * m2 + 1) * WINDOW:(2 * m2 + 2) * WINDOW, :]
                out = jnp.where(lo_q, even[:, 0:LANES], odd[:, LANES:2 * LANES])
                rsum = jnp.where(lo_q, even[:, LANES:2 * LANES], odd[:, 0:LANES])
                sterm = jnp.where(lo_q, sink_terms[2 * m2], sink_terms[2 * m2 + 1])
                o_scr[r0:r0 + WINDOW, c * LANES:(c + 1) * LANES] = (
                    out * (1.0 / (rsum + sterm))).astype(_BF)
        for _ in range(-(-len(side_jobs) // (nblk - blk))):
            side_jobs.pop(0)()
    k_ext[:, 0:WINDOW, :] = k_ext[:, t:t + WINDOW, :]
    v_ext[:, 0:WINDOW, :] = v_ext[:, t:t + WINDOW, :]
    yb = _dot(o_scr[...], wao_ref[...])

    mixed = (side["ga"] * side["ya"] + side["gb"] * yb).astype(_BF)
    x1_ref[...] = x_ref[...] + g_ref[...] * _dot(mixed, wmo_ref[...])
    h_scr[...] = h_next[...]


def _const_spec(shape):
    nd = len(shape)
    return pl.BlockSpec(shape, lambda *_: (0,) * nd, pipeline_mode=pl.Buffered(1))


def _tile_specs(tile, nt, steps, mod_row0):
    def nxt(s):
        return jnp.minimum(s + 1, steps - 1)

    x_spec = pl.BlockSpec((None, tile, D_MODEL), lambda s: (s // nt, s % nt, 0))
    xn_spec = pl.BlockSpec((None, tile, D_MODEL), lambda s: (nxt(s) // nt, nxt(s) % nt, 0))

    def mod_spec(chunk):
        return pl.BlockSpec((None, 1, D_MODEL), lambda s: (mod_row0 + s // nt, 0, chunk))

    def modn_spec(chunk):
        return pl.BlockSpec((None, 1, D_MODEL), lambda s: (mod_row0 + nxt(s) // nt, 0, chunk))

    return x_spec, xn_spec, mod_spec, modn_spec


BF16_SUBLANES = 16


def _cast_stream_spec(shape, steps):
    rows, cols = shape
    for hold in range(1, steps + 1):
        nblocks, rem = divmod(steps, hold)
        if rem == 0 and rows % nblocks == 0 and (rows // nblocks) % BF16_SUBLANES == 0:
            return pl.BlockSpec((rows // nblocks, cols), lambda s: (s // hold, 0))
    raise ValueError(f"no bf16-aligned row split of {shape} over {steps} steps")


def _mix_prompt_call(x, mod3, mod_row0, n1g, win, cw, wco, sinks, wao, wmo, cast_along, tile):
    bsz, seq, _ = x.shape
    nt = seq // tile
    steps = bsz * nt
    x_spec, xn_spec, mod_spec, modn_spec = _tile_specs(tile, nt, steps, mod_row0)
    cast_specs = [_cast_stream_spec(w.shape, steps) for w in cast_along]

    def state_spec(d1, d2):
        return pl.BlockSpec((None, d1, d2), lambda s: (s // nt, 0, 0))

    return pl.pallas_call(
        functools.partial(_mix_prompt_kernel, tiles_per_seq=nt, n_cast=len(cast_along)),
        grid=(steps,),
        in_specs=[x_spec, mod_spec(0), mod_spec(1), mod_spec(2),
                  xn_spec, modn_spec(0), modn_spec(1),
                  _const_spec((1, D_MODEL)),
                  _const_spec((D_MODEL, IN_WIDTH)),
                  _const_spec((3, D_MODEL)),
                  _const_spec((D_MODEL, D_MODEL)),
                  pl.BlockSpec(memory_space=pltpu.SMEM),
                  _const_spec((D_MODEL, D_MODEL)),
                  _const_spec((D_MODEL, D_MODEL))] + cast_specs,
        out_specs=[x_spec,
                   state_spec(2, D_MODEL),
                   state_spec(KV_WIDTH, WINDOW),
                   state_spec(KV_WIDTH, WINDOW)] + cast_specs,
        out_shape=[jax.ShapeDtypeStruct((bsz, seq, D_MODEL), _F32),
                   jax.ShapeDtypeStruct((bsz, 2, D_MODEL), _F32),
                   jax.ShapeDtypeStruct((bsz, KV_WIDTH, WINDOW), _F32),
                   jax.ShapeDtypeStruct((bsz, KV_WIDTH, WINDOW), _F32)]
        + [jax.ShapeDtypeStruct(w.shape, _BF) for w in cast_along],
        scratch_shapes=[pltpu.VMEM((CARRY_ROWS + tile, D_MODEL), _F32),
                        pltpu.VMEM((N_KV_HEADS, WINDOW + tile, LANES), _BF),
                        pltpu.VMEM((N_KV_HEADS, WINDOW + tile, 2 * LANES), _BF),
                        pltpu.VMEM((tile, D_MODEL), _BF),
                        pltpu.VMEM((N_HEADS, WINDOW, 2 * WINDOW), _F32),
                        pltpu.VMEM((tile, D_MODEL), _BF),
                        pltpu.VMEM((tile, D_MODEL), _BF)],
        compiler_params=pltpu.CompilerParams(
            dimension_semantics=("arbitrary",),
            vmem_limit_bytes=VMEM_LIMIT),
        name="mix_prompt",
    )(x, mod3, mod3, mod3, x, mod3, mod3, n1g, win, cw, wco, sinks, wao, wmo, *cast_along)


def _conv_seq(u2d, st, cw, nseq, tlen):
    width = u2d.shape[-1]
    shape3 = (nseq, tlen, width)
    u3 = u2d.reshape(shape3)
    r1 = pltpu.roll(u2d, 1, 0).reshape(shape3)
    r2 = pltpu.roll(u2d, 2, 0).reshape(shape3)
    tpos = lax.broadcasted_iota(jnp.int32, shape3, 1)
    st0 = jnp.broadcast_to(st[:, 0:1, :], shape3)
    st1 = jnp.broadcast_to(st[:, 1:2, :], shape3)
    um1 = jnp.where(tpos == 0, st1, r1)
    um2 = jnp.where(tpos == 0, st0, jnp.where(tpos == 1, st1, r2))
    y = um2 * cw[0:1, :] + um1 * cw[1:2, :] + u3 * cw[2:3, :]
    return y.reshape(nseq * tlen, width), u3[:, tlen - 2:tlen, :]


def _mix_sample_kernel(x_ref, sh_ref, sc_ref, g_ref, n1g_ref, win_ref, cw_ref, wco_ref,
                       sinks_ref, wao_ref, wmo_ref, st_ref, ck_ref, cv_ref,
                       x1_ref, conv_ref, k_ref, v_ref):
    nseq, tlen, _ = x_ref.shape
    rows = nseq * tlen

    assert rows == LANES, "one step's new keys must fill exactly one lane tile"

    x3 = x_ref[...]
    h3 = _rms(x3, n1g_ref[...]) * (1.0 + sc_ref[...]) + sh_ref[...]
    hf = h3.reshape(rows, D_MODEL)
    h_now = [hf.astype(_BF)]

    def proj(lo, width):
        return _dot(h_now[0], win_ref[:, lo:lo + width])

    def order_after(tile_f32):
        bits = pltpu.bitcast(tile_f32[0:8, :], jnp.uint32)
        zero = ((bits >> 16) >> 16).astype(_F32)
        h_now[0] = (hf + jnp.tile(zero, (rows // 8, D_MODEL // LANES))).astype(_BF)

    side = {}

    def job_conv_in():
        u = proj(_OFF_C, D_MODEL) * proj(_OFF_X, D_MODEL)
        side["uc"], new_st = _conv_seq(u, st_ref[...], cw_ref[...], nseq, tlen)
        conv_ref[...] = new_st

    def job_gate_b():
        side["pb"] = proj(_OFF_B, D_MODEL)

    def job_conv_out():
        side["ya"] = _dot((side.pop("pb") * side.pop("uc")).astype(_BF), wco_ref[...])

    def job_ga():
        side["ga"] = jax.nn.sigmoid(proj(_OFF_GA, D_MODEL))

    def job_gb():
        side["gb"] = jax.nn.sigmoid(proj(_OFF_GB, D_MODEL))

    keep = WINDOW - tlen
    qf = proj(_OFF_Q, D_MODEL) * Q_SCALE
    kvt_new = proj(_OFF_K, 2 * KV_WIDTH).T
    keep_old = lax.broadcasted_iota(jnp.int32, (KV_WIDTH, WINDOW), 1) < keep

    def window_update(s):
        shift = (keep - s * tlen) % LANES
        for cache_ref, out_ref, base in ((ck_ref, k_ref, 0), (cv_ref, v_ref, KV_WIDTH)):
            rolled = pltpu.roll(cache_ref[s], keep, 1)
            new = pltpu.roll(kvt_new[base:base + KV_WIDTH, :], shift, 1)
            win = jnp.where(keep_old, rolled, new)
            out_ref[s] = win
        return win

    lo = lax.broadcasted_iota(jnp.int32, (rows, LANES), 1) < HEAD_DIM
    zero = jnp.zeros((rows, LANES), _F32)
    n_tiles = KV_WIDTH // LANES
    heads_per_tile = N_HEADS // n_tiles
    m_rows = heads_per_tile * tlen
    row = lax.broadcasted_iota(jnp.int32, (m_rows, WINDOW), 0)
    col = lax.broadcasted_iota(jnp.int32, (m_rows, WINDOW), 1)
    dist_c = WINDOW + row % tlen - col
    valid_c = dist_c <= WINDOW
    shape3 = (nseq, m_rows, LANES)
    seq3 = lax.broadcasted_iota(jnp.int32, shape3, 0)
    row3 = lax.broadcasted_iota(jnp.int32, shape3, 1)
    col3 = lax.broadcasted_iota(jnp.int32, shape3, 2)
    dist_n = row3 % tlen - col3 % tlen
    valid_n = (col3 // tlen == seq3) & (dist_n >= 0)
    row1 = lax.broadcasted_iota(jnp.int32, (m_rows, 1), 0)

    def attention(kc):
        parts = []
        slope = jnp.zeros((m_rows, 1), _F32)
        sink = jnp.zeros((m_rows, 1), _F32)
        for j in range(heads_per_tile):
            head = heads_per_tile * kc + j
            c, par, half = head // HEADS_PER_COL, j % HEADS_PER_COL, j // (heads_per_tile // 2)
            qc = qf[:, c * LANES:(c + 1) * LANES]
            src = qc if par == half else pltpu.roll(qc, HEAD_DIM, 1)
            part = jnp.where(lo, src, zero) if half == 0 else jnp.where(lo, zero, src)
            parts.append(part.reshape(nseq, tlen, LANES))
            mine = (row1 >= j * tlen) & (row1 < (j + 1) * tlen)
            slope = jnp.where(mine, _slope(head), slope)
            sink = jnp.where(mine, sinks_ref[head] * LOG2E, sink)
        lhs = jnp.concatenate(parts, axis=1).astype(_BF)
        lhs2 = lhs.reshape(nseq * m_rows, LANES)
        rsl = slice(kc * LANES, (kc + 1) * LANES)
        k_new = kvt_new[rsl, :].astype(_BF)
        v_new = kvt_new[KV_WIDTH + kc * LANES:KV_WIDTH + (kc + 1) * LANES, :].astype(_BF)
        s_c = jnp.einsum('smk,skn->smn', lhs, ck_ref[:, rsl, :].astype(_BF),
                         preferred_element_type=_F32)
        s_n = _dot(lhs2, k_new).reshape(shape3)
        sc = jnp.concatenate(
            [s_c + jnp.where(valid_c, -(slope * dist_c.astype(_F32)), NEG_INF),
             s_n + jnp.where(valid_n, -(slope * dist_n.astype(_F32)), NEG_INF)], axis=2)
        p, inv = _softmax_parts(sc, sink)
        p = p.astype(_BF)
        r_c = jnp.einsum('smk,snk->smn', p[:, :, 0:WINDOW], cv_ref[:, rsl, :].astype(_BF),
                         preferred_element_type=_F32)
        r_n = lax.dot_general(p[:, :, WINDOW:].reshape(nseq * m_rows, LANES), v_new, _NT,
                              preferred_element_type=_F32).reshape(shape3)
        r = (r_c + r_n) * inv
        cols = {}
        for m2 in range(heads_per_tile // HEADS_PER_COL):
            c = (heads_per_tile // HEADS_PER_COL) * kc + m2
            half = m2 // (heads_per_tile // (2 * HEADS_PER_COL))
            r0 = r[:, (2 * m2) * tlen:(2 * m2 + 1) * tlen, :].reshape(rows, LANES)
            r1 = r[:, (2 * m2 + 1) * tlen:(2 * m2 + 2) * tlen, :].reshape(rows, LANES)
            if half == 0:
                cols[c] = jnp.where(lo, r0, pltpu.roll(r1, HEAD_DIM, 1))
            else:
                cols[c] = jnp.where(lo, pltpu.roll(r0, HEAD_DIM, 1), r1)
        return cols

    o_cols = {}
    job_conv_in()
    window_jobs = [job_gate_b, job_conv_out, job_ga, job_gb]
    group = -(-nseq // len(window_jobs))
    for s in range(nseq):
        if s == ATTN_AFTER_WINDOWS:
            for kc in range(n_tiles):
                o_cols.update(attention(kc))
        last_win = window_update(s)
        if window_jobs and (s + 1) % group == 0:
            window_jobs.pop(0)()
            order_after(last_win)
    while window_jobs:
        window_jobs.pop(0)()
    o = jnp.concatenate([o_cols[c] for c in range(N_QCOLS)], axis=1).astype(_BF)
    yb = _dot(o, wao_ref[...])

    mixed = (side["ga"] * side["ya"] + side["gb"] * yb).astype(_BF)
    y = _dot(mixed, wmo_ref[...]).reshape(nseq, tlen, D_MODEL)
    x1_ref[...] = x3 + g_ref[...] * y


def _mix_sample_call(x, mod3, n1g, win, cw, wco, sinks, wao, wmo, st, ck, cv, chunk):
    nseq, tlen, _ = x.shape
    assert LANES % tlen == 0 and chunk % (LANES // tlen) == 0

    def mod_spec(col):
        return pl.BlockSpec((chunk, 1, D_MODEL), lambda s: (s, 0, col))

    def seq_spec(d1, d2):
        return pl.BlockSpec((chunk, d1, d2), lambda s: (s, 0, 0))

    return pl.pallas_call(
        _mix_sample_kernel,
        grid=(nseq // chunk,),
        in_specs=[seq_spec(tlen, D_MODEL),
                  mod_spec(0), mod_spec(1), mod_spec(2),
                  _const_spec((1, D_MODEL)),
                  _const_spec((D_MODEL, IN_WIDTH)),
                  _const_spec((3, D_MODEL)),
                  _const_spec((D_MODEL, D_MODEL)),
                  pl.BlockSpec(memory_space=pltpu.SMEM),
                  _const_spec((D_MODEL, D_MODEL)),
                  _const_spec((D_MODEL, D_MODEL)),
                  seq_spec(2, D_MODEL),
                  seq_spec(KV_WIDTH, WINDOW),
                  seq_spec(KV_WIDTH, WINDOW)],
        out_specs=[seq_spec(tlen, D_MODEL),
                   seq_spec(2, D_MODEL),
                   seq_spec(KV_WIDTH, WINDOW),
                   seq_spec(KV_WIDTH, WINDOW)],
        out_shape=[jax.ShapeDtypeStruct((nseq, tlen, D_MODEL), _F32),
                   jax.ShapeDtypeStruct((nseq, 2, D_MODEL), _F32),
                   jax.ShapeDtypeStruct((nseq, KV_WIDTH, WINDOW), _F32),
                   jax.ShapeDtypeStruct((nseq, KV_WIDTH, WINDOW), _F32)],
        compiler_params=pltpu.CompilerParams(
            dimension_semantics=("arbitrary",),
            vmem_limit_bytes=VMEM_LIMIT),
        name="mix_sample",
    )(x, mod3, mod3, mod3, n1g, win, cw, wco, sinks, wao, wmo, st, ck, cv)


def _ffn_tail(x1, ac, val, g, wdown_ref, fg):
    hmid = (_gelu(ac) * val).astype(_BF)
    x2 = x1 + g * _dot(hmid, wdown_ref[...])
    return _rms(x2, fg)


def _ffn_prompt_kernel(x1_ref, sh_ref, sc_ref, g_ref, n2g_ref, wup_ref, fcw_ref, wdown_ref,
                       fg_ref, y_ref, ffn_ref, a_ext, *, parts):
    i = pl.program_id(1)
    t = x1_ref.shape[0]
    rows = t // parts

    @pl.when(i == 0)
    def _():
        a_ext[0:CARRY_ROWS, :] = jnp.zeros((CARRY_ROWS, D_FF), _F32)

    fcw = fcw_ref[...]

    def up(part):
        r0 = part * rows
        e0 = CARRY_ROWS + r0
        h2 = _mod_norm(x1_ref[r0:r0 + rows, :], n2g_ref[...], sc_ref[...], sh_ref[...])
        a = _dot(h2, wup_ref[:, 0:D_FF])
        val = _dot(h2, wup_ref[:, D_FF:2 * D_FF])
        a_ext[e0:e0 + rows, :] = a
        ac = (a_ext[e0 - 2:e0 - 2 + rows, :] * fcw[0:1, :]
              + a_ext[e0 - 1:e0 - 1 + rows, :] * fcw[1:2, :]
              + a * fcw[2:3, :])
        return ac, val

    def down(part, ac, val):
        r0 = part * rows
        y_ref[r0:r0 + rows, :] = _ffn_tail(x1_ref[r0:r0 + rows, :], ac, val, g_ref[...],
                                           wdown_ref, fg_ref[...])

    for part in range(parts):
        down(part, *up(part))
    ffn_ref[...] = a_ext[CARRY_ROWS + t - 2:CARRY_ROWS + t, :]
    a_ext[0:CARRY_ROWS, :] = a_ext[t:t + CARRY_ROWS, :]


def _ffn_prompt_call(x1, mod3, mod_row0, n2g, wup, fcw, wdown, fg, tile, parts):
    bsz, seq, _ = x1.shape

    def mod_spec(chunk):
        return pl.BlockSpec((None, 1, D_MODEL), lambda b, i: (mod_row0 + b, 0, chunk))

    x_spec = pl.BlockSpec((None, tile, D_MODEL), lambda b, i: (b, i, 0))
    return pl.pallas_call(
        functools.partial(_ffn_prompt_kernel, parts=parts),
        grid=(bsz, seq // tile),
        in_specs=[x_spec, mod_spec(3), mod_spec(4), mod_spec(5),
                  _const_spec((1, D_MODEL)),
                  _const_spec((D_MODEL, 2 * D_FF)),
                  _const_spec((3, D_FF)),
                  _const_spec((D_FF, D_MODEL)),
                  _const_spec((1, D_MODEL))],
        out_specs=[x_spec,
                   pl.BlockSpec((None, 2, D_FF), lambda b, i: (b, 0, 0))],
        out_shape=[jax.ShapeDtypeStruct((bsz, seq, D_MODEL), _F32),
                   jax.ShapeDtypeStruct((bsz, 2, D_FF), _F32)],
        scratch_shapes=[pltpu.VMEM((CARRY_ROWS + tile, D_FF), _F32)],
        compiler_params=pltpu.CompilerParams(
            dimension_semantics=("arbitrary", "arbitrary"),
            vmem_limit_bytes=VMEM_LIMIT),
        name="ffn_prompt",
    )(x1, mod3, mod3, mod3, n2g, wup, fcw, wdown, fg)


def _ffn_sample_kernel(x1_ref, sh_ref, sc_ref, g_ref, n2g_ref, wup_ref, fcw_ref, wdown_ref,
                       fg_ref, st_ref, y_ref, ffn_ref):
    nseq, tlen, _ = x1_ref.shape
    rows = nseq * tlen
    x3 = x1_ref[...]
    h3 = _rms(x3, n2g_ref[...]) * (1.0 + sc_ref[...]) + sh_ref[...]
    h2 = h3.reshape(rows, D_MODEL).astype(_BF)
    a = _dot(h2, wup_ref[:, 0:D_FF])
    val = _dot(h2, wup_ref[:, D_FF:2 * D_FF])
    ac, new_st = _conv_seq(a, st_ref[...], fcw_ref[...], nseq, tlen)
    ffn_ref[...] = new_st
    hmid = (_gelu(ac) * val).astype(_BF)
    y = _dot(hmid, wdown_ref[...]).reshape(nseq, tlen, D_MODEL)
    x2 = x3 + g_ref[...] * y
    y_ref[...] = _rms(x2, fg_ref[...])


def _ffn_sample_call(x1, mod3, n2g, wup, fcw, wdown, fg, st, chunk):
    nseq, tlen, _ = x1.shape

    def mod_spec(col):
        return pl.BlockSpec((chunk, 1, D_MODEL), lambda s: (s, 0, col))

    def seq_spec(d1, d2):
        return pl.BlockSpec((chunk, d1, d2), lambda s: (s, 0, 0))

    return pl.pallas_call(
        _ffn_sample_kernel,
        grid=(nseq // chunk,),
        in_specs=[seq_spec(tlen, D_MODEL),
                  mod_spec(3), mod_spec(4), mod_spec(5),
                  _const_spec((1, D_MODEL)),
                  _const_spec((D_MODEL, 2 * D_FF)),
                  _const_spec((3, D_FF)),
                  _const_spec((D_FF, D_MODEL)),
                  _const_spec((1, D_MODEL)),
                  seq_spec(2, D_FF)],
        out_specs=[seq_spec(tlen, D_MODEL), seq_spec(2, D_FF)],
        out_shape=[jax.ShapeDtypeStruct((nseq, tlen, D_MODEL), _F32),
                   jax.ShapeDtypeStruct((nseq, 2, D_FF), _F32)],
        compiler_params=pltpu.CompilerParams(
            dimension_semantics=("arbitrary",),
            vmem_limit_bytes=VMEM_LIMIT),
        name="ffn_sample",
    )(x1, mod3, mod3, mod3, n2g, wup, fcw, wdown, fg, st)


PROMPT_TILE = 512
FFN_TILE = 1024
FFN_PARTS = 2
SAMPLE_CHUNK = 16
ATTN_AFTER_WINDOWS = 4
FFN_SAMPLE_CHUNK = 64


def kernel(x_prompt, x_sample, c_prompt, c_sample, state_conv, cache_k_win, cache_v_win,
           state_ffn_conv, norm1_g, norm2_g, w_ada, b_ada, w_in, conv_w, w_conv_out,
           attn_sinks, w_attn_out, w_mix_out, w_up, ffn_conv_w, w_down, final_g):
    depth = w_in.shape[0]
    assert depth == 1, "the final norm is fused into the (single) layer's FFN call"
    nsamp = x_sample.shape[0]
    bsz = x_prompt.shape[0]
    xp, xs = x_prompt, x_sample
    fg = final_g.reshape(1, D_MODEL)
    outs = [[] for _ in range(8)]
    for layer in range(depth):
        mod3, win, wco, wao, wmo = _mod_call(
            c_sample, c_prompt, w_ada[layer], b_ada[layer].reshape(1, -1),
            (w_in[layer], w_conv_out[layer], w_attn_out[layer], w_mix_out[layer]))
        n1g = norm1_g[layer].reshape(1, D_MODEL)
        n2g = norm2_g[layer].reshape(1, D_MODEL)
        cw = conv_w[layer]
        fcw = ffn_conv_w[layer]
        sinks = attn_sinks[layer]
        fg_l = fg
        def to_dim_major(win_state):
            return jnp.transpose(win_state, (0, 2, 3, 1)).reshape(-1, KV_WIDTH, WINDOW)

        def to_pos_major(t3):
            return jnp.transpose(t3.reshape(-1, N_KV_HEADS, HEAD_DIM, WINDOW), (0, 3, 1, 2))

        x1p, conv_p, k_p, v_p, wup, wdown = _mix_prompt_call(
            xp, mod3, nsamp, n1g, win, cw, wco, sinks, wao, wmo,
            (w_up[layer], w_down[layer]), PROMPT_TILE)
        x1s, conv_s, k_s, v_s = _mix_sample_call(
            xs, mod3, n1g, win, cw, wco, sinks, wao, wmo, state_conv[layer],
            to_dim_major(cache_k_win[layer]), to_dim_major(cache_v_win[layer]), SAMPLE_CHUNK)
        xp, ffn_p = _ffn_prompt_call(x1p, mod3, nsamp, n2g, wup, fcw, wdown, fg_l,
                                     FFN_TILE, FFN_PARTS)
        xs, ffn_s = _ffn_sample_call(x1s, mod3, n2g, wup, fcw, wdown, fg_l,
                                     state_ffn_conv[layer], FFN_SAMPLE_CHUNK)
        for lst, val in zip(outs, (conv_p, to_pos_major(k_p), to_pos_major(v_p), ffn_p,
                                   conv_s, to_pos_major(k_s), to_pos_major(v_s), ffn_s)):
            lst.append(val)
    return (xp, xs) + tuple(jnp.stack(lst, axis=0) for lst in outs)
```

```python
import functools

import jax
import jax.numpy as jnp
from jax import lax
from jax.experimental import pallas as pl
from jax.experimental.pallas import tpu as pltpu

D_MODEL = 1024
HEAD_DIM = 64
N_HEADS = 16
N_KV_HEADS = 4
WINDOW = 128
D_FF = 2816
KV_WIDTH = N_KV_HEADS * HEAD_DIM
IN_WIDTH = 3 * D_MODEL + D_MODEL + 2 * KV_WIDTH + 2 * D_MODEL
RMS_EPS = 1e-6
NEG_INF = -1e30
LANES = 128
HEADS_PER_COL = LANES // HEAD_DIM
N_QCOLS = N_HEADS * HEAD_DIM // LANES
CARRY_ROWS = 8
VMEM_LIMIT = 60 * 1024 * 1024

_OFF_B, _OFF_C, _OFF_X = 0, D_MODEL, 2 * D_MODEL
_OFF_Q = 3 * D_MODEL
_OFF_K = _OFF_Q + D_MODEL
_OFF_V = _OFF_K + KV_WIDTH
_OFF_GA = _OFF_V + KV_WIDTH
_OFF_GB = _OFF_GA + D_MODEL

_BF = jnp.bfloat16
_F32 = jnp.float32
_NT = (((1,), (1,)), ((), ()))


LOG2E = 1.4426950408889634
Q_SCALE = HEAD_DIM ** -0.5 * LOG2E


def _slope(head):
    return 2.0 ** (-8.0 * (head + 1) / N_HEADS) * LOG2E


def _rms(x, g):
    return x * lax.rsqrt(jnp.mean(x * x, axis=-1, keepdims=True) + RMS_EPS) * g


def _dot(a, b):
    return jnp.dot(a, b, preferred_element_type=_F32)


def _gelu(x):
    return 0.5 * x * (1.0 + lax.erf(x * (0.5 ** 0.5)))


def _split_heads(t2d, lo):
    rolled = pltpu.roll(t2d, HEAD_DIM, 1)
    zero = jnp.zeros_like(t2d)
    return (jnp.where(lo, t2d, zero), jnp.where(lo, zero, rolled),
            jnp.where(lo, rolled, zero), jnp.where(lo, zero, t2d))


def _softmax_parts(s, sink):
    m = jnp.maximum(jnp.max(s, axis=-1, keepdims=True), sink)
    p = jnp.exp2(s - m)
    denom = jnp.sum(p, axis=-1, keepdims=True) + jnp.exp2(sink - m)
    return p, 1.0 / denom


N_MOD = 6
MOD_COLS = 768


def _mod_kernel(cs_ref, cp_ref, w_ref, b_ref, *rest):
    n_cast = (len(rest) - 1) // 2
    cast_in, o_ref, cast_out = rest[:n_cast], rest[n_cast], rest[n_cast + 1:]
    for src, dst in zip(cast_in, cast_out):
        dst[...] = src[...].astype(_BF)
    w = w_ref[...].astype(_BF)
    n0 = cs_ref.shape[0]
    for c_ref, r0 in ((cs_ref, 0), (cp_ref, n0)):
        c = c_ref[...]
        a = (c * jax.nn.sigmoid(c)).astype(_BF)
        res = _dot(a, w) + b_ref[...]
        n = c.shape[0]
        o_ref[r0:r0 + n] = res.reshape(n, 1, res.shape[-1])


def _mod_call(c_sample, c_prompt, w_ada, b_ada, cast_along):
    ns, npr = c_sample.shape[0], c_prompt.shape[0]
    n = ns + npr
    steps = N_MOD * D_MODEL // MOD_COLS
    cast_specs = [_cast_stream_spec(w.shape, steps) for w in cast_along]
    return pl.pallas_call(
        _mod_kernel,
        grid=(steps,),
        in_specs=[pl.BlockSpec((ns, D_MODEL), lambda j: (0, 0)),
                  pl.BlockSpec((npr, D_MODEL), lambda j: (0, 0)),
                  pl.BlockSpec((D_MODEL, MOD_COLS), lambda j: (0, j)),
                  pl.BlockSpec((1, MOD_COLS), lambda j: (0, j))] + cast_specs,
        out_specs=[pl.BlockSpec((n, 1, MOD_COLS), lambda j: (0, 0, j))] + cast_specs,
        out_shape=[jax.ShapeDtypeStruct((n, 1, N_MOD * D_MODEL), _F32)]
        + [jax.ShapeDtypeStruct(w.shape, _BF) for w in cast_along],
        compiler_params=pltpu.CompilerParams(vmem_limit_bytes=VMEM_LIMIT),
        name="adaln_mod",
    )(c_sample, c_prompt, w_ada, b_ada, *cast_along)


def _mod_norm(x, g, sc, sh):
    return (_rms(x, g) * (1.0 + sc) + sh).astype(_BF)


def _mix_prompt_kernel(x_ref, sh_ref, sc_ref, g_ref, xn_ref, shn_ref, scn_ref, n1g_ref,
                       win_ref, cw_ref, wco_ref, sinks_ref, wao_ref, wmo_ref, *rest,
                       tiles_per_seq, n_cast):
    cast_in, rest = rest[:n_cast], rest[n_cast:]
    (x1_ref, conv_ref, k_ref, v_ref), rest = rest[:4], rest[4:]
    cast_out, (u_ext, k_ext, v_ext, o_scr, bias_scr, h_scr) = rest[:n_cast], rest[n_cast:]
    step = pl.program_id(0)
    i = step % tiles_per_seq
    t = x_ref.shape[0]
    nblk = t // WINDOW

    for src, dst in zip(cast_in, cast_out):
        dst[...] = src[...].astype(_BF)

    @pl.when(step == 0)
    def _():
        h_scr[...] = _mod_norm(x_ref[...], n1g_ref[...], sc_ref[...], sh_ref[...])
        row = lax.broadcasted_iota(jnp.int32, (WINDOW, 2 * WINDOW), 0)
        col = lax.broadcasted_iota(jnp.int32, (WINDOW, 2 * WINDOW), 1)
        dist = WINDOW + row - col
        valid = (dist >= 0) & (dist <= WINDOW)
        distf = dist.astype(_F32)
        for head in range(N_HEADS):
            bias_scr[head] = jnp.where(valid, -(_slope(head) * distf), NEG_INF)

    @pl.when(i == 0)
    def _():
        u_ext[0:CARRY_ROWS, :] = jnp.zeros((CARRY_ROWS, D_MODEL), _F32)
        k_ext[:, 0:WINDOW, :] = jnp.zeros((2 * N_KV_HEADS, WINDOW, LANES), _BF)
        v_ext[:, 0:WINDOW, :] = jnp.zeros((2 * N_KV_HEADS, WINDOW, LANES), _BF)

    def proj(lo, width):
        return _dot(h_scr[...], win_ref[:, lo:lo + width])

    def next_tile_norm():
        h_scr[...] = _mod_norm(xn_ref[...], n1g_ref[...], scn_ref[...], shn_ref[...])

    side = {}

    def conv_in():
        u = proj(_OFF_C, D_MODEL) * proj(_OFF_X, D_MODEL)
        u_ext[CARRY_ROWS:CARRY_ROWS + t, :] = u
        cw = cw_ref[...]
        side["uc"] = (u_ext[CARRY_ROWS - 2:CARRY_ROWS - 2 + t, :] * cw[0:1, :]
                      + u_ext[CARRY_ROWS - 1:CARRY_ROWS - 1 + t, :] * cw[1:2, :]
                      + u * cw[2:3, :])
        conv_ref[...] = u[t - 2:t, :]
        u_ext[0:CARRY_ROWS, :] = u[t - CARRY_ROWS:t, :]

    def conv_out():
        pb = proj(_OFF_B, D_MODEL)
        side["ya"] = _dot((pb * side.pop("uc")).astype(_BF), wco_ref[...])

    def gate_a():
        side["ga"] = jax.nn.sigmoid(proj(_OFF_GA, D_MODEL))

    def gate_b():
        side["gb"] = jax.nn.sigmoid(proj(_OFF_GB, D_MODEL))
        next_tile_norm()

    side_jobs = [conv_in, conv_out, gate_a, gate_b]

    q = (proj(_OFF_Q, D_MODEL) * Q_SCALE).astype(_BF)
    kf = proj(_OFF_K, KV_WIDTH)
    vf = proj(_OFF_V, KV_WIDTH)
    k_ref[...] = kf[t - WINDOW:t, :].T
    v_ref[...] = vf[t - WINDOW:t, :].T
    lane_t = lax.broadcasted_iota(jnp.int32, (t, LANES), 1)
    lo = lane_t < HEAD_DIM
    ones_lane = (HEAD_DIM, 0)
    for kc in range(KV_WIDTH // LANES):
        ks = _split_heads(kf[:, kc * LANES:(kc + 1) * LANES], lo)
        vs = _split_heads(vf[:, kc * LANES:(kc + 1) * LANES], lo)
        for j in range(4):
            k_ext[4 * kc + j, WINDOW:WINDOW + t, :] = ks[j].astype(_BF)
            v_one = jnp.where(lane_t == ones_lane[j % HEADS_PER_COL], 1.0, vs[j])
            v_ext[4 * kc + j, WINDOW:WINDOW + t, :] = v_one.astype(_BF)

    first_cols = lax.broadcasted_iota(jnp.int32, (WINDOW, 2 * WINDOW), 1) < WINDOW
    no_prev = first_cols & (i == 0)
    lo_q = lax.broadcasted_iota(jnp.int32, (WINDOW, LANES), 1) < HEAD_DIM
    for blk in range(nblk):
        r0 = blk * WINDOW
        for c in range(N_QCOLS):
            hkv = c // 2
            qc = q[r0:r0 + WINDOW, c * LANES:(c + 1) * LANES]
            o_pair = []
            for par in range(HEADS_PER_COL):
                head = HEADS_PER_COL * c + par
                kk = k_ext[2 * hkv + par, r0:r0 + 2 * WINDOW, :]
                s = lax.dot_general(qc, kk, _NT, preferred_element_type=_F32) + bias_scr[head]
                if blk == 0:
                    s = jnp.where(no_prev, NEG_INF, s)
                sink = sinks_ref[head] * LOG2E
                m = jnp.max(s, axis=-1, keepdims=True)
                p = jnp.exp2((s - m).astype(_BF))
                o_h = _dot(p, v_ext[2 * hkv + par, r0:r0 + 2 * WINDOW, :])
                col = ones_lane[par]
                denom = o_h[:, col:col + 1] + jnp.exp2(sink - m)
                o_pair.append(o_h * (1.0 / denom))
            o_c = jnp.where(lo_q, o_pair[0], o_pair[1])
            o_scr[r0:r0 + WINDOW, c * LANES:(c + 1) * LANES] = o_c.astype(_BF)
        for _ in range(-(-len(side_jobs) // (nblk - blk))):
            side_jobs.pop(0)()
    k_ext[:, 0:WINDOW, :] = k_ext[:, t:t + WINDOW, :]
    v_ext[:, 0:WINDOW, :] = v_ext[:, t:t + WINDOW, :]
    yb = _dot(o_scr[...], wao_ref[...])

    mixed = (side["ga"] * side["ya"] + side["gb"] * yb).astype(_BF)
    x1_ref[...] = x_ref[...] + g_ref[...] * _dot(mixed, wmo_ref[...])


def _const_spec(shape):
    nd = len(shape)
    return pl.BlockSpec(shape, lambda *_: (0,) * nd, pipeline_mode=pl.Buffered(1))


def _tile_specs(tile, nt, steps, mod_row0):
    def nxt(s):
        return jnp.minimum(s + 1, steps - 1)

    x_spec = pl.BlockSpec((None, tile, D_MODEL), lambda s: (s // nt, s % nt, 0))
    xn_spec = pl.BlockSpec((None, tile, D_MODEL), lambda s: (nxt(s) // nt, nxt(s) % nt, 0))

    def mod_spec(chunk):
        return pl.BlockSpec((None, 1, D_MODEL), lambda s: (mod_row0 + s // nt, 0, chunk))

    def modn_spec(chunk):
        return pl.BlockSpec((None, 1, D_MODEL), lambda s: (mod_row0 + nxt(s) // nt, 0, chunk))

    return x_spec, xn_spec, mod_spec, modn_spec


BF16_SUBLANES = 16


def _cast_stream_spec(shape, steps):
    rows, cols = shape
    for hold in range(1, steps + 1):
        nblocks, rem = divmod(steps, hold)
        if rem == 0 and rows % nblocks == 0 and (rows // nblocks) % BF16_SUBLANES == 0:
            return pl.BlockSpec((rows // nblocks, cols), lambda s: (s // hold, 0))
    raise ValueError(f"no bf16-aligned row split of {shape} over {steps} steps")


def _mix_prompt_call(x, mod3, mod_row0, n1g, win, cw, wco, sinks, wao, wmo, cast_along, tile):
    bsz, seq, _ = x.shape
    nt = seq // tile
    steps = bsz * nt
    x_spec, xn_spec, mod_spec, modn_spec = _tile_specs(tile, nt, steps, mod_row0)
    cast_specs = [_cast_stream_spec(w.shape, steps) for w in cast_along]

    def state_spec(d1, d2):
        return pl.BlockSpec((None, d1, d2), lambda s: (s // nt, 0, 0))

    return pl.pallas_call(
        functools.partial(_mix_prompt_kernel, tiles_per_seq=nt, n_cast=len(cast_along)),
        grid=(steps,),
        in_specs=[x_spec, mod_spec(0), mod_spec(1), mod_spec(2),
                  xn_spec, modn_spec(0), modn_spec(1),
                  _const_spec((1, D_MODEL)),
                  _const_spec((D_MODEL, IN_WIDTH)),
                  _const_spec((3, D_MODEL)),
                  _const_spec((D_MODEL, D_MODEL)),
                  pl.BlockSpec(memory_space=pltpu.SMEM),
                  _const_spec((D_MODEL, D_MODEL)),
                  _const_spec((D_MODEL, D_MODEL))] + cast_specs,
        out_specs=[x_spec,
                   state_spec(2, D_MODEL),
                   state_spec(KV_WIDTH, WINDOW),
                   state_spec(KV_WIDTH, WINDOW)] + cast_specs,
        out_shape=[jax.ShapeDtypeStruct((bsz, seq, D_MODEL), _F32),
                   jax.ShapeDtypeStruct((bsz, 2, D_MODEL), _F32),
                   jax.ShapeDtypeStruct((bsz, KV_WIDTH, WINDOW), _F32),
                   jax.ShapeDtypeStruct((bsz, KV_WIDTH, WINDOW), _F32)]
        + [jax.ShapeDtypeStruct(w.shape, _BF) for w in cast_along],
        scratch_shapes=[pltpu.VMEM((CARRY_ROWS + tile, D_MODEL), _F32),
                        pltpu.VMEM((2 * N_KV_HEADS, WINDOW + tile, LANES), _BF),
                        pltpu.VMEM((2 * N_KV_HEADS, WINDOW + tile, LANES), _BF),
                        pltpu.VMEM((tile, D_MODEL), _BF),
                        pltpu.VMEM((N_HEADS, WINDOW, 2 * WINDOW), _F32),
                        pltpu.VMEM((tile, D_MODEL), _BF)],
        compiler_params=pltpu.CompilerParams(
            dimension_semantics=("arbitrary",),
            vmem_limit_bytes=VMEM_LIMIT),
        name="mix_prompt",
    )(x, mod3, mod3, mod3, x, mod3, mod3, n1g, win, cw, wco, sinks, wao, wmo, *cast_along)


def _conv_seq(u2d, st, cw, nseq, tlen):
    width = u2d.shape[-1]
    shape3 = (nseq, tlen, width)
    u3 = u2d.reshape(shape3)
    r1 = pltpu.roll(u2d, 1, 0).reshape(shape3)
    r2 = pltpu.roll(u2d, 2, 0).reshape(shape3)
    tpos = lax.broadcasted_iota(jnp.int32, shape3, 1)
    st0 = jnp.broadcast_to(st[:, 0:1, :], shape3)
    st1 = jnp.broadcast_to(st[:, 1:2, :], shape3)
    um1 = jnp.where(tpos == 0, st1, r1)
    um2 = jnp.where(tpos == 0, st0, jnp.where(tpos == 1, st1, r2))
    y = um2 * cw[0:1, :] + um1 * cw[1:2, :] + u3 * cw[2:3, :]
    return y.reshape(nseq * tlen, width), u3[:, tlen - 2:tlen, :]


def _mix_sample_kernel(x_ref, sh_ref, sc_ref, g_ref, n1g_ref, win_ref, cw_ref, wco_ref,
                       sinks_ref, wao_ref, wmo_ref, st_ref, ck_ref, cv_ref,
                       x1_ref, conv_ref, k_ref, v_ref):
    nseq, tlen, _ = x_ref.shape
    rows = nseq * tlen

    assert rows == LANES, "one step's new keys must fill exactly one lane tile"

    x3 = x_ref[...]
    h3 = _rms(x3, n1g_ref[...]) * (1.0 + sc_ref[...]) + sh_ref[...]
    hf = h3.reshape(rows, D_MODEL)
    h_now = [hf.astype(_BF)]

    def proj(lo, width):
        return _dot(h_now[0], win_ref[:, lo:lo + width])

    def order_after(tile_f32):
        bits = pltpu.bitcast(tile_f32[0:8, :], jnp.uint32)
        zero = ((bits >> 16) >> 16).astype(_F32)
        h_now[0] = (hf + jnp.tile(zero, (rows // 8, D_MODEL // LANES))).astype(_BF)

    side = {}

    def job_conv_in():
        u = proj(_OFF_C, D_MODEL) * proj(_OFF_X, D_MODEL)
        side["uc"], new_st = _conv_seq(u, st_ref[...], cw_ref[...], nseq, tlen)
        conv_ref[...] = new_st

    def job_gate_b():
        side["pb"] = proj(_OFF_B, D_MODEL)

    def job_conv_out():
        side["ya"] = _dot((side.pop("pb") * side.pop("uc")).astype(_BF), wco_ref[...])

    def job_ga():
        side["ga"] = jax.nn.sigmoid(proj(_OFF_GA, D_MODEL))

    def job_gb():
        side["gb"] = jax.nn.sigmoid(proj(_OFF_GB, D_MODEL))

    keep = WINDOW - tlen
    qf = proj(_OFF_Q, D_MODEL) * Q_SCALE
    kvt_new = proj(_OFF_K, 2 * KV_WIDTH).T
    keep_old = lax.broadcasted_iota(jnp.int32, (KV_WIDTH, WINDOW), 1) < keep

    def window_update(s):
        shift = (keep - s * tlen) % LANES
        for cache_ref, out_ref, base in ((ck_ref, k_ref, 0), (cv_ref, v_ref, KV_WIDTH)):
            rolled = pltpu.roll(cache_ref[s], keep, 1)
            new = pltpu.roll(kvt_new[base:base + KV_WIDTH, :], shift, 1)
            win = jnp.where(keep_old, rolled, new)
            out_ref[s] = win
        return win

    lo = lax.broadcasted_iota(jnp.int32, (rows, LANES), 1) < HEAD_DIM
    zero = jnp.zeros((rows, LANES), _F32)
    n_tiles = KV_WIDTH // LANES
    heads_per_tile = N_HEADS // n_tiles
    m_rows = heads_per_tile * tlen
    row = lax.broadcasted_iota(jnp.int32, (m_rows, WINDOW), 0)
    col = lax.broadcasted_iota(jnp.int32, (m_rows, WINDOW), 1)
    dist_c = WINDOW + row % tlen - col
    valid_c = dist_c <= WINDOW
    shape3 = (nseq, m_rows, LANES)
    seq3 = lax.broadcasted_iota(jnp.int32, shape3, 0)
    row3 = lax.broadcasted_iota(jnp.int32, shape3, 1)
    col3 = lax.broadcasted_iota(jnp.int32, shape3, 2)
    dist_n = row3 % tlen - col3 % tlen
    valid_n = (col3 // tlen == seq3) & (dist_n >= 0)
    row1 = lax.broadcasted_iota(jnp.int32, (m_rows, 1), 0)

    def attention(kc):
        parts = []
        slope = jnp.zeros((m_rows, 1), _F32)
        sink = jnp.zeros((m_rows, 1), _F32)
        for j in range(heads_per_tile):
            head = heads_per_tile * kc + j
            c, par, half = head // HEADS_PER_COL, j % HEADS_PER_COL, j // (heads_per_tile // 2)
            qc = qf[:, c * LANES:(c + 1) * LANES]
            src = qc if par == half else pltpu.roll(qc, HEAD_DIM, 1)
            part = jnp.where(lo, src, zero) if half == 0 else jnp.where(lo, zero, src)
            parts.append(part.reshape(nseq, tlen, LANES))
            mine = (row1 >= j * tlen) & (row1 < (j + 1) * tlen)
            slope = jnp.where(mine, _slope(head), slope)
            sink = jnp.where(mine, sinks_ref[head] * LOG2E, sink)
        lhs = jnp.concatenate(parts, axis=1).astype(_BF)
        lhs2 = lhs.reshape(nseq * m_rows, LANES)
        rsl = slice(kc * LANES, (kc + 1) * LANES)
        k_new = kvt_new[rsl, :].astype(_BF)
        v_new = kvt_new[KV_WIDTH + kc * LANES:KV_WIDTH + (kc + 1) * LANES, :].astype(_BF)
        s_c = jnp.einsum('smk,skn->smn', lhs, ck_ref[:, rsl, :].astype(_BF),
                         preferred_element_type=_F32)
        s_n = _dot(lhs2, k_new).reshape(shape3)
        sc = jnp.concatenate(
            [s_c + jnp.where(valid_c, -(slope * dist_c.astype(_F32)), NEG_INF),
             s_n + jnp.where(valid_n, -(slope * dist_n.astype(_F32)), NEG_INF)], axis=2)
        p, inv = _softmax_parts(sc, sink)
        p = p.astype(_BF)
        r_c = jnp.einsum('smk,snk->smn', p[:, :, 0:WINDOW], cv_ref[:, rsl, :].astype(_BF),
                         preferred_element_type=_F32)
        r_n = lax.dot_general(p[:, :, WINDOW:].reshape(nseq * m_rows, LANES), v_new, _NT,
                              preferred_element_type=_F32).reshape(shape3)
        r = (r_c + r_n) * inv
        cols = {}
        for m2 in range(heads_per_tile // HEADS_PER_COL):
            c = (heads_per_tile // HEADS_PER_COL) * kc + m2
            half = m2 // (heads_per_tile // (2 * HEADS_PER_COL))
            r0 = r[:, (2 * m2) * tlen:(2 * m2 + 1) * tlen, :].reshape(rows, LANES)
            r1 = r[:, (2 * m2 + 1) * tlen:(2 * m2 + 2) * tlen, :].reshape(rows, LANES)
            if half == 0:
                cols[c] = jnp.where(lo, r0, pltpu.roll(r1, HEAD_DIM, 1))
            else:
                cols[c] = jnp.where(lo, pltpu.roll(r0, HEAD_DIM, 1), r1)
        return cols

    o_cols = {}
    job_conv_in()
    window_jobs = [job_gate_b, job_conv_out, job_ga, job_gb]
    group = -(-nseq // len(window_jobs))
    for s in range(nseq):
        if s == ATTN_AFTER_WINDOWS:
            for kc in range(n_tiles):
                o_cols.update(attention(kc))
        last_win = window_update(s)
        if window_jobs and (s + 1) % group == 0:
            window_jobs.pop(0)()
            order_after(last_win)
    while window_jobs:
        window_jobs.pop(0)()
    o = jnp.concatenate([o_cols[c] for c in range(N_QCOLS)], axis=1).astype(_BF)
    yb = _dot(o, wao_ref[...])

    mixed = (side["ga"] * side["ya"] + side["gb"] * yb).astype(_BF)
    y = _dot(mixed, wmo_ref[...]).reshape(nseq, tlen, D_MODEL)
    x1_ref[...] = x3 + g_ref[...] * y


def _mix_sample_call(x, mod3, n1g, win, cw, wco, sinks, wao, wmo, st, ck, cv, chunk):
    nseq, tlen, _ = x.shape
    assert LANES % tlen == 0 and chunk % (LANES // tlen) == 0

    def mod_spec(col):
        return pl.BlockSpec((chunk, 1, D_MODEL), lambda s: (s, 0, col))

    def seq_spec(d1, d2):
        return pl.BlockSpec((chunk, d1, d2), lambda s: (s, 0, 0))

    return pl.pallas_call(
        _mix_sample_kernel,
        grid=(nseq // chunk,),
        in_specs=[seq_spec(tlen, D_MODEL),
                  mod_spec(0), mod_spec(1), mod_spec(2),
                  _const_spec((1, D_MODEL)),
                  _const_spec((D_MODEL, IN_WIDTH)),
                  _const_spec((3, D_MODEL)),
                  _const_spec((D_MODEL, D_MODEL)),
                  pl.BlockSpec(memory_space=pltpu.SMEM),
                  _const_spec((D_MODEL, D_MODEL)),
                  _const_spec((D_MODEL, D_MODEL)),
                  seq_spec(2, D_MODEL),
                  seq_spec(KV_WIDTH, WINDOW),
                  seq_spec(KV_WIDTH, WINDOW)],
        out_specs=[seq_spec(tlen, D_MODEL),
                   seq_spec(2, D_MODEL),
                   seq_spec(KV_WIDTH, WINDOW),
                   seq_spec(KV_WIDTH, WINDOW)],
        out_shape=[jax.ShapeDtypeStruct((nseq, tlen, D_MODEL), _F32),
                   jax.ShapeDtypeStruct((nseq, 2, D_MODEL), _F32),
                   jax.ShapeDtypeStruct((nseq, KV_WIDTH, WINDOW), _F32),
                   jax.ShapeDtypeStruct((nseq, KV_WIDTH, WINDOW), _F32)],
        compiler_params=pltpu.CompilerParams(
            dimension_semantics=("arbitrary",),
            vmem_limit_bytes=VMEM_LIMIT),
        name="mix_sample",
    )(x, mod3, mod3, mod3, n1g, win, cw, wco, sinks, wao, wmo, st, ck, cv)


def _ffn_tail(x1, ac, val, g, wdown_ref, fg):
    hmid = (_gelu(ac) * val).astype(_BF)
    x2 = x1 + g * _dot(hmid, wdown_ref[...])
    return _rms(x2, fg)


def _ffn_prompt_kernel(x1_ref, sh_ref, sc_ref, g_ref, n2g_ref, wup_ref, fcw_ref, wdown_ref,
                       fg_ref, y_ref, ffn_ref, a_ext, *, parts):
    i = pl.program_id(1)
    t = x1_ref.shape[0]
    rows = t // parts

    @pl.when(i == 0)
    def _():
        a_ext[0:CARRY_ROWS, :] = jnp.zeros((CARRY_ROWS, D_FF), _F32)

    fcw = fcw_ref[...]

    def up(part):
        r0 = part * rows
        e0 = CARRY_ROWS + r0
        h2 = _mod_norm(x1_ref[r0:r0 + rows, :], n2g_ref[...], sc_ref[...], sh_ref[...])
        a = _dot(h2, wup_ref[:, 0:D_FF])
        val = _dot(h2, wup_ref[:, D_FF:2 * D_FF])
        a_ext[e0:e0 + rows, :] = a
        ac = (a_ext[e0 - 2:e0 - 2 + rows, :] * fcw[0:1, :]
              + a_ext[e0 - 1:e0 - 1 + rows, :] * fcw[1:2, :]
              + a * fcw[2:3, :])
        return ac, val

    def down(part, ac, val):
        r0 = part * rows
        y_ref[r0:r0 + rows, :] = _ffn_tail(x1_ref[r0:r0 + rows, :], ac, val, g_ref[...],
                                           wdown_ref, fg_ref[...])

    for part in range(parts):
        down(part, *up(part))
    ffn_ref[...] = a_ext[CARRY_ROWS + t - 2:CARRY_ROWS + t, :]
    a_ext[0:CARRY_ROWS, :] = a_ext[t:t + CARRY_ROWS, :]


def _ffn_prompt_call(x1, mod3, mod_row0, n2g, wup, fcw, wdown, fg, tile, parts):
    bsz, seq, _ = x1.shape

    def mod_spec(chunk):
        return pl.BlockSpec((None, 1, D_MODEL), lambda b, i: (mod_row0 + b, 0, chunk))

    x_spec = pl.BlockSpec((None, tile, D_MODEL), lambda b, i: (b, i, 0))
    return pl.pallas_call(
        functools.partial(_ffn_prompt_kernel, parts=parts),
        grid=(bsz, seq // tile),
        in_specs=[x_spec, mod_spec(3), mod_spec(4), mod_spec(5),
                  _const_spec((1, D_MODEL)),
                  _const_spec((D_MODEL, 2 * D_FF)),
                  _const_spec((3, D_FF)),
                  _const_spec((D_FF, D_MODEL)),
                  _const_spec((1, D_MODEL))],
        out_specs=[x_spec,
                   pl.BlockSpec((None, 2, D_FF), lambda b, i: (b, 0, 0))],
        out_shape=[jax.ShapeDtypeStruct((bsz, seq, D_MODEL), _F32),
                   jax.ShapeDtypeStruct((bsz, 2, D_FF), _F32)],
        scratch_shapes=[pltpu.VMEM((CARRY_ROWS + tile, D_FF), _F32)],
        compiler_params=pltpu.CompilerParams(
            dimension_semantics=("arbitrary", "arbitrary"),
            vmem_limit_bytes=VMEM_LIMIT),
        name="ffn_prompt",
    )(x1, mod3, mod3, mod3, n2g, wup, fcw, wdown, fg)


def _ffn_sample_kernel(x1_ref, sh_ref, sc_ref, g_ref, n2g_ref, wup_ref, fcw_ref, wdown_ref,
                       fg_ref, st_ref, y_ref, ffn_ref):
    nseq, tlen, _ = x1_ref.shape
    rows = nseq * tlen
    x3 = x1_ref[...]
    h3 = _rms(x3, n2g_ref[...]) * (1.0 + sc_ref[...]) + sh_ref[...]
    h2 = h3.reshape(rows, D_MODEL).astype(_BF)
    a = _dot(h2, wup_ref[:, 0:D_FF])
    val = _dot(h2, wup_ref[:, D_FF:2 * D_FF])
    ac, new_st = _conv_seq(a, st_ref[...], fcw_ref[...], nseq, tlen)
    ffn_ref[...] = new_st
    hmid = (_gelu(ac) * val).astype(_BF)
    y = _dot(hmid, wdown_ref[...]).reshape(nseq, tlen, D_MODEL)
    x2 = x3 + g_ref[...] * y
    y_ref[...] = _rms(x2, fg_ref[...])


def _ffn_sample_call(x1, mod3, n2g, wup, fcw, wdown, fg, st, chunk):
    nseq, tlen, _ = x1.shape

    def mod_spec(col):
        return pl.BlockSpec((chunk, 1, D_MODEL), lambda s: (s, 0, col))

    def seq_spec(d1, d2):
        return pl.BlockSpec((chunk, d1, d2), lambda s: (s, 0, 0))

    return pl.pallas_call(
        _ffn_sample_kernel,
        grid=(nseq // chunk,),
        in_specs=[seq_spec(tlen, D_MODEL),
                  mod_spec(3), mod_spec(4), mod_spec(5),
                  _const_spec((1, D_MODEL)),
                  _const_spec((D_MODEL, 2 * D_FF)),
                  _const_spec((3, D_FF)),
                  _const_spec((D_FF, D_MODEL)),
                  _const_spec((1, D_MODEL)),
                  seq_spec(2, D_FF)],
        out_specs=[seq_spec(tlen, D_MODEL), seq_spec(2, D_FF)],
        out_shape=[jax.ShapeDtypeStruct((nseq, tlen, D_MODEL), _F32),
                   jax.ShapeDtypeStruct((nseq, 2, D_FF), _F32)],
        compiler_params=pltpu.CompilerParams(
            dimension_semantics=("arbitrary",),
            vmem_limit_bytes=VMEM_LIMIT),
        name="ffn_sample",
    )(x1, mod3, mod3, mod3, n2g, wup, fcw, wdown, fg, st)


PROMPT_TILE = 512
FFN_TILE = 1024
FFN_PARTS = 2
SAMPLE_CHUNK = 16
ATTN_AFTER_WINDOWS = 4
FFN_SAMPLE_CHUNK = 64


def kernel(x_prompt, x_sample, c_prompt, c_sample, state_conv, cache_k_win, cache_v_win,
           state_ffn_conv, norm1_g, norm2_g, w_ada, b_ada, w_in, conv_w, w_conv_out,
           attn_sinks, w_attn_out, w_mix_out, w_up, ffn_conv_w, w_down, final_g):
    depth = w_in.shape[0]
    assert depth == 1, "the final norm is fused into the (single) layer's FFN call"
    nsamp = x_sample.shape[0]
    bsz = x_prompt.shape[0]
    xp, xs = x_prompt, x_sample
    fg = final_g.reshape(1, D_MODEL)
    outs = [[] for _ in range(8)]
    for layer in range(depth):
        mod3, win, wco, wao, wmo = _mod_call(
            c_sample, c_prompt, w_ada[layer], b_ada[layer].reshape(1, -1),
            (w_in[layer], w_conv_out[layer], w_attn_out[layer], w_mix_out[layer]))
        n1g = norm1_g[layer].reshape(1, D_MODEL)
        n2g = norm2_g[layer].reshape(1, D_MODEL)
        cw = conv_w[layer]
        fcw = ffn_conv_w[layer]
        sinks = attn_sinks[layer]
        fg_l = fg
        def to_dim_major(win_state):
            return jnp.transpose(win_state, (0, 2, 3, 1)).reshape(-1, KV_WIDTH, WINDOW)

        def to_pos_major(t3):
            return jnp.transpose(t3.reshape(-1, N_KV_HEADS, HEAD_DIM, WINDOW), (0, 3, 1, 2))

        x1p, conv_p, k_p, v_p, wup, wdown = _mix_prompt_call(
            xp, mod3, nsamp, n1g, win, cw, wco, sinks, wao, wmo,
            (w_up[layer], w_down[layer]), PROMPT_TILE)
        x1s, conv_s, k_s, v_s = _mix_sample_call(
            xs, mod3, n1g, win, cw, wco, sinks, wao, wmo, state_conv[layer],
            to_dim_major(cache_k_win[layer]), to_dim_major(cache_v_win[layer]), SAMPLE_CHUNK)
        xp, ffn_p = _ffn_prompt_call(x1p, mod3, nsamp, n2g, wup, fcw, wdown, fg_l,
                                     FFN_TILE, FFN_PARTS)
        xs, ffn_s = _ffn_sample_call(x1s, mod3, n2g, wup, fcw, wdown, fg_l,
                                     state_ffn_conv[layer], FFN_SAMPLE_CHUNK)
        for lst, val in zip(outs, (conv_p, to_pos_major(k_p), to_pos_major(v_p), ffn_p,
                                   conv_s, to_pos_major(k_s), to_pos_major(v_s), ffn_s)):
            lst.append(val)
    return (xp, xs) + tuple(jnp.stack(lst, axis=0) for lst in outs)
```

```python
import functools

import jax
import jax.numpy as jnp
from jax import lax
from jax.experimental import pallas as pl
from jax.experimental.pallas import tpu as pltpu

D_MODEL = 1024
HEAD_DIM = 64
N_HEADS = 16
N_KV_HEADS = 4
WINDOW = 128
D_FF = 2816
KV_WIDTH = N_KV_HEADS * HEAD_DIM
IN_WIDTH = 3 * D_MODEL + D_MODEL + 2 * KV_WIDTH + 2 * D_MODEL
RMS_EPS = 1e-6
NEG_INF = -1e30
LANES = 128
HEADS_PER_COL = LANES // HEAD_DIM
N_QCOLS = N_HEADS * HEAD_DIM // LANES
CARRY_ROWS = 8
VMEM_LIMIT = 60 * 1024 * 1024

_OFF_B, _OFF_C, _OFF_X = 0, D_MODEL, 2 * D_MODEL
_OFF_Q = 3 * D_MODEL
_OFF_K = _OFF_Q + D_MODEL
_OFF_V = _OFF_K + KV_WIDTH
_OFF_GA = _OFF_V + KV_WIDTH
_OFF_GB = _OFF_GA + D_MODEL

_BF = jnp.bfloat16
_F32 = jnp.float32
_NT = (((1,), (1,)), ((), ()))


LOG2E = 1.4426950408889634
Q_SCALE = HEAD_DIM ** -0.5 * LOG2E


def _slope(head):
    return 2.0 ** (-8.0 * (head + 1) / N_HEADS) * LOG2E


def _rms(x, g):
    return x * lax.rsqrt(jnp.mean(x * x, axis=-1, keepdims=True) + RMS_EPS) * g


def _dot(a, b):
    return jnp.dot(a, b, preferred_element_type=_F32)


def _gelu(x):
    return 0.5 * x * (1.0 + lax.erf(x * (0.5 ** 0.5)))


def _split_heads(t2d, lo):
    rolled = pltpu.roll(t2d, HEAD_DIM, 1)
    zero = jnp.zeros_like(t2d)
    return (jnp.where(lo, t2d, zero), jnp.where(lo, zero, rolled),
            jnp.where(lo, rolled, zero), jnp.where(lo, zero, t2d))


def _softmax_parts(s, sink):
    m = jnp.max(s, axis=-1, keepdims=True)
    p = jnp.exp2(s - m)
    denom = jnp.sum(p, axis=-1, keepdims=True) + jnp.exp2(sink - m)
    return p, 1.0 / denom


N_MOD = 6
MOD_COLS = 768


def _mod_kernel(cs_ref, cp_ref, w_ref, b_ref, *rest):
    n_cast = (len(rest) - 1) // 2
    cast_in, o_ref, cast_out = rest[:n_cast], rest[n_cast], rest[n_cast + 1:]
    for src, dst in zip(cast_in, cast_out):
        dst[...] = src[...].astype(_BF)
    w = w_ref[...].astype(_BF)
    n0 = cs_ref.shape[0]
    for c_ref, r0 in ((cs_ref, 0), (cp_ref, n0)):
        c = c_ref[...]
        a = (c * jax.nn.sigmoid(c)).astype(_BF)
        res = _dot(a, w) + b_ref[...]
        n = c.shape[0]
        o_ref[r0:r0 + n] = res.reshape(n, 1, res.shape[-1])


def _mod_call(c_sample, c_prompt, w_ada, b_ada, cast_along):
    ns, npr = c_sample.shape[0], c_prompt.shape[0]
    n = ns + npr
    steps = N_MOD * D_MODEL // MOD_COLS
    cast_specs = [_cast_stream_spec(w.shape, steps) for w in cast_along]
    return pl.pallas_call(
        _mod_kernel,
        grid=(steps,),
        in_specs=[pl.BlockSpec((ns, D_MODEL), lambda j: (0, 0)),
                  pl.BlockSpec((npr, D_MODEL), lambda j: (0, 0)),
                  pl.BlockSpec((D_MODEL, MOD_COLS), lambda j: (0, j)),
                  pl.BlockSpec((1, MOD_COLS), lambda j: (0, j))] + cast_specs,
        out_specs=[pl.BlockSpec((n, 1, MOD_COLS), lambda j: (0, 0, j))] + cast_specs,
        out_shape=[jax.ShapeDtypeStruct((n, 1, N_MOD * D_MODEL), _F32)]
        + [jax.ShapeDtypeStruct(w.shape, _BF) for w in cast_along],
        compiler_params=pltpu.CompilerParams(vmem_limit_bytes=VMEM_LIMIT),
        name="adaln_mod",
    )(c_sample, c_prompt, w_ada, b_ada, *cast_along)


def _mod_norm(x, g, sc, sh):
    return (_rms(x, g) * (1.0 + sc) + sh).astype(_BF)


def _mix_prompt_kernel(x_ref, sh_ref, sc_ref, g_ref, xn_ref, shn_ref, scn_ref, n1g_ref,
                       win_ref, cw_ref, wco_ref, sinks_ref, wao_ref, wmo_ref, *rest,
                       tiles_per_seq, n_cast):
    cast_in, rest = rest[:n_cast], rest[n_cast:]
    (x1_ref, conv_ref, k_ref, v_ref), rest = rest[:4], rest[4:]
    cast_out, (u_ext, k_ext, v_ext, o_scr, bias_scr, h_scr) = rest[:n_cast], rest[n_cast:]
    step = pl.program_id(0)
    i = step % tiles_per_seq
    t = x_ref.shape[0]
    nblk = t // WINDOW

    for src, dst in zip(cast_in, cast_out):
        dst[...] = src[...].astype(_BF)

    @pl.when(step == 0)
    def _():
        h_scr[...] = _mod_norm(x_ref[...], n1g_ref[...], sc_ref[...], sh_ref[...])
        row = lax.broadcasted_iota(jnp.int32, (WINDOW, 2 * WINDOW), 0)
        col = lax.broadcasted_iota(jnp.int32, (WINDOW, 2 * WINDOW), 1)
        dist = WINDOW + row - col
        valid = (dist >= 0) & (dist <= WINDOW)
        distf = dist.astype(_F32)
        for head in range(N_HEADS):
            bias_scr[head] = jnp.where(valid, -(_slope(head) * distf), NEG_INF)

    @pl.when(i == 0)
    def _():
        u_ext[0:CARRY_ROWS, :] = jnp.zeros((CARRY_ROWS, D_MODEL), _F32)
        k_ext[:, 0:WINDOW, :] = jnp.zeros((2 * N_KV_HEADS, WINDOW, LANES), _BF)
        v_ext[:, 0:WINDOW, :] = jnp.zeros((2 * N_KV_HEADS, WINDOW, LANES), _BF)

    def proj(lo, width):
        return _dot(h_scr[...], win_ref[:, lo:lo + width])

    def next_tile_norm():
        h_scr[...] = _mod_norm(xn_ref[...], n1g_ref[...], scn_ref[...], shn_ref[...])

    side = {}

    def conv_in():
        u = proj(_OFF_C, D_MODEL) * proj(_OFF_X, D_MODEL)
        u_ext[CARRY_ROWS:CARRY_ROWS + t, :] = u
        cw = cw_ref[...]
        side["uc"] = (u_ext[CARRY_ROWS - 2:CARRY_ROWS - 2 + t, :] * cw[0:1, :]
                      + u_ext[CARRY_ROWS - 1:CARRY_ROWS - 1 + t, :] * cw[1:2, :]
                      + u * cw[2:3, :])
        conv_ref[...] = u[t - 2:t, :]
        u_ext[0:CARRY_ROWS, :] = u[t - CARRY_ROWS:t, :]

    def conv_out():
        pb = proj(_OFF_B, D_MODEL)
        side["ya"] = _dot((pb * side.pop("uc")).astype(_BF), wco_ref[...])

    def gate_a():
        side["ga"] = jax.nn.sigmoid(proj(_OFF_GA, D_MODEL))

    def gate_b():
        side["gb"] = jax.nn.sigmoid(proj(_OFF_GB, D_MODEL))
        next_tile_norm()

    side_jobs = [conv_in, conv_out, gate_a, gate_b]

    q = (proj(_OFF_Q, D_MODEL) * Q_SCALE).astype(_BF)
    kf = proj(_OFF_K, KV_WIDTH)
    vf = proj(_OFF_V, KV_WIDTH)
    k_ref[...] = kf[t - WINDOW:t, :].T
    v_ref[...] = vf[t - WINDOW:t, :].T
    lane_t = lax.broadcasted_iota(jnp.int32, (t, LANES), 1)
    lo = lane_t < HEAD_DIM
    ones_lane = (HEAD_DIM, 0)
    for kc in range(KV_WIDTH // LANES):
        ks = _split_heads(kf[:, kc * LANES:(kc + 1) * LANES], lo)
        vs = _split_heads(vf[:, kc * LANES:(kc + 1) * LANES], lo)
        for j in range(4):
            k_ext[4 * kc + j, WINDOW:WINDOW + t, :] = ks[j].astype(_BF)
            v_one = jnp.where(lane_t == ones_lane[j % HEADS_PER_COL], 1.0, vs[j])
            v_ext[4 * kc + j, WINDOW:WINDOW + t, :] = v_one.astype(_BF)

    first_cols = lax.broadcasted_iota(jnp.int32, (WINDOW, 2 * WINDOW), 1) < WINDOW
    no_prev = first_cols & (i == 0)
    lo_q = lax.broadcasted_iota(jnp.int32, (WINDOW, LANES), 1) < HEAD_DIM
    for blk in range(nblk):
        r0 = blk * WINDOW
        for c in range(N_QCOLS):
            hkv = c // 2
            qc = q[r0:r0 + WINDOW, c * LANES:(c + 1) * LANES]
            o_pair = []
            for par in range(HEADS_PER_COL):
                head = HEADS_PER_COL * c + par
                kk = k_ext[2 * hkv + par, r0:r0 + 2 * WINDOW, :]
                s = lax.dot_general(qc, kk, _NT, preferred_element_type=_F32) + bias_scr[head]
                if blk == 0:
                    s = jnp.where(no_prev, NEG_INF, s)
                sink = sinks_ref[head] * LOG2E
                m = jnp.max(s, axis=-1, keepdims=True)
                p = jnp.exp2(s - m).astype(_BF)
                o_h = _dot(p, v_ext[2 * hkv + par, r0:r0 + 2 * WINDOW, :])
                col = ones_lane[par]
                denom = o_h[:, col:col + 1] + jnp.exp2(sink - m)
                o_pair.append(o_h * (1.0 / denom))
            o_c = jnp.where(lo_q, o_pair[0], o_pair[1])
            o_scr[r0:r0 + WINDOW, c * LANES:(c + 1) * LANES] = o_c.astype(_BF)
        for _ in range(-(-len(side_jobs) // (nblk - blk))):
            side_jobs.pop(0)()
    k_ext[:, 0:WINDOW, :] = k_ext[:, t:t + WINDOW, :]
    v_ext[:, 0:WINDOW, :] = v_ext[:, t:t + WINDOW, :]
    yb = _dot(o_scr[...], wao_ref[...])

    mixed = (side["ga"] * side["ya"] + side["gb"] * yb).astype(_BF)
    x1_ref[...] = x_ref[...] + g_ref[...] * _dot(mixed, wmo_ref[...])


def _const_spec(shape):
    nd = len(shape)
    return pl.BlockSpec(shape, lambda *_: (0,) * nd, pipeline_mode=pl.Buffered(1))


def _tile_specs(tile, nt, steps, mod_row0):
    def nxt(s):
        return jnp.minimum(s + 1, steps - 1)

    x_spec = pl.BlockSpec((None, tile, D_MODEL), lambda s: (s // nt, s % nt, 0))
    xn_spec = pl.BlockSpec((None, tile, D_MODEL), lambda s: (nxt(s) // nt, nxt(s) % nt, 0))

    def mod_spec(chunk):
        return pl.BlockSpec((None, 1, D_MODEL), lambda s: (mod_row0 + s // nt, 0, chunk))

    def modn_spec(chunk):
        return pl.BlockSpec((None, 1, D_MODEL), lambda s: (mod_row0 + nxt(s) // nt, 0, chunk))

    return x_spec, xn_spec, mod_spec, modn_spec


BF16_SUBLANES = 16


def _cast_stream_spec(shape, steps):
    rows, cols = shape
    for hold in range(1, steps + 1):
        nblocks, rem = divmod(steps, hold)
        if rem == 0 and rows % nblocks == 0 and (rows // nblocks) % BF16_SUBLANES == 0:
            return pl.BlockSpec((rows // nblocks, cols), lambda s: (s // hold, 0))
    raise ValueError(f"no bf16-aligned row split of {shape} over {steps} steps")


def _mix_prompt_call(x, mod3, mod_row0, n1g, win, cw, wco, sinks, wao, wmo, cast_along, tile):
    bsz, seq, _ = x.shape
    nt = seq // tile
    steps = bsz * nt
    x_spec, xn_spec, mod_spec, modn_spec = _tile_specs(tile, nt, steps, mod_row0)
    cast_specs = [_cast_stream_spec(w.shape, steps) for w in cast_along]

    def state_spec(d1, d2):
        return pl.BlockSpec((None, d1, d2), lambda s: (s // nt, 0, 0))

    return pl.pallas_call(
        functools.partial(_mix_prompt_kernel, tiles_per_seq=nt, n_cast=len(cast_along)),
        grid=(steps,),
        in_specs=[x_spec, mod_spec(0), mod_spec(1), mod_spec(2),
                  xn_spec, modn_spec(0), modn_spec(1),
                  _const_spec((1, D_MODEL)),
                  _const_spec((D_MODEL, IN_WIDTH)),
                  _const_spec((3, D_MODEL)),
                  _const_spec((D_MODEL, D_MODEL)),
                  pl.BlockSpec(memory_space=pltpu.SMEM),
                  _const_spec((D_MODEL, D_MODEL)),
                  _const_spec((D_MODEL, D_MODEL))] + cast_specs,
        out_specs=[x_spec,
                   state_spec(2, D_MODEL),
                   state_spec(KV_WIDTH, WINDOW),
                   state_spec(KV_WIDTH, WINDOW)] + cast_specs,
        out_shape=[jax.ShapeDtypeStruct((bsz, seq, D_MODEL), _F32),
                   jax.ShapeDtypeStruct((bsz, 2, D_MODEL), _F32),
                   jax.ShapeDtypeStruct((bsz, KV_WIDTH, WINDOW), _F32),
                   jax.ShapeDtypeStruct((bsz, KV_WIDTH, WINDOW), _F32)]
        + [jax.ShapeDtypeStruct(w.shape, _BF) for w in cast_along],
        scratch_shapes=[pltpu.VMEM((CARRY_ROWS + tile, D_MODEL), _F32),
                        pltpu.VMEM((2 * N_KV_HEADS, WINDOW + tile, LANES), _BF),
                        pltpu.VMEM((2 * N_KV_HEADS, WINDOW + tile, LANES), _BF),
                        pltpu.VMEM((tile, D_MODEL), _BF),
                        pltpu.VMEM((N_HEADS, WINDOW, 2 * WINDOW), _F32),
                        pltpu.VMEM((tile, D_MODEL), _BF)],
        compiler_params=pltpu.CompilerParams(
            dimension_semantics=("arbitrary",),
            vmem_limit_bytes=VMEM_LIMIT),
        name="mix_prompt",
    )(x, mod3, mod3, mod3, x, mod3, mod3, n1g, win, cw, wco, sinks, wao, wmo, *cast_along)


def _conv_seq(u2d, st, cw, nseq, tlen):
    width = u2d.shape[-1]
    shape3 = (nseq, tlen, width)
    u3 = u2d.reshape(shape3)
    r1 = pltpu.roll(u2d, 1, 0).reshape(shape3)
    r2 = pltpu.roll(u2d, 2, 0).reshape(shape3)
    tpos = lax.broadcasted_iota(jnp.int32, shape3, 1)
    st0 = jnp.broadcast_to(st[:, 0:1, :], shape3)
    st1 = jnp.broadcast_to(st[:, 1:2, :], shape3)
    um1 = jnp.where(tpos == 0, st1, r1)
    um2 = jnp.where(tpos == 0, st0, jnp.where(tpos == 1, st1, r2))
    y = um2 * cw[0:1, :] + um1 * cw[1:2, :] + u3 * cw[2:3, :]
    return y.reshape(nseq * tlen, width), u3[:, tlen - 2:tlen, :]


def _mix_sample_kernel(x_ref, sh_ref, sc_ref, g_ref, n1g_ref, win_ref, cw_ref, wco_ref,
                       sinks_ref, wao_ref, wmo_ref, st_ref, ck_ref, cv_ref,
                       x1_ref, conv_ref, k_ref, v_ref):
    nseq, tlen, _ = x_ref.shape
    rows = nseq * tlen

    assert rows == LANES, "one step's new keys must fill exactly one lane tile"

    x3 = x_ref[...]
    h3 = _rms(x3, n1g_ref[...]) * (1.0 + sc_ref[...]) + sh_ref[...]
    hf = h3.reshape(rows, D_MODEL)
    h_now = [hf.astype(_BF)]

    def proj(lo, width):
        return _dot(h_now[0], win_ref[:, lo:lo + width])

    def order_after(tile_f32):
        bits = pltpu.bitcast(tile_f32[0:8, :], jnp.uint32)
        zero = ((bits >> 16) >> 16).astype(_F32)
        h_now[0] = (hf + jnp.tile(zero, (rows // 8, D_MODEL // LANES))).astype(_BF)

    side = {}

    def job_conv_in():
        u = proj(_OFF_C, D_MODEL) * proj(_OFF_X, D_MODEL)
        side["uc"], new_st = _conv_seq(u, st_ref[...], cw_ref[...], nseq, tlen)
        conv_ref[...] = new_st

    def job_gate_b():
        side["pb"] = proj(_OFF_B, D_MODEL)

    def job_conv_out():
        side["ya"] = _dot((side.pop("pb") * side.pop("uc")).astype(_BF), wco_ref[...])

    def job_ga():
        side["ga"] = jax.nn.sigmoid(proj(_OFF_GA, D_MODEL))

    def job_gb():
        side["gb"] = jax.nn.sigmoid(proj(_OFF_GB, D_MODEL))

    keep = WINDOW - tlen
    qf = proj(_OFF_Q, D_MODEL) * Q_SCALE
    kvt_new = proj(_OFF_K, 2 * KV_WIDTH).T
    keep_old = lax.broadcasted_iota(jnp.int32, (KV_WIDTH, WINDOW), 1) < keep

    def window_update(s):
        shift = (keep - s * tlen) % LANES
        for cache_ref, out_ref, base in ((ck_ref, k_ref, 0), (cv_ref, v_ref, KV_WIDTH)):
            rolled = pltpu.roll(cache_ref[s], keep, 1)
            new = pltpu.roll(kvt_new[base:base + KV_WIDTH, :], shift, 1)
            win = jnp.where(keep_old, rolled, new)
            out_ref[s] = win
        return win

    lo = lax.broadcasted_iota(jnp.int32, (rows, LANES), 1) < HEAD_DIM
    zero = jnp.zeros((rows, LANES), _F32)
    n_tiles = KV_WIDTH // LANES
    heads_per_tile = N_HEADS // n_tiles
    m_rows = heads_per_tile * tlen
    row = lax.broadcasted_iota(jnp.int32, (m_rows, WINDOW), 0)
    col = lax.broadcasted_iota(jnp.int32, (m_rows, WINDOW), 1)
    dist_c = WINDOW + row % tlen - col
    valid_c = dist_c <= WINDOW
    shape3 = (nseq, m_rows, LANES)
    seq3 = lax.broadcasted_iota(jnp.int32, shape3, 0)
    row3 = lax.broadcasted_iota(jnp.int32, shape3, 1)
    col3 = lax.broadcasted_iota(jnp.int32, shape3, 2)
    dist_n = row3 % tlen - col3 % tlen
    valid_n = (col3 // tlen == seq3) & (dist_n >= 0)
    row1 = lax.broadcasted_iota(jnp.int32, (m_rows, 1), 0)

    def attention(kc):
        parts = []
        slope = jnp.zeros((m_rows, 1), _F32)
        sink = jnp.zeros((m_rows, 1), _F32)
        for j in range(heads_per_tile):
            head = heads_per_tile * kc + j
            c, par, half = head // HEADS_PER_COL, j % HEADS_PER_COL, j // (heads_per_tile // 2)
            qc = qf[:, c * LANES:(c + 1) * LANES]
            src = qc if par == half else pltpu.roll(qc, HEAD_DIM, 1)
            part = jnp.where(lo, src, zero) if half == 0 else jnp.where(lo, zero, src)
            parts.append(part.reshape(nseq, tlen, LANES))
            mine = (row1 >= j * tlen) & (row1 < (j + 1) * tlen)
            slope = jnp.where(mine, _slope(head), slope)
            sink = jnp.where(mine, sinks_ref[head] * LOG2E, sink)
        lhs = jnp.concatenate(parts, axis=1).astype(_BF)
        lhs2 = lhs.reshape(nseq * m_rows, LANES)
        rsl = slice(kc * LANES, (kc + 1) * LANES)
        k_new = kvt_new[rsl, :].astype(_BF)
        v_new = kvt_new[KV_WIDTH + kc * LANES:KV_WIDTH + (kc + 1) * LANES, :].astype(_BF)
        s_c = jnp.einsum('smk,skn->smn', lhs, ck_ref[:, rsl, :].astype(_BF),
                         preferred_element_type=_F32)
        s_n = _dot(lhs2, k_new).reshape(shape3)
        sc = jnp.concatenate(
            [s_c + jnp.where(valid_c, -(slope * dist_c.astype(_F32)), NEG_INF),
             s_n + jnp.where(valid_n, -(slope * dist_n.astype(_F32)), NEG_INF)], axis=2)
        p, inv = _softmax_parts(sc, sink)
        p = p.astype(_BF)
        r_c = jnp.einsum('smk,snk->smn', p[:, :, 0:WINDOW], cv_ref[:, rsl, :].astype(_BF),
                         preferred_element_type=_F32)
        r_n = lax.dot_general(p[:, :, WINDOW:].reshape(nseq * m_rows, LANES), v_new, _NT,
                              preferred_element_type=_F32).reshape(shape3)
        r = (r_c + r_n) * inv
        cols = {}
        for m2 in range(heads_per_tile // HEADS_PER_COL):
            c = (heads_per_tile // HEADS_PER_COL) * kc + m2
            half = m2 // (heads_per_tile // (2 * HEADS_PER_COL))
            r0 = r[:, (2 * m2) * tlen:(2 * m2 + 1) * tlen, :].reshape(rows, LANES)
            r1 = r[:, (2 * m2 + 1) * tlen:(2 * m2 + 2) * tlen, :].reshape(rows, LANES)
            if half == 0:
                cols[c] = jnp.where(lo, r0, pltpu.roll(r1, HEAD_DIM, 1))
            else:
                cols[c] = jnp.where(lo, pltpu.roll(r0, HEAD_DIM, 1), r1)
        return cols

    o_cols = {}
    job_conv_in()
    window_jobs = [job_gate_b, job_conv_out, job_ga, job_gb]
    group = -(-nseq // len(window_jobs))
    for s in range(nseq):
        if s == ATTN_AFTER_WINDOWS:
            for kc in range(n_tiles):
                o_cols.update(attention(kc))
        last_win = window_update(s)
        if window_jobs and (s + 1) % group == 0:
            window_jobs.pop(0)()
            order_after(last_win)
    while window_jobs:
        window_jobs.pop(0)()
    o = jnp.concatenate([o_cols[c] for c in range(N_QCOLS)], axis=1).astype(_BF)
    yb = _dot(o, wao_ref[...])

    mixed = (side["ga"] * side["ya"] + side["gb"] * yb).astype(_BF)
    y = _dot(mixed, wmo_ref[...]).reshape(nseq, tlen, D_MODEL)
    x1_ref[...] = x3 + g_ref[...] * y


def _mix_sample_call(x, mod3, n1g, win, cw, wco, sinks, wao, wmo, st, ck, cv, chunk):
    nseq, tlen, _ = x.shape
    assert LANES % tlen == 0 and chunk % (LANES // tlen) == 0

    def mod_spec(col):
        return pl.BlockSpec((chunk, 1, D_MODEL), lambda s: (s, 0, col))

    def seq_spec(d1, d2):
        return pl.BlockSpec((chunk, d1, d2), lambda s: (s, 0, 0))

    return pl.pallas_call(
        _mix_sample_kernel,
        grid=(nseq // chunk,),
        in_specs=[seq_spec(tlen, D_MODEL),
                  mod_spec(0), mod_spec(1), mod_spec(2),
                  _const_spec((1, D_MODEL)),
                  _const_spec((D_MODEL, IN_WIDTH)),
                  _const_spec((3, D_MODEL)),
                  _const_spec((D_MODEL, D_MODEL)),
                  pl.BlockSpec(memory_space=pltpu.SMEM),
                  _const_spec((D_MODEL, D_MODEL)),
                  _const_spec((D_MODEL, D_MODEL)),
                  seq_spec(2, D_MODEL),
                  seq_spec(KV_WIDTH, WINDOW),
                  seq_spec(KV_WIDTH, WINDOW)],
        out_specs=[seq_spec(tlen, D_MODEL),
                   seq_spec(2, D_MODEL),
                   seq_spec(KV_WIDTH, WINDOW),
                   seq_spec(KV_WIDTH, WINDOW)],
        out_shape=[jax.ShapeDtypeStruct((nseq, tlen, D_MODEL), _F32),
                   jax.ShapeDtypeStruct((nseq, 2, D_MODEL), _F32),
                   jax.ShapeDtypeStruct((nseq, KV_WIDTH, WINDOW), _F32),
                   jax.ShapeDtypeStruct((nseq, KV_WIDTH, WINDOW), _F32)],
        compiler_params=pltpu.CompilerParams(
            dimension_semantics=("arbitrary",),
            vmem_limit_bytes=VMEM_LIMIT),
        name="mix_sample",
    )(x, mod3, mod3, mod3, n1g, win, cw, wco, sinks, wao, wmo, st, ck, cv)


def _ffn_tail(x1, ac, val, g, wdown_ref, fg):
    hmid = (_gelu(ac) * val).astype(_BF)
    x2 = x1 + g * _dot(hmid, wdown_ref[...])
    return _rms(x2, fg)


def _ffn_prompt_kernel(x1_ref, sh_ref, sc_ref, g_ref, n2g_ref, wup_ref, fcw_ref, wdown_ref,
                       fg_ref, y_ref, ffn_ref, a_ext, *, parts):
    i = pl.program_id(1)
    t = x1_ref.shape[0]
    rows = t // parts

    @pl.when(i == 0)
    def _():
        a_ext[0:CARRY_ROWS, :] = jnp.zeros((CARRY_ROWS, D_FF), _F32)

    fcw = fcw_ref[...]

    def up(part):
        r0 = part * rows
        e0 = CARRY_ROWS + r0
        h2 = _mod_norm(x1_ref[r0:r0 + rows, :], n2g_ref[...], sc_ref[...], sh_ref[...])
        a = _dot(h2, wup_ref[:, 0:D_FF])
        val = _dot(h2, wup_ref[:, D_FF:2 * D_FF])
        a_ext[e0:e0 + rows, :] = a
        ac = (a_ext[e0 - 2:e0 - 2 + rows, :] * fcw[0:1, :]
              + a_ext[e0 - 1:e0 - 1 + rows, :] * fcw[1:2, :]
              + a * fcw[2:3, :])
        return ac, val

    def down(part, ac, val):
        r0 = part * rows
        y_ref[r0:r0 + rows, :] = _ffn_tail(x1_ref[r0:r0 + rows, :], ac, val, g_ref[...],
                                           wdown_ref, fg_ref[...])

    for part in range(parts):
        down(part, *up(part))
    ffn_ref[...] = a_ext[CARRY_ROWS + t - 2:CARRY_ROWS + t, :]
    a_ext[0:CARRY_ROWS, :] = a_ext[t:t + CARRY_ROWS, :]


def _ffn_prompt_call(x1, mod3, mod_row0, n2g, wup, fcw, wdown, fg, tile, parts):
    bsz, seq, _ = x1.shape

    def mod_spec(chunk):
        return pl.BlockSpec((None, 1, D_MODEL), lambda b, i: (mod_row0 + b, 0, chunk))

    x_spec = pl.BlockSpec((None, tile, D_MODEL), lambda b, i: (b, i, 0))
    return pl.pallas_call(
        functools.partial(_ffn_prompt_kernel, parts=parts),
        grid=(bsz, seq // tile),
        in_specs=[x_spec, mod_spec(3), mod_spec(4), mod_spec(5),
                  _const_spec((1, D_MODEL)),
                  _const_spec((D_MODEL, 2 * D_FF)),
                  _const_spec((3, D_FF)),
                  _const_spec((D_FF, D_MODEL)),
                  _const_spec((1, D_MODEL))],
        out_specs=[x_spec,
                   pl.BlockSpec((None, 2, D_FF), lambda b, i: (b, 0, 0))],
        out_shape=[jax.ShapeDtypeStruct((bsz, seq, D_MODEL), _F32),
                   jax.ShapeDtypeStruct((bsz, 2, D_FF), _F32)],
        scratch_shapes=[pltpu.VMEM((CARRY_ROWS + tile, D_FF), _F32)],
        compiler_params=pltpu.CompilerParams(
            dimension_semantics=("arbitrary", "arbitrary"),
            vmem_limit_bytes=VMEM_LIMIT),
        name="ffn_prompt",
    )(x1, mod3, mod3, mod3, n2g, wup, fcw, wdown, fg)


def _ffn_sample_kernel(x1_ref, sh_ref, sc_ref, g_ref, n2g_ref, wup_ref, fcw_ref, wdown_ref,
                       fg_ref, st_ref, y_ref, ffn_ref):
    nseq, tlen, _ = x1_ref.shape
    rows = nseq * tlen
    x3 = x1_ref[...]
    h3 = _rms(x3, n2g_ref[...]) * (1.0 + sc_ref[...]) + sh_ref[...]
    h2 = h3.reshape(rows, D_MODEL).astype(_BF)
    a = _dot(h2, wup_ref[:, 0:D_FF])
    val = _dot(h2, wup_ref[:, D_FF:2 * D_FF])
    ac, new_st = _conv_seq(a, st_ref[...], fcw_ref[...], nseq, tlen)
    ffn_ref[...] = new_st
    hmid = (_gelu(ac) * val).astype(_BF)
    y = _dot(hmid, wdown_ref[...]).reshape(nseq, tlen, D_MODEL)
    x2 = x3 + g_ref[...] * y
    y_ref[...] = _rms(x2, fg_ref[...])


def _ffn_sample_call(x1, mod3, n2g, wup, fcw, wdown, fg, st, chunk):
    nseq, tlen, _ = x1.shape

    def mod_spec(col):
        return pl.BlockSpec((chunk, 1, D_MODEL), lambda s: (s, 0, col))

    def seq_spec(d1, d2):
        return pl.BlockSpec((chunk, d1, d2), lambda s: (s, 0, 0))

    return pl.pallas_call(
        _ffn_sample_kernel,
        grid=(nseq // chunk,),
        in_specs=[seq_spec(tlen, D_MODEL),
                  mod_spec(3), mod_spec(4), mod_spec(5),
                  _const_spec((1, D_MODEL)),
                  _const_spec((D_MODEL, 2 * D_FF)),
                  _const_spec((3, D_FF)),
                  _const_spec((D_FF, D_MODEL)),
                  _const_spec((1, D_MODEL)),
                  seq_spec(2, D_FF)],
        out_specs=[seq_spec(tlen, D_MODEL), seq_spec(2, D_FF)],
        out_shape=[jax.ShapeDtypeStruct((nseq, tlen, D_MODEL), _F32),
                   jax.ShapeDtypeStruct((nseq, 2, D_FF), _F32)],
        compiler_params=pltpu.CompilerParams(
            dimension_semantics=("arbitrary",),
            vmem_limit_bytes=VMEM_LIMIT),
        name="ffn_sample",
    )(x1, mod3, mod3, mod3, n2g, wup, fcw, wdown, fg, st)


PROMPT_TILE = 512
FFN_TILE = 1024
FFN_PARTS = 2
SAMPLE_CHUNK = 16
ATTN_AFTER_WINDOWS = 4
FFN_SAMPLE_CHUNK = 64


def kernel(x_prompt, x_sample, c_prompt, c_sample, state_conv, cache_k_win, cache_v_win,
           state_ffn_conv, norm1_g, norm2_g, w_ada, b_ada, w_in, conv_w, w_conv_out,
           attn_sinks, w_attn_out, w_mix_out, w_up, ffn_conv_w, w_down, final_g):
    depth = w_in.shape[0]
    assert depth == 1, "the final norm is fused into the (single) layer's FFN call"
    nsamp = x_sample.shape[0]
    bsz = x_prompt.shape[0]
    xp, xs = x_prompt, x_sample
    fg = final_g.reshape(1, D_MODEL)
    outs = [[] for _ in range(8)]
    for layer in range(depth):
        mod3, win, wco, wao, wmo = _mod_call(
            c_sample, c_prompt, w_ada[layer], b_ada[layer].reshape(1, -1),
            (w_in[layer], w_conv_out[layer], w_attn_out[layer], w_mix_out[layer]))
        n1g = norm1_g[layer].reshape(1, D_MODEL)
        n2g = norm2_g[layer].reshape(1, D_MODEL)
        cw = conv_w[layer]
        fcw = ffn_conv_w[layer]
        sinks = attn_sinks[layer]
        fg_l = fg
        def to_dim_major(win_state):
            return jnp.transpose(win_state, (0, 2, 3, 1)).reshape(-1, KV_WIDTH, WINDOW)

        def to_pos_major(t3):
            return jnp.transpose(t3.reshape(-1, N_KV_HEADS, HEAD_DIM, WINDOW), (0, 3, 1, 2))

        x1p, conv_p, k_p, v_p, wup, wdown = _mix_prompt_call(
            xp, mod3, nsamp, n1g, win, cw, wco, sinks, wao, wmo,
            (w_up[layer], w_down[layer]), PROMPT_TILE)
        x1s, conv_s, k_s, v_s = _mix_sample_call(
            xs, mod3, n1g, win, cw, wco, sinks, wao, wmo, state_conv[layer],
            to_dim_major(cache_k_win[layer]), to_dim_major(cache_v_win[layer]), SAMPLE_CHUNK)
        xp, ffn_p = _ffn_prompt_call(x1p, mod3, nsamp, n2g, wup, fcw, wdown, fg_l,
                                     FFN_TILE, FFN_PARTS)
        xs, ffn_s = _ffn_sample_call(x1s, mod3, n2g, wup, fcw, wdown, fg_l,
                                     state_ffn_conv[layer], FFN_SAMPLE_CHUNK)
        for lst, val in zip(outs, (conv_p, to_pos_major(k_p), to_pos_major(v_p), ffn_p,
                                   conv_s, to_pos_major(k_s), to_pos_major(v_s), ffn_s)):
            lst.append(val)
    return (xp, xs) + tuple(jnp.stack(lst, axis=0) for lst in outs)
```
